```python
import math
import jax, jax.numpy as jnp
from jax import lax
import numpy as np

D_MODEL = 2048
BATCH = 2
SEQ = 4096
DEPTH = 2

CTX_LEN = 256
GRID_W = 64
W_CONV = 1024
CONV_K = 3
SSM_H = 16
SSM_P = 64
W_SSM = 1024
SSM_G = W_SSM // SSM_H
W_FFT = 1024
FFT_GROUPS = 4
FFT_GW = W_FFT // FFT_GROUPS
N_BRANCH = 3
RMS_EPS = 1e-6
U_OFF = 4 * W_CONV
IN_WIDTH = 4 * W_CONV + 2 * W_SSM + 2 * W_FFT + N_BRANCH * D_MODEL

kernel_name = "hybrid_conv_s5_fnet_parallel_dit_block"


def _split_proj(p):
    idx = [W_CONV, 2 * W_CONV, 3 * W_CONV, 4 * W_CONV,
           4 * W_CONV + W_SSM, 4 * W_CONV + 2 * W_SSM,
           4 * W_CONV + 2 * W_SSM + W_FFT, 4 * W_CONV + 2 * W_SSM + 2 * W_FFT]
    return jnp.split(p, idx, axis=-1)


def _rmsnorm(x, g):
    x32 = x.astype(jnp.float32)
    y = x32 * lax.rsqrt(jnp.mean(x32 * x32, axis=-1, keepdims=True) + RMS_EPS)
    return (y * g.astype(jnp.float32)).astype(x.dtype)


def _short_conv(v, w):
    ch = v.shape[-1]
    return lax.conv_general_dilated(
        v, w[:, None, :].astype(v.dtype), window_strides=(1,), padding=((1, 1),),
        dimension_numbers=('NWC', 'WIO', 'NWC'), feature_group_count=ch)


def _conv_branch(xa, ba, ca, za, w, on_grid):
    v = ca * xa
    if on_grid:
        bn, t, ch = v.shape
        rows = t // GRID_W
        y = _short_conv(v.reshape(bn * rows, GRID_W, ch), w).reshape(bn, t, ch)
    else:
        y = _short_conv(v, w)
    return ba * y * jax.nn.silu(za)


def _fourier_mix(f):
    bn, t, _ = f.shape
    fg = f.astype(jnp.float32).reshape(bn, t, FFT_GROUPS, FFT_GW)
    y = jnp.fft.fft2(fg, axes=(1, 3), norm='ortho').real
    return y.reshape(bn, t, W_FFT).astype(f.dtype)


def _zoh(lam_re, lam_im, log_dt, b_re, b_im):
    lam_re = lam_re.astype(jnp.float32)
    lam_im = lam_im.astype(jnp.float32)
    b_re = b_re.astype(jnp.float32)
    b_im = b_im.astype(jnp.float32)
    dt = jnp.exp(log_dt.astype(jnp.float32))[:, None]
    mag = jnp.exp(lam_re * dt)
    ang = lam_im * dt
    a_re = mag * jnp.cos(ang)
    a_im = mag * jnp.sin(ang)
    n_re = a_re - 1.0
    n_im = a_im
    den = lam_re * lam_re + lam_im * lam_im
    q_re = (n_re * lam_re + n_im * lam_im) / den
    q_im = (n_im * lam_re - n_re * lam_im) / den
    bb_re = q_re[..., None] * b_re - q_im[..., None] * b_im
    bb_im = q_re[..., None] * b_im + q_im[..., None] * b_re
    return a_re, a_im, bb_re, bb_im


def _combine(e1, e2):
    a1r, a1i, b1r, b1i = e1
    a2r, a2i, b2r, b2i = e2
    return (a2r * a1r - a2i * a1i,
            a2r * a1i + a2i * a1r,
            a2r * b1r - a2i * b1i + b2r,
            a2r * b1i + a2i * b1r + b2i)


def _scan_states(u_g, a_re, a_im, bb_re, bb_im, h0, reverse):
    s_re = jnp.einsum('btgh,gph->btgp', u_g, bb_re)
    s_im = jnp.einsum('btgh,gph->btgp', u_g, bb_im)
    if h0 is not None:
        h_re, h_im = h0
        edge = -1 if reverse else 0
        s_re = s_re.at[:, edge].add(a_re * h_re - a_im * h_im)
        s_im = s_im.at[:, edge].add(a_re * h_im + a_im * h_re)
    t = u_g.shape[1]
    ar = jnp.broadcast_to(a_re, (1, t) + a_re.shape)
    ai = jnp.broadcast_to(a_im, (1, t) + a_im.shape)
    _, _, x_re, x_im = lax.associative_scan(_combine, (ar, ai, s_re, s_im), reverse=reverse, axis=1)
    return x_re, x_im


def _readout(x_re, x_im, c_re, c_im):
    y = (jnp.einsum('btgp,ghp->btgh', x_re, c_re.astype(jnp.float32))
         - jnp.einsum('btgp,ghp->btgh', x_im, c_im.astype(jnp.float32)))
    return y.reshape(y.shape[0], y.shape[1], W_SSM)


def _branches_out(s, parts, y_ssm, on_grid, gate, conv_w, ssm_d, glu_wa, glu_wb,
                  fourier_w, proj_a, proj_b, proj_c, w_out, g_post):
    xa, ba, ca, za, u, zb, f, zc, gl = parts
    ya = _conv_branch(xa, ba, ca, za, conv_w, on_grid) @ proj_a
    y = y_ssm.astype(u.dtype) + ssm_d * u
    y = jax.nn.gelu(y)
    y = (y @ glu_wa) * jax.nn.sigmoid(y @ glu_wb) * jax.nn.silu(zb)
    yb = y @ proj_b
    yc = ((_fourier_mix(f) @ fourier_w) * jax.nn.silu(zc)) @ proj_c
    g = jax.nn.sigmoid(gl).reshape(gl.shape[:-1] + (N_BRANCH, D_MODEL))
    m = g[..., 0, :] * ya + g[..., 1, :] * yb + g[..., 2, :] * yc
    o = m @ w_out
    return s + gate * _rmsnorm(o, g_post)


def setup_inputs(seed: int = 0) -> dict:
    key = jax.random.key(seed)
    ks = jax.random.split(key, 26)
    f32 = jnp.float32
    nrm = lambda k, shape, std: jax.random.normal(k, shape, f32) * std
    inp = {}
    inp['x'] = nrm(ks[0], (BATCH, SEQ, D_MODEL), 1.0)
    inp['c'] = nrm(ks[1], (BATCH, D_MODEL), 1.0)
    inp['ctx'] = nrm(ks[2], (BATCH, CTX_LEN, D_MODEL), 1.0)
    inp['c_ctx'] = nrm(ks[3], (D_MODEL,), 1.0)
    inp['w_ada'] = nrm(ks[4], (DEPTH, D_MODEL, 3 * D_MODEL), 0.5 * D_MODEL ** -0.5)
    inp['b_ada'] = nrm(ks[5], (DEPTH, 3 * D_MODEL), 0.01)
    inp['g_pre'] = 1.0 + nrm(ks[6], (DEPTH, D_MODEL), 0.02)
    inp['g_post'] = 1.0 + nrm(ks[7], (DEPTH, D_MODEL), 0.02)
    inp['w_in'] = nrm(ks[8], (DEPTH, D_MODEL, IN_WIDTH), D_MODEL ** -0.5)
    inp['conv_w'] = nrm(ks[9], (DEPTH, CONV_K, W_CONV), CONV_K ** -0.5)
    inp['ssm_lam_re'] = -0.5 + nrm(ks[10], (DEPTH, 2, SSM_G, SSM_P), 0.01)
    inp['ssm_lam_im'] = (math.pi * jnp.arange(SSM_P, dtype=f32)
                         + nrm(ks[11], (DEPTH, 2, SSM_G, SSM_P), 0.01))
    inp['ssm_log_dt'] = jax.random.uniform(ks[12], (DEPTH, 2, SSM_G), f32,
                                           minval=math.log(1e-3), maxval=math.log(1e-1))
    inp['ssm_b_re'] = nrm(ks[13], (DEPTH, 2, SSM_G, SSM_P, SSM_H), (2 * SSM_H) ** -0.5)
    inp['ssm_b_im'] = nrm(ks[14], (DEPTH, 2, SSM_G, SSM_P, SSM_H), (2 * SSM_H) ** -0.5)
    inp['ssm_c_re'] = nrm(ks[15], (DEPTH, 2, SSM_G, SSM_H, SSM_P), SSM_P ** -0.5)
    inp['ssm_c_im'] = nrm(ks[16], (DEPTH, 2, SSM_G, SSM_H, SSM_P), SSM_P ** -0.5)
    inp['ssm_d'] = nrm(ks[17], (DEPTH, W_SSM), 1.0)
    inp['glu_wa'] = nrm(ks[18], (DEPTH, W_SSM, W_SSM), W_SSM ** -0.5)
    inp['glu_wb'] = nrm(ks[19], (DEPTH, W_SSM, W_SSM), W_SSM ** -0.5)
    inp['fourier_w'] = nrm(ks[20], (DEPTH, W_FFT, W_FFT), W_FFT ** -0.5)
    inp['proj_a'] = nrm(ks[21], (DEPTH, W_CONV, D_MODEL), W_CONV ** -0.5)
    inp['proj_b'] = nrm(ks[22], (DEPTH, W_SSM, D_MODEL), W_SSM ** -0.5)
    inp['proj_c'] = nrm(ks[23], (DEPTH, W_FFT, D_MODEL), W_FFT ** -0.5)
    inp['w_out'] = nrm(ks[24], (DEPTH, D_MODEL, D_MODEL), D_MODEL ** -0.5)
    return inp


def reference(x, c, ctx, c_ctx, w_ada, b_ada, g_pre, g_post, w_in, conv_w,
              ssm_lam_re, ssm_lam_im, ssm_log_dt, ssm_b_re, ssm_b_im, ssm_c_re, ssm_c_im,
              ssm_d, glu_wa, glu_wb, fourier_w, proj_a, proj_b, proj_c, w_out):
    bn, t, _ = x.shape
    lc = ctx.shape[1]
    for l in range(DEPTH):
        last = l == DEPTH - 1
        mod_x = jax.nn.silu(c) @ w_ada[l] + b_ada[l]
        sh_x, sc_x, gt_x = jnp.split(mod_x[:, None, :], 3, axis=-1)
        mod_c = jax.nn.silu(c_ctx) @ w_ada[l] + b_ada[l]
        sh_c, sc_c, gt_c = jnp.split(mod_c, 3)
        h_x = _rmsnorm(x, g_pre[l]) * (1 + sc_x) + sh_x
        h_c = _rmsnorm(ctx, g_pre[l]) * (1 + sc_c) + sh_c
        parts_x = _split_proj(h_x @ w_in[l])
        if last:
            parts_c = None
            u_c = h_c @ w_in[l][:, U_OFF:U_OFF + W_SSM]
        else:
            parts_c = _split_proj(h_c @ w_in[l])
            u_c = parts_c[4]
        u_c_g = u_c.astype(jnp.float32).reshape(bn, lc, SSM_G, SSM_H)
        u_x_g = parts_x[4].astype(jnp.float32).reshape(bn, t, SSM_G, SSM_H)
        ys_x = []
        ys_c = []
        for d in range(2):
            rev = d == 1
            disc = _zoh(ssm_lam_re[l, d], ssm_lam_im[l, d], ssm_log_dt[l, d],
                        ssm_b_re[l, d], ssm_b_im[l, d])
            cr, ci = _scan_states(u_c_g, *disc, None, rev)
            edge = 0 if rev else -1
            h0 = (cr[:, edge], ci[:, edge])
            xr, xi = _scan_states(u_x_g, *disc, h0, rev)
            ys_x.append(_readout(xr, xi, ssm_c_re[l, d], ssm_c_im[l, d]))
            if not last:
                ys_c.append(_readout(cr, ci, ssm_c_re[l, d], ssm_c_im[l, d]))
        new_x = _branches_out(x, parts_x, ys_x[0] + ys_x[1], True, gt_x, conv_w[l], ssm_d[l],
                              glu_wa[l], glu_wb[l], fourier_w[l], proj_a[l], proj_b[l],
                              proj_c[l], w_out[l], g_post[l])
        if not last:
            ctx = _branches_out(ctx, parts_c, ys_c[0] + ys_c[1], False, gt_c, conv_w[l], ssm_d[l],
                                glu_wa[l], glu_wb[l], fourier_w[l], proj_a[l], proj_b[l],
                                proj_c[l], w_out[l], g_post[l])
        x = new_x
    return x
```

```python
import functools
import math

import jax
import jax.numpy as jnp
from jax import lax
from jax.experimental import pallas as pl
from jax.experimental.pallas import tpu as pltpu

F32 = jnp.float32
BF16 = jnp.bfloat16

GRID_W = 64
FFT_GROUPS = 4
N_BRANCH = 3
RMS_EPS = 1e-6
LANES = 128
SSM_CHUNK = 8
SSM_SEGS = 8
VMEM_LIMIT = 56 * 1024 * 1024


def _cparams(*sem):
    return pltpu.CompilerParams(dimension_semantics=sem, vmem_limit_bytes=VMEM_LIMIT)


def _sigmoid(v):
    return 1.0 / (1.0 + jnp.exp(-v))


def _silu(v):
    return v * _sigmoid(v)


def _gelu_tanh(v):
    return v * (0.5 * (1.0 + jnp.tanh(math.sqrt(2.0 / math.pi) * (v + 0.044715 * (v * v * v)))))


def _ada_kernel(c_ref, w_ref, b_ref, o_ref):
    c = c_ref[...]
    s = _silu(c).astype(BF16)
    o_ref[0] = jnp.dot(s, w_ref[0].astype(BF16), preferred_element_type=F32) + b_ref[0]


def _ada_mods(cc, w_ada, b_ada):
    depth, d, n3 = w_ada.shape
    tn = min(1024, n3)
    return pl.pallas_call(
        _ada_kernel,
        grid=(depth, n3 // tn),
        in_specs=[
            pl.BlockSpec((8, d), lambda l, j: (0, 0)),
            pl.BlockSpec((1, d, tn), lambda l, j: (l, 0, j)),
            pl.BlockSpec((1, 1, tn), lambda l, j: (l, 0, j)),
        ],
        out_specs=pl.BlockSpec((1, 8, tn), lambda l, j: (l, 0, j)),
        out_shape=jax.ShapeDtypeStruct((depth, 8, n3), F32),
        compiler_params=_cparams("arbitrary", "arbitrary"),
        name="ada_mods",
    )(cc, w_ada, b_ada.reshape(depth, 1, n3))


def _inproj_kernel(x_ref, sc_ref, sh_ref, g_ref, w_ref, p_ref, u_ref, h_scr, *, u_tile):
    j = pl.program_id(1)

    @pl.when(j == 0)
    def _():
        x = x_ref[...]
        ms = jnp.mean(x * x, axis=-1, keepdims=True)
        y = x * lax.rsqrt(ms + RMS_EPS) * g_ref[0]
        h_scr[...] = (y * (1.0 + sc_ref[0]) + sh_ref[0]).astype(BF16)

    acc = jnp.dot(h_scr[...], w_ref[0], preferred_element_type=F32)
    p_ref[...] = acc.astype(BF16)

    @pl.when(j == u_tile)
    def _():
        u_ref[...] = acc


def _inproj(x2, sc, sh, g_pre, w_b, layer, tile0, ntiles, u_tile, rows_per_batch):
    n, d = x2.shape
    tn = 1024
    tm = min(1024, rows_per_batch)
    tpb = rows_per_batch // tm
    return pl.pallas_call(
        functools.partial(_inproj_kernel, u_tile=u_tile),
        grid=(n // tm, ntiles),
        in_specs=[
            pl.BlockSpec((tm, d), lambda i, j: (i, 0)),
            pl.BlockSpec((1, 1, d), lambda i, j: (i // tpb, 0, 0)),
            pl.BlockSpec((1, 1, d), lambda i, j: (i // tpb, 0, 0)),
            pl.BlockSpec((1, 1, d), lambda i, j: (layer, 0, 0)),
            pl.BlockSpec((1, d, tn), lambda i, j: (layer, 0, tile0 + j)),
        ],
        out_specs=[
            pl.BlockSpec((tm, tn), lambda i, j: (i, j)),
            pl.BlockSpec((tm, tn), lambda i, j: (i, 0)),
        ],
        out_shape=[
            jax.ShapeDtypeStruct((n, ntiles * tn), BF16),
            jax.ShapeDtypeStruct((n, tn), F32),
        ],
        scratch_shapes=[pltpu.VMEM((tm, d), BF16)],
        compiler_params=_cparams("arbitrary", "arbitrary"),
        name="inproj",
    )(x2, sc, sh, g_pre.reshape(g_pre.shape[0], 1, d), w_b)


def _ssm_weights(lam_re, lam_im, log_dt, b_re, b_im, c_re, c_im, seg_len):
    hp = lax.Precision.HIGHEST
    L = SSM_CHUNK
    _, G, P = lam_re.shape
    H = b_re.shape[-1]
    gpt = LANES // H
    O = G // gpt
    lam_re = lam_re.astype(F32)
    lam_im = lam_im.astype(F32)
    dt = jnp.exp(log_dt.astype(F32))[..., None]
    lr = lam_re * dt
    li = lam_im * dt
    mag = jnp.exp(lr)
    a_re = mag * jnp.cos(li)
    a_im = mag * jnp.sin(li)
    n_re = a_re - 1.0
    n_im = a_im
    den = lam_re * lam_re + lam_im * lam_im
    q_re = (n_re * lam_re + n_im * lam_im) / den
    q_im = (n_im * lam_re - n_re * lam_im) / den
    bb_re = q_re[..., None] * b_re - q_im[..., None] * b_im
    bb_im = q_re[..., None] * b_im + q_im[..., None] * b_re

    def apower(k):
        k = jnp.asarray(k, F32).reshape((-1, 1, 1, 1))
        m = jnp.exp(k * lr)
        return m * jnp.cos(k * li), m * jnp.sin(k * li)

    pk_re, pk_im = apower(jnp.arange(L + 1))
    ab_re = pk_re[:L, ..., None] * bb_re - pk_im[:L, ..., None] * bb_im
    ab_im = pk_re[:L, ..., None] * bb_im + pk_im[:L, ..., None] * bb_re
    ca_re = c_re * pk_re[:, :, :, None, :] - c_im * pk_im[:, :, :, None, :]
    ca_im = c_re * pk_im[:, :, :, None, :] + c_im * pk_re[:, :, :, None, :]
    taps = (jnp.einsum('dgqp,kdgph->kdgqh', c_re, ab_re, precision=hp)
            - jnp.einsum('dgqp,kdgph->kdgqh', c_im, ab_im, precision=hp))

    eye = jnp.eye(gpt, dtype=F32)
    jj = jnp.arange(L)

    e_f = L - 1 - jj
    e_b = jj
    ab = jnp.stack([ab_re, ab_im], axis=2)
    ab_sel = jnp.stack([ab[e_f, 0], ab[e_b, 1]], axis=1)
    ab_sel = ab_sel.reshape(L, 2, 2, O, gpt, P, H)
    ws = jnp.einsum('ba,jdrobph->ojbhdrap', eye, ab_sel).reshape(O, L * LANES, 4 * gpt * P)

    ca = jnp.stack([ca_re, -ca_im], axis=2)
    ca_sel = jnp.stack([ca[jj + 1, 0], ca[L - jj, 1]], axis=1)
    ca_sel = ca_sel.reshape(L, 2, 2, O, gpt, H, P)
    wo = jnp.einsum('ba,jdrobqp->odrbpjaq', eye, ca_sel).reshape(O, 4 * gpt * P, L * LANES)

    lag = jj[None, :] - jj[:, None]
    tf = jnp.where((lag >= 0)[:, :, None, None, None], taps[jnp.clip(lag, 0, L - 1), 0], 0.0)
    tb = jnp.where((lag <= 0)[:, :, None, None, None], taps[jnp.clip(-lag, 0, L - 1), 1], 0.0)
    tt = (tf + tb).reshape(L, L, O, gpt, H, H)
    m = jnp.einsum('ba,jkobqh->ojbhkaq', eye, tt).reshape(O, L * LANES, L * LANES)

    w3 = jnp.concatenate([m, wo], axis=1)

    def cols(re, im):
        k = re.shape[0]
        z = jnp.stack([re, im], axis=2).reshape(k, 2, 2, O, gpt * P)
        return jnp.transpose(z, (3, 0, 1, 2, 4)).reshape(O, k, 4 * gpt * P)

    ap = cols(*apower(jnp.array([L, L * seg_len])))
    apow = jnp.concatenate([ap, jnp.zeros((O, 6, ap.shape[-1]), F32)], axis=1)
    ii = jnp.arange(seg_len)
    pf_re, pf_im = apower(L * ii)
    pb_re, pb_im = apower(L * (seg_len - 1 - ii))
    pw = cols(jnp.stack([pf_re[:, 0], pb_re[:, 1]], axis=1), jnp.stack([pf_im[:, 0], pb_im[:, 1]], axis=1))
    return ws.astype(BF16), w3.astype(BF16), apow, pw


def _cmul_add(ar, ai, zr, zi, sr, si):
    return ar * zr - ai * zi + sr, ar * zi + ai * zr + si


def _ssm_kernel(*refs, nc, nx, seg_len, ctx_out):
    if ctx_out:
        ux_ref, uc_ref, ws_ref, w3_ref, apow_ref, pw_ref, yx_ref, yc_ref, lhs, st = refs
    else:
        ux_ref, uc_ref, ws_ref, w3_ref, apow_ref, pw_ref, yx_ref, lhs, st = refs
        yc_ref = None
    L = SSM_CHUNK
    lk = L * LANES
    nlt = st.shape[0] // 4
    sw = nlt * LANES
    FR, FI, BR, BI = range(4)

    def ld(part, rows):
        return jnp.concatenate([st[part * nlt + q, rows, :] for q in range(nlt)], axis=1)

    def sto(part, rows, val):
        for q in range(nlt):
            st[part * nlt + q, rows, :] = val[:, q * LANES:(q + 1) * LANES]

    def part_cols(part):
        return slice(part * sw, (part + 1) * sw)

    for j in range(L):
        lhs[0:nc, j * LANES:(j + 1) * LANES] = uc_ref[pl.ds(j, nc, stride=L), :].astype(BF16)
        lhs[nc:nc + nx, j * LANES:(j + 1) * LANES] = ux_ref[pl.ds(j, nx, stride=L), :].astype(BF16)

    rb = 128
    blocks = [(0, nc)] + [(nc + r * rb, rb) for r in range(nx // rb)]
    for r0, nr in blocks:
        s = jnp.dot(lhs[r0:r0 + nr, 0:lk], ws_ref[0], preferred_element_type=F32)
        for k in range(4 * nlt):
            st[k, r0:r0 + nr, :] = s[:, k * LANES:(k + 1) * LANES]

    a_l = apow_ref[0, 0:1, :]
    a_seg = apow_ref[0, 1:2, :]
    al = [a_l[:, part_cols(p)] for p in range(4)]
    aseg = [a_seg[:, part_cols(p)] for p in range(4)]

    zero = jnp.zeros((1, sw), F32)

    def ctx_step(pr, pi):
        def step(i, carry):
            er, ei = carry
            row = pl.ds(i, 1)
            sr, si = ld(pr, row), ld(pi, row)
            sto(pr, row, er)
            sto(pi, row, ei)
            return _cmul_add(al[pr], al[pi], er, ei, sr, si)
        return step

    h0f = lax.fori_loop(0, nc, ctx_step(FR, FI), (zero, zero))
    bwd_step = ctx_step(BR, BI)
    h0b = lax.fori_loop(0, nc, lambda k, carry: bwd_step(nc - 1 - k, carry), (zero, zero))
    for k in range(4 * nlt):
        lhs[0:nc, lk + k * LANES:lk + (k + 1) * LANES] = st[k, 0:nc, :].astype(BF16)

    nseg = SSM_SEGS
    zseg = jnp.zeros((nseg, sw), F32)

    def seg_pass(pr, pi, order):
        zr, zi = zseg, zseg
        for i in order:
            rows = pl.ds(nc + i, nseg, stride=seg_len)
            sr, si = ld(pr, rows), ld(pi, rows)
            sto(pr, rows, zr)
            sto(pi, rows, zi)
            zr, zi = _cmul_add(al[pr], al[pi], zr, zi, sr, si)
        return zr, zi

    zfr, zfi = seg_pass(FR, FI, range(seg_len))
    zbr, zbi = seg_pass(BR, BI, range(seg_len - 1, -1, -1))

    ef = [h0f]
    for s in range(nseg - 1):
        ef.append(_cmul_add(aseg[FR], aseg[FI], ef[s][0], ef[s][1], zfr[s:s + 1], zfi[s:s + 1]))
    eb = [None] * nseg
    eb[nseg - 1] = h0b
    for s in range(nseg - 1, 0, -1):
        eb[s - 1] = _cmul_add(aseg[BR], aseg[BI], eb[s][0], eb[s][1], zbr[s:s + 1], zbi[s:s + 1])
    for s in range(nseg):
        r0 = nc + s * seg_len
        rows = slice(r0, r0 + seg_len)
        for (pr, pi, e) in ((FR, FI, ef[s]), (BR, BI, eb[s])):
            xr, xi = _cmul_add(pw_ref[0, :, part_cols(pr)], pw_ref[0, :, part_cols(pi)], e[0], e[1],
                               ld(pr, rows), ld(pi, rows))
            lhs[rows, lk + pr * sw:lk + (pr + 1) * sw] = xr.astype(BF16)
            lhs[rows, lk + pi * sw:lk + (pi + 1) * sw] = xi.astype(BF16)

    if ctx_out:
        yc = jnp.dot(lhs[0:nc, :], w3_ref[0], preferred_element_type=F32)
        for j in range(L):
            yc_ref[pl.ds(j, nc, stride=L), :] = yc[:, j * LANES:(j + 1) * LANES]
    for r in range(nx // rb):
        r0 = nc + r * rb
        y = jnp.dot(lhs[r0:r0 + rb, :], w3_ref[0], preferred_element_type=F32)
        for j in range(L):
            yx_ref[pl.ds(r * rb * L + j, rb, stride=L), :] = y[:, j * LANES:(j + 1) * LANES]


def _ssm_scan(ux, uc, ws, w3, apow, pw, batch, ctx_out):
    L = SSM_CHUNK
    n_x, w = ux.shape
    t = n_x // batch
    lc = uc.shape[0] // batch
    nx, nc = t // L, lc // L
    seg_len = nx // SSM_SEGS
    o = w // LANES
    lk = L * LANES
    sdim = ws.shape[-1]
    out_shape = [jax.ShapeDtypeStruct(ux.shape, F32)]
    out_specs = [pl.BlockSpec((t, LANES), lambda oi, b: (b, oi))]
    if ctx_out:
        out_shape.append(jax.ShapeDtypeStruct(uc.shape, F32))
        out_specs.append(pl.BlockSpec((lc, LANES), lambda oi, b: (b, oi)))
    res = pl.pallas_call(
        functools.partial(_ssm_kernel, nc=nc, nx=nx, seg_len=seg_len, ctx_out=ctx_out),
        grid=(o, batch),
        in_specs=[
            pl.BlockSpec((t, LANES), lambda oi, b: (b, oi)),
            pl.BlockSpec((lc, LANES), lambda oi, b: (b, oi)),
            pl.BlockSpec((1, lk, sdim), lambda oi, b: (oi, 0, 0)),
            pl.BlockSpec((1, lk + sdim, lk), lambda oi, b: (oi, 0, 0)),
            pl.BlockSpec((1, 8, sdim), lambda oi, b: (oi, 0, 0)),
            pl.BlockSpec((1, seg_len, sdim), lambda oi, b: (oi, 0, 0)),
        ],
        out_specs=out_specs,
        out_shape=out_shape,
        scratch_shapes=[
            pltpu.VMEM((nc + nx, lk + sdim), BF16),
            pltpu.VMEM((sdim // LANES, nc + nx, LANES), F32),
        ],
        compiler_params=_cparams("arbitrary", "arbitrary"),
        name="ssm_scan",
    )(ux, uc, ws, w3, apow, pw)
    return (res[0], res[1]) if ctx_out else (res[0], None)


def _dft_tables(t, gw):
    def tab(n, sign):
        k = jnp.arange(n, dtype=jnp.int32)
        ang = ((k[:, None] * k[None, :]) % n).astype(F32) * (2.0 * math.pi / n)
        return jnp.concatenate([jnp.cos(ang), sign * jnp.sin(ang)], axis=1).astype(BF16)
    return tab(t, -1.0), tab(gw, 1.0)


def _fft_kernel(f_ref, cs_ref, tab_ref, o_ref, data, *, t, gw, groups, scale):
    i = pl.program_id(1)

    @pl.when(i == 0)
    def _():
        rb = min(512, t)
        for r in range(t // rb):
            for g in range(groups):
                fg = f_ref[r * rb:(r + 1) * rb, g * gw:(g + 1) * gw]
                z = jnp.dot(fg, cs_ref[...], preferred_element_type=F32)
                data[r * rb:(r + 1) * rb, g * gw:(g + 1) * gw] = z[:, :gw].astype(BF16)
                data[t + r * rb:t + (r + 1) * rb, g * gw:(g + 1) * gw] = z[:, gw:].astype(BF16)

    y = jnp.dot(tab_ref[...], data[...], preferred_element_type=F32)
    o_ref[...] = (y * scale).astype(BF16)


def _fourier(p, col_tile, t, batch, tabs):
    tab_t, tab_c = tabs
    wf = 1024
    gw = wf // FFT_GROUPS
    tm = min(256, t)
    return pl.pallas_call(
        functools.partial(_fft_kernel, t=t, gw=gw, groups=FFT_GROUPS, scale=1.0 / math.sqrt(t * gw)),
        grid=(batch, t // tm),
        in_specs=[
            pl.BlockSpec((t, wf), lambda b, i: (b, col_tile)),
            pl.BlockSpec((gw, 2 * gw), lambda b, i: (0, 0)),
            pl.BlockSpec((tm, 2 * t), lambda b, i: (i, 0)),
        ],
        out_specs=pl.BlockSpec((tm, wf), lambda b, i: (b * (t // tm) + i, 0)),
        out_shape=jax.ShapeDtypeStruct((batch * t, wf), BF16),
        scratch_shapes=[pltpu.VMEM((2 * t, wf), BF16)],
        compiler_params=_cparams("arbitrary", "arbitrary"),
        name="fourier",
    )(p, tab_c, tab_t)


def _branch_kernel(xa_ref, ba_ref, ca_ref, za_ref, zb_ref, zc_ref, g0_ref, g1_ref, g2_ref,
                   u_ref, ys_ref, yf_ref, cw_ref, sd_ref, wa_ref, wb_ref, fw_ref,
                   pa_ref, pb_ref, pc_ref, m_ref, *, row_len):
    tm = xa_ref.shape[0]
    f = lambda r: r[...].astype(F32)
    v = f(ca_ref) * f(xa_ref)
    pos = lax.broadcasted_iota(jnp.int32, v.shape, 0) % row_len
    v_prev = jnp.where(pos == 0, 0.0, pltpu.roll(v, 1, 0))
    v_next = jnp.where(pos == row_len - 1, 0.0, pltpu.roll(v, tm - 1, 0))
    cw = cw_ref[0]
    conv = v_prev * cw[0:1, :] + v * cw[1:2, :] + v_next * cw[2:3, :]
    a = f(ba_ref) * conv * _silu(f(za_ref))
    ya = jnp.dot(a.astype(BF16), pa_ref[0], preferred_element_type=F32)
    acc = _sigmoid(f(g0_ref)) * ya
    y = _gelu_tanh(ys_ref[...] + sd_ref[0] * u_ref[...]).astype(BF16)
    glu = (jnp.dot(y, wa_ref[0], preferred_element_type=F32)
           * _sigmoid(jnp.dot(y, wb_ref[0], preferred_element_type=F32)) * _silu(f(zb_ref)))
    yb = jnp.dot(glu.astype(BF16), pb_ref[0], preferred_element_type=F32)
    acc = acc + _sigmoid(f(g1_ref)) * yb
    c = jnp.dot(yf_ref[...], fw_ref[0], preferred_element_type=F32) * _silu(f(zc_ref))
    yc = jnp.dot(c.astype(BF16), pc_ref[0], preferred_element_type=F32)
    acc = acc + _sigmoid(f(g2_ref)) * yc
    m_ref[...] = acc.astype(BF16)


def _branches(p, u, ys, yf, conv_w, ssm_d, wa, wb, fw, pa, pb, pc, layer, row_len):
    n = p.shape[0]
    wc = conv_w.shape[-1]
    d = pa.shape[-1]
    tm = 256
    tn = 1024
    col = lambda k: pl.BlockSpec((tm, tn), lambda i: (i, k))
    const = lambda shape: pl.BlockSpec(shape, lambda i: (layer,) + (0,) * (len(shape) - 1),
                                       pipeline_mode=pl.Buffered(1))
    g_tile0 = 8
    return pl.pallas_call(
        functools.partial(_branch_kernel, row_len=row_len),
        grid=(n // tm,),
        in_specs=[
            col(0), col(1), col(2), col(3), col(5), col(7),
            pl.BlockSpec((tm, d), lambda i: (i, g_tile0 * tn // d)),
            pl.BlockSpec((tm, d), lambda i: (i, g_tile0 * tn // d + 1)),
            pl.BlockSpec((tm, d), lambda i: (i, g_tile0 * tn // d + 2)),
            pl.BlockSpec((tm, tn), lambda i: (i, 0)),
            pl.BlockSpec((tm, tn), lambda i: (i, 0)),
            pl.BlockSpec((tm, tn), lambda i: (i, 0)),
            const((1, 3, wc)), const((1, 1, tn)),
            const((1, tn, tn)), const((1, tn, tn)), const((1, tn, tn)),
            const((1, wc, d)), const((1, tn, d)), const((1, tn, d)),
        ],
        out_specs=pl.BlockSpec((tm, d), lambda i: (i, 0)),
        out_shape=jax.ShapeDtypeStruct((n, d), BF16),
        compiler_params=_cparams("arbitrary"),
        name="branches",
    )(p, p, p, p, p, p, p, p, p, u, ys, yf, conv_w, ssm_d.reshape(ssm_d.shape[0], 1, -1),
      wa, wb, fw, pa, pb, pc)


def _out_kernel(m_ref, x_ref, gt_ref, g_ref, w_ref, o_ref):
    o = jnp.dot(m_ref[...], w_ref[0], preferred_element_type=F32)
    ms = jnp.mean(o * o, axis=-1, keepdims=True)
    y = o * lax.rsqrt(ms + RMS_EPS) * g_ref[0]
    o_ref[...] = x_ref[...] + gt_ref[0] * y


def _outproj(m, x2, gt, g_post, w_out_b, layer, rows_per_batch):
    n, d = x2.shape
    tm = min(512, rows_per_batch)
    tpb = rows_per_batch // tm
    return pl.pallas_call(
        _out_kernel,
        grid=(n // tm,),
        in_specs=[
            pl.BlockSpec((tm, d), lambda i: (i, 0)),
            pl.BlockSpec((tm, d), lambda i: (i, 0)),
            pl.BlockSpec((1, 1, d), lambda i: (i // tpb, 0, 0)),
            pl.BlockSpec((1, 1, d), lambda i: (layer, 0, 0)),
            pl.BlockSpec((1, d, d), lambda i: (layer, 0, 0), pipeline_mode=pl.Buffered(1)),
        ],
        out_specs=pl.BlockSpec((tm, d), lambda i: (i, 0)),
        out_shape=jax.ShapeDtypeStruct((n, d), F32),
        compiler_params=_cparams("arbitrary"),
        name="outproj",
    )(m, x2, gt, g_post.reshape(g_post.shape[0], 1, d), w_out_b)


def kernel(x, c, ctx, c_ctx, w_ada, b_ada, g_pre, g_post, w_in, conv_w, ssm_lam_re, ssm_lam_im, ssm_log_dt,
           ssm_b_re, ssm_b_im, ssm_c_re, ssm_c_im, ssm_d, glu_wa, glu_wb, fourier_w, proj_a, proj_b, proj_c,
           w_out):
    bn, t, d = x.shape
    lc = ctx.shape[1]
    depth = w_ada.shape[0]
    w_conv = conv_w.shape[-1]
    w_ssm = ssm_d.shape[-1]
    tn = 1024
    assert w_conv == tn and w_ssm == tn and fourier_w.shape[-1] == tn and d % tn == 0
    assert bn + 1 <= 8 and t % GRID_W == 0
    u_tile = 4 * w_conv // tn
    f_tile = u_tile + 2
    n_tiles = w_in.shape[-1] // tn
    seg_len = (t // SSM_CHUNK) // SSM_SEGS

    cc = jnp.concatenate([c, c_ctx[None, :], jnp.zeros((8 - bn - 1, d), F32)], axis=0)
    mods = _ada_mods(cc, w_ada, b_ada)

    to_b = lambda w: w.astype(BF16)
    w_in_b, wa_b, wb_b, fw_b = to_b(w_in), to_b(glu_wa), to_b(glu_wb), to_b(fourier_w)
    pa_b, pb_b, pc_b, wo_b = to_b(proj_a), to_b(proj_b), to_b(proj_c), to_b(w_out)
    tabs_x = _dft_tables(t, tn // FFT_GROUPS)
    tabs_c = _dft_tables(lc, tn // FFT_GROUPS)

    x2 = x.reshape(bn * t, d)
    c2 = ctx.reshape(bn * lc, d)
    for l in range(depth):
        last = l == depth - 1
        sh_x, sc_x, gt_x = (mods[l, :bn, k * d:(k + 1) * d].reshape(bn, 1, d) for k in range(3))
        sh_c, sc_c, gt_c = (jnp.broadcast_to(mods[l, bn, k * d:(k + 1) * d], (bn, 1, d)) for k in range(3))
        ws, w3, apow, pw = _ssm_weights(ssm_lam_re[l], ssm_lam_im[l], ssm_log_dt[l], ssm_b_re[l], ssm_b_im[l],
                                        ssm_c_re[l], ssm_c_im[l], seg_len)

        px, ux = _inproj(x2, sc_x, sh_x, g_pre, w_in_b, l, 0, n_tiles, u_tile, t)
        if last:
            pc_, uc = _inproj(c2, sc_c, sh_c, g_pre, w_in_b, l, u_tile, 1, 0, lc)
        else:
            pc_, uc = _inproj(c2, sc_c, sh_c, g_pre, w_in_b, l, 0, n_tiles, u_tile, lc)

        ys_x, ys_c = _ssm_scan(ux, uc, ws, w3, apow, pw, bn, ctx_out=not last)

        yf_x = _fourier(px, f_tile, t, bn, tabs_x)
        mx = _branches(px, ux, ys_x, yf_x, conv_w, ssm_d, wa_b, wb_b, fw_b, pa_b, pb_b, pc_b, l, GRID_W)
        new_x2 = _outproj(mx, x2, gt_x, g_post, wo_b, l, t)
        if not last:
            yf_c = _fourier(pc_, f_tile, lc, bn, tabs_c)
            mc = _branches(pc_, uc, ys_c, yf_c, conv_w, ssm_d, wa_b, wb_b, fw_b, pa_b, pb_b, pc_b, l, lc)
            c2 = _outproj(mc, c2, gt_c, g_post, wo_b, l, lc)
        x2 = new_x2
    return x2.reshape(bn, t, d)
```

```python
import functools
import math

import jax
import jax.numpy as jnp
from jax import lax
from jax.experimental import pallas as pl
from jax.experimental.pallas import tpu as pltpu

F32 = jnp.float32
BF16 = jnp.bfloat16

GRID_W = 64
FFT_GROUPS = 4
N_BRANCH = 3
RMS_EPS = 1e-6
LANES = 128
SSM_CHUNK = 8
SSM_SEGS = 8
DFT_ROWS = 64
VMEM_LIMIT = 56 * 1024 * 1024


def _cparams(*sem):
    return pltpu.CompilerParams(dimension_semantics=sem, vmem_limit_bytes=VMEM_LIMIT)


def _sigmoid(v):
    return 1.0 / (1.0 + jnp.exp(-v))


def _silu(v):
    return v * _sigmoid(v)


def _gelu_tanh(v):
    return v * (0.5 * (1.0 + jnp.tanh(math.sqrt(2.0 / math.pi) * (v + 0.044715 * (v * v * v)))))


def _cmul(ar, ai, br, bi):
    return ar * br - ai * bi, ar * bi + ai * br


def _cmul_add(ar, ai, zr, zi, sr, si):
    return ar * zr - ai * zi + sr, ar * zi + ai * zr + si


def _ada_kernel(c_ref, w_ref, b_ref, o_ref):
    c = c_ref[...]
    s = _silu(c).astype(BF16)
    o_ref[0] = jnp.dot(s, w_ref[0].astype(BF16), preferred_element_type=F32) + b_ref[0]


def _ada_mods(cc, w_ada, b_ada):
    depth, d, n3 = w_ada.shape
    tn = min(1024, n3)
    return pl.pallas_call(
        _ada_kernel,
        grid=(depth, n3 // tn),
        in_specs=[
            pl.BlockSpec((8, d), lambda l, j: (0, 0)),
            pl.BlockSpec((1, d, tn), lambda l, j: (l, 0, j)),
            pl.BlockSpec((1, 1, tn), lambda l, j: (l, 0, j)),
        ],
        out_specs=pl.BlockSpec((1, 8, tn), lambda l, j: (l, 0, j)),
        out_shape=jax.ShapeDtypeStruct((depth, 8, n3), F32),
        compiler_params=_cparams("arbitrary", "arbitrary"),
        name="ada_mods",
    )(cc, w_ada, b_ada.reshape(depth, 1, n3))


def _inproj_kernel(x_ref, sc_ref, sh_ref, g_ref, w_ref, p_ref, u_ref, h_scr, *, u_tile):
    j = pl.program_id(1)

    @pl.when(j == 0)
    def _():
        x = x_ref[...]
        ms = jnp.mean(x * x, axis=-1, keepdims=True)
        y = x * lax.rsqrt(ms + RMS_EPS) * g_ref[0]
        h_scr[...] = (y * (1.0 + sc_ref[0]) + sh_ref[0]).astype(BF16)

    acc = jnp.dot(h_scr[...], w_ref[0], preferred_element_type=F32)
    p_ref[...] = acc.astype(BF16)

    @pl.when(j == u_tile)
    def _():
        u_ref[...] = acc


def _inproj(x2, sc, sh, g_pre, w_b, layer, tile0, ntiles, u_tile, rows_per_batch):
    n, d = x2.shape
    tn = 1024
    tm = min(1024, rows_per_batch)
    tpb = rows_per_batch // tm
    return pl.pallas_call(
        functools.partial(_inproj_kernel, u_tile=u_tile),
        grid=(n // tm, ntiles),
        in_specs=[
            pl.BlockSpec((tm, d), lambda i, j: (i, 0)),
            pl.BlockSpec((1, 1, d), lambda i, j: (i // tpb, 0, 0)),
            pl.BlockSpec((1, 1, d), lambda i, j: (i // tpb, 0, 0)),
            pl.BlockSpec((1, 1, d), lambda i, j: (layer, 0, 0)),
            pl.BlockSpec((1, d, tn), lambda i, j: (layer, 0, tile0 + j)),
        ],
        out_specs=[
            pl.BlockSpec((tm, tn), lambda i, j: (i, j)),
            pl.BlockSpec((tm, tn), lambda i, j: (i, 0)),
        ],
        out_shape=[
            jax.ShapeDtypeStruct((n, ntiles * tn), BF16),
            jax.ShapeDtypeStruct((n, tn), F32),
        ],
        scratch_shapes=[pltpu.VMEM((tm, d), BF16)],
        compiler_params=_cparams("arbitrary", "arbitrary"),
        name="inproj",
    )(x2, sc, sh, g_pre.reshape(g_pre.shape[0], 1, d), w_b)


def _ssm_params(lam_re, lam_im, log_dt, b_re, b_im, c_re, c_im, seg_len):
    L = SSM_CHUNK
    _, G, P = lam_re.shape
    H = b_re.shape[-1]
    gpt = LANES // H
    O = G // gpt
    lam_re = lam_re.astype(F32)
    lam_im = lam_im.astype(F32)
    dt = jnp.exp(log_dt.astype(F32))[..., None]
    lr = lam_re * dt
    li = lam_im * dt
    mag = jnp.exp(lr)
    a_re = mag * jnp.cos(li)
    a_im = mag * jnp.sin(li)
    n_re = a_re - 1.0
    n_im = a_im
    den = lam_re * lam_re + lam_im * lam_im
    q_re = (n_re * lam_re + n_im * lam_im) / den
    q_im = (n_im * lam_re - n_re * lam_im) / den
    bb_re = q_re[..., None] * b_re - q_im[..., None] * b_im
    bb_im = q_re[..., None] * b_im + q_im[..., None] * b_re

    bbs = jnp.stack([bb_re, bb_im], axis=1).reshape(2, 2, O, gpt, P, H)
    bc = jnp.transpose(bbs, (2, 0, 1, 5, 3, 4)).reshape(O, 4, H, gpt * P)
    cs = jnp.stack([c_re.astype(F32), c_im.astype(F32)], axis=1).reshape(2, 2, O, gpt, H, P)
    cct = jnp.transpose(cs, (2, 0, 1, 3, 5, 4)).reshape(O, 4, gpt * P, 1, H)
    cct = jnp.broadcast_to(cct, (O, 4, gpt * P, gpt, H)).reshape(O, 4, gpt * P, LANES)
    a4 = jnp.transpose(jnp.stack([a_re, a_im], axis=1).reshape(2, 2, O, gpt * P), (2, 0, 1, 3)).reshape(O, 4, gpt * P)
    arow = jnp.concatenate([a4, jnp.zeros_like(a4)], axis=1)
    acol = jnp.broadcast_to(a4[..., None], (O, 4, gpt * P, LANES))

    def apower(k):
        k = jnp.asarray(k, F32).reshape((-1, 1, 1, 1))
        m = jnp.exp(k * lr)
        return m * jnp.cos(k * li), m * jnp.sin(k * li)

    def cols(re, im):
        k = re.shape[0]
        z = jnp.stack([re, im], axis=2).reshape(k, 2, 2, O, gpt * P)
        return jnp.transpose(z, (3, 0, 1, 2, 4)).reshape(O, k, 4 * gpt * P)

    ap = cols(*apower(jnp.array([L, L * seg_len])))
    apow = jnp.concatenate([ap, jnp.zeros((O, 6, ap.shape[-1]), F32)], axis=1)
    ii = jnp.arange(seg_len)
    pf_re, pf_im = apower(L * ii)
    pb_re, pb_im = apower(L * (seg_len - 1 - ii))
    pw = cols(jnp.stack([pf_re[:, 0], pb_re[:, 1]], axis=1), jnp.stack([pf_im[:, 0], pb_im[:, 1]], axis=1))
    return bc, cct, arow, acol, apow, pw


def _ssm_build_operators(bc_ref, cct_ref, arow_ref, acol_ref, ws, w3):
    L = SSM_CHUNK
    lk = L * LANES
    hh = bc_ref.shape[2]
    sw = bc_ref.shape[3]
    gpt = LANES // hh
    pp = sw // gpt
    hp = lax.Precision.HIGHEST

    same_b = (lax.broadcasted_iota(jnp.int32, (LANES, sw), 0) // hh
              == lax.broadcasted_iota(jnp.int32, (LANES, sw), 1) // pp)
    same_c = (lax.broadcasted_iota(jnp.int32, (sw, LANES), 0) // pp
              == lax.broadcasted_iota(jnp.int32, (sw, LANES), 1) // hh)

    def powers(re, im, n):
        out = [(jnp.ones_like(re), jnp.zeros_like(re))]
        for _ in range(n):
            out.append(_cmul(out[-1][0], out[-1][1], re, im))
        return out

    taps = []
    for d in range(2):
        b_re = jnp.where(same_b, jnp.concatenate([bc_ref[0, 2 * d]] * gpt, axis=0), 0.0)
        b_im = jnp.where(same_b, jnp.concatenate([bc_ref[0, 2 * d + 1]] * gpt, axis=0), 0.0)
        c_re = jnp.where(same_c, cct_ref[0, 2 * d], 0.0)
        c_im = jnp.where(same_c, cct_ref[0, 2 * d + 1], 0.0)
        prow = powers(arow_ref[0, 2 * d:2 * d + 1, :], arow_ref[0, 2 * d + 1:2 * d + 2, :], L - 1)
        pcol = powers(acol_ref[0, 2 * d], acol_ref[0, 2 * d + 1], L)
        tap_d = []
        for k in range(L):
            ab_re, ab_im = _cmul(b_re, b_im, prow[k][0], prow[k][1])
            j = L - 1 - k if d == 0 else k
            ws[j * LANES:(j + 1) * LANES, (2 * d) * sw:(2 * d + 1) * sw] = ab_re.astype(BF16)
            ws[j * LANES:(j + 1) * LANES, (2 * d + 1) * sw:(2 * d + 2) * sw] = ab_im.astype(BF16)
            tap_d.append(jnp.dot(ab_re, c_re, precision=hp, preferred_element_type=F32)
                         - jnp.dot(ab_im, c_im, precision=hp, preferred_element_type=F32))
        taps.append(tap_d)
        for jo in range(L):
            e = jo + 1 if d == 0 else L - jo
            ca_re, ca_im = _cmul(c_re, c_im, pcol[e][0], pcol[e][1])
            r0 = lk + (2 * d) * sw
            w3[r0:r0 + sw, jo * LANES:(jo + 1) * LANES] = ca_re.astype(BF16)
            w3[r0 + sw:r0 + 2 * sw, jo * LANES:(jo + 1) * LANES] = (-ca_im).astype(BF16)
    for j in range(L):
        for jo in range(L):
            lag = jo - j
            t = taps[0][lag] if lag > 0 else taps[1][-lag] if lag < 0 else taps[0][0] + taps[1][0]
            w3[j * LANES:(j + 1) * LANES, jo * LANES:(jo + 1) * LANES] = t.astype(BF16)


def _ssm_kernel(*refs, nc, nx, seg_len, ctx_out):
    if ctx_out:
        (ux_ref, uc_ref, bc_ref, cct_ref, arow_ref, acol_ref, apow_ref, pw_ref,
         yx_ref, yc_ref, ws, w3, lhs, st) = refs
    else:
        (ux_ref, uc_ref, bc_ref, cct_ref, arow_ref, acol_ref, apow_ref, pw_ref,
         yx_ref, ws, w3, lhs, st) = refs
        yc_ref = None
    L = SSM_CHUNK
    lk = L * LANES
    nlt = st.shape[0] // 4
    sw = nlt * LANES
    FR, FI, BR, BI = range(4)

    @pl.when(pl.program_id(1) == 0)
    def _():
        _ssm_build_operators(bc_ref, cct_ref, arow_ref, acol_ref, ws, w3)

    def ld(part, rows):
        return jnp.concatenate([st[part * nlt + q, rows, :] for q in range(nlt)], axis=1)

    def sto(part, rows, val):
        for q in range(nlt):
            st[part * nlt + q, rows, :] = val[:, q * LANES:(q + 1) * LANES]

    def part_cols(part):
        return slice(part * sw, (part + 1) * sw)

    for j in range(L):
        lhs[0:nc, j * LANES:(j + 1) * LANES] = uc_ref[pl.ds(j, nc, stride=L), :].astype(BF16)
        lhs[nc:nc + nx, j * LANES:(j + 1) * LANES] = ux_ref[pl.ds(j, nx, stride=L), :].astype(BF16)

    rb = 128
    blocks = [(0, nc)] + [(nc + r * rb, rb) for r in range(nx // rb)]
    for r0, nr in blocks:
        s = jnp.dot(lhs[r0:r0 + nr, 0:lk], ws[...], preferred_element_type=F32)
        for k in range(4 * nlt):
            st[k, r0:r0 + nr, :] = s[:, k * LANES:(k + 1) * LANES]

    a_l = apow_ref[0, 0:1, :]
    a_seg = apow_ref[0, 1:2, :]
    al = [a_l[:, part_cols(p)] for p in range(4)]
    aseg = [a_seg[:, part_cols(p)] for p in range(4)]

    zero = jnp.zeros((1, sw), F32)

    def ctx_step(pr, pi):
        def step(i, carry):
            er, ei = carry
            row = pl.ds(i, 1)
            sr, si = ld(pr, row), ld(pi, row)
            sto(pr, row, er)
            sto(pi, row, ei)
            return _cmul_add(al[pr], al[pi], er, ei, sr, si)
        return step

    h0f = lax.fori_loop(0, nc, ctx_step(FR, FI), (zero, zero))
    bwd_step = ctx_step(BR, BI)
    h0b = lax.fori_loop(0, nc, lambda k, carry: bwd_step(nc - 1 - k, carry), (zero, zero))
    for k in range(4 * nlt):
        lhs[0:nc, lk + k * LANES:lk + (k + 1) * LANES] = st[k, 0:nc, :].astype(BF16)

    nseg = SSM_SEGS
    zseg = jnp.zeros((nseg, sw), F32)

    def seg_pass(pr, pi, order):
        zr, zi = zseg, zseg
        for i in order:
            rows = pl.ds(nc + i, nseg, stride=seg_len)
            sr, si = ld(pr, rows), ld(pi, rows)
            sto(pr, rows, zr)
            sto(pi, rows, zi)
            zr, zi = _cmul_add(al[pr], al[pi], zr, zi, sr, si)
        return zr, zi

    zfr, zfi = seg_pass(FR, FI, range(seg_len))
    zbr, zbi = seg_pass(BR, BI, range(seg_len - 1, -1, -1))

    ef = [h0f]
    for s in range(nseg - 1):
        ef.append(_cmul_add(aseg[FR], aseg[FI], ef[s][0], ef[s][1], zfr[s:s + 1], zfi[s:s + 1]))
    eb = [None] * nseg
    eb[nseg - 1] = h0b
    for s in range(nseg - 1, 0, -1):
        eb[s - 1] = _cmul_add(aseg[BR], aseg[BI], eb[s][0], eb[s][1], zbr[s:s + 1], zbi[s:s + 1])
    for s in range(nseg):
        r0 = nc + s * seg_len
        rows = slice(r0, r0 + seg_len)
        for (pr, pi, e) in ((FR, FI, ef[s]), (BR, BI, eb[s])):
            xr, xi = _cmul_add(pw_ref[0, :, part_cols(pr)], pw_ref[0, :, part_cols(pi)], e[0], e[1],
                               ld(pr, rows), ld(pi, rows))
            lhs[rows, lk + pr * sw:lk + (pr + 1) * sw] = xr.astype(BF16)
            lhs[rows, lk + pi * sw:lk + (pi + 1) * sw] = xi.astype(BF16)

    if ctx_out:
        yc = jnp.dot(lhs[0:nc, :], w3[...], preferred_element_type=F32)
        for j in range(L):
            yc_ref[pl.ds(j, nc, stride=L), :] = yc[:, j * LANES:(j + 1) * LANES]
    for r in range(nx // rb):
        r0 = nc + r * rb
        y = jnp.dot(lhs[r0:r0 + rb, :], w3[...], preferred_element_type=F32)
        for j in range(L):
            yx_ref[pl.ds(r * rb * L + j, rb, stride=L), :] = y[:, j * LANES:(j + 1) * LANES]


def _ssm_scan(ux, uc, params, batch, ctx_out):
    bc, cct, arow, acol, apow, pw = params
    L = SSM_CHUNK
    n_x, w = ux.shape
    t = n_x // batch
    lc = uc.shape[0] // batch
    nx, nc = t // L, lc // L
    seg_len = nx // SSM_SEGS
    o = w // LANES
    lk = L * LANES
    sw = bc.shape[-1]
    sdim = 4 * sw
    out_shape = [jax.ShapeDtypeStruct(ux.shape, F32)]
    out_specs = [pl.BlockSpec((t, LANES), lambda oi, b: (b, oi))]
    if ctx_out:
        out_shape.append(jax.ShapeDtypeStruct(uc.shape, F32))
        out_specs.append(pl.BlockSpec((lc, LANES), lambda oi, b: (b, oi)))
    tile = lambda a: pl.BlockSpec((1,) + a.shape[1:], lambda oi, b: (oi,) + (0,) * (a.ndim - 1))
    res = pl.pallas_call(
        functools.partial(_ssm_kernel, nc=nc, nx=nx, seg_len=seg_len, ctx_out=ctx_out),
        grid=(o, batch),
        in_specs=[
            pl.BlockSpec((t, LANES), lambda oi, b: (b, oi)),
            pl.BlockSpec((lc, LANES), lambda oi, b: (b, oi)),
            tile(bc), tile(cct), tile(arow), tile(acol), tile(apow), tile(pw),
        ],
        out_specs=out_specs,
        out_shape=out_shape,
        scratch_shapes=[
            pltpu.VMEM((lk, sdim), BF16),
            pltpu.VMEM((lk + sdim, lk), BF16),
            pltpu.VMEM((nc + nx, lk + sdim), BF16),
            pltpu.VMEM((sdim // LANES, nc + nx, LANES), F32),
        ],
        compiler_params=_cparams("arbitrary", "arbitrary"),
        name="ssm_scan",
    )(ux, uc, bc, cct, arow, acol, apow, pw)
    return (res[0], res[1]) if ctx_out else (res[0], None)


def _dft_table_kernel(e1_ref, e2_ref, o_ref, *, t):
    e1c, e1s = e1_ref[0, :, 0:t], e1_ref[0, :, t:2 * t]
    e2c, e2s = e2_ref[:, 0:t], e2_ref[:, t:2 * t]
    c, s = _cmul(e2c, e2s, e1c, e1s)
    o_ref[:, 0:t] = c.astype(BF16)
    o_ref[:, t:2 * t] = (-s).astype(BF16)


def _dft_tables(t, gw):
    def cis(rows_mult, nrows):
        k = jnp.arange(nrows, dtype=jnp.int32)[:, None] * rows_mult
        n = jnp.arange(t, dtype=jnp.int32)[None, :]
        ang = ((k * n) % t).astype(F32) * (2.0 * math.pi / t)
        return jnp.concatenate([jnp.cos(ang), jnp.sin(ang)], axis=1)
    r = DFT_ROWS
    e1 = cis(r, t // r).reshape(t // r, 1, 2 * t)
    e2 = cis(1, r)
    tab_t = pl.pallas_call(
        functools.partial(_dft_table_kernel, t=t),
        grid=(t // r,),
        in_specs=[
            pl.BlockSpec((1, 1, 2 * t), lambda i: (i, 0, 0)),
            pl.BlockSpec((r, 2 * t), lambda i: (0, 0)),
        ],
        out_specs=pl.BlockSpec((r, 2 * t), lambda i: (i, 0)),
        out_shape=jax.ShapeDtypeStruct((t, 2 * t), BF16),
        compiler_params=_cparams("arbitrary"),
        name="dft_table",
    )(e1, e2)
    kc = jnp.arange(gw, dtype=jnp.int32)
    ang = ((kc[:, None] * kc[None, :]) % gw).astype(F32) * (2.0 * math.pi / gw)
    tab_c = jnp.concatenate([jnp.cos(ang), jnp.sin(ang)], axis=1).astype(BF16)
    return tab_t, tab_c


def _fft_kernel(f_ref, cs_ref, tab_ref, o_ref, data, *, t, gw, groups, scale):
    i = pl.program_id(1)

    @pl.when(i == 0)
    def _():
        rb = min(512, t)
        for r in range(t // rb):
            for g in range(groups):
                fg = f_ref[r * rb:(r + 1) * rb, g * gw:(g + 1) * gw]
                z = jnp.dot(fg, cs_ref[...], preferred_element_type=F32)
                data[r * rb:(r + 1) * rb, g * gw:(g + 1) * gw] = z[:, :gw].astype(BF16)
                data[t + r * rb:t + (r + 1) * rb, g * gw:(g + 1) * gw] = z[:, gw:].astype(BF16)

    y = jnp.dot(tab_ref[...], data[...], preferred_element_type=F32)
    o_ref[...] = (y * scale).astype(BF16)


def _fourier(p, col_tile, t, batch, tabs):
    tab_t, tab_c = tabs
    wf = 1024
    gw = wf // FFT_GROUPS
    tm = min(256, t)
    return pl.pallas_call(
        functools.partial(_fft_kernel, t=t, gw=gw, groups=FFT_GROUPS, scale=1.0 / math.sqrt(t * gw)),
        grid=(batch, t // tm),
        in_specs=[
            pl.BlockSpec((t, wf), lambda b, i: (b, col_tile)),
            pl.BlockSpec((gw, 2 * gw), lambda b, i: (0, 0)),
            pl.BlockSpec((tm, 2 * t), lambda b, i: (i, 0)),
        ],
        out_specs=pl.BlockSpec((tm, wf), lambda b, i: (b * (t // tm) + i, 0)),
        out_shape=jax.ShapeDtypeStruct((batch * t, wf), BF16),
        scratch_shapes=[pltpu.VMEM((2 * t, wf), BF16)],
        compiler_params=_cparams("arbitrary", "arbitrary"),
        name="fourier",
    )(p, tab_c, tab_t)


def _branch_kernel(xa_ref, ba_ref, ca_ref, za_ref, zb_ref, zc_ref, g0_ref, g1_ref, g2_ref,
                   u_ref, ys_ref, yf_ref, cw_ref, sd_ref, wa_ref, wb_ref, fw_ref,
                   pa_ref, pb_ref, pc_ref, m_ref, *, row_len):
    tm = xa_ref.shape[0]
    f = lambda r: r[...].astype(F32)
    v = f(ca_ref) * f(xa_ref)
    pos = lax.broadcasted_iota(jnp.int32, v.shape, 0) % row_len
    v_prev = jnp.where(pos == 0, 0.0, pltpu.roll(v, 1, 0))
    v_next = jnp.where(pos == row_len - 1, 0.0, pltpu.roll(v, tm - 1, 0))
    cw = cw_ref[0]
    conv = v_prev * cw[0:1, :] + v * cw[1:2, :] + v_next * cw[2:3, :]
    a = f(ba_ref) * conv * _silu(f(za_ref))
    ya = jnp.dot(a.astype(BF16), pa_ref[0], preferred_element_type=F32)
    acc = _sigmoid(f(g0_ref)) * ya
    y = _gelu_tanh(ys_ref[...] + sd_ref[0] * u_ref[...]).astype(BF16)
    glu = (jnp.dot(y, wa_ref[0], preferred_element_type=F32)
           * _sigmoid(jnp.dot(y, wb_ref[0], preferred_element_type=F32)) * _silu(f(zb_ref)))
    yb = jnp.dot(glu.astype(BF16), pb_ref[0], preferred_element_type=F32)
    acc = acc + _sigmoid(f(g1_ref)) * yb
    c = jnp.dot(yf_ref[...], fw_ref[0], preferred_element_type=F32) * _silu(f(zc_ref))
    yc = jnp.dot(c.astype(BF16), pc_ref[0], preferred_element_type=F32)
    acc = acc + _sigmoid(f(g2_ref)) * yc
    m_ref[...] = acc.astype(BF16)


def _branches(p, u, ys, yf, conv_w, ssm_d, wa, wb, fw, pa, pb, pc, layer, row_len):
    n = p.shape[0]
    wc = conv_w.shape[-1]
    d = pa.shape[-1]
    tm = 256
    tn = 1024
    col = lambda k: pl.BlockSpec((tm, tn), lambda i: (i, k))
    const = lambda shape: pl.BlockSpec(shape, lambda i: (layer,) + (0,) * (len(shape) - 1),
                                       pipeline_mode=pl.Buffered(1))
    g_tile0 = 8
    return pl.pallas_call(
        functools.partial(_branch_kernel, row_len=row_len),
        grid=(n // tm,),
        in_specs=[
            col(0), col(1), col(2), col(3), col(5), col(7),
            pl.BlockSpec((tm, d), lambda i: (i, g_tile0 * tn // d)),
            pl.BlockSpec((tm, d), lambda i: (i, g_tile0 * tn // d + 1)),
            pl.BlockSpec((tm, d), lambda i: (i, g_tile0 * tn // d + 2)),
            pl.BlockSpec((tm, tn), lambda i: (i, 0)),
            pl.BlockSpec((tm, tn), lambda i: (i, 0)),
            pl.BlockSpec((tm, tn), lambda i: (i, 0)),
            const((1, 3, wc)), const((1, 1, tn)),
            const((1, tn, tn)), const((1, tn, tn)), const((1, tn, tn)),
            const((1, wc, d)), const((1, tn, d)), const((1, tn, d)),
        ],
        out_specs=pl.BlockSpec((tm, d), lambda i: (i, 0)),
        out_shape=jax.ShapeDtypeStruct((n, d), BF16),
        compiler_params=_cparams("arbitrary"),
        name="branches",
    )(p, p, p, p, p, p, p, p, p, u, ys, yf, conv_w, ssm_d.reshape(ssm_d.shape[0], 1, -1),
      wa, wb, fw, pa, pb, pc)


def _out_kernel(m_ref, x_ref, gt_ref, g_ref, w_ref, o_ref):
    o = jnp.dot(m_ref[...], w_ref[0], preferred_element_type=F32)
    ms = jnp.mean(o * o, axis=-1, keepdims=True)
    y = o * lax.rsqrt(ms + RMS_EPS) * g_ref[0]
    o_ref[...] = x_ref[...] + gt_ref[0] * y


def _outproj(m, x2, gt, g_post, w_out_b, layer, rows_per_batch):
    n, d = x2.shape
    tm = min(512, rows_per_batch)
    tpb = rows_per_batch // tm
    return pl.pallas_call(
        _out_kernel,
        grid=(n // tm,),
        in_specs=[
            pl.BlockSpec((tm, d), lambda i: (i, 0)),
            pl.BlockSpec((tm, d), lambda i: (i, 0)),
            pl.BlockSpec((1, 1, d), lambda i: (i // tpb, 0, 0)),
            pl.BlockSpec((1, 1, d), lambda i: (layer, 0, 0)),
            pl.BlockSpec((1, d, d), lambda i: (layer, 0, 0), pipeline_mode=pl.Buffered(1)),
        ],
        out_specs=pl.BlockSpec((tm, d), lambda i: (i, 0)),
        out_shape=jax.ShapeDtypeStruct((n, d), F32),
        compiler_params=_cparams("arbitrary"),
        name="outproj",
    )(m, x2, gt, g_post.reshape(g_post.shape[0], 1, d), w_out_b)


def kernel(x, c, ctx, c_ctx, w_ada, b_ada, g_pre, g_post, w_in, conv_w, ssm_lam_re, ssm_lam_im, ssm_log_dt,
           ssm_b_re, ssm_b_im, ssm_c_re, ssm_c_im, ssm_d, glu_wa, glu_wb, fourier_w, proj_a, proj_b, proj_c,
           w_out):
    bn, t, d = x.shape
    lc = ctx.shape[1]
    depth = w_ada.shape[0]
    w_conv = conv_w.shape[-1]
    w_ssm = ssm_d.shape[-1]
    tn = 1024
    assert w_conv == tn and w_ssm == tn and fourier_w.shape[-1] == tn and d % tn == 0
    assert bn + 1 <= 8 and t % GRID_W == 0
    u_tile = 4 * w_conv // tn
    f_tile = u_tile + 2
    n_tiles = w_in.shape[-1] // tn
    seg_len = (t // SSM_CHUNK) // SSM_SEGS

    cc = jnp.concatenate([c, c_ctx[None, :], jnp.zeros((8 - bn - 1, d), F32)], axis=0)
    mods = _ada_mods(cc, w_ada, b_ada)

    to_b = lambda w: w.astype(BF16)
    w_in_b, wa_b, wb_b, fw_b = to_b(w_in), to_b(glu_wa), to_b(glu_wb), to_b(fourier_w)
    pa_b, pb_b, pc_b, wo_b = to_b(proj_a), to_b(proj_b), to_b(proj_c), to_b(w_out)
    tabs_x = _dft_tables(t, tn // FFT_GROUPS)
    tabs_c = _dft_tables(lc, tn // FFT_GROUPS)

    x2 = x.reshape(bn * t, d)
    c2 = ctx.reshape(bn * lc, d)
    for l in range(depth):
        last = l == depth - 1
        sh_x, sc_x, gt_x = (mods[l, :bn, k * d:(k + 1) * d].reshape(bn, 1, d) for k in range(3))
        sh_c, sc_c, gt_c = (jnp.broadcast_to(mods[l, bn, k * d:(k + 1) * d], (bn, 1, d)) for k in range(3))
        ssm_p = _ssm_params(ssm_lam_re[l], ssm_lam_im[l], ssm_log_dt[l], ssm_b_re[l], ssm_b_im[l],
                            ssm_c_re[l], ssm_c_im[l], seg_len)

        px, ux = _inproj(x2, sc_x, sh_x, g_pre, w_in_b, l, 0, n_tiles, u_tile, t)
        if last:
            pc_, uc = _inproj(c2, sc_c, sh_c, g_pre, w_in_b, l, u_tile, 1, 0, lc)
        else:
            pc_, uc = _inproj(c2, sc_c, sh_c, g_pre, w_in_b, l, 0, n_tiles, u_tile, lc)

        ys_x, ys_c = _ssm_scan(ux, uc, ssm_p, bn, ctx_out=not last)

        yf_x = _fourier(px, f_tile, t, bn, tabs_x)
        mx = _branches(px, ux, ys_x, yf_x, conv_w, ssm_d, wa_b, wb_b, fw_b, pa_b, pb_b, pc_b, l, GRID_W)
        new_x2 = _outproj(mx, x2, gt_x, g_post, wo_b, l, t)
        if not last:
            yf_c = _fourier(pc_, f_tile, lc, bn, tabs_c)
            mc = _branches(pc_, uc, ys_c, yf_c, conv_w, ssm_d, wa_b, wb_b, fw_b, pa_b, pb_b, pc_b, l, lc)
            c2 = _outproj(mc, c2, gt_c, g_post, wo_b, l, lc)
        x2 = new_x2
    return x2.reshape(bn, t, d)
```

```python
import functools
import math

import jax
import jax.numpy as jnp
from jax import lax
from jax.experimental import pallas as pl
from jax.experimental.pallas import tpu as pltpu

F32 = jnp.float32
BF16 = jnp.bfloat16

GRID_W = 64
FFT_GROUPS = 4
N_BRANCH = 3
RMS_EPS = 1e-6
LANES = 128
SSM_CHUNK = 8
SSM_SEGS = 8
DFT_ROWS = 64
VMEM_LIMIT = 56 * 1024 * 1024


def _cparams(*sem):
    return pltpu.CompilerParams(dimension_semantics=sem, vmem_limit_bytes=VMEM_LIMIT)


def _sigmoid(v):
    return 1.0 / (1.0 + jnp.exp(-v))


def _silu(v):
    return v * _sigmoid(v)


def _gelu_tanh(v):
    return v * (0.5 * (1.0 + jnp.tanh(math.sqrt(2.0 / math.pi) * (v + 0.044715 * (v * v * v)))))


def _cmul(ar, ai, br, bi):
    return ar * br - ai * bi, ar * bi + ai * br


def _cmul_add(ar, ai, zr, zi, sr, si):
    return ar * zr - ai * zi + sr, ar * zi + ai * zr + si


def _ada_kernel(c_ref, w_ref, b_ref, o_ref):
    c = c_ref[...]
    s = _silu(c).astype(BF16)
    o_ref[0] = jnp.dot(s, w_ref[0].astype(BF16), preferred_element_type=F32) + b_ref[0]


def _ada_mods(cc, w_ada, b_ada):
    depth, d, n3 = w_ada.shape
    tn = min(1024, n3)
    return pl.pallas_call(
        _ada_kernel,
        grid=(depth, n3 // tn),
        in_specs=[
            pl.BlockSpec((8, d), lambda l, j: (0, 0)),
            pl.BlockSpec((1, d, tn), lambda l, j: (l, 0, j)),
            pl.BlockSpec((1, 1, tn), lambda l, j: (l, 0, j)),
        ],
        out_specs=pl.BlockSpec((1, 8, tn), lambda l, j: (l, 0, j)),
        out_shape=jax.ShapeDtypeStruct((depth, 8, n3), F32),
        compiler_params=_cparams("arbitrary", "arbitrary"),
        name="ada_mods",
    )(cc, w_ada, b_ada.reshape(depth, 1, n3))


def _inproj_kernel(x_ref, sc_ref, sh_ref, g_ref, w_ref, p_ref, u_ref, h_scr, *, u_tile):
    j = pl.program_id(1)

    @pl.when(j == 0)
    def _():
        x = x_ref[...]
        ms = jnp.mean(x * x, axis=-1, keepdims=True)
        y = x * lax.rsqrt(ms + RMS_EPS) * g_ref[0]
        h_scr[...] = (y * (1.0 + sc_ref[0]) + sh_ref[0]).astype(BF16)

    acc = jnp.dot(h_scr[...], w_ref[0], preferred_element_type=F32)
    p_ref[...] = acc.astype(BF16)

    @pl.when(j == u_tile)
    def _():
        u_ref[...] = acc


def _inproj(x2, sc, sh, g_pre, w_b, layer, tile0, ntiles, u_tile, rows_per_batch):
    n, d = x2.shape
    tn = 1024
    tm = min(1024, rows_per_batch)
    tpb = rows_per_batch // tm
    return pl.pallas_call(
        functools.partial(_inproj_kernel, u_tile=u_tile),
        grid=(n // tm, ntiles),
        in_specs=[
            pl.BlockSpec((tm, d), lambda i, j: (i, 0)),
            pl.BlockSpec((1, 1, d), lambda i, j: (i // tpb, 0, 0)),
            pl.BlockSpec((1, 1, d), lambda i, j: (i // tpb, 0, 0)),
            pl.BlockSpec((1, 1, d), lambda i, j: (layer, 0, 0)),
            pl.BlockSpec((1, d, tn), lambda i, j: (layer, 0, tile0 + j)),
        ],
        out_specs=[
            pl.BlockSpec((tm, tn), lambda i, j: (i, j)),
            pl.BlockSpec((tm, tn), lambda i, j: (i, 0)),
        ],
        out_shape=[
            jax.ShapeDtypeStruct((n, ntiles * tn), BF16),
            jax.ShapeDtypeStruct((n, tn), F32),
        ],
        scratch_shapes=[pltpu.VMEM((tm, d), BF16)],
        compiler_params=_cparams("arbitrary", "arbitrary"),
        name="inproj",
    )(x2, sc, sh, g_pre.reshape(g_pre.shape[0], 1, d), w_b)


def _ssm_params(lam_re, lam_im, log_dt, b_re, b_im, c_re, c_im, seg_len):
    L = SSM_CHUNK
    _, G, P = lam_re.shape
    H = b_re.shape[-1]
    gpt = LANES // H
    O = G // gpt
    lam_re = lam_re.astype(F32)
    lam_im = lam_im.astype(F32)
    dt = jnp.exp(log_dt.astype(F32))[..., None]
    lr = lam_re * dt
    li = lam_im * dt
    mag = jnp.exp(lr)
    a_re = mag * jnp.cos(li)
    a_im = mag * jnp.sin(li)
    n_re = a_re - 1.0
    n_im = a_im
    den = lam_re * lam_re + lam_im * lam_im
    q_re = (n_re * lam_re + n_im * lam_im) / den
    q_im = (n_im * lam_re - n_re * lam_im) / den
    bb_re = q_re[..., None] * b_re - q_im[..., None] * b_im
    bb_im = q_re[..., None] * b_im + q_im[..., None] * b_re

    bbs = jnp.stack([bb_re, bb_im], axis=1).reshape(2, 2, O, gpt, P, H)
    bc = jnp.transpose(bbs, (2, 0, 1, 5, 3, 4)).reshape(O, 4, H, gpt * P)
    cs = jnp.stack([c_re.astype(F32), c_im.astype(F32)], axis=1).reshape(2, 2, O, gpt, H, P)
    cct = jnp.transpose(cs, (2, 0, 1, 3, 5, 4)).reshape(O, 4, gpt * P, 1, H)
    cct = jnp.broadcast_to(cct, (O, 4, gpt * P, gpt, H)).reshape(O, 4, gpt * P, LANES)
    a4 = jnp.transpose(jnp.stack([a_re, a_im], axis=1).reshape(2, 2, O, gpt * P), (2, 0, 1, 3)).reshape(O, 4, gpt * P)
    arow = jnp.concatenate([a4, jnp.zeros_like(a4)], axis=1)
    acol = jnp.broadcast_to(a4[..., None], (O, 4, gpt * P, LANES))

    def apower(k):
        k = jnp.asarray(k, F32).reshape((-1, 1, 1, 1))
        m = jnp.exp(k * lr)
        return m * jnp.cos(k * li), m * jnp.sin(k * li)

    def cols(re, im):
        k = re.shape[0]
        z = jnp.stack([re, im], axis=2).reshape(k, 2, 2, O, gpt * P)
        return jnp.transpose(z, (3, 0, 1, 2, 4)).reshape(O, k, 4 * gpt * P)

    ap = cols(*apower(jnp.array([L, L * seg_len])))
    apow = jnp.concatenate([ap, jnp.zeros((O, 6, ap.shape[-1]), F32)], axis=1)
    ii = jnp.arange(seg_len)
    pf_re, pf_im = apower(L * ii)
    pb_re, pb_im = apower(L * (seg_len - 1 - ii))
    pw = cols(jnp.stack([pf_re[:, 0], pb_re[:, 1]], axis=1), jnp.stack([pf_im[:, 0], pb_im[:, 1]], axis=1))
    return bc, cct, arow, acol, apow, pw


def _ssm_build_operators(bc_ref, cct_ref, arow_ref, acol_ref, ws, ws_lo, w3):
    L = SSM_CHUNK
    lk = L * LANES
    hh = bc_ref.shape[2]
    sw = bc_ref.shape[3]
    gpt = LANES // hh
    pp = sw // gpt

    same_b = (lax.broadcasted_iota(jnp.int32, (LANES, sw), 0) // hh
              == lax.broadcasted_iota(jnp.int32, (LANES, sw), 1) // pp)
    same_c = (lax.broadcasted_iota(jnp.int32, (sw, LANES), 0) // pp
              == lax.broadcasted_iota(jnp.int32, (sw, LANES), 1) // hh)

    def powers(re, im, n):
        out = [(jnp.ones_like(re), jnp.zeros_like(re))]
        for _ in range(n):
            out.append(_cmul(out[-1][0], out[-1][1], re, im))
        return out

    def split(v):
        hi = v.astype(BF16)
        return hi, (v - hi.astype(F32)).astype(BF16)

    taps = []
    for d in range(2):
        b_re = jnp.where(same_b, jnp.concatenate([bc_ref[0, 2 * d]] * gpt, axis=0), 0.0)
        b_im = jnp.where(same_b, jnp.concatenate([bc_ref[0, 2 * d + 1]] * gpt, axis=0), 0.0)
        c_re = jnp.where(same_c, cct_ref[0, 2 * d], 0.0)
        c_im = jnp.where(same_c, cct_ref[0, 2 * d + 1], 0.0)
        prow = powers(arow_ref[0, 2 * d:2 * d + 1, :], arow_ref[0, 2 * d + 1:2 * d + 2, :], L - 1)
        pcol = powers(acol_ref[0, 2 * d], acol_ref[0, 2 * d + 1], L)
        dcols = slice((2 * d) * sw, (2 * d + 2) * sw)
        for k in range(L):
            ab_re, ab_im = _cmul(b_re, b_im, prow[k][0], prow[k][1])
            j = L - 1 - k if d == 0 else k
            rows = slice(j * LANES, (j + 1) * LANES)
            for part, v in ((2 * d, ab_re), (2 * d + 1, ab_im)):
                hi, lo = split(v)
                ws[rows, part * sw:(part + 1) * sw] = hi
                ws_lo[rows, part * sw:(part + 1) * sw] = lo
        (cr_hi, cr_lo), (ci_hi, ci_lo) = split(c_re), split(-c_im)
        c_hi = jnp.concatenate([cr_hi, ci_hi], axis=0)
        c_lo = jnp.concatenate([cr_lo, ci_lo], axis=0)
        t = jnp.dot(jnp.concatenate([ws[:, dcols], ws[:, dcols], ws_lo[:, dcols]], axis=1),
                    jnp.concatenate([c_hi, c_lo, c_hi], axis=0), preferred_element_type=F32)
        taps.append([t[(L - 1 - k if d == 0 else k) * LANES:(L - k if d == 0 else k + 1) * LANES, :]
                     for k in range(L)])
        for jo in range(L):
            e = jo + 1 if d == 0 else L - jo
            ca_re, ca_im = _cmul(c_re, c_im, pcol[e][0], pcol[e][1])
            r0 = lk + (2 * d) * sw
            w3[r0:r0 + sw, jo * LANES:(jo + 1) * LANES] = ca_re.astype(BF16)
            w3[r0 + sw:r0 + 2 * sw, jo * LANES:(jo + 1) * LANES] = (-ca_im).astype(BF16)
    for j in range(L):
        for jo in range(L):
            lag = jo - j
            t = taps[0][lag] if lag > 0 else taps[1][-lag] if lag < 0 else taps[0][0] + taps[1][0]
            w3[j * LANES:(j + 1) * LANES, jo * LANES:(jo + 1) * LANES] = t.astype(BF16)


def _seg_pitch(seg_len):
    tiles = seg_len // 8 + 1
    return 8 * (tiles + 1 - tiles % 2)


def _ssm_kernel(*refs, nc, nx, seg_len, ctx_out):
    if ctx_out:
        (ux_ref, uc_ref, bc_ref, cct_ref, arow_ref, acol_ref, apow_ref, pw_ref,
         yx_ref, yc_ref, ws, ws_lo, w3, lhs, st) = refs
    else:
        (ux_ref, uc_ref, bc_ref, cct_ref, arow_ref, acol_ref, apow_ref, pw_ref,
         yx_ref, ws, ws_lo, w3, lhs, st) = refs
        yc_ref = None
    L = SSM_CHUNK
    lk = L * LANES
    nlt = st.shape[0] // 4
    sw = nlt * LANES
    FR, FI, BR, BI = range(4)
    nseg = SSM_SEGS
    pitch = _seg_pitch(seg_len)

    @pl.when(pl.program_id(1) == 0)
    def _():
        _ssm_build_operators(bc_ref, cct_ref, arow_ref, acol_ref, ws, ws_lo, w3)

    def ld(part, rows):
        return jnp.concatenate([st[part * nlt + q, rows, :] for q in range(nlt)], axis=1)

    def sto(part, rows, val):
        for q in range(nlt):
            st[part * nlt + q, rows, :] = val[:, q * LANES:(q + 1) * LANES]

    def part_cols(part):
        return slice(part * sw, (part + 1) * sw)

    for j in range(L):
        lhs[0:nc, j * LANES:(j + 1) * LANES] = uc_ref[pl.ds(j, nc, stride=L), :].astype(BF16)
        lhs[nc:nc + nx, j * LANES:(j + 1) * LANES] = ux_ref[pl.ds(j, nx, stride=L), :].astype(BF16)

    s = jnp.dot(lhs[:, 0:lk], ws[...], preferred_element_type=F32)
    for k in range(4 * nlt):
        st[k, 0:nc, :] = s[0:nc, k * LANES:(k + 1) * LANES]
        for g in range(nseg):
            st[k, nc + g * pitch:nc + g * pitch + seg_len, :] = (
                s[nc + g * seg_len:nc + (g + 1) * seg_len, k * LANES:(k + 1) * LANES])

    a_l = apow_ref[0, 0:1, :]
    a_seg = apow_ref[0, 1:2, :]
    al = [a_l[:, part_cols(p)] for p in range(4)]
    aseg = [a_seg[:, part_cols(p)] for p in range(4)]

    zero = jnp.zeros((1, sw), F32)

    def ctx_step(pr, pi):
        def step(i, carry):
            er, ei = carry
            row = pl.ds(i, 1)
            sr, si = ld(pr, row), ld(pi, row)
            sto(pr, row, er)
            sto(pi, row, ei)
            return _cmul_add(al[pr], al[pi], er, ei, sr, si)
        return step

    h0f = lax.fori_loop(0, nc, ctx_step(FR, FI), (zero, zero))
    bwd_step = ctx_step(BR, BI)
    h0b = lax.fori_loop(0, nc, lambda k, carry: bwd_step(nc - 1 - k, carry), (zero, zero))
    for k in range(4 * nlt):
        lhs[0:nc, lk + k * LANES:lk + (k + 1) * LANES] = st[k, 0:nc, :].astype(BF16)

    zseg = jnp.zeros((nseg, sw), F32)

    def seg_pass(pr, pi, order):
        zr, zi = zseg, zseg
        for i in order:
            rows = pl.ds(nc + i, nseg, stride=pitch)
            sr, si = ld(pr, rows), ld(pi, rows)
            sto(pr, rows, zr)
            sto(pi, rows, zi)
            zr, zi = _cmul_add(al[pr], al[pi], zr, zi, sr, si)
        return zr, zi

    zfr, zfi = seg_pass(FR, FI, range(seg_len))
    zbr, zbi = seg_pass(BR, BI, range(seg_len - 1, -1, -1))

    ef = [h0f]
    for s in range(nseg - 1):
        ef.append(_cmul_add(aseg[FR], aseg[FI], ef[s][0], ef[s][1], zfr[s:s + 1], zfi[s:s + 1]))
    eb = [None] * nseg
    eb[nseg - 1] = h0b
    for s in range(nseg - 1, 0, -1):
        eb[s - 1] = _cmul_add(aseg[BR], aseg[BI], eb[s][0], eb[s][1], zbr[s:s + 1], zbi[s:s + 1])
    for s in range(nseg):
        srows = slice(nc + s * pitch, nc + s * pitch + seg_len)
        rows = slice(nc + s * seg_len, nc + (s + 1) * seg_len)
        for (pr, pi, e) in ((FR, FI, ef[s]), (BR, BI, eb[s])):
            xr, xi = _cmul_add(pw_ref[0, :, part_cols(pr)], pw_ref[0, :, part_cols(pi)], e[0], e[1],
                               ld(pr, srows), ld(pi, srows))
            lhs[rows, lk + pr * sw:lk + (pr + 1) * sw] = xr.astype(BF16)
            lhs[rows, lk + pi * sw:lk + (pi + 1) * sw] = xi.astype(BF16)

    y = jnp.dot(lhs[...], w3[...], preferred_element_type=F32)
    for j in range(L):
        if ctx_out:
            yc_ref[pl.ds(j, nc, stride=L), :] = y[0:nc, j * LANES:(j + 1) * LANES]
        yx_ref[pl.ds(j, nx, stride=L), :] = y[nc:nc + nx, j * LANES:(j + 1) * LANES]


def _ssm_scan(ux, uc, params, batch, ctx_out):
    bc, cct, arow, acol, apow, pw = params
    L = SSM_CHUNK
    n_x, w = ux.shape
    t = n_x // batch
    lc = uc.shape[0] // batch
    nx, nc = t // L, lc // L
    seg_len = nx // SSM_SEGS
    o = w // LANES
    lk = L * LANES
    sw = bc.shape[-1]
    sdim = 4 * sw
    out_shape = [jax.ShapeDtypeStruct(ux.shape, F32)]
    out_specs = [pl.BlockSpec((t, LANES), lambda oi, b: (b, oi))]
    if ctx_out:
        out_shape.append(jax.ShapeDtypeStruct(uc.shape, F32))
        out_specs.append(pl.BlockSpec((lc, LANES), lambda oi, b: (b, oi)))
    tile = lambda a: pl.BlockSpec((1,) + a.shape[1:], lambda oi, b: (oi,) + (0,) * (a.ndim - 1))
    res = pl.pallas_call(
        functools.partial(_ssm_kernel, nc=nc, nx=nx, seg_len=seg_len, ctx_out=ctx_out),
        grid=(o, batch),
        in_specs=[
            pl.BlockSpec((t, LANES), lambda oi, b: (b, oi)),
            pl.BlockSpec((lc, LANES), lambda oi, b: (b, oi)),
            tile(bc), tile(cct), tile(arow), tile(acol), tile(apow), tile(pw),
        ],
        out_specs=out_specs,
        out_shape=out_shape,
        scratch_shapes=[
            pltpu.VMEM((lk, sdim), BF16),
            pltpu.VMEM((lk, sdim), BF16),
            pltpu.VMEM((lk + sdim, lk), BF16),
            pltpu.VMEM((nc + nx, lk + sdim), BF16),
            pltpu.VMEM((sdim // LANES, nc + SSM_SEGS * _seg_pitch(seg_len), LANES), F32),
        ],
        compiler_params=_cparams("arbitrary", "arbitrary"),
        name="ssm_scan",
    )(ux, uc, bc, cct, arow, acol, apow, pw)
    return (res[0], res[1]) if ctx_out else (res[0], None)


def _dft_table_kernel(e1_ref, e2_ref, o_ref, *, t):
    e1c, e1s = e1_ref[0, :, 0:t], e1_ref[0, :, t:2 * t]
    e2c, e2s = e2_ref[:, 0:t], e2_ref[:, t:2 * t]
    c, s = _cmul(e2c, e2s, e1c, e1s)
    o_ref[:, 0:t] = c.astype(BF16)
    o_ref[:, t:2 * t] = (-s).astype(BF16)


def _dft_tables(t, gw):
    def cis(rows_mult, nrows):
        k = jnp.arange(nrows, dtype=jnp.int32)[:, None] * rows_mult
        n = jnp.arange(t, dtype=jnp.int32)[None, :]
        ang = ((k * n) % t).astype(F32) * (2.0 * math.pi / t)
        return jnp.concatenate([jnp.cos(ang), jnp.sin(ang)], axis=1)
    r = DFT_ROWS
    e1 = cis(r, t // r).reshape(t // r, 1, 2 * t)
    e2 = cis(1, r)
    tab_t = pl.pallas_call(
        functools.partial(_dft_table_kernel, t=t),
        grid=(t // r,),
        in_specs=[
            pl.BlockSpec((1, 1, 2 * t), lambda i: (i, 0, 0)),
            pl.BlockSpec((r, 2 * t), lambda i: (0, 0)),
        ],
        out_specs=pl.BlockSpec((r, 2 * t), lambda i: (i, 0)),
        out_shape=jax.ShapeDtypeStruct((t, 2 * t), BF16),
        compiler_params=_cparams("arbitrary"),
        name="dft_table",
    )(e1, e2)
    kc = jnp.arange(gw, dtype=jnp.int32)
    ang = ((kc[:, None] * kc[None, :]) % gw).astype(F32) * (2.0 * math.pi / gw)
    tab_c = jnp.concatenate([jnp.cos(ang), jnp.sin(ang)], axis=1).astype(BF16)
    return tab_t, tab_c


def _fft_kernel(f_ref, cs_ref, tab_ref, o_ref, data, *, t, gw, groups, scale):
    i = pl.program_id(1)

    @pl.when(i == 0)
    def _():
        rb = min(512, t)
        for r in range(t // rb):
            for g in range(groups):
                fg = f_ref[r * rb:(r + 1) * rb, g * gw:(g + 1) * gw]
                z = jnp.dot(fg, cs_ref[...], preferred_element_type=F32)
                data[r * rb:(r + 1) * rb, g * gw:(g + 1) * gw] = z[:, :gw].astype(BF16)
                data[t + r * rb:t + (r + 1) * rb, g * gw:(g + 1) * gw] = z[:, gw:].astype(BF16)

    y = jnp.dot(tab_ref[...], data[...], preferred_element_type=F32)
    o_ref[...] = (y * scale).astype(BF16)


def _fourier(p, col_tile, t, batch, tabs):
    tab_t, tab_c = tabs
    wf = 1024
    gw = wf // FFT_GROUPS
    tm = min(256, t)
    return pl.pallas_call(
        functools.partial(_fft_kernel, t=t, gw=gw, groups=FFT_GROUPS, scale=1.0 / math.sqrt(t * gw)),
        grid=(batch, t // tm),
        in_specs=[
            pl.BlockSpec((t, wf), lambda b, i: (b, col_tile)),
            pl.BlockSpec((gw, 2 * gw), lambda b, i: (0, 0)),
            pl.BlockSpec((tm, 2 * t), lambda b, i: (i, 0)),
        ],
        out_specs=pl.BlockSpec((tm, wf), lambda b, i: (b * (t // tm) + i, 0)),
        out_shape=jax.ShapeDtypeStruct((batch * t, wf), BF16),
        scratch_shapes=[pltpu.VMEM((2 * t, wf), BF16)],
        compiler_params=_cparams("arbitrary", "arbitrary"),
        name="fourier",
    )(p, tab_c, tab_t)


def _branch_kernel(xa_ref, ba_ref, ca_ref, za_ref, zb_ref, zc_ref, g0_ref, g1_ref, g2_ref,
                   u_ref, ys_ref, yf_ref, cw_ref, sd_ref, wa_ref, wb_ref, fw_ref,
                   pa_ref, pb_ref, pc_ref, m_ref, *, row_len):
    tm = xa_ref.shape[0]
    f = lambda r: r[...].astype(F32)
    v = f(ca_ref) * f(xa_ref)
    pos = lax.broadcasted_iota(jnp.int32, v.shape, 0) % row_len
    v_prev = jnp.where(pos == 0, 0.0, pltpu.roll(v, 1, 0))
    v_next = jnp.where(pos == row_len - 1, 0.0, pltpu.roll(v, tm - 1, 0))
    cw = cw_ref[0]
    conv = v_prev * cw[0:1, :] + v * cw[1:2, :] + v_next * cw[2:3, :]
    a = f(ba_ref) * conv * _silu(f(za_ref))
    ya = jnp.dot(a.astype(BF16), pa_ref[0], preferred_element_type=F32)
    acc = _sigmoid(f(g0_ref)) * ya
    y = _gelu_tanh(ys_ref[...] + sd_ref[0] * u_ref[...]).astype(BF16)
    glu = (jnp.dot(y, wa_ref[0], preferred_element_type=F32)
           * _sigmoid(jnp.dot(y, wb_ref[0], preferred_element_type=F32)) * _silu(f(zb_ref)))
    yb = jnp.dot(glu.astype(BF16), pb_ref[0], preferred_element_type=F32)
    acc = acc + _sigmoid(f(g1_ref)) * yb
    c = jnp.dot(yf_ref[...], fw_ref[0], preferred_element_type=F32) * _silu(f(zc_ref))
    yc = jnp.dot(c.astype(BF16), pc_ref[0], preferred_element_type=F32)
    acc = acc + _sigmoid(f(g2_ref)) * yc
    m_ref[...] = acc.astype(BF16)


def _branches(p, u, ys, yf, conv_w, ssm_d, wa, wb, fw, pa, pb, pc, layer, row_len):
    n = p.shape[0]
    wc = conv_w.shape[-1]
    d = pa.shape[-1]
    tm = 256
    tn = 1024
    col = lambda k: pl.BlockSpec((tm, tn), lambda i: (i, k))
    const = lambda shape: pl.BlockSpec(shape, lambda i: (layer,) + (0,) * (len(shape) - 1),
                                       pipeline_mode=pl.Buffered(1))
    g_tile0 = 8
    return pl.pallas_call(
        functools.partial(_branch_kernel, row_len=row_len),
        grid=(n // tm,),
        in_specs=[
            col(0), col(1), col(2), col(3), col(5), col(7),
            pl.BlockSpec((tm, d), lambda i: (i, g_tile0 * tn // d)),
            pl.BlockSpec((tm, d), lambda i: (i, g_tile0 * tn // d + 1)),
            pl.BlockSpec((tm, d), lambda i: (i, g_tile0 * tn // d + 2)),
            pl.BlockSpec((tm, tn), lambda i: (i, 0)),
            pl.BlockSpec((tm, tn), lambda i: (i, 0)),
            pl.BlockSpec((tm, tn), lambda i: (i, 0)),
            const((1, 3, wc)), const((1, 1, tn)),
            const((1, tn, tn)), const((1, tn, tn)), const((1, tn, tn)),
            const((1, wc, d)), const((1, tn, d)), const((1, tn, d)),
        ],
        out_specs=pl.BlockSpec((tm, d), lambda i: (i, 0)),
        out_shape=jax.ShapeDtypeStruct((n, d), BF16),
        compiler_params=_cparams("arbitrary"),
        name="branches",
    )(p, p, p, p, p, p, p, p, p, u, ys, yf, conv_w, ssm_d.reshape(ssm_d.shape[0], 1, -1),
      wa, wb, fw, pa, pb, pc)


def _out_kernel(m_ref, x_ref, gt_ref, g_ref, w_ref, o_ref):
    o = jnp.dot(m_ref[...], w_ref[0], preferred_element_type=F32)
    ms = jnp.mean(o * o, axis=-1, keepdims=True)
    y = o * lax.rsqrt(ms + RMS_EPS) * g_ref[0]
    o_ref[...] = x_ref[...] + gt_ref[0] * y


def _outproj(m, x2, gt, g_post, w_out_b, layer, rows_per_batch):
    n, d = x2.shape
    tm = min(512, rows_per_batch)
    tpb = rows_per_batch // tm
    return pl.pallas_call(
        _out_kernel,
        grid=(n // tm,),
        in_specs=[
            pl.BlockSpec((tm, d), lambda i: (i, 0)),
            pl.BlockSpec((tm, d), lambda i: (i, 0)),
            pl.BlockSpec((1, 1, d), lambda i: (i // tpb, 0, 0)),
            pl.BlockSpec((1, 1, d), lambda i: (layer, 0, 0)),
            pl.BlockSpec((1, d, d), lambda i: (layer, 0, 0), pipeline_mode=pl.Buffered(1)),
        ],
        out_specs=pl.BlockSpec((tm, d), lambda i: (i, 0)),
        out_shape=jax.ShapeDtypeStruct((n, d), F32),
        compiler_params=_cparams("arbitrary"),
        name="outproj",
    )(m, x2, gt, g_post.reshape(g_post.shape[0], 1, d), w_out_b)


def kernel(x, c, ctx, c_ctx, w_ada, b_ada, g_pre, g_post, w_in, conv_w, ssm_lam_re, ssm_lam_im, ssm_log_dt,
           ssm_b_re, ssm_b_im, ssm_c_re, ssm_c_im, ssm_d, glu_wa, glu_wb, fourier_w, proj_a, proj_b, proj_c,
           w_out):
    bn, t, d = x.shape
    lc = ctx.shape[1]
    depth = w_ada.shape[0]
    w_conv = conv_w.shape[-1]
    w_ssm = ssm_d.shape[-1]
    tn = 1024
    assert w_conv == tn and w_ssm == tn and fourier_w.shape[-1] == tn and d % tn == 0
    assert bn + 1 <= 8 and t % GRID_W == 0
    u_tile = 4 * w_conv // tn
    f_tile = u_tile + 2
    n_tiles = w_in.shape[-1] // tn
    seg_len = (t // SSM_CHUNK) // SSM_SEGS

    cc = jnp.concatenate([c, c_ctx[None, :], jnp.zeros((8 - bn - 1, d), F32)], axis=0)
    mods = _ada_mods(cc, w_ada, b_ada)

    to_b = lambda w: w.astype(BF16)
    w_in_b, wa_b, wb_b, fw_b = to_b(w_in), to_b(glu_wa), to_b(glu_wb), to_b(fourier_w)
    pa_b, pb_b, pc_b, wo_b = to_b(proj_a), to_b(proj_b), to_b(proj_c), to_b(w_out)
    tabs_x = _dft_tables(t, tn // FFT_GROUPS)
    tabs_c = _dft_tables(lc, tn // FFT_GROUPS)

    x2 = x.reshape(bn * t, d)
    c2 = ctx.reshape(bn * lc, d)
    for l in range(depth):
        last = l == depth - 1
        sh_x, sc_x, gt_x = (mods[l, :bn, k * d:(k + 1) * d].reshape(bn, 1, d) for k in range(3))
        sh_c, sc_c, gt_c = (jnp.broadcast_to(mods[l, bn, k * d:(k + 1) * d], (bn, 1, d)) for k in range(3))
        ssm_p = _ssm_params(ssm_lam_re[l], ssm_lam_im[l], ssm_log_dt[l], ssm_b_re[l], ssm_b_im[l],
                            ssm_c_re[l], ssm_c_im[l], seg_len)

        px, ux = _inproj(x2, sc_x, sh_x, g_pre, w_in_b, l, 0, n_tiles, u_tile, t)
        if last:
            pc_, uc = _inproj(c2, sc_c, sh_c, g_pre, w_in_b, l, u_tile, 1, 0, lc)
        else:
            pc_, uc = _inproj(c2, sc_c, sh_c, g_pre, w_in_b, l, 0, n_tiles, u_tile, lc)

        ys_x, ys_c = _ssm_scan(ux, uc, ssm_p, bn, ctx_out=not last)

        yf_x = _fourier(px, f_tile, t, bn, tabs_x)
        mx = _branches(px, ux, ys_x, yf_x, conv_w, ssm_d, wa_b, wb_b, fw_b, pa_b, pb_b, pc_b, l, GRID_W)
        new_x2 = _outproj(mx, x2, gt_x, g_post, wo_b, l, t)
        if not last:
            yf_c = _fourier(pc_, f_tile, lc, bn, tabs_c)
            mc = _branches(pc_, uc, ys_c, yf_c, conv_w, ssm_d, wa_b, wb_b, fw_b, pa_b, pb_b, pc_b, l, lc)
            c2 = _outproj(mc, c2, gt_c, g_post, wo_b, l, lc)
        x2 = new_x2
    return x2.reshape(bn, t, d)
```

```python
import functools
import math

import jax
import jax.numpy as jnp
from jax import lax
from jax.experimental import pallas as pl
from jax.experimental.pallas import tpu as pltpu

F32 = jnp.float32
BF16 = jnp.bfloat16

GRID_W = 64
FFT_GROUPS = 4
N_BRANCH = 3
RMS_EPS = 1e-6
LANES = 128
SSM_CHUNK = 8
SSM_SEGS = 8
DFT_ROWS = 64
VMEM_LIMIT = 56 * 1024 * 1024


def _cparams(*sem):
    return pltpu.CompilerParams(dimension_semantics=sem, vmem_limit_bytes=VMEM_LIMIT)


def _sigmoid(v):
    return 1.0 / (1.0 + jnp.exp(-v))


def _silu(v):
    return v * _sigmoid(v)


def _gelu_tanh(v):
    return v * (0.5 * (1.0 + jnp.tanh(math.sqrt(2.0 / math.pi) * (v + 0.044715 * (v * v * v)))))


def _cmul(ar, ai, br, bi):
    return ar * br - ai * bi, ar * bi + ai * br


def _cmul_add(ar, ai, zr, zi, sr, si):
    return ar * zr - ai * zi + sr, ar * zi + ai * zr + si


def _ada_kernel(c_ref, w_ref, b_ref, o_ref):
    c = c_ref[...]
    s = _silu(c).astype(BF16)
    o_ref[0] = jnp.dot(s, w_ref[0].astype(BF16), preferred_element_type=F32) + b_ref[0]


def _ada_mods(cc, w_ada, b_ada):
    depth, d, n3 = w_ada.shape
    tn = min(1024, n3)
    return pl.pallas_call(
        _ada_kernel,
        grid=(depth, n3 // tn),
        in_specs=[
            pl.BlockSpec((8, d), lambda l, j: (0, 0)),
            pl.BlockSpec((1, d, tn), lambda l, j: (l, 0, j)),
            pl.BlockSpec((1, 1, tn), lambda l, j: (l, 0, j)),
        ],
        out_specs=pl.BlockSpec((1, 8, tn), lambda l, j: (l, 0, j)),
        out_shape=jax.ShapeDtypeStruct((depth, 8, n3), F32),
        compiler_params=_cparams("arbitrary", "arbitrary"),
        name="ada_mods",
    )(cc, w_ada, b_ada.reshape(depth, 1, n3))


def _inproj_kernel(x_ref, sc_ref, sh_ref, g_ref, w_ref, p_ref, u_ref, h_scr, *, u_tile):
    j = pl.program_id(1)

    @pl.when(j == 0)
    def _():
        x = x_ref[...]
        ms = jnp.mean(x * x, axis=-1, keepdims=True)
        y = x * lax.rsqrt(ms + RMS_EPS) * g_ref[0]
        h_scr[...] = (y * (1.0 + sc_ref[0]) + sh_ref[0]).astype(BF16)

    acc = jnp.dot(h_scr[...], w_ref[0], preferred_element_type=F32)
    p_ref[...] = acc.astype(BF16)

    @pl.when(j == u_tile)
    def _():
        u_ref[...] = acc


def _inproj(x2, sc, sh, g_pre, w_b, layer, tile0, ntiles, u_tile, rows_per_batch):
    n, d = x2.shape
    tn = 1024
    tm = min(1024, rows_per_batch)
    tpb = rows_per_batch // tm
    return pl.pallas_call(
        functools.partial(_inproj_kernel, u_tile=u_tile),
        grid=(n // tm, ntiles),
        in_specs=[
            pl.BlockSpec((tm, d), lambda i, j: (i, 0)),
            pl.BlockSpec((1, 1, d), lambda i, j: (i // tpb, 0, 0)),
            pl.BlockSpec((1, 1, d), lambda i, j: (i // tpb, 0, 0)),
            pl.BlockSpec((1, 1, d), lambda i, j: (layer, 0, 0)),
            pl.BlockSpec((1, d, tn), lambda i, j: (layer, 0, tile0 + j)),
        ],
        out_specs=[
            pl.BlockSpec((tm, tn), lambda i, j: (i, j)),
            pl.BlockSpec((tm, tn), lambda i, j: (i, 0)),
        ],
        out_shape=[
            jax.ShapeDtypeStruct((n, ntiles * tn), BF16),
            jax.ShapeDtypeStruct((n, tn), F32),
        ],
        scratch_shapes=[pltpu.VMEM((tm, d), BF16)],
        compiler_params=_cparams("arbitrary", "arbitrary"),
        name="inproj",
    )(x2, sc, sh, g_pre.reshape(g_pre.shape[0], 1, d), w_b)


def _ssm_params(lam_re, lam_im, log_dt, b_re, b_im, c_re, c_im, seg_len):
    L = SSM_CHUNK
    _, G, P = lam_re.shape
    H = b_re.shape[-1]
    gpt = LANES // H
    O = G // gpt
    lam_re = lam_re.astype(F32)
    lam_im = lam_im.astype(F32)
    dt = jnp.exp(log_dt.astype(F32))[..., None]
    lr = lam_re * dt
    li = lam_im * dt
    mag = jnp.exp(lr)
    a_re = mag * jnp.cos(li)
    a_im = mag * jnp.sin(li)
    n_re = a_re - 1.0
    n_im = a_im
    den = lam_re * lam_re + lam_im * lam_im
    q_re = (n_re * lam_re + n_im * lam_im) / den
    q_im = (n_im * lam_re - n_re * lam_im) / den
    bb_re = q_re[..., None] * b_re - q_im[..., None] * b_im
    bb_im = q_re[..., None] * b_im + q_im[..., None] * b_re

    bbs = jnp.stack([bb_re, bb_im], axis=1).reshape(2, 2, O, gpt, P, H)
    bc = jnp.transpose(bbs, (2, 0, 1, 5, 3, 4)).reshape(O, 4, H, gpt * P)
    cs = jnp.stack([c_re.astype(F32), c_im.astype(F32)], axis=1).reshape(2, 2, O, gpt, H, P)
    cct = jnp.transpose(cs, (2, 0, 1, 3, 5, 4)).reshape(O, 4, gpt * P, 1, H)
    cct = jnp.broadcast_to(cct, (O, 4, gpt * P, gpt, H)).reshape(O, 4, gpt * P, LANES)
    a4 = jnp.transpose(jnp.stack([a_re, a_im], axis=1).reshape(2, 2, O, gpt * P), (2, 0, 1, 3)).reshape(O, 4, gpt * P)
    arow = jnp.concatenate([a4, jnp.zeros_like(a4)], axis=1)
    acol = jnp.broadcast_to(a4[..., None], (O, 4, gpt * P, LANES))

    def apower(k):
        k = jnp.asarray(k, F32).reshape((-1, 1, 1, 1))
        m = jnp.exp(k * lr)
        return m * jnp.cos(k * li), m * jnp.sin(k * li)

    def cols(re, im):
        k = re.shape[0]
        z = jnp.stack([re, im], axis=2).reshape(k, 2, 2, O, gpt * P)
        return jnp.transpose(z, (3, 0, 1, 2, 4)).reshape(O, k, 4 * gpt * P)

    ap = cols(*apower(jnp.array([L, L * seg_len])))
    apow = jnp.concatenate([ap, jnp.zeros((O, 6, ap.shape[-1]), F32)], axis=1)
    ii = jnp.arange(seg_len)
    pf_re, pf_im = apower(L * ii)
    pb_re, pb_im = apower(L * (seg_len - 1 - ii))
    pw = cols(jnp.stack([pf_re[:, 0], pb_re[:, 1]], axis=1), jnp.stack([pf_im[:, 0], pb_im[:, 1]], axis=1))
    return bc, cct, arow, acol, apow, pw


def _ssm_build_operators(bc_ref, cct_ref, arow_ref, acol_ref, ws, ws_lo, w3):
    L = SSM_CHUNK
    lk = L * LANES
    hh = bc_ref.shape[2]
    sw = bc_ref.shape[3]
    gpt = LANES // hh
    pp = sw // gpt

    same_b = (lax.broadcasted_iota(jnp.int32, (LANES, sw), 0) // hh
              == lax.broadcasted_iota(jnp.int32, (LANES, sw), 1) // pp)
    same_c = (lax.broadcasted_iota(jnp.int32, (sw, LANES), 0) // pp
              == lax.broadcasted_iota(jnp.int32, (sw, LANES), 1) // hh)

    def powers(re, im, n):
        out = [(jnp.ones_like(re), jnp.zeros_like(re))]
        for _ in range(n):
            out.append(_cmul(out[-1][0], out[-1][1], re, im))
        return out

    def split(v):
        hi = v.astype(BF16)
        return hi, (v - hi.astype(F32)).astype(BF16)

    taps = []
    for d in range(2):
        b_re = jnp.where(same_b, jnp.concatenate([bc_ref[0, 2 * d]] * gpt, axis=0), 0.0)
        b_im = jnp.where(same_b, jnp.concatenate([bc_ref[0, 2 * d + 1]] * gpt, axis=0), 0.0)
        c_re = jnp.where(same_c, cct_ref[0, 2 * d], 0.0)
        c_im = jnp.where(same_c, cct_ref[0, 2 * d + 1], 0.0)
        prow = powers(arow_ref[0, 2 * d:2 * d + 1, :], arow_ref[0, 2 * d + 1:2 * d + 2, :], L - 1)
        pcol = powers(acol_ref[0, 2 * d], acol_ref[0, 2 * d + 1], L)
        dcols = slice((2 * d) * sw, (2 * d + 2) * sw)
        for k in range(L):
            ab_re, ab_im = _cmul(b_re, b_im, prow[k][0], prow[k][1])
            j = L - 1 - k if d == 0 else k
            rows = slice(j * LANES, (j + 1) * LANES)
            for part, v in ((2 * d, ab_re), (2 * d + 1, ab_im)):
                hi, lo = split(v)
                ws[rows, part * sw:(part + 1) * sw] = hi
                ws_lo[rows, part * sw:(part + 1) * sw] = lo
        (cr_hi, cr_lo), (ci_hi, ci_lo) = split(c_re), split(-c_im)
        c_hi = jnp.concatenate([cr_hi, ci_hi], axis=0)
        c_lo = jnp.concatenate([cr_lo, ci_lo], axis=0)
        t = jnp.dot(jnp.concatenate([ws[:, dcols], ws[:, dcols], ws_lo[:, dcols]], axis=1),
                    jnp.concatenate([c_hi, c_lo, c_hi], axis=0), preferred_element_type=F32)
        taps.append([t[(L - 1 - k if d == 0 else k) * LANES:(L - k if d == 0 else k + 1) * LANES, :]
                     for k in range(L)])
        for jo in range(L):
            e = jo + 1 if d == 0 else L - jo
            ca_re, ca_im = _cmul(c_re, c_im, pcol[e][0], pcol[e][1])
            r0 = lk + (2 * d) * sw
            w3[r0:r0 + sw, jo * LANES:(jo + 1) * LANES] = ca_re.astype(BF16)
            w3[r0 + sw:r0 + 2 * sw, jo * LANES:(jo + 1) * LANES] = (-ca_im).astype(BF16)
    for j in range(L):
        for jo in range(L):
            lag = jo - j
            t = taps[0][lag] if lag > 0 else taps[1][-lag] if lag < 0 else taps[0][0] + taps[1][0]
            w3[j * LANES:(j + 1) * LANES, jo * LANES:(jo + 1) * LANES] = t.astype(BF16)


def _seg_pitch(seg_len):
    tiles = seg_len // 8 + 1
    return 8 * (tiles + 1 - tiles % 2)


def _ssm_kernel(*refs, nc, nx, seg_len, ctx_out):
    if ctx_out:
        (ux_ref, uc_ref, bc_ref, cct_ref, arow_ref, acol_ref, apow_ref, pw_ref,
         yx_ref, yc_ref, ws, ws_lo, w3, lhs, st) = refs
    else:
        (ux_ref, uc_ref, bc_ref, cct_ref, arow_ref, acol_ref, apow_ref, pw_ref,
         yx_ref, ws, ws_lo, w3, lhs, st) = refs
        yc_ref = None
    L = SSM_CHUNK
    lk = L * LANES
    nlt = st.shape[0] // 4
    sw = nlt * LANES
    FR, FI, BR, BI = range(4)
    nseg = SSM_SEGS
    pitch = _seg_pitch(seg_len)

    @pl.when(pl.program_id(1) == 0)
    def _():
        _ssm_build_operators(bc_ref, cct_ref, arow_ref, acol_ref, ws, ws_lo, w3)

    def ld(part, rows):
        return jnp.concatenate([st[part * nlt + q, rows, :] for q in range(nlt)], axis=1)

    def sto(part, rows, val):
        for q in range(nlt):
            st[part * nlt + q, rows, :] = val[:, q * LANES:(q + 1) * LANES]

    def part_cols(part):
        return slice(part * sw, (part + 1) * sw)

    for j in range(L):
        lhs[0:nc, j * LANES:(j + 1) * LANES] = uc_ref[pl.ds(j, nc, stride=L), :].astype(BF16)
        lhs[nc:nc + nx, j * LANES:(j + 1) * LANES] = ux_ref[pl.ds(j, nx, stride=L), :].astype(BF16)

    s = jnp.dot(lhs[:, 0:lk], ws[...], preferred_element_type=F32)
    for k in range(4 * nlt):
        st[k, 0:nc, :] = s[0:nc, k * LANES:(k + 1) * LANES]
        for g in range(nseg):
            st[k, nc + g * pitch:nc + g * pitch + seg_len, :] = (
                s[nc + g * seg_len:nc + (g + 1) * seg_len, k * LANES:(k + 1) * LANES])

    a_l = apow_ref[0, 0:1, :]
    a_seg = apow_ref[0, 1:2, :]
    al = [a_l[:, part_cols(p)] for p in range(4)]
    aseg = [a_seg[:, part_cols(p)] for p in range(4)]

    zero = jnp.zeros((1, sw), F32)

    def ctx_step(pr, pi):
        def step(i, carry):
            er, ei = carry
            row = pl.ds(i, 1)
            sr, si = ld(pr, row), ld(pi, row)
            sto(pr, row, er)
            sto(pi, row, ei)
            return _cmul_add(al[pr], al[pi], er, ei, sr, si)
        return step

    h0f = lax.fori_loop(0, nc, ctx_step(FR, FI), (zero, zero))
    bwd_step = ctx_step(BR, BI)
    h0b = lax.fori_loop(0, nc, lambda k, carry: bwd_step(nc - 1 - k, carry), (zero, zero))
    for k in range(4 * nlt):
        lhs[0:nc, lk + k * LANES:lk + (k + 1) * LANES] = st[k, 0:nc, :].astype(BF16)

    zseg = jnp.zeros((nseg, sw), F32)

    def seg_pass(pr, pi, order):
        zr, zi = zseg, zseg
        for i in order:
            rows = pl.ds(nc + i, nseg, stride=pitch)
            sr, si = ld(pr, rows), ld(pi, rows)
            sto(pr, rows, zr)
            sto(pi, rows, zi)
            zr, zi = _cmul_add(al[pr], al[pi], zr, zi, sr, si)
        return zr, zi

    zfr, zfi = seg_pass(FR, FI, range(seg_len))
    zbr, zbi = seg_pass(BR, BI, range(seg_len - 1, -1, -1))

    ef = [h0f]
    for s in range(nseg - 1):
        ef.append(_cmul_add(aseg[FR], aseg[FI], ef[s][0], ef[s][1], zfr[s:s + 1], zfi[s:s + 1]))
    eb = [None] * nseg
    eb[nseg - 1] = h0b
    for s in range(nseg - 1, 0, -1):
        eb[s - 1] = _cmul_add(aseg[BR], aseg[BI], eb[s][0], eb[s][1], zbr[s:s + 1], zbi[s:s + 1])
    for s in range(nseg):
        srows = slice(nc + s * pitch, nc + s * pitch + seg_len)
        rows = slice(nc + s * seg_len, nc + (s + 1) * seg_len)
        for (pr, pi, e) in ((FR, FI, ef[s]), (BR, BI, eb[s])):
            xr, xi = _cmul_add(pw_ref[0, :, part_cols(pr)], pw_ref[0, :, part_cols(pi)], e[0], e[1],
                               ld(pr, srows), ld(pi, srows))
            lhs[rows, lk + pr * sw:lk + (pr + 1) * sw] = xr.astype(BF16)
            lhs[rows, lk + pi * sw:lk + (pi + 1) * sw] = xi.astype(BF16)

    y = jnp.dot(lhs[...], w3[...], preferred_element_type=F32)
    for j in range(L):
        if ctx_out:
            yc_ref[pl.ds(j, nc, stride=L), :] = y[0:nc, j * LANES:(j + 1) * LANES]
        yx_ref[pl.ds(j, nx, stride=L), :] = y[nc:nc + nx, j * LANES:(j + 1) * LANES]


def _ssm_scan(ux, uc, params, batch, ctx_out):
    bc, cct, arow, acol, apow, pw = params
    L = SSM_CHUNK
    n_x, w = ux.shape
    t = n_x // batch
    lc = uc.shape[0] // batch
    nx, nc = t // L, lc // L
    seg_len = nx // SSM_SEGS
    o = w // LANES
    lk = L * LANES
    sw = bc.shape[-1]
    sdim = 4 * sw
    out_shape = [jax.ShapeDtypeStruct(ux.shape, F32)]
    out_specs = [pl.BlockSpec((t, LANES), lambda oi, b: (b, oi))]
    if ctx_out:
        out_shape.append(jax.ShapeDtypeStruct(uc.shape, F32))
        out_specs.append(pl.BlockSpec((lc, LANES), lambda oi, b: (b, oi)))
    tile = lambda a: pl.BlockSpec((1,) + a.shape[1:], lambda oi, b: (oi,) + (0,) * (a.ndim - 1))
    res = pl.pallas_call(
        functools.partial(_ssm_kernel, nc=nc, nx=nx, seg_len=seg_len, ctx_out=ctx_out),
        grid=(o, batch),
        in_specs=[
            pl.BlockSpec((t, LANES), lambda oi, b: (b, oi)),
            pl.BlockSpec((lc, LANES), lambda oi, b: (b, oi)),
            tile(bc), tile(cct), tile(arow), tile(acol), tile(apow), tile(pw),
        ],
        out_specs=out_specs,
        out_shape=out_shape,
        scratch_shapes=[
            pltpu.VMEM((lk, sdim), BF16),
            pltpu.VMEM((lk, sdim), BF16),
            pltpu.VMEM((lk + sdim, lk), BF16),
            pltpu.VMEM((nc + nx, lk + sdim), BF16),
            pltpu.VMEM((sdim // LANES, nc + SSM_SEGS * _seg_pitch(seg_len), LANES), F32),
        ],
        compiler_params=_cparams("arbitrary", "arbitrary"),
        name="ssm_scan",
    )(ux, uc, bc, cct, arow, acol, apow, pw)
    return (res[0], res[1]) if ctx_out else (res[0], None)


def _dft_table_kernel(e1_ref, e2_ref, o_ref, *, t):
    e1c, e1s = e1_ref[0, :, 0:t], e1_ref[0, :, t:2 * t]
    e2c, e2s = e2_ref[:, 0:t], e2_ref[:, t:2 * t]
    c, s = _cmul(e2c, e2s, e1c, e1s)
    o_ref[:, 0:t] = c.astype(BF16)
    o_ref[:, t:2 * t] = (-s).astype(BF16)


def _dft_tables(t, gw):
    def cis(rows_mult, nrows):
        k = jnp.arange(nrows, dtype=jnp.int32)[:, None] * rows_mult
        n = jnp.arange(t, dtype=jnp.int32)[None, :]
        ang = ((k * n) % t).astype(F32) * (2.0 * math.pi / t)
        return jnp.concatenate([jnp.cos(ang), jnp.sin(ang)], axis=1)
    r = DFT_ROWS
    e1 = cis(r, t // r).reshape(t // r, 1, 2 * t)
    e2 = cis(1, r)
    tab_t = pl.pallas_call(
        functools.partial(_dft_table_kernel, t=t),
        grid=(t // r,),
        in_specs=[
            pl.BlockSpec((1, 1, 2 * t), lambda i: (i, 0, 0)),
            pl.BlockSpec((r, 2 * t), lambda i: (0, 0)),
        ],
        out_specs=pl.BlockSpec((r, 2 * t), lambda i: (i, 0)),
        out_shape=jax.ShapeDtypeStruct((t, 2 * t), BF16),
        compiler_params=_cparams("arbitrary"),
        name="dft_table",
    )(e1, e2)
    kc = jnp.arange(gw, dtype=jnp.int32)
    ang = ((kc[:, None] * kc[None, :]) % gw).astype(F32) * (2.0 * math.pi / gw)
    tab_c = jnp.concatenate([jnp.cos(ang), jnp.sin(ang)], axis=1).astype(BF16)
    return tab_t, tab_c


def _fft_kernel(f_ref, cs_ref, tab_ref, o_ref, data, *, t, gw, groups, scale):
    i = pl.program_id(1)

    @pl.when(i == 0)
    def _():
        rb = min(512, t)
        for r in range(t // rb):
            for g in range(groups):
                fg = f_ref[r * rb:(r + 1) * rb, g * gw:(g + 1) * gw]
                z = jnp.dot(fg, cs_ref[...], preferred_element_type=F32)
                data[r * rb:(r + 1) * rb, g * gw:(g + 1) * gw] = z[:, :gw].astype(BF16)
                data[t + r * rb:t + (r + 1) * rb, g * gw:(g + 1) * gw] = z[:, gw:].astype(BF16)

    y = jnp.dot(tab_ref[...], data[...], preferred_element_type=F32)
    o_ref[...] = (y * scale).astype(BF16)


def _fourier(p, col_tile, t, batch, tabs):
    tab_t, tab_c = tabs
    wf = 1024
    gw = wf // FFT_GROUPS
    tm = min(256, t)
    return pl.pallas_call(
        functools.partial(_fft_kernel, t=t, gw=gw, groups=FFT_GROUPS, scale=1.0 / math.sqrt(t * gw)),
        grid=(batch, t // tm),
        in_specs=[
            pl.BlockSpec((t, wf), lambda b, i: (b, col_tile)),
            pl.BlockSpec((gw, 2 * gw), lambda b, i: (0, 0)),
            pl.BlockSpec((tm, 2 * t), lambda b, i: (i, 0)),
        ],
        out_specs=pl.BlockSpec((tm, wf), lambda b, i: (b * (t // tm) + i, 0)),
        out_shape=jax.ShapeDtypeStruct((batch * t, wf), BF16),
        scratch_shapes=[pltpu.VMEM((2 * t, wf), BF16)],
        compiler_params=_cparams("arbitrary", "arbitrary"),
        name="fourier",
    )(p, tab_c, tab_t)


def _cmul_const(xr, xi, c, s):
    def scaled(v, k):
        if abs(k) < 1e-12:
            return None
        return v if abs(k - 1.0) < 1e-12 else -v if abs(k + 1.0) < 1e-12 else v * k

    def add(a, b):
        return b if a is None else a if b is None else a + b

    return add(scaled(xr, c), scaled(xi, -s)), add(scaled(xi, c), scaled(xr, s))


def _dft4(y):
    (ar, ai), (br, bi), (cr, ci), (dr, di) = y
    sr, si, tr, ti = ar + cr, ai + ci, ar - cr, ai - ci
    ur, ui, vr, vi = br + dr, bi + di, br - dr, bi - di
    return [(sr + ur, si + ui), (tr + vi, ti - vr), (sr - ur, si - ui), (tr - vi, ti + vr)]


def _dft16(x):
    out = [None] * 16
    p = [_dft4([x[4 * a + b] for a in range(4)]) for b in range(4)]
    for ka in range(4):
        q = []
        for b in range(4):
            ang = -2.0 * math.pi * ka * b / 16.0
            q.append(_cmul_const(p[b][ka][0], p[b][ka][1], math.cos(ang), math.sin(ang)))
        r = _dft4(q)
        for kb in range(4):
            out[ka + 4 * kb] = r[kb]
    return out


def _fft_fast_kernel(f_ref, csc_ref, tw_ref, cst_ref, o_ref, z, o_scr, *, n1, n2, gw, scale):
    slabs_per_dot = 4
    for q in range(n2 // slabs_per_dot):
        r0 = q * slabs_per_dot * n1
        zz = jnp.dot(f_ref[r0:r0 + slabs_per_dot * n1, :], csc_ref[...], preferred_element_type=F32)
        for s in range(slabs_per_dot):
            z[0, q * slabs_per_dot + s] = zz[s * n1:(s + 1) * n1, 0:gw]
            z[1, q * slabs_per_dot + s] = zz[s * n1:(s + 1) * n1, gw:2 * gw]

    def tile_body(r, carry):
        rows = pl.ds(pl.multiple_of(r * 8, 8), 8)
        for c in range(gw // LANES):
            cols = slice(c * LANES, (c + 1) * LANES)
            h = _dft16([(z[0, t, rows, cols], z[1, t, rows, cols]) for t in range(n2)])
            for k2 in range(n2):
                hr, hi = h[k2]
                if k2 > 0:
                    hr, hi = _cmul(hr, hi, tw_ref[0, k2, rows, :], tw_ref[1, k2, rows, :])
                z[0, k2, rows, cols] = hr
                z[1, k2, rows, cols] = hi
        return carry

    lax.fori_loop(0, n1 // 8, tile_body, 0)

    for k2 in range(n2):
        rhs = jnp.concatenate([z[0, k2], z[1, k2]], axis=0).astype(BF16)
        y = jnp.dot(cst_ref[...], rhs, preferred_element_type=F32) * scale
        for c in range(gw // LANES):
            o_scr[c, pl.ds(k2, n1, stride=n2), :] = y[:, c * LANES:(c + 1) * LANES]
    o_ref[...] = jnp.concatenate([o_scr[c] for c in range(gw // LANES)], axis=1).astype(BF16)


def _fourier_fast(p, col_tile, t, batch):
    n1, n2 = 256, 16
    assert t == n1 * n2
    wf = 1024
    gw = wf // FFT_GROUPS
    ang = lambda a, b, n: ((a[:, None] * b[None, :]) % n).astype(F32) * (2.0 * math.pi / n)
    kc = jnp.arange(gw, dtype=jnp.int32)
    a_c = ang(kc, kc, gw)
    csc = jnp.concatenate([jnp.cos(a_c), -jnp.sin(a_c)], axis=1).astype(BF16)
    k1 = jnp.arange(n1, dtype=jnp.int32)
    a_t = ang(k1, k1, n1)
    cst = jnp.concatenate([jnp.cos(a_t), jnp.sin(a_t)], axis=1).astype(BF16)
    a_w = ang(jnp.arange(n2, dtype=jnp.int32), k1, t)
    tw = jnp.stack([jnp.cos(a_w), -jnp.sin(a_w)], axis=0)
    tw = jnp.broadcast_to(tw[..., None], (2, n2, n1, LANES))
    ngrp = wf // gw
    return pl.pallas_call(
        functools.partial(_fft_fast_kernel, n1=n1, n2=n2, gw=gw, scale=1.0 / math.sqrt(t * gw)),
        grid=(batch, ngrp),
        in_specs=[
            pl.BlockSpec((t, gw), lambda b, g: (b, col_tile * ngrp + g)),
            pl.BlockSpec((gw, 2 * gw), lambda b, g: (0, 0)),
            pl.BlockSpec((2, n2, n1, LANES), lambda b, g: (0, 0, 0, 0), pipeline_mode=pl.Buffered(1)),
            pl.BlockSpec((n1, 2 * n1), lambda b, g: (0, 0)),
        ],
        out_specs=pl.BlockSpec((t, gw), lambda b, g: (b, g)),
        out_shape=jax.ShapeDtypeStruct((batch * t, wf), BF16),
        scratch_shapes=[
            pltpu.VMEM((2, n2, n1, gw), F32),
            pltpu.VMEM((gw // LANES, t, LANES), F32),
        ],
        compiler_params=_cparams("arbitrary", "arbitrary"),
        name="fourier_fast",
    )(p, csc, tw, cst)


def _branch_kernel(xa_ref, ba_ref, ca_ref, za_ref, zb_ref, zc_ref, g0_ref, g1_ref, g2_ref,
                   u_ref, ys_ref, yf_ref, cw_ref, sd_ref, wa_ref, wb_ref, fw_ref,
                   pa_ref, pb_ref, pc_ref, m_ref, *, row_len):
    tm = xa_ref.shape[0]
    f = lambda r: r[...].astype(F32)
    v = f(ca_ref) * f(xa_ref)
    pos = lax.broadcasted_iota(jnp.int32, v.shape, 0) % row_len
    v_prev = jnp.where(pos == 0, 0.0, pltpu.roll(v, 1, 0))
    v_next = jnp.where(pos == row_len - 1, 0.0, pltpu.roll(v, tm - 1, 0))
    cw = cw_ref[0]
    conv = v_prev * cw[0:1, :] + v * cw[1:2, :] + v_next * cw[2:3, :]
    a = f(ba_ref) * conv * _silu(f(za_ref))
    ya = jnp.dot(a.astype(BF16), pa_ref[0], preferred_element_type=F32)
    acc = _sigmoid(f(g0_ref)) * ya
    y = _gelu_tanh(ys_ref[...] + sd_ref[0] * u_ref[...]).astype(BF16)
    glu = (jnp.dot(y, wa_ref[0], preferred_element_type=F32)
           * _sigmoid(jnp.dot(y, wb_ref[0], preferred_element_type=F32)) * _silu(f(zb_ref)))
    yb = jnp.dot(glu.astype(BF16), pb_ref[0], preferred_element_type=F32)
    acc = acc + _sigmoid(f(g1_ref)) * yb
    c = jnp.dot(yf_ref[...], fw_ref[0], preferred_element_type=F32) * _silu(f(zc_ref))
    yc = jnp.dot(c.astype(BF16), pc_ref[0], preferred_element_type=F32)
    acc = acc + _sigmoid(f(g2_ref)) * yc
    m_ref[...] = acc.astype(BF16)


def _branches(p, u, ys, yf, conv_w, ssm_d, wa, wb, fw, pa, pb, pc, layer, row_len):
    n = p.shape[0]
    wc = conv_w.shape[-1]
    d = pa.shape[-1]
    tm = 256
    tn = 1024
    col = lambda k: pl.BlockSpec((tm, tn), lambda i: (i, k))
    const = lambda shape: pl.BlockSpec(shape, lambda i: (layer,) + (0,) * (len(shape) - 1),
                                       pipeline_mode=pl.Buffered(1))
    g_tile0 = 8
    return pl.pallas_call(
        functools.partial(_branch_kernel, row_len=row_len),
        grid=(n // tm,),
        in_specs=[
            col(0), col(1), col(2), col(3), col(5), col(7),
            pl.BlockSpec((tm, d), lambda i: (i, g_tile0 * tn // d)),
            pl.BlockSpec((tm, d), lambda i: (i, g_tile0 * tn // d + 1)),
            pl.BlockSpec((tm, d), lambda i: (i, g_tile0 * tn // d + 2)),
            pl.BlockSpec((tm, tn), lambda i: (i, 0)),
            pl.BlockSpec((tm, tn), lambda i: (i, 0)),
            pl.BlockSpec((tm, tn), lambda i: (i, 0)),
            const((1, 3, wc)), const((1, 1, tn)),
            const((1, tn, tn)), const((1, tn, tn)), const((1, tn, tn)),
            const((1, wc, d)), const((1, tn, d)), const((1, tn, d)),
        ],
        out_specs=pl.BlockSpec((tm, d), lambda i: (i, 0)),
        out_shape=jax.ShapeDtypeStruct((n, d), BF16),
        compiler_params=_cparams("arbitrary"),
        name="branches",
    )(p, p, p, p, p, p, p, p, p, u, ys, yf, conv_w, ssm_d.reshape(ssm_d.shape[0], 1, -1),
      wa, wb, fw, pa, pb, pc)


def _out_kernel(m_ref, x_ref, gt_ref, g_ref, w_ref, o_ref):
    o = jnp.dot(m_ref[...], w_ref[0], preferred_element_type=F32)
    ms = jnp.mean(o * o, axis=-1, keepdims=True)
    y = o * lax.rsqrt(ms + RMS_EPS) * g_ref[0]
    o_ref[...] = x_ref[...] + gt_ref[0] * y


def _outproj(m, x2, gt, g_post, w_out_b, layer, rows_per_batch):
    n, d = x2.shape
    tm = min(512, rows_per_batch)
    tpb = rows_per_batch // tm
    return pl.pallas_call(
        _out_kernel,
        grid=(n // tm,),
        in_specs=[
            pl.BlockSpec((tm, d), lambda i: (i, 0)),
            pl.BlockSpec((tm, d), lambda i: (i, 0)),
            pl.BlockSpec((1, 1, d), lambda i: (i // tpb, 0, 0)),
            pl.BlockSpec((1, 1, d), lambda i: (layer, 0, 0)),
            pl.BlockSpec((1, d, d), lambda i: (layer, 0, 0), pipeline_mode=pl.Buffered(1)),
        ],
        out_specs=pl.BlockSpec((tm, d), lambda i: (i, 0)),
        out_shape=jax.ShapeDtypeStruct((n, d), F32),
        compiler_params=_cparams("arbitrary"),
        name="outproj",
    )(m, x2, gt, g_post.reshape(g_post.shape[0], 1, d), w_out_b)


def kernel(x, c, ctx, c_ctx, w_ada, b_ada, g_pre, g_post, w_in, conv_w, ssm_lam_re, ssm_lam_im, ssm_log_dt,
           ssm_b_re, ssm_b_im, ssm_c_re, ssm_c_im, ssm_d, glu_wa, glu_wb, fourier_w, proj_a, proj_b, proj_c,
           w_out):
    bn, t, d = x.shape
    lc = ctx.shape[1]
    depth = w_ada.shape[0]
    w_conv = conv_w.shape[-1]
    w_ssm = ssm_d.shape[-1]
    tn = 1024
    assert w_conv == tn and w_ssm == tn and fourier_w.shape[-1] == tn and d % tn == 0
    assert bn + 1 <= 8 and t % GRID_W == 0
    u_tile = 4 * w_conv // tn
    f_tile = u_tile + 2
    n_tiles = w_in.shape[-1] // tn
    seg_len = (t // SSM_CHUNK) // SSM_SEGS

    cc = jnp.concatenate([c, c_ctx[None, :], jnp.zeros((8 - bn - 1, d), F32)], axis=0)
    mods = _ada_mods(cc, w_ada, b_ada)

    to_b = lambda w: w.astype(BF16)
    w_in_b, wa_b, wb_b, fw_b = to_b(w_in), to_b(glu_wa), to_b(glu_wb), to_b(fourier_w)
    pa_b, pb_b, pc_b, wo_b = to_b(proj_a), to_b(proj_b), to_b(proj_c), to_b(w_out)
    fast_dft = t == 256 * 16
    tabs_x = None if fast_dft else _dft_tables(t, tn // FFT_GROUPS)
    tabs_c = _dft_tables(lc, tn // FFT_GROUPS)

    x2 = x.reshape(bn * t, d)
    c2 = ctx.reshape(bn * lc, d)
    for l in range(depth):
        last = l == depth - 1
        sh_x, sc_x, gt_x = (mods[l, :bn, k * d:(k + 1) * d].reshape(bn, 1, d) for k in range(3))
        sh_c, sc_c, gt_c = (jnp.broadcast_to(mods[l, bn, k * d:(k + 1) * d], (bn, 1, d)) for k in range(3))
        ssm_p = _ssm_params(ssm_lam_re[l], ssm_lam_im[l], ssm_log_dt[l], ssm_b_re[l], ssm_b_im[l],
                            ssm_c_re[l], ssm_c_im[l], seg_len)

        px, ux = _inproj(x2, sc_x, sh_x, g_pre, w_in_b, l, 0, n_tiles, u_tile, t)
        if last:
            pc_, uc = _inproj(c2, sc_c, sh_c, g_pre, w_in_b, l, u_tile, 1, 0, lc)
        else:
            pc_, uc = _inproj(c2, sc_c, sh_c, g_pre, w_in_b, l, 0, n_tiles, u_tile, lc)

        ys_x, ys_c = _ssm_scan(ux, uc, ssm_p, bn, ctx_out=not last)

        yf_x = _fourier_fast(px, f_tile, t, bn) if fast_dft else _fourier(px, f_tile, t, bn, tabs_x)
        mx = _branches(px, ux, ys_x, yf_x, conv_w, ssm_d, wa_b, wb_b, fw_b, pa_b, pb_b, pc_b, l, GRID_W)
        new_x2 = _outproj(mx, x2, gt_x, g_post, wo_b, l, t)
        if not last:
            yf_c = _fourier(pc_, f_tile, lc, bn, tabs_c)
            mc = _branches(pc_, uc, ys_c, yf_c, conv_w, ssm_d, wa_b, wb_b, fw_b, pa_b, pb_b, pc_b, l, lc)
            c2 = _outproj(mc, c2, gt_c, g_post, wo_b, l, lc)
        x2 = new_x2
    return x2.reshape(bn, t, d)
```

```python
import functools
import math

import jax
import jax.numpy as jnp
from jax import lax
from jax.experimental import pallas as pl
from jax.experimental.pallas import tpu as pltpu

F32 = jnp.float32
BF16 = jnp.bfloat16

GRID_W = 64
FFT_GROUPS = 4
N_BRANCH = 3
RMS_EPS = 1e-6
LANES = 128
SSM_CHUNK = 8
SSM_SEGS = 8
DFT_ROWS = 64
VMEM_LIMIT = 56 * 1024 * 1024


def _cparams(*sem):
    return pltpu.CompilerParams(dimension_semantics=sem, vmem_limit_bytes=VMEM_LIMIT)


def _sigmoid(v):
    return 1.0 / (1.0 + jnp.exp(-v))


def _silu(v):
    return v * _sigmoid(v)


def _gelu_tanh(v):
    return v * (0.5 * (1.0 + jnp.tanh(math.sqrt(2.0 / math.pi) * (v + 0.044715 * (v * v * v)))))


def _cmul(ar, ai, br, bi):
    return ar * br - ai * bi, ar * bi + ai * br


def _cmul_add(ar, ai, zr, zi, sr, si):
    return ar * zr - ai * zi + sr, ar * zi + ai * zr + si


def _ada_kernel(c_ref, w_ref, b_ref, o_ref):
    c = c_ref[...]
    s = _silu(c).astype(BF16)
    o_ref[0] = jnp.dot(s, w_ref[0].astype(BF16), preferred_element_type=F32) + b_ref[0]


def _ada_mods(cc, w_ada, b_ada):
    depth, d, n3 = w_ada.shape
    tn = min(1024, n3)
    return pl.pallas_call(
        _ada_kernel,
        grid=(depth, n3 // tn),
        in_specs=[
            pl.BlockSpec((8, d), lambda l, j: (0, 0)),
            pl.BlockSpec((1, d, tn), lambda l, j: (l, 0, j)),
            pl.BlockSpec((1, 1, tn), lambda l, j: (l, 0, j)),
        ],
        out_specs=pl.BlockSpec((1, 8, tn), lambda l, j: (l, 0, j)),
        out_shape=jax.ShapeDtypeStruct((depth, 8, n3), F32),
        compiler_params=_cparams("arbitrary", "arbitrary"),
        name="ada_mods",
    )(cc, w_ada, b_ada.reshape(depth, 1, n3))


def _prenorm_kernel(x_ref, sc_ref, sh_ref, g_ref, h_ref):
    x = x_ref[...]
    ms = jnp.mean(x * x, axis=-1, keepdims=True)
    y = x * lax.rsqrt(ms + RMS_EPS) * g_ref[0]
    h_ref[...] = (y * (1.0 + sc_ref[0]) + sh_ref[0]).astype(BF16)


def _prenorm(x2, sc, sh, g_pre, layer, rows_per_batch):
    n, d = x2.shape
    tm = min(512, rows_per_batch)
    tpb = rows_per_batch // tm
    return pl.pallas_call(
        _prenorm_kernel,
        grid=(n // tm,),
        in_specs=[
            pl.BlockSpec((tm, d), lambda i: (i, 0)),
            pl.BlockSpec((1, 1, d), lambda i: (i // tpb, 0, 0)),
            pl.BlockSpec((1, 1, d), lambda i: (i // tpb, 0, 0)),
            pl.BlockSpec((1, 1, d), lambda i: (layer, 0, 0)),
        ],
        out_specs=pl.BlockSpec((tm, d), lambda i: (i, 0)),
        out_shape=jax.ShapeDtypeStruct((n, d), BF16),
        compiler_params=_cparams("arbitrary"),
        name="prenorm",
    )(x2, sc, sh, g_pre.reshape(g_pre.shape[0], 1, d))


def _inproj_kernel(h_ref, w_ref, p_ref, wb):
    @pl.when(pl.program_id(1) == 0)
    def _():
        wb[...] = w_ref[0].astype(BF16)

    p_ref[...] = jnp.dot(h_ref[...], wb[...], preferred_element_type=F32).astype(BF16)


def _inproj(h, w_in, layer, tile0, ntiles):
    n, d = h.shape
    tn = 1024
    tm = min(1024, n)
    return pl.pallas_call(
        _inproj_kernel,
        grid=(ntiles, n // tm),
        in_specs=[
            pl.BlockSpec((tm, d), lambda j, i: (i, 0)),
            pl.BlockSpec((1, d, tn), lambda j, i: (layer, 0, tile0 + j)),
        ],
        out_specs=pl.BlockSpec((tm, tn), lambda j, i: (i, j)),
        out_shape=jax.ShapeDtypeStruct((n, ntiles * tn), BF16),
        scratch_shapes=[pltpu.VMEM((d, tn), BF16)],
        compiler_params=_cparams("arbitrary", "arbitrary"),
        name="inproj",
    )(h, w_in)


def _ssm_params(lam_re, lam_im, log_dt, b_re, b_im, c_re, c_im, seg_len):
    L = SSM_CHUNK
    _, G, P = lam_re.shape
    H = b_re.shape[-1]
    gpt = LANES // H
    O = G // gpt
    lam_re = lam_re.astype(F32)
    lam_im = lam_im.astype(F32)
    dt = jnp.exp(log_dt.astype(F32))[..., None]
    lr = lam_re * dt
    li = lam_im * dt
    mag = jnp.exp(lr)
    a_re = mag * jnp.cos(li)
    a_im = mag * jnp.sin(li)
    n_re = a_re - 1.0
    n_im = a_im
    den = lam_re * lam_re + lam_im * lam_im
    q_re = (n_re * lam_re + n_im * lam_im) / den
    q_im = (n_im * lam_re - n_re * lam_im) / den
    bb_re = q_re[..., None] * b_re - q_im[..., None] * b_im
    bb_im = q_re[..., None] * b_im + q_im[..., None] * b_re

    bbs = jnp.stack([bb_re, bb_im], axis=1).reshape(2, 2, O, gpt, P, H)
    bc = jnp.transpose(bbs, (2, 0, 1, 5, 3, 4)).reshape(O, 4, H, gpt * P)
    cs = jnp.stack([c_re.astype(F32), c_im.astype(F32)], axis=1).reshape(2, 2, O, gpt, H, P)
    cct = jnp.transpose(cs, (2, 0, 1, 3, 5, 4)).reshape(O, 4, gpt * P, 1, H)
    cct = jnp.broadcast_to(cct, (O, 4, gpt * P, gpt, H)).reshape(O, 4, gpt * P, LANES)
    a4 = jnp.transpose(jnp.stack([a_re, a_im], axis=1).reshape(2, 2, O, gpt * P), (2, 0, 1, 3)).reshape(O, 4, gpt * P)
    arow = jnp.concatenate([a4, jnp.zeros_like(a4)], axis=1)
    acol = jnp.broadcast_to(a4[..., None], (O, 4, gpt * P, LANES))

    def apower(k):
        k = jnp.asarray(k, F32).reshape((-1, 1, 1, 1))
        m = jnp.exp(k * lr)
        return m * jnp.cos(k * li), m * jnp.sin(k * li)

    def cols(re, im):
        k = re.shape[0]
        z = jnp.stack([re, im], axis=2).reshape(k, 2, 2, O, gpt * P)
        return jnp.transpose(z, (3, 0, 1, 2, 4)).reshape(O, k, 4 * gpt * P)

    ap = cols(*apower(jnp.array([L, L * seg_len])))
    apow = jnp.concatenate([ap, jnp.zeros((O, 6, ap.shape[-1]), F32)], axis=1)
    ii = jnp.arange(seg_len)
    pf_re, pf_im = apower(L * ii)
    pb_re, pb_im = apower(L * (seg_len - 1 - ii))
    pw = cols(jnp.stack([pf_re[:, 0], pb_re[:, 1]], axis=1), jnp.stack([pf_im[:, 0], pb_im[:, 1]], axis=1))
    return bc, cct, arow, acol, apow, pw


def _ssm_build_operators(bc_ref, cct_ref, arow_ref, acol_ref, ws, ws_lo, w3):
    L = SSM_CHUNK
    lk = L * LANES
    hh = bc_ref.shape[2]
    sw = bc_ref.shape[3]
    gpt = LANES // hh
    pp = sw // gpt

    same_b = (lax.broadcasted_iota(jnp.int32, (LANES, sw), 0) // hh
              == lax.broadcasted_iota(jnp.int32, (LANES, sw), 1) // pp)
    same_c = (lax.broadcasted_iota(jnp.int32, (sw, LANES), 0) // pp
              == lax.broadcasted_iota(jnp.int32, (sw, LANES), 1) // hh)

    def powers(re, im, n):
        out = [(jnp.ones_like(re), jnp.zeros_like(re))]
        for _ in range(n):
            out.append(_cmul(out[-1][0], out[-1][1], re, im))
        return out

    def split(v):
        hi = v.astype(BF16)
        return hi, (v - hi.astype(F32)).astype(BF16)

    taps = []
    for d in range(2):
        b_re = jnp.where(same_b, jnp.concatenate([bc_ref[0, 2 * d]] * gpt, axis=0), 0.0)
        b_im = jnp.where(same_b, jnp.concatenate([bc_ref[0, 2 * d + 1]] * gpt, axis=0), 0.0)
        c_re = jnp.where(same_c, cct_ref[0, 2 * d], 0.0)
        c_im = jnp.where(same_c, cct_ref[0, 2 * d + 1], 0.0)
        prow = powers(arow_ref[0, 2 * d:2 * d + 1, :], arow_ref[0, 2 * d + 1:2 * d + 2, :], L - 1)
        pcol = powers(acol_ref[0, 2 * d], acol_ref[0, 2 * d + 1], L)
        dcols = slice((2 * d) * sw, (2 * d + 2) * sw)
        for k in range(L):
            ab_re, ab_im = _cmul(b_re, b_im, prow[k][0], prow[k][1])
            j = L - 1 - k if d == 0 else k
            rows = slice(j * LANES, (j + 1) * LANES)
            for part, v in ((2 * d, ab_re), (2 * d + 1, ab_im)):
                hi, lo = split(v)
                ws[rows, part * sw:(part + 1) * sw] = hi
                ws_lo[rows, part * sw:(part + 1) * sw] = lo
        (cr_hi, cr_lo), (ci_hi, ci_lo) = split(c_re), split(-c_im)
        c_hi = jnp.concatenate([cr_hi, ci_hi], axis=0)
        c_lo = jnp.concatenate([cr_lo, ci_lo], axis=0)
        t = jnp.dot(jnp.concatenate([ws[:, dcols], ws[:, dcols], ws_lo[:, dcols]], axis=1),
                    jnp.concatenate([c_hi, c_lo, c_hi], axis=0), preferred_element_type=F32)
        taps.append([t[(L - 1 - k if d == 0 else k) * LANES:(L - k if d == 0 else k + 1) * LANES, :]
                     for k in range(L)])
        for jo in range(L):
            e = jo + 1 if d == 0 else L - jo
            ca_re, ca_im = _cmul(c_re, c_im, pcol[e][0], pcol[e][1])
            r0 = lk + (2 * d) * sw
            w3[r0:r0 + sw, jo * LANES:(jo + 1) * LANES] = ca_re.astype(BF16)
            w3[r0 + sw:r0 + 2 * sw, jo * LANES:(jo + 1) * LANES] = (-ca_im).astype(BF16)
    for j in range(L):
        for jo in range(L):
            lag = jo - j
            t = taps[0][lag] if lag > 0 else taps[1][-lag] if lag < 0 else taps[0][0] + taps[1][0]
            w3[j * LANES:(j + 1) * LANES, jo * LANES:(jo + 1) * LANES] = t.astype(BF16)


def _seg_pitch(seg_len):
    tiles = seg_len // 8 + 1
    return 8 * (tiles + 1 - tiles % 2)


def _ssm_kernel(*refs, nc, nx, seg_len, ctx_out):
    if ctx_out:
        (ux_ref, uc_ref, bc_ref, cct_ref, arow_ref, acol_ref, apow_ref, pw_ref,
         yx_ref, yc_ref, ws, ws_lo, w3, lhs, st, uxf, ucf) = refs
    else:
        (ux_ref, uc_ref, bc_ref, cct_ref, arow_ref, acol_ref, apow_ref, pw_ref,
         yx_ref, ws, ws_lo, w3, lhs, st, uxf, ucf) = refs
        yc_ref = None
    L = SSM_CHUNK
    lk = L * LANES
    nlt = st.shape[0] // 4
    sw = nlt * LANES
    FR, FI, BR, BI = range(4)
    nseg = SSM_SEGS
    pitch = _seg_pitch(seg_len)

    @pl.when(pl.program_id(1) == 0)
    def _():
        _ssm_build_operators(bc_ref, cct_ref, arow_ref, acol_ref, ws, ws_lo, w3)

    def ld(part, rows):
        return jnp.concatenate([st[part * nlt + q, rows, :] for q in range(nlt)], axis=1)

    def sto(part, rows, val):
        for q in range(nlt):
            st[part * nlt + q, rows, :] = val[:, q * LANES:(q + 1) * LANES]

    def part_cols(part):
        return slice(part * sw, (part + 1) * sw)

    ucf[...] = uc_ref[...].astype(F32)
    uxf[...] = ux_ref[...].astype(F32)
    for j in range(L):
        lhs[0:nc, j * LANES:(j + 1) * LANES] = ucf[pl.ds(j, nc, stride=L), :].astype(BF16)
        lhs[nc:nc + nx, j * LANES:(j + 1) * LANES] = uxf[pl.ds(j, nx, stride=L), :].astype(BF16)

    s = jnp.dot(lhs[:, 0:lk], ws[...], preferred_element_type=F32)
    for k in range(4 * nlt):
        st[k, 0:nc, :] = s[0:nc, k * LANES:(k + 1) * LANES]
        for g in range(nseg):
            st[k, nc + g * pitch:nc + g * pitch + seg_len, :] = (
                s[nc + g * seg_len:nc + (g + 1) * seg_len, k * LANES:(k + 1) * LANES])

    a_l = apow_ref[0, 0:1, :]
    a_seg = apow_ref[0, 1:2, :]
    al = [a_l[:, part_cols(p)] for p in range(4)]
    aseg = [a_seg[:, part_cols(p)] for p in range(4)]

    zero = jnp.zeros((1, sw), F32)

    def ctx_step(pr, pi):
        def step(i, carry):
            er, ei = carry
            row = pl.ds(i, 1)
            sr, si = ld(pr, row), ld(pi, row)
            sto(pr, row, er)
            sto(pi, row, ei)
            return _cmul_add(al[pr], al[pi], er, ei, sr, si)
        return step

    h0f = lax.fori_loop(0, nc, ctx_step(FR, FI), (zero, zero))
    bwd_step = ctx_step(BR, BI)
    h0b = lax.fori_loop(0, nc, lambda k, carry: bwd_step(nc - 1 - k, carry), (zero, zero))
    for k in range(4 * nlt):
        lhs[0:nc, lk + k * LANES:lk + (k + 1) * LANES] = st[k, 0:nc, :].astype(BF16)

    zseg = jnp.zeros((nseg, sw), F32)

    def seg_pass(pr, pi, order):
        zr, zi = zseg, zseg
        for i in order:
            rows = pl.ds(nc + i, nseg, stride=pitch)
            sr, si = ld(pr, rows), ld(pi, rows)
            sto(pr, rows, zr)
            sto(pi, rows, zi)
            zr, zi = _cmul_add(al[pr], al[pi], zr, zi, sr, si)
        return zr, zi

    zfr, zfi = seg_pass(FR, FI, range(seg_len))
    zbr, zbi = seg_pass(BR, BI, range(seg_len - 1, -1, -1))

    ef = [h0f]
    for s in range(nseg - 1):
        ef.append(_cmul_add(aseg[FR], aseg[FI], ef[s][0], ef[s][1], zfr[s:s + 1], zfi[s:s + 1]))
    eb = [None] * nseg
    eb[nseg - 1] = h0b
    for s in range(nseg - 1, 0, -1):
        eb[s - 1] = _cmul_add(aseg[BR], aseg[BI], eb[s][0], eb[s][1], zbr[s:s + 1], zbi[s:s + 1])
    for s in range(nseg):
        srows = slice(nc + s * pitch, nc + s * pitch + seg_len)
        rows = slice(nc + s * seg_len, nc + (s + 1) * seg_len)
        for (pr, pi, e) in ((FR, FI, ef[s]), (BR, BI, eb[s])):
            xr, xi = _cmul_add(pw_ref[0, :, part_cols(pr)], pw_ref[0, :, part_cols(pi)], e[0], e[1],
                               ld(pr, srows), ld(pi, srows))
            lhs[rows, lk + pr * sw:lk + (pr + 1) * sw] = xr.astype(BF16)
            lhs[rows, lk + pi * sw:lk + (pi + 1) * sw] = xi.astype(BF16)

    y = jnp.dot(lhs[...], w3[...], preferred_element_type=F32)
    for j in range(L):
        if ctx_out:
            yc_ref[pl.ds(j, nc, stride=L), :] = y[0:nc, j * LANES:(j + 1) * LANES]
        yx_ref[pl.ds(j, nx, stride=L), :] = y[nc:nc + nx, j * LANES:(j + 1) * LANES]


def _ssm_scan(px, pc, col_x, col_c, w, params, batch, ctx_out):
    bc, cct, arow, acol, apow, pw = params
    L = SSM_CHUNK
    t = px.shape[0] // batch
    lc = pc.shape[0] // batch
    nx, nc = t // L, lc // L
    seg_len = nx // SSM_SEGS
    o = w // LANES
    lk = L * LANES
    sw = bc.shape[-1]
    sdim = 4 * sw
    out_shape = [jax.ShapeDtypeStruct((batch * t, w), F32)]
    out_specs = [pl.BlockSpec((t, LANES), lambda oi, b: (b, oi))]
    if ctx_out:
        out_shape.append(jax.ShapeDtypeStruct((batch * lc, w), F32))
        out_specs.append(pl.BlockSpec((lc, LANES), lambda oi, b: (b, oi)))
    tile = lambda a: pl.BlockSpec((1,) + a.shape[1:], lambda oi, b: (oi,) + (0,) * (a.ndim - 1))
    res = pl.pallas_call(
        functools.partial(_ssm_kernel, nc=nc, nx=nx, seg_len=seg_len, ctx_out=ctx_out),
        grid=(o, batch),
        in_specs=[
            pl.BlockSpec((t, LANES), lambda oi, b: (b, col_x + oi)),
            pl.BlockSpec((lc, LANES), lambda oi, b: (b, col_c + oi)),
            tile(bc), tile(cct), tile(arow), tile(acol), tile(apow), tile(pw),
        ],
        out_specs=out_specs,
        out_shape=out_shape,
        scratch_shapes=[
            pltpu.VMEM((lk, sdim), BF16),
            pltpu.VMEM((lk, sdim), BF16),
            pltpu.VMEM((lk + sdim, lk), BF16),
            pltpu.VMEM((nc + nx, lk + sdim), BF16),
            pltpu.VMEM((sdim // LANES, nc + SSM_SEGS * _seg_pitch(seg_len), LANES), F32),
            pltpu.VMEM((t, LANES), F32),
            pltpu.VMEM((lc, LANES), F32),
        ],
        compiler_params=_cparams("arbitrary", "arbitrary"),
        name="ssm_scan",
    )(px, pc, bc, cct, arow, acol, apow, pw)
    return (res[0], res[1]) if ctx_out else (res[0], None)


def _dft_table_kernel(e1_ref, e2_ref, o_ref, *, t):
    e1c, e1s = e1_ref[0, :, 0:t], e1_ref[0, :, t:2 * t]
    e2c, e2s = e2_ref[:, 0:t], e2_ref[:, t:2 * t]
    c, s = _cmul(e2c, e2s, e1c, e1s)
    o_ref[:, 0:t] = c.astype(BF16)
    o_ref[:, t:2 * t] = (-s).astype(BF16)


def _dft_tables(t, gw):
    def cis(rows_mult, nrows):
        k = jnp.arange(nrows, dtype=jnp.int32)[:, None] * rows_mult
        n = jnp.arange(t, dtype=jnp.int32)[None, :]
        ang = ((k * n) % t).astype(F32) * (2.0 * math.pi / t)
        return jnp.concatenate([jnp.cos(ang), jnp.sin(ang)], axis=1)
    r = DFT_ROWS
    e1 = cis(r, t // r).reshape(t // r, 1, 2 * t)
    e2 = cis(1, r)
    tab_t = pl.pallas_call(
        functools.partial(_dft_table_kernel, t=t),
        grid=(t // r,),
        in_specs=[
            pl.BlockSpec((1, 1, 2 * t), lambda i: (i, 0, 0)),
            pl.BlockSpec((r, 2 * t), lambda i: (0, 0)),
        ],
        out_specs=pl.BlockSpec((r, 2 * t), lambda i: (i, 0)),
        out_shape=jax.ShapeDtypeStruct((t, 2 * t), BF16),
        compiler_params=_cparams("arbitrary"),
        name="dft_table",
    )(e1, e2)
    kc = jnp.arange(gw, dtype=jnp.int32)
    ang = ((kc[:, None] * kc[None, :]) % gw).astype(F32) * (2.0 * math.pi / gw)
    tab_c = jnp.concatenate([jnp.cos(ang), jnp.sin(ang)], axis=1).astype(BF16)
    return tab_t, tab_c


def _fft_kernel(f_ref, cs_ref, tab_ref, o_ref, data, *, t, gw, groups, scale):
    i = pl.program_id(1)

    @pl.when(i == 0)
    def _():
        rb = min(512, t)
        for r in range(t // rb):
            for g in range(groups):
                fg = f_ref[r * rb:(r + 1) * rb, g * gw:(g + 1) * gw]
                z = jnp.dot(fg, cs_ref[...], preferred_element_type=F32)
                data[r * rb:(r + 1) * rb, g * gw:(g + 1) * gw] = z[:, :gw].astype(BF16)
                data[t + r * rb:t + (r + 1) * rb, g * gw:(g + 1) * gw] = z[:, gw:].astype(BF16)

    y = jnp.dot(tab_ref[...], data[...], preferred_element_type=F32)
    o_ref[...] = (y * scale).astype(BF16)


def _fourier(p, col_tile, t, batch, tabs):
    tab_t, tab_c = tabs
    wf = 1024
    gw = wf // FFT_GROUPS
    tm = min(256, t)
    return pl.pallas_call(
        functools.partial(_fft_kernel, t=t, gw=gw, groups=FFT_GROUPS, scale=1.0 / math.sqrt(t * gw)),
        grid=(batch, t // tm),
        in_specs=[
            pl.BlockSpec((t, wf), lambda b, i: (b, col_tile)),
            pl.BlockSpec((gw, 2 * gw), lambda b, i: (0, 0)),
            pl.BlockSpec((tm, 2 * t), lambda b, i: (i, 0)),
        ],
        out_specs=pl.BlockSpec((tm, wf), lambda b, i: (b * (t // tm) + i, 0)),
        out_shape=jax.ShapeDtypeStruct((batch * t, wf), BF16),
        scratch_shapes=[pltpu.VMEM((2 * t, wf), BF16)],
        compiler_params=_cparams("arbitrary", "arbitrary"),
        name="fourier",
    )(p, tab_c, tab_t)


def _cmul_const(xr, xi, c, s):
    def scaled(v, k):
        if abs(k) < 1e-12:
            return None
        return v if abs(k - 1.0) < 1e-12 else -v if abs(k + 1.0) < 1e-12 else v * k

    def add(a, b):
        return b if a is None else a if b is None else a + b

    return add(scaled(xr, c), scaled(xi, -s)), add(scaled(xi, c), scaled(xr, s))


def _dft4(y):
    (ar, ai), (br, bi), (cr, ci), (dr, di) = y
    sr, si, tr, ti = ar + cr, ai + ci, ar - cr, ai - ci
    ur, ui, vr, vi = br + dr, bi + di, br - dr, bi - di
    return [(sr + ur, si + ui), (tr + vi, ti - vr), (sr - ur, si - ui), (tr - vi, ti + vr)]


def _dft16(x):
    out = [None] * 16
    p = [_dft4([x[4 * a + b] for a in range(4)]) for b in range(4)]
    for ka in range(4):
        q = []
        for b in range(4):
            ang = -2.0 * math.pi * ka * b / 16.0
            q.append(_cmul_const(p[b][ka][0], p[b][ka][1], math.cos(ang), math.sin(ang)))
        r = _dft4(q)
        for kb in range(4):
            out[ka + 4 * kb] = r[kb]
    return out


def _fft_fast_kernel(f_ref, csc_ref, tw_ref, cst_ref, o_ref, z, o_scr, *, n1, n2, gw, scale):
    slabs_per_dot = 4
    for q in range(n2 // slabs_per_dot):
        r0 = q * slabs_per_dot * n1
        zz = jnp.dot(f_ref[r0:r0 + slabs_per_dot * n1, :], csc_ref[...], preferred_element_type=F32)
        for s in range(slabs_per_dot):
            z[0, q * slabs_per_dot + s] = zz[s * n1:(s + 1) * n1, 0:gw]
            z[1, q * slabs_per_dot + s] = zz[s * n1:(s + 1) * n1, gw:2 * gw]

    def tile_body(r, carry):
        rows = pl.ds(pl.multiple_of(r * 8, 8), 8)
        for c in range(gw // LANES):
            cols = slice(c * LANES, (c + 1) * LANES)
            h = _dft16([(z[0, t, rows, cols], z[1, t, rows, cols]) for t in range(n2)])
            for k2 in range(n2):
                hr, hi = h[k2]
                if k2 > 0:
                    hr, hi = _cmul(hr, hi, tw_ref[0, k2, rows, :], tw_ref[1, k2, rows, :])
                z[0, k2, rows, cols] = hr
                z[1, k2, rows, cols] = hi
        return carry

    lax.fori_loop(0, n1 // 8, tile_body, 0)

    for k2 in range(n2):
        rhs = jnp.concatenate([z[0, k2], z[1, k2]], axis=0).astype(BF16)
        y = jnp.dot(cst_ref[...], rhs, preferred_element_type=F32) * scale
        for c in range(gw // LANES):
            o_scr[c, pl.ds(k2, n1, stride=n2), :] = y[:, c * LANES:(c + 1) * LANES]
    o_ref[...] = jnp.concatenate([o_scr[c] for c in range(gw // LANES)], axis=1).astype(BF16)


def _fourier_fast(p, col_tile, t, batch):
    n1, n2 = 256, 16
    assert t == n1 * n2
    wf = 1024
    gw = wf // FFT_GROUPS
    ang = lambda a, b, n: ((a[:, None] * b[None, :]) % n).astype(F32) * (2.0 * math.pi / n)
    kc = jnp.arange(gw, dtype=jnp.int32)
    a_c = ang(kc, kc, gw)
    csc = jnp.concatenate([jnp.cos(a_c), -jnp.sin(a_c)], axis=1).astype(BF16)
    k1 = jnp.arange(n1, dtype=jnp.int32)
    a_t = ang(k1, k1, n1)
    cst = jnp.concatenate([jnp.cos(a_t), jnp.sin(a_t)], axis=1).astype(BF16)
    a_w = ang(jnp.arange(n2, dtype=jnp.int32), k1, t)
    tw = jnp.stack([jnp.cos(a_w), -jnp.sin(a_w)], axis=0)
    tw = jnp.broadcast_to(tw[..., None], (2, n2, n1, LANES))
    ngrp = wf // gw
    return pl.pallas_call(
        functools.partial(_fft_fast_kernel, n1=n1, n2=n2, gw=gw, scale=1.0 / math.sqrt(t * gw)),
        grid=(batch, ngrp),
        in_specs=[
            pl.BlockSpec((t, gw), lambda b, g: (b, col_tile * ngrp + g)),
            pl.BlockSpec((gw, 2 * gw), lambda b, g: (0, 0)),
            pl.BlockSpec((2, n2, n1, LANES), lambda b, g: (0, 0, 0, 0), pipeline_mode=pl.Buffered(1)),
            pl.BlockSpec((n1, 2 * n1), lambda b, g: (0, 0)),
        ],
        out_specs=pl.BlockSpec((t, gw), lambda b, g: (b, g)),
        out_shape=jax.ShapeDtypeStruct((batch * t, wf), BF16),
        scratch_shapes=[
            pltpu.VMEM((2, n2, n1, gw), F32),
            pltpu.VMEM((gw // LANES, t, LANES), F32),
        ],
        compiler_params=_cparams("arbitrary", "arbitrary"),
        name="fourier_fast",
    )(p, csc, tw, cst)


def _branch_kernel(xa_ref, ba_ref, ca_ref, za_ref, zb_ref, zc_ref, g0_ref, g1_ref, g2_ref,
                   u_ref, ys_ref, yf_ref, cw_ref, sd_ref, wa_ref, wb_ref, fw_ref,
                   pa_ref, pb_ref, pc_ref, m_ref, *, row_len):
    tm = xa_ref.shape[0]
    f = lambda r: r[...].astype(F32)
    v = f(ca_ref) * f(xa_ref)
    pos = lax.broadcasted_iota(jnp.int32, v.shape, 0) % row_len
    v_prev = jnp.where(pos == 0, 0.0, pltpu.roll(v, 1, 0))
    v_next = jnp.where(pos == row_len - 1, 0.0, pltpu.roll(v, tm - 1, 0))
    cw = cw_ref[0]
    conv = v_prev * cw[0:1, :] + v * cw[1:2, :] + v_next * cw[2:3, :]
    a = f(ba_ref) * conv * _silu(f(za_ref))
    ya = jnp.dot(a.astype(BF16), pa_ref[0], preferred_element_type=F32)
    acc = _sigmoid(f(g0_ref)) * ya
    y = _gelu_tanh(ys_ref[...] + sd_ref[0] * f(u_ref)).astype(BF16)
    glu = (jnp.dot(y, wa_ref[0], preferred_element_type=F32)
           * _sigmoid(jnp.dot(y, wb_ref[0], preferred_element_type=F32)) * _silu(f(zb_ref)))
    yb = jnp.dot(glu.astype(BF16), pb_ref[0], preferred_element_type=F32)
    acc = acc + _sigmoid(f(g1_ref)) * yb
    c = jnp.dot(yf_ref[...], fw_ref[0], preferred_element_type=F32) * _silu(f(zc_ref))
    yc = jnp.dot(c.astype(BF16), pc_ref[0], preferred_element_type=F32)
    acc = acc + _sigmoid(f(g2_ref)) * yc
    m_ref[...] = acc.astype(BF16)


def _branches(p, ys, yf, conv_w, ssm_d, wa, wb, fw, pa, pb, pc, layer, row_len):
    n = p.shape[0]
    wc = conv_w.shape[-1]
    d = pa.shape[-1]
    tm = 256
    tn = 1024
    col = lambda k: pl.BlockSpec((tm, tn), lambda i: (i, k))
    const = lambda shape: pl.BlockSpec(shape, lambda i: (layer,) + (0,) * (len(shape) - 1),
                                       pipeline_mode=pl.Buffered(1))
    g_tile0 = 8
    return pl.pallas_call(
        functools.partial(_branch_kernel, row_len=row_len),
        grid=(n // tm,),
        in_specs=[
            col(0), col(1), col(2), col(3), col(5), col(7),
            pl.BlockSpec((tm, d), lambda i: (i, g_tile0 * tn // d)),
            pl.BlockSpec((tm, d), lambda i: (i, g_tile0 * tn // d + 1)),
            pl.BlockSpec((tm, d), lambda i: (i, g_tile0 * tn // d + 2)),
            col(4),
            pl.BlockSpec((tm, tn), lambda i: (i, 0)),
            pl.BlockSpec((tm, tn), lambda i: (i, 0)),
            const((1, 3, wc)), const((1, 1, tn)),
            const((1, tn, tn)), const((1, tn, tn)), const((1, tn, tn)),
            const((1, wc, d)), const((1, tn, d)), const((1, tn, d)),
        ],
        out_specs=pl.BlockSpec((tm, d), lambda i: (i, 0)),
        out_shape=jax.ShapeDtypeStruct((n, d), BF16),
        compiler_params=_cparams("arbitrary"),
        name="branches",
    )(p, p, p, p, p, p, p, p, p, p, ys, yf, conv_w, ssm_d.reshape(ssm_d.shape[0], 1, -1),
      wa, wb, fw, pa, pb, pc)


def _out_kernel(m_ref, x_ref, gt_ref, g_ref, w_ref, o_ref):
    o = jnp.dot(m_ref[...], w_ref[0], preferred_element_type=F32)
    ms = jnp.mean(o * o, axis=-1, keepdims=True)
    y = o * lax.rsqrt(ms + RMS_EPS) * g_ref[0]
    o_ref[...] = x_ref[...] + gt_ref[0] * y


def _outproj(m, x2, gt, g_post, w_out_b, layer, rows_per_batch):
    n, d = x2.shape
    tm = min(512, rows_per_batch)
    tpb = rows_per_batch // tm
    return pl.pallas_call(
        _out_kernel,
        grid=(n // tm,),
        in_specs=[
            pl.BlockSpec((tm, d), lambda i: (i, 0)),
            pl.BlockSpec((tm, d), lambda i: (i, 0)),
            pl.BlockSpec((1, 1, d), lambda i: (i // tpb, 0, 0)),
            pl.BlockSpec((1, 1, d), lambda i: (layer, 0, 0)),
            pl.BlockSpec((1, d, d), lambda i: (layer, 0, 0), pipeline_mode=pl.Buffered(1)),
        ],
        out_specs=pl.BlockSpec((tm, d), lambda i: (i, 0)),
        out_shape=jax.ShapeDtypeStruct((n, d), F32),
        compiler_params=_cparams("arbitrary"),
        name="outproj",
    )(m, x2, gt, g_post.reshape(g_post.shape[0], 1, d), w_out_b)


def kernel(x, c, ctx, c_ctx, w_ada, b_ada, g_pre, g_post, w_in, conv_w, ssm_lam_re, ssm_lam_im, ssm_log_dt,
           ssm_b_re, ssm_b_im, ssm_c_re, ssm_c_im, ssm_d, glu_wa, glu_wb, fourier_w, proj_a, proj_b, proj_c,
           w_out):
    bn, t, d = x.shape
    lc = ctx.shape[1]
    depth = w_ada.shape[0]
    w_conv = conv_w.shape[-1]
    w_ssm = ssm_d.shape[-1]
    tn = 1024
    assert w_conv == tn and w_ssm == tn and fourier_w.shape[-1] == tn and d % tn == 0
    assert bn + 1 <= 8 and t % GRID_W == 0
    u_tile = 4 * w_conv // tn
    f_tile = u_tile + 2
    n_tiles = w_in.shape[-1] // tn
    seg_len = (t // SSM_CHUNK) // SSM_SEGS

    cc = jnp.concatenate([c, c_ctx[None, :], jnp.zeros((8 - bn - 1, d), F32)], axis=0)
    mods = _ada_mods(cc, w_ada, b_ada)

    to_b = lambda w: w.astype(BF16)
    wa_b, wb_b, fw_b = to_b(glu_wa), to_b(glu_wb), to_b(fourier_w)
    pa_b, pb_b, pc_b, wo_b = to_b(proj_a), to_b(proj_b), to_b(proj_c), to_b(w_out)
    fast_dft = t == 256 * 16
    tabs_x = None if fast_dft else _dft_tables(t, tn // FFT_GROUPS)
    tabs_c = _dft_tables(lc, tn // FFT_GROUPS)

    x2 = x.reshape(bn * t, d)
    c2 = ctx.reshape(bn * lc, d)
    for l in range(depth):
        last = l == depth - 1
        sh_x, sc_x, gt_x = (mods[l, :bn, k * d:(k + 1) * d].reshape(bn, 1, d) for k in range(3))
        sh_c, sc_c, gt_c = (jnp.broadcast_to(mods[l, bn, k * d:(k + 1) * d], (bn, 1, d)) for k in range(3))
        ssm_p = _ssm_params(ssm_lam_re[l], ssm_lam_im[l], ssm_log_dt[l], ssm_b_re[l], ssm_b_im[l],
                            ssm_c_re[l], ssm_c_im[l], seg_len)

        px = _inproj(_prenorm(x2, sc_x, sh_x, g_pre, l, t), w_in, l, 0, n_tiles)
        hc = _prenorm(c2, sc_c, sh_c, g_pre, l, lc)
        pc_ = _inproj(hc, w_in, l, u_tile, 1) if last else _inproj(hc, w_in, l, 0, n_tiles)
        u_lane_tile = u_tile * (tn // LANES)
        ys_x, ys_c = _ssm_scan(px, pc_, u_lane_tile, 0 if last else u_lane_tile, w_ssm, ssm_p, bn,
                               ctx_out=not last)

        yf_x = _fourier_fast(px, f_tile, t, bn) if fast_dft else _fourier(px, f_tile, t, bn, tabs_x)
        mx = _branches(px, ys_x, yf_x, conv_w, ssm_d, wa_b, wb_b, fw_b, pa_b, pb_b, pc_b, l, GRID_W)
        new_x2 = _outproj(mx, x2, gt_x, g_post, wo_b, l, t)
        if not last:
            yf_c = _fourier(pc_, f_tile, lc, bn, tabs_c)
            mc = _branches(pc_, ys_c, yf_c, conv_w, ssm_d, wa_b, wb_b, fw_b, pa_b, pb_b, pc_b, l, lc)
            c2 = _outproj(mc, c2, gt_c, g_post, wo_b, l, lc)
        x2 = new_x2
    return x2.reshape(bn, t, d)
```

```python
import functools
import math

import jax
import jax.numpy as jnp
from jax import lax
from jax.experimental import pallas as pl
from jax.experimental.pallas import tpu as pltpu

F32 = jnp.float32
BF16 = jnp.bfloat16

GRID_W = 64
FFT_GROUPS = 4
N_BRANCH = 3
RMS_EPS = 1e-6
LANES = 128
SSM_CHUNK = 8
SSM_SEGS = 8
DFT_ROWS = 64
VMEM_LIMIT = 56 * 1024 * 1024


def _cparams(*sem):
    return pltpu.CompilerParams(dimension_semantics=sem, vmem_limit_bytes=VMEM_LIMIT)


def _sigmoid(v):
    return 1.0 / (1.0 + jnp.exp(-v))


def _silu(v):
    return v * _sigmoid(v)


def _gelu_tanh(v):
    return v * (0.5 * (1.0 + jnp.tanh(math.sqrt(2.0 / math.pi) * (v + 0.044715 * (v * v * v)))))


def _cmul(ar, ai, br, bi):
    return ar * br - ai * bi, ar * bi + ai * br


def _cmul_add(ar, ai, zr, zi, sr, si):
    return ar * zr - ai * zi + sr, ar * zi + ai * zr + si


def _ada_kernel(c_ref, w_ref, b_ref, o_ref):
    c = c_ref[...]
    s = _silu(c).astype(BF16)
    o_ref[0] = jnp.dot(s, w_ref[0].astype(BF16), preferred_element_type=F32) + b_ref[0]


def _ada_mods(cc, w_ada, b_ada):
    depth, d, n3 = w_ada.shape
    tn = min(1024, n3)
    return pl.pallas_call(
        _ada_kernel,
        grid=(depth, n3 // tn),
        in_specs=[
            pl.BlockSpec((8, d), lambda l, j: (0, 0)),
            pl.BlockSpec((1, d, tn), lambda l, j: (l, 0, j)),
            pl.BlockSpec((1, 1, tn), lambda l, j: (l, 0, j)),
        ],
        out_specs=pl.BlockSpec((1, 8, tn), lambda l, j: (l, 0, j)),
        out_shape=jax.ShapeDtypeStruct((depth, 8, n3), F32),
        compiler_params=_cparams("arbitrary", "arbitrary"),
        name="ada_mods",
    )(cc, w_ada, b_ada.reshape(depth, 1, n3))


def _modnorm(x, g, sc, sh):
    ms = jnp.mean(x * x, axis=-1, keepdims=True)
    return (x * lax.rsqrt(ms + RMS_EPS) * g) * (1.0 + sc) + sh


def _prenorm_kernel(x_ref, sc_ref, sh_ref, g_ref, h_ref):
    rb = 16

    def body(r, carry):
        rows = pl.ds(pl.multiple_of(r * rb, rb), rb)
        h_ref[rows, :] = _modnorm(x_ref[rows, :], g_ref[0], sc_ref[0], sh_ref[0]).astype(BF16)
        return carry

    lax.fori_loop(0, x_ref.shape[0] // rb, body, 0)


def _prenorm(x2, sc, sh, g_pre, layer, rows_per_batch):
    n, d = x2.shape
    tm = min(512, rows_per_batch)
    tpb = rows_per_batch // tm
    return pl.pallas_call(
        _prenorm_kernel,
        grid=(n // tm,),
        in_specs=[
            pl.BlockSpec((tm, d), lambda i: (i, 0)),
            pl.BlockSpec((1, 1, d), lambda i: (i // tpb, 0, 0)),
            pl.BlockSpec((1, 1, d), lambda i: (i // tpb, 0, 0)),
            pl.BlockSpec((1, 1, d), lambda i: (layer, 0, 0)),
        ],
        out_specs=pl.BlockSpec((tm, d), lambda i: (i, 0)),
        out_shape=jax.ShapeDtypeStruct((n, d), BF16),
        compiler_params=_cparams("arbitrary"),
        name="prenorm",
    )(x2, sc, sh, g_pre.reshape(g_pre.shape[0], 1, d))


def _inproj_kernel(h_ref, w_ref, p_ref, wb):
    @pl.when(pl.program_id(1) == 0)
    def _():
        wb[...] = w_ref[0].astype(BF16)

    p_ref[...] = jnp.dot(h_ref[...], wb[...], preferred_element_type=F32).astype(BF16)


def _inproj(h, w_in, layer, tile0, ntiles):
    n, d = h.shape
    tn = 1024
    tm = min(1024, n)
    return pl.pallas_call(
        _inproj_kernel,
        grid=(ntiles, n // tm),
        in_specs=[
            pl.BlockSpec((tm, d), lambda j, i: (i, 0)),
            pl.BlockSpec((1, d, tn), lambda j, i: (layer, 0, tile0 + j)),
        ],
        out_specs=pl.BlockSpec((tm, tn), lambda j, i: (i, j)),
        out_shape=jax.ShapeDtypeStruct((n, ntiles * tn), BF16),
        scratch_shapes=[pltpu.VMEM((d, tn), BF16)],
        compiler_params=_cparams("arbitrary", "arbitrary"),
        name="inproj",
    )(h, w_in)


def _ssm_params(lam_re, lam_im, log_dt, b_re, b_im, c_re, c_im, seg_len):
    L = SSM_CHUNK
    _, G, P = lam_re.shape
    H = b_re.shape[-1]
    gpt = LANES // H
    O = G // gpt
    lam_re = lam_re.astype(F32)
    lam_im = lam_im.astype(F32)
    dt = jnp.exp(log_dt.astype(F32))[..., None]
    lr = lam_re * dt
    li = lam_im * dt
    mag = jnp.exp(lr)
    a_re = mag * jnp.cos(li)
    a_im = mag * jnp.sin(li)
    n_re = a_re - 1.0
    n_im = a_im
    den = lam_re * lam_re + lam_im * lam_im
    q_re = (n_re * lam_re + n_im * lam_im) / den
    q_im = (n_im * lam_re - n_re * lam_im) / den
    bb_re = q_re[..., None] * b_re - q_im[..., None] * b_im
    bb_im = q_re[..., None] * b_im + q_im[..., None] * b_re

    bbs = jnp.stack([bb_re, bb_im], axis=1).reshape(2, 2, O, gpt, P, H)
    bc = jnp.transpose(bbs, (2, 0, 1, 5, 3, 4)).reshape(O, 4, H, gpt * P)
    cs = jnp.stack([c_re.astype(F32), c_im.astype(F32)], axis=1).reshape(2, 2, O, gpt, H, P)
    cct = jnp.transpose(cs, (2, 0, 1, 3, 5, 4)).reshape(O, 4, gpt * P, 1, H)
    cct = jnp.broadcast_to(cct, (O, 4, gpt * P, gpt, H)).reshape(O, 4, gpt * P, LANES)
    a4 = jnp.transpose(jnp.stack([a_re, a_im], axis=1).reshape(2, 2, O, gpt * P), (2, 0, 1, 3)).reshape(O, 4, gpt * P)
    arow = jnp.concatenate([a4, jnp.zeros_like(a4)], axis=1)
    acol = jnp.broadcast_to(a4[..., None], (O, 4, gpt * P, LANES))

    def apower(k):
        k = jnp.asarray(k, F32).reshape((-1, 1, 1, 1))
        m = jnp.exp(k * lr)
        return m * jnp.cos(k * li), m * jnp.sin(k * li)

    def cols(re, im):
        k = re.shape[0]
        z = jnp.stack([re, im], axis=2).reshape(k, 2, 2, O, gpt * P)
        return jnp.transpose(z, (3, 0, 1, 2, 4)).reshape(O, k, 4 * gpt * P)

    ap = cols(*apower(jnp.array([L, L * seg_len])))
    apow = jnp.concatenate([ap, jnp.zeros((O, 6, ap.shape[-1]), F32)], axis=1)
    ii = jnp.arange(seg_len)
    pf_re, pf_im = apower(L * ii)
    pb_re, pb_im = apower(L * (seg_len - 1 - ii))
    pw = cols(jnp.stack([pf_re[:, 0], pb_re[:, 1]], axis=1), jnp.stack([pf_im[:, 0], pb_im[:, 1]], axis=1))
    return bc, cct, arow, acol, apow, pw


def _ssm_build_operators(bc_ref, cct_ref, arow_ref, acol_ref, ws, ws_lo, w3):
    L = SSM_CHUNK
    lk = L * LANES
    hh = bc_ref.shape[2]
    sw = bc_ref.shape[3]
    gpt = LANES // hh
    pp = sw // gpt

    same_b = (lax.broadcasted_iota(jnp.int32, (LANES, sw), 0) // hh
              == lax.broadcasted_iota(jnp.int32, (LANES, sw), 1) // pp)
    same_c = (lax.broadcasted_iota(jnp.int32, (sw, LANES), 0) // pp
              == lax.broadcasted_iota(jnp.int32, (sw, LANES), 1) // hh)

    def powers(re, im, n):
        out = [(jnp.ones_like(re), jnp.zeros_like(re))]
        for _ in range(n):
            out.append(_cmul(out[-1][0], out[-1][1], re, im))
        return out

    def split(v):
        hi = v.astype(BF16)
        return hi, (v - hi.astype(F32)).astype(BF16)

    taps = []
    for d in range(2):
        b_re = jnp.where(same_b, jnp.concatenate([bc_ref[0, 2 * d]] * gpt, axis=0), 0.0)
        b_im = jnp.where(same_b, jnp.concatenate([bc_ref[0, 2 * d + 1]] * gpt, axis=0), 0.0)
        c_re = jnp.where(same_c, cct_ref[0, 2 * d], 0.0)
        c_im = jnp.where(same_c, cct_ref[0, 2 * d + 1], 0.0)
        prow = powers(arow_ref[0, 2 * d:2 * d + 1, :], arow_ref[0, 2 * d + 1:2 * d + 2, :], L - 1)
        pcol = powers(acol_ref[0, 2 * d], acol_ref[0, 2 * d + 1], L)
        dcols = slice((2 * d) * sw, (2 * d + 2) * sw)
        for k in range(L):
            ab_re, ab_im = _cmul(b_re, b_im, prow[k][0], prow[k][1])
            j = L - 1 - k if d == 0 else k
            rows = slice(j * LANES, (j + 1) * LANES)
            for part, v in ((2 * d, ab_re), (2 * d + 1, ab_im)):
                hi, lo = split(v)
                ws[rows, part * sw:(part + 1) * sw] = hi
                ws_lo[rows, part * sw:(part + 1) * sw] = lo
        (cr_hi, cr_lo), (ci_hi, ci_lo) = split(c_re), split(-c_im)
        c_hi = jnp.concatenate([cr_hi, ci_hi], axis=0)
        c_lo = jnp.concatenate([cr_lo, ci_lo], axis=0)
        t = jnp.dot(jnp.concatenate([ws[:, dcols], ws[:, dcols], ws_lo[:, dcols]], axis=1),
                    jnp.concatenate([c_hi, c_lo, c_hi], axis=0), preferred_element_type=F32)
        taps.append([t[(L - 1 - k if d == 0 else k) * LANES:(L - k if d == 0 else k + 1) * LANES, :]
                     for k in range(L)])
        for jo in range(L):
            e = jo + 1 if d == 0 else L - jo
            ca_re, ca_im = _cmul(c_re, c_im, pcol[e][0], pcol[e][1])
            r0 = lk + (2 * d) * sw
            w3[r0:r0 + sw, jo * LANES:(jo + 1) * LANES] = ca_re.astype(BF16)
            w3[r0 + sw:r0 + 2 * sw, jo * LANES:(jo + 1) * LANES] = (-ca_im).astype(BF16)
    for j in range(L):
        for jo in range(L):
            lag = jo - j
            t = taps[0][lag] if lag > 0 else taps[1][-lag] if lag < 0 else taps[0][0] + taps[1][0]
            w3[j * LANES:(j + 1) * LANES, jo * LANES:(jo + 1) * LANES] = t.astype(BF16)


def _seg_pitch(seg_len):
    tiles = seg_len // 8 + 1
    return 8 * (tiles + 1 - tiles % 2)


def _ssm_kernel(*refs, nc, nx, seg_len, ctx_out):
    if ctx_out:
        (ux_ref, uc_ref, bc_ref, cct_ref, arow_ref, acol_ref, apow_ref, pw_ref,
         yx_ref, yc_ref, ws, ws_lo, w3, lhs, st, uxf, ucf) = refs
    else:
        (ux_ref, uc_ref, bc_ref, cct_ref, arow_ref, acol_ref, apow_ref, pw_ref,
         yx_ref, ws, ws_lo, w3, lhs, st, uxf, ucf) = refs
        yc_ref = None
    L = SSM_CHUNK
    lk = L * LANES
    nlt = st.shape[0] // 4
    sw = nlt * LANES
    FR, FI, BR, BI = range(4)
    nseg = SSM_SEGS
    pitch = _seg_pitch(seg_len)

    @pl.when(pl.program_id(1) == 0)
    def _():
        _ssm_build_operators(bc_ref, cct_ref, arow_ref, acol_ref, ws, ws_lo, w3)

    def ld(part, rows):
        return jnp.concatenate([st[part * nlt + q, rows, :] for q in range(nlt)], axis=1)

    def sto(part, rows, val):
        for q in range(nlt):
            st[part * nlt + q, rows, :] = val[:, q * LANES:(q + 1) * LANES]

    def part_cols(part):
        return slice(part * sw, (part + 1) * sw)

    ucf[...] = uc_ref[...].astype(F32)
    uxf[...] = ux_ref[...].astype(F32)
    for j in range(L):
        lhs[0:nc, j * LANES:(j + 1) * LANES] = ucf[pl.ds(j, nc, stride=L), :].astype(BF16)
        lhs[nc:nc + nx, j * LANES:(j + 1) * LANES] = uxf[pl.ds(j, nx, stride=L), :].astype(BF16)

    s = jnp.dot(lhs[:, 0:lk], ws[...], preferred_element_type=F32)
    for k in range(4 * nlt):
        st[k, 0:nc, :] = s[0:nc, k * LANES:(k + 1) * LANES]
        for g in range(nseg):
            st[k, nc + g * pitch:nc + g * pitch + seg_len, :] = (
                s[nc + g * seg_len:nc + (g + 1) * seg_len, k * LANES:(k + 1) * LANES])

    a_l = apow_ref[0, 0:1, :]
    a_seg = apow_ref[0, 1:2, :]
    al = [a_l[:, part_cols(p)] for p in range(4)]
    aseg = [a_seg[:, part_cols(p)] for p in range(4)]

    zero = jnp.zeros((1, sw), F32)

    def ctx_step(pr, pi):
        def step(i, carry):
            er, ei = carry
            row = pl.ds(i, 1)
            sr, si = ld(pr, row), ld(pi, row)
            sto(pr, row, er)
            sto(pi, row, ei)
            return _cmul_add(al[pr], al[pi], er, ei, sr, si)
        return step

    h0f = lax.fori_loop(0, nc, ctx_step(FR, FI), (zero, zero))
    bwd_step = ctx_step(BR, BI)
    h0b = lax.fori_loop(0, nc, lambda k, carry: bwd_step(nc - 1 - k, carry), (zero, zero))
    for k in range(4 * nlt):
        lhs[0:nc, lk + k * LANES:lk + (k + 1) * LANES] = st[k, 0:nc, :].astype(BF16)

    zseg = jnp.zeros((nseg, sw), F32)

    def seg_pass(pr, pi, order):
        zr, zi = zseg, zseg
        for i in order:
            rows = pl.ds(nc + i, nseg, stride=pitch)
            sr, si = ld(pr, rows), ld(pi, rows)
            sto(pr, rows, zr)
            sto(pi, rows, zi)
            zr, zi = _cmul_add(al[pr], al[pi], zr, zi, sr, si)
        return zr, zi

    zfr, zfi = seg_pass(FR, FI, range(seg_len))
    zbr, zbi = seg_pass(BR, BI, range(seg_len - 1, -1, -1))

    ef = [h0f]
    for s in range(nseg - 1):
        ef.append(_cmul_add(aseg[FR], aseg[FI], ef[s][0], ef[s][1], zfr[s:s + 1], zfi[s:s + 1]))
    eb = [None] * nseg
    eb[nseg - 1] = h0b
    for s in range(nseg - 1, 0, -1):
        eb[s - 1] = _cmul_add(aseg[BR], aseg[BI], eb[s][0], eb[s][1], zbr[s:s + 1], zbi[s:s + 1])
    for s in range(nseg):
        srows = slice(nc + s * pitch, nc + s * pitch + seg_len)
        rows = slice(nc + s * seg_len, nc + (s + 1) * seg_len)
        for (pr, pi, e) in ((FR, FI, ef[s]), (BR, BI, eb[s])):
            xr, xi = _cmul_add(pw_ref[0, :, part_cols(pr)], pw_ref[0, :, part_cols(pi)], e[0], e[1],
                               ld(pr, srows), ld(pi, srows))
            lhs[rows, lk + pr * sw:lk + (pr + 1) * sw] = xr.astype(BF16)
            lhs[rows, lk + pi * sw:lk + (pi + 1) * sw] = xi.astype(BF16)

    y = jnp.dot(lhs[...], w3[...], preferred_element_type=F32)
    for j in range(L):
        if ctx_out:
            yc_ref[pl.ds(j, nc, stride=L), :] = y[0:nc, j * LANES:(j + 1) * LANES]
        yx_ref[pl.ds(j, nx, stride=L), :] = y[nc:nc + nx, j * LANES:(j + 1) * LANES]


def _ssm_scan(px, pc, col_x, col_c, w, params, batch, ctx_out):
    bc, cct, arow, acol, apow, pw = params
    L = SSM_CHUNK
    t = px.shape[0] // batch
    lc = pc.shape[0] // batch
    nx, nc = t // L, lc // L
    seg_len = nx // SSM_SEGS
    o = w // LANES
    lk = L * LANES
    sw = bc.shape[-1]
    sdim = 4 * sw
    out_shape = [jax.ShapeDtypeStruct((batch * t, w), F32)]
    out_specs = [pl.BlockSpec((t, LANES), lambda oi, b: (b, oi))]
    if ctx_out:
        out_shape.append(jax.ShapeDtypeStruct((batch * lc, w), F32))
        out_specs.append(pl.BlockSpec((lc, LANES), lambda oi, b: (b, oi)))
    tile = lambda a: pl.BlockSpec((1,) + a.shape[1:], lambda oi, b: (oi,) + (0,) * (a.ndim - 1))
    res = pl.pallas_call(
        functools.partial(_ssm_kernel, nc=nc, nx=nx, seg_len=seg_len, ctx_out=ctx_out),
        grid=(o, batch),
        in_specs=[
            pl.BlockSpec((t, LANES), lambda oi, b: (b, col_x + oi)),
            pl.BlockSpec((lc, LANES), lambda oi, b: (b, col_c + oi)),
            tile(bc), tile(cct), tile(arow), tile(acol), tile(apow), tile(pw),
        ],
        out_specs=out_specs,
        out_shape=out_shape,
        scratch_shapes=[
            pltpu.VMEM((lk, sdim), BF16),
            pltpu.VMEM((lk, sdim), BF16),
            pltpu.VMEM((lk + sdim, lk), BF16),
            pltpu.VMEM((nc + nx, lk + sdim), BF16),
            pltpu.VMEM((sdim // LANES, nc + SSM_SEGS * _seg_pitch(seg_len), LANES), F32),
            pltpu.VMEM((t, LANES), F32),
            pltpu.VMEM((lc, LANES), F32),
        ],
        compiler_params=_cparams("arbitrary", "arbitrary"),
        name="ssm_scan",
    )(px, pc, bc, cct, arow, acol, apow, pw)
    return (res[0], res[1]) if ctx_out else (res[0], None)


def _dft_table_kernel(e1_ref, e2_ref, o_ref, *, t):
    e1c, e1s = e1_ref[0, :, 0:t], e1_ref[0, :, t:2 * t]
    e2c, e2s = e2_ref[:, 0:t], e2_ref[:, t:2 * t]
    c, s = _cmul(e2c, e2s, e1c, e1s)
    o_ref[:, 0:t] = c.astype(BF16)
    o_ref[:, t:2 * t] = (-s).astype(BF16)


def _dft_tables(t, gw):
    def cis(rows_mult, nrows):
        k = jnp.arange(nrows, dtype=jnp.int32)[:, None] * rows_mult
        n = jnp.arange(t, dtype=jnp.int32)[None, :]
        ang = ((k * n) % t).astype(F32) * (2.0 * math.pi / t)
        return jnp.concatenate([jnp.cos(ang), jnp.sin(ang)], axis=1)
    r = DFT_ROWS
    e1 = cis(r, t // r).reshape(t // r, 1, 2 * t)
    e2 = cis(1, r)
    tab_t = pl.pallas_call(
        functools.partial(_dft_table_kernel, t=t),
        grid=(t // r,),
        in_specs=[
            pl.BlockSpec((1, 1, 2 * t), lambda i: (i, 0, 0)),
            pl.BlockSpec((r, 2 * t), lambda i: (0, 0)),
        ],
        out_specs=pl.BlockSpec((r, 2 * t), lambda i: (i, 0)),
        out_shape=jax.ShapeDtypeStruct((t, 2 * t), BF16),
        compiler_params=_cparams("arbitrary"),
        name="dft_table",
    )(e1, e2)
    kc = jnp.arange(gw, dtype=jnp.int32)
    ang = ((kc[:, None] * kc[None, :]) % gw).astype(F32) * (2.0 * math.pi / gw)
    tab_c = jnp.concatenate([jnp.cos(ang), jnp.sin(ang)], axis=1).astype(BF16)
    return tab_t, tab_c


def _fft_kernel(f_ref, cs_ref, tab_ref, o_ref, data, *, t, gw, groups, scale):
    i = pl.program_id(1)

    @pl.when(i == 0)
    def _():
        rb = min(512, t)
        for r in range(t // rb):
            for g in range(groups):
                fg = f_ref[r * rb:(r + 1) * rb, g * gw:(g + 1) * gw]
                z = jnp.dot(fg, cs_ref[...], preferred_element_type=F32)
                data[r * rb:(r + 1) * rb, g * gw:(g + 1) * gw] = z[:, :gw].astype(BF16)
                data[t + r * rb:t + (r + 1) * rb, g * gw:(g + 1) * gw] = z[:, gw:].astype(BF16)

    y = jnp.dot(tab_ref[...], data[...], preferred_element_type=F32)
    o_ref[...] = (y * scale).astype(BF16)


def _fourier(p, col_tile, t, batch, tabs):
    tab_t, tab_c = tabs
    wf = 1024
    gw = wf // FFT_GROUPS
    tm = min(256, t)
    return pl.pallas_call(
        functools.partial(_fft_kernel, t=t, gw=gw, groups=FFT_GROUPS, scale=1.0 / math.sqrt(t * gw)),
        grid=(batch, t // tm),
        in_specs=[
            pl.BlockSpec((t, wf), lambda b, i: (b, col_tile)),
            pl.BlockSpec((gw, 2 * gw), lambda b, i: (0, 0)),
            pl.BlockSpec((tm, 2 * t), lambda b, i: (i, 0)),
        ],
        out_specs=pl.BlockSpec((tm, wf), lambda b, i: (b * (t // tm) + i, 0)),
        out_shape=jax.ShapeDtypeStruct((batch * t, wf), BF16),
        scratch_shapes=[pltpu.VMEM((2 * t, wf), BF16)],
        compiler_params=_cparams("arbitrary", "arbitrary"),
        name="fourier",
    )(p, tab_c, tab_t)


def _cmul_const(xr, xi, c, s):
    def scaled(v, k):
        if abs(k) < 1e-12:
            return None
        return v if abs(k - 1.0) < 1e-12 else -v if abs(k + 1.0) < 1e-12 else v * k

    def add(a, b):
        return b if a is None else a if b is None else a + b

    return add(scaled(xr, c), scaled(xi, -s)), add(scaled(xi, c), scaled(xr, s))


def _dft4(y):
    (ar, ai), (br, bi), (cr, ci), (dr, di) = y
    sr, si, tr, ti = ar + cr, ai + ci, ar - cr, ai - ci
    ur, ui, vr, vi = br + dr, bi + di, br - dr, bi - di
    return [(sr + ur, si + ui), (tr + vi, ti - vr), (sr - ur, si - ui), (tr - vi, ti + vr)]


def _dft16(x):
    out = [None] * 16
    p = [_dft4([x[4 * a + b] for a in range(4)]) for b in range(4)]
    for ka in range(4):
        q = []
        for b in range(4):
            ang = -2.0 * math.pi * ka * b / 16.0
            q.append(_cmul_const(p[b][ka][0], p[b][ka][1], math.cos(ang), math.sin(ang)))
        r = _dft4(q)
        for kb in range(4):
            out[ka + 4 * kb] = r[kb]
    return out


def _fft_fast_kernel(f_ref, csc_ref, tw_ref, cst_ref, o_ref, z, o_scr, *, n1, n2, gw, scale):
    slabs_per_dot = 4
    for q in range(n2 // slabs_per_dot):
        r0 = q * slabs_per_dot * n1
        zz = jnp.dot(f_ref[r0:r0 + slabs_per_dot * n1, :], csc_ref[...], preferred_element_type=F32)
        for s in range(slabs_per_dot):
            z[0, q * slabs_per_dot + s] = zz[s * n1:(s + 1) * n1, 0:gw]
            z[1, q * slabs_per_dot + s] = zz[s * n1:(s + 1) * n1, gw:2 * gw]

    def tile_body(r, carry):
        rows = pl.ds(pl.multiple_of(r * 8, 8), 8)
        for c in range(gw // LANES):
            cols = slice(c * LANES, (c + 1) * LANES)
            h = _dft16([(z[0, t, rows, cols], z[1, t, rows, cols]) for t in range(n2)])
            for k2 in range(n2):
                hr, hi = h[k2]
                if k2 > 0:
                    hr, hi = _cmul(hr, hi, tw_ref[0, k2, rows, :], tw_ref[1, k2, rows, :])
                z[0, k2, rows, cols] = hr
                z[1, k2, rows, cols] = hi
        return carry

    lax.fori_loop(0, n1 // 8, tile_body, 0)

    for k2 in range(n2):
        rhs = jnp.concatenate([z[0, k2], z[1, k2]], axis=0).astype(BF16)
        y = jnp.dot(cst_ref[...], rhs, preferred_element_type=F32) * scale
        for c in range(gw // LANES):
            o_scr[c, pl.ds(k2, n1, stride=n2), :] = y[:, c * LANES:(c + 1) * LANES]
    o_ref[...] = jnp.concatenate([o_scr[c] for c in range(gw // LANES)], axis=1).astype(BF16)


def _fourier_fast(p, col_tile, t, batch):
    n1, n2 = 256, 16
    assert t == n1 * n2
    wf = 1024
    gw = wf // FFT_GROUPS
    ang = lambda a, b, n: ((a[:, None] * b[None, :]) % n).astype(F32) * (2.0 * math.pi / n)
    kc = jnp.arange(gw, dtype=jnp.int32)
    a_c = ang(kc, kc, gw)
    csc = jnp.concatenate([jnp.cos(a_c), -jnp.sin(a_c)], axis=1).astype(BF16)
    k1 = jnp.arange(n1, dtype=jnp.int32)
    a_t = ang(k1, k1, n1)
    cst = jnp.concatenate([jnp.cos(a_t), jnp.sin(a_t)], axis=1).astype(BF16)
    a_w = ang(jnp.arange(n2, dtype=jnp.int32), k1, t)
    tw = jnp.stack([jnp.cos(a_w), -jnp.sin(a_w)], axis=0)
    tw = jnp.broadcast_to(tw[..., None], (2, n2, n1, LANES))
    ngrp = wf // gw
    return pl.pallas_call(
        functools.partial(_fft_fast_kernel, n1=n1, n2=n2, gw=gw, scale=1.0 / math.sqrt(t * gw)),
        grid=(batch, ngrp),
        in_specs=[
            pl.BlockSpec((t, gw), lambda b, g: (b, col_tile * ngrp + g)),
            pl.BlockSpec((gw, 2 * gw), lambda b, g: (0, 0)),
            pl.BlockSpec((2, n2, n1, LANES), lambda b, g: (0, 0, 0, 0), pipeline_mode=pl.Buffered(1)),
            pl.BlockSpec((n1, 2 * n1), lambda b, g: (0, 0)),
        ],
        out_specs=pl.BlockSpec((t, gw), lambda b, g: (b, g)),
        out_shape=jax.ShapeDtypeStruct((batch * t, wf), BF16),
        scratch_shapes=[
            pltpu.VMEM((2, n2, n1, gw), F32),
            pltpu.VMEM((gw // LANES, t, LANES), F32),
        ],
        compiler_params=_cparams("arbitrary", "arbitrary"),
        name="fourier_fast",
    )(p, csc, tw, cst)


def _branch_kernel(xa_ref, ba_ref, ca_ref, za_ref, zb_ref, zc_ref, g0_ref, g1_ref, g2_ref,
                   u_ref, ys_ref, yf_ref, cw_ref, sd_ref, wa_ref, wb_ref, fw_ref,
                   pa_ref, pb_ref, pc_ref, m_ref, *, row_len):
    tm = xa_ref.shape[0]
    f = lambda r: r[...].astype(F32)
    v = f(ca_ref) * f(xa_ref)
    pos = lax.broadcasted_iota(jnp.int32, v.shape, 0) % row_len
    v_prev = jnp.where(pos == 0, 0.0, pltpu.roll(v, 1, 0))
    v_next = jnp.where(pos == row_len - 1, 0.0, pltpu.roll(v, tm - 1, 0))
    cw = cw_ref[0]
    conv = v_prev * cw[0:1, :] + v * cw[1:2, :] + v_next * cw[2:3, :]
    a = f(ba_ref) * conv * _silu(f(za_ref))
    ya = jnp.dot(a.astype(BF16), pa_ref[0], preferred_element_type=F32)
    acc = _sigmoid(f(g0_ref)) * ya
    y = _gelu_tanh(ys_ref[...] + sd_ref[0] * f(u_ref)).astype(BF16)
    glu = (jnp.dot(y, wa_ref[0], preferred_element_type=F32)
           * _sigmoid(jnp.dot(y, wb_ref[0], preferred_element_type=F32)) * _silu(f(zb_ref)))
    yb = jnp.dot(glu.astype(BF16), pb_ref[0], preferred_element_type=F32)
    acc = acc + _sigmoid(f(g1_ref)) * yb
    c = jnp.dot(yf_ref[...], fw_ref[0], preferred_element_type=F32) * _silu(f(zc_ref))
    yc = jnp.dot(c.astype(BF16), pc_ref[0], preferred_element_type=F32)
    acc = acc + _sigmoid(f(g2_ref)) * yc
    m_ref[...] = acc.astype(BF16)


def _branches(p, ys, yf, conv_w, ssm_d, wa, wb, fw, pa, pb, pc, layer, row_len):
    n = p.shape[0]
    wc = conv_w.shape[-1]
    d = pa.shape[-1]
    tm = 256
    tn = 1024
    col = lambda k: pl.BlockSpec((tm, tn), lambda i: (i, k))
    const = lambda shape: pl.BlockSpec(shape, lambda i: (layer,) + (0,) * (len(shape) - 1),
                                       pipeline_mode=pl.Buffered(1))
    g_tile0 = 8
    return pl.pallas_call(
        functools.partial(_branch_kernel, row_len=row_len),
        grid=(n // tm,),
        in_specs=[
            col(0), col(1), col(2), col(3), col(5), col(7),
            pl.BlockSpec((tm, d), lambda i: (i, g_tile0 * tn // d)),
            pl.BlockSpec((tm, d), lambda i: (i, g_tile0 * tn // d + 1)),
            pl.BlockSpec((tm, d), lambda i: (i, g_tile0 * tn // d + 2)),
            col(4),
            pl.BlockSpec((tm, tn), lambda i: (i, 0)),
            pl.BlockSpec((tm, tn), lambda i: (i, 0)),
            const((1, 3, wc)), const((1, 1, tn)),
            const((1, tn, tn)), const((1, tn, tn)), const((1, tn, tn)),
            const((1, wc, d)), const((1, tn, d)), const((1, tn, d)),
        ],
        out_specs=pl.BlockSpec((tm, d), lambda i: (i, 0)),
        out_shape=jax.ShapeDtypeStruct((n, d), BF16),
        compiler_params=_cparams("arbitrary"),
        name="branches",
    )(p, p, p, p, p, p, p, p, p, p, ys, yf, conv_w, ssm_d.reshape(ssm_d.shape[0], 1, -1),
      wa, wb, fw, pa, pb, pc)


def _out_kernel(*refs, with_next):
    if with_next:
        m_ref, x_ref, gt_ref, g_ref, w_ref, scn_ref, shn_ref, gn_ref, o_ref, h_ref = refs
    else:
        m_ref, x_ref, gt_ref, g_ref, w_ref, o_ref = refs
    o = jnp.dot(m_ref[...], w_ref[0], preferred_element_type=F32)
    ms = jnp.mean(o * o, axis=-1, keepdims=True)
    y = o * lax.rsqrt(ms + RMS_EPS) * g_ref[0]
    xn = x_ref[...] + gt_ref[0] * y
    o_ref[...] = xn
    if with_next:
        h_ref[...] = _modnorm(xn, gn_ref[0], scn_ref[0], shn_ref[0]).astype(BF16)


def _outproj(m, x2, gt, g_post, w_out_b, layer, rows_per_batch, next_mod=None):
    n, d = x2.shape
    tm = min(512, rows_per_batch)
    tpb = rows_per_batch // tm
    per_batch = pl.BlockSpec((1, 1, d), lambda i: (i // tpb, 0, 0))
    in_specs = [
        pl.BlockSpec((tm, d), lambda i: (i, 0)),
        pl.BlockSpec((tm, d), lambda i: (i, 0)),
        per_batch,
        pl.BlockSpec((1, 1, d), lambda i: (layer, 0, 0)),
        pl.BlockSpec((1, d, d), lambda i: (layer, 0, 0), pipeline_mode=pl.Buffered(1)),
    ]
    args = [m, x2, gt, g_post.reshape(g_post.shape[0], 1, d), w_out_b]
    out_specs = [pl.BlockSpec((tm, d), lambda i: (i, 0))]
    out_shape = [jax.ShapeDtypeStruct((n, d), F32)]
    if next_mod is not None:
        sc_n, sh_n, g_pre = next_mod
        in_specs += [per_batch, per_batch, pl.BlockSpec((1, 1, d), lambda i: (layer + 1, 0, 0))]
        args += [sc_n, sh_n, g_pre.reshape(g_pre.shape[0], 1, d)]
        out_specs.append(pl.BlockSpec((tm, d), lambda i: (i, 0)))
        out_shape.append(jax.ShapeDtypeStruct((n, d), BF16))
    res = pl.pallas_call(
        functools.partial(_out_kernel, with_next=next_mod is not None),
        grid=(n // tm,),
        in_specs=in_specs,
        out_specs=out_specs,
        out_shape=out_shape,
        compiler_params=_cparams("arbitrary"),
        name="outproj",
    )(*args)
    return (res[0], res[1]) if next_mod is not None else (res[0], None)


def kernel(x, c, ctx, c_ctx, w_ada, b_ada, g_pre, g_post, w_in, conv_w, ssm_lam_re, ssm_lam_im, ssm_log_dt,
           ssm_b_re, ssm_b_im, ssm_c_re, ssm_c_im, ssm_d, glu_wa, glu_wb, fourier_w, proj_a, proj_b, proj_c,
           w_out):
    bn, t, d = x.shape
    lc = ctx.shape[1]
    depth = w_ada.shape[0]
    w_conv = conv_w.shape[-1]
    w_ssm = ssm_d.shape[-1]
    tn = 1024
    assert w_conv == tn and w_ssm == tn and fourier_w.shape[-1] == tn and d % tn == 0
    assert bn + 1 <= 8 and t % GRID_W == 0
    u_tile = 4 * w_conv // tn
    f_tile = u_tile + 2
    n_tiles = w_in.shape[-1] // tn
    seg_len = (t // SSM_CHUNK) // SSM_SEGS

    cc = jnp.concatenate([c, c_ctx[None, :], jnp.zeros((8 - bn - 1, d), F32)], axis=0)
    mods = _ada_mods(cc, w_ada, b_ada)

    to_b = lambda w: w.astype(BF16)
    wa_b, wb_b, fw_b = to_b(glu_wa), to_b(glu_wb), to_b(fourier_w)
    pa_b, pb_b, pc_b, wo_b = to_b(proj_a), to_b(proj_b), to_b(proj_c), to_b(w_out)
    fast_dft = t == 256 * 16
    tabs_x = None if fast_dft else _dft_tables(t, tn // FFT_GROUPS)
    tabs_c = _dft_tables(lc, tn // FFT_GROUPS)

    x2 = x.reshape(bn * t, d)
    c2 = ctx.reshape(bn * lc, d)
    def mod_x(l):
        return tuple(mods[l, :bn, k * d:(k + 1) * d].reshape(bn, 1, d) for k in range(3))

    def mod_c(l):
        return tuple(jnp.broadcast_to(mods[l, bn, k * d:(k + 1) * d], (bn, 1, d)) for k in range(3))

    hx = _prenorm(x2, mod_x(0)[1], mod_x(0)[0], g_pre, 0, t)
    hc = _prenorm(c2, mod_c(0)[1], mod_c(0)[0], g_pre, 0, lc)
    for l in range(depth):
        last = l == depth - 1
        sh_x, sc_x, gt_x = mod_x(l)
        sh_c, sc_c, gt_c = mod_c(l)
        next_x = None if last else (mod_x(l + 1)[1], mod_x(l + 1)[0], g_pre)
        next_c = None if last else (mod_c(l + 1)[1], mod_c(l + 1)[0], g_pre)
        ssm_p = _ssm_params(ssm_lam_re[l], ssm_lam_im[l], ssm_log_dt[l], ssm_b_re[l], ssm_b_im[l],
                            ssm_c_re[l], ssm_c_im[l], seg_len)

        px = _inproj(hx, w_in, l, 0, n_tiles)
        pc_ = _inproj(hc, w_in, l, u_tile, 1) if last else _inproj(hc, w_in, l, 0, n_tiles)
        u_lane_tile = u_tile * (tn // LANES)
        ys_x, ys_c = _ssm_scan(px, pc_, u_lane_tile, 0 if last else u_lane_tile, w_ssm, ssm_p, bn,
                               ctx_out=not last)

        yf_x = _fourier_fast(px, f_tile, t, bn) if fast_dft else _fourier(px, f_tile, t, bn, tabs_x)
        mx = _branches(px, ys_x, yf_x, conv_w, ssm_d, wa_b, wb_b, fw_b, pa_b, pb_b, pc_b, l, GRID_W)
        new_x2, hx = _outproj(mx, x2, gt_x, g_post, wo_b, l, t, next_x)
        if not last:
            yf_c = _fourier(pc_, f_tile, lc, bn, tabs_c)
            mc = _branches(pc_, ys_c, yf_c, conv_w, ssm_d, wa_b, wb_b, fw_b, pa_b, pb_b, pc_b, l, lc)
            c2, hc = _outproj(mc, c2, gt_c, g_post, wo_b, l, lc, next_c)
        x2 = new_x2
    return x2.reshape(bn, t, d)
```

```python
import functools
import math

import jax
import jax.numpy as jnp
from jax import lax
from jax.experimental import pallas as pl
from jax.experimental.pallas import tpu as pltpu

F32 = jnp.float32
BF16 = jnp.bfloat16

GRID_W = 64
FFT_GROUPS = 4
N_BRANCH = 3
RMS_EPS = 1e-6
LANES = 128
SSM_CHUNK = 8
SSM_SEGS = 8
DFT_ROWS = 64
VMEM_LIMIT = 56 * 1024 * 1024


def _cparams(*sem):
    return pltpu.CompilerParams(dimension_semantics=sem, vmem_limit_bytes=VMEM_LIMIT)


def _sigmoid(v):
    return 1.0 / (1.0 + jnp.exp(-v))


def _silu(v):
    return v * _sigmoid(v)


def _gelu_tanh(v):
    return v * (0.5 * (1.0 + jnp.tanh(math.sqrt(2.0 / math.pi) * (v + 0.044715 * (v * v * v)))))


def _cmul(ar, ai, br, bi):
    return ar * br - ai * bi, ar * bi + ai * br


def _cmul_add(ar, ai, zr, zi, sr, si):
    return ar * zr - ai * zi + sr, ar * zi + ai * zr + si


def _ada_kernel(c_ref, w_ref, b_ref, o_ref):
    c = c_ref[...]
    s = _silu(c).astype(BF16)
    o_ref[0] = jnp.dot(s, w_ref[0].astype(BF16), preferred_element_type=F32) + b_ref[0]


def _ada_mods(cc, w_ada, b_ada):
    depth, d, n3 = w_ada.shape
    tn = min(1024, n3)
    return pl.pallas_call(
        _ada_kernel,
        grid=(depth, n3 // tn),
        in_specs=[
            pl.BlockSpec((8, d), lambda l, j: (0, 0)),
            pl.BlockSpec((1, d, tn), lambda l, j: (l, 0, j)),
            pl.BlockSpec((1, 1, tn), lambda l, j: (l, 0, j)),
        ],
        out_specs=pl.BlockSpec((1, 8, tn), lambda l, j: (l, 0, j)),
        out_shape=jax.ShapeDtypeStruct((depth, 8, n3), F32),
        compiler_params=_cparams("arbitrary", "arbitrary"),
        name="ada_mods",
    )(cc, w_ada, b_ada.reshape(depth, 1, n3))


def _modnorm(x, g, sc, sh):
    ms = jnp.mean(x * x, axis=-1, keepdims=True)
    return (x * lax.rsqrt(ms + RMS_EPS) * g) * (1.0 + sc) + sh


def _prenorm_kernel(x_ref, sc_ref, sh_ref, g_ref, h_ref):
    rb = 64

    def body(r, carry):
        rows = pl.ds(pl.multiple_of(r * rb, rb), rb)
        h_ref[rows, :] = _modnorm(x_ref[rows, :], g_ref[0], sc_ref[0], sh_ref[0]).astype(BF16)
        return carry

    lax.fori_loop(0, x_ref.shape[0] // rb, body, 0)


def _prenorm(x2, sc, sh, g_pre, layer, rows_per_batch):
    n, d = x2.shape
    tm = min(512, rows_per_batch)
    tpb = rows_per_batch // tm
    return pl.pallas_call(
        _prenorm_kernel,
        grid=(n // tm,),
        in_specs=[
            pl.BlockSpec((tm, d), lambda i: (i, 0)),
            pl.BlockSpec((1, 1, d), lambda i: (i // tpb, 0, 0)),
            pl.BlockSpec((1, 1, d), lambda i: (i // tpb, 0, 0)),
            pl.BlockSpec((1, 1, d), lambda i: (layer, 0, 0)),
        ],
        out_specs=pl.BlockSpec((tm, d), lambda i: (i, 0)),
        out_shape=jax.ShapeDtypeStruct((n, d), BF16),
        compiler_params=_cparams("arbitrary"),
        name="prenorm",
    )(x2, sc, sh, g_pre.reshape(g_pre.shape[0], 1, d))


def _inproj_kernel(h_ref, w_ref, p_ref, wb):
    @pl.when(pl.program_id(1) == 0)
    def _():
        wb[...] = w_ref[0].astype(BF16)

    p_ref[...] = jnp.dot(h_ref[...], wb[...], preferred_element_type=F32).astype(BF16)


def _inproj(h, w_in, layer, tile0, ntiles):
    n, d = h.shape
    tn = 1024
    tm = min(1024, n)
    return pl.pallas_call(
        _inproj_kernel,
        grid=(ntiles, n // tm),
        in_specs=[
            pl.BlockSpec((tm, d), lambda j, i: (i, 0)),
            pl.BlockSpec((1, d, tn), lambda j, i: (layer, 0, tile0 + j)),
        ],
        out_specs=pl.BlockSpec((tm, tn), lambda j, i: (i, j)),
        out_shape=jax.ShapeDtypeStruct((n, ntiles * tn), BF16),
        scratch_shapes=[pltpu.VMEM((d, tn), BF16)],
        compiler_params=_cparams("arbitrary", "arbitrary"),
        name="inproj",
    )(h, w_in)


def _ssm_params(lam_re, lam_im, log_dt, b_re, b_im, c_re, c_im, seg_len):
    L = SSM_CHUNK
    _, G, P = lam_re.shape
    H = b_re.shape[-1]
    gpt = LANES // H
    O = G // gpt
    lam_re = lam_re.astype(F32)
    lam_im = lam_im.astype(F32)
    dt = jnp.exp(log_dt.astype(F32))[..., None]
    lr = lam_re * dt
    li = lam_im * dt
    mag = jnp.exp(lr)
    a_re = mag * jnp.cos(li)
    a_im = mag * jnp.sin(li)
    n_re = a_re - 1.0
    n_im = a_im
    den = lam_re * lam_re + lam_im * lam_im
    q_re = (n_re * lam_re + n_im * lam_im) / den
    q_im = (n_im * lam_re - n_re * lam_im) / den
    bb_re = q_re[..., None] * b_re - q_im[..., None] * b_im
    bb_im = q_re[..., None] * b_im + q_im[..., None] * b_re

    bbs = jnp.stack([bb_re, bb_im], axis=1).reshape(2, 2, O, gpt, P, H)
    bc = jnp.transpose(bbs, (2, 0, 1, 5, 3, 4)).reshape(O, 4, H, gpt * P)
    cs = jnp.stack([c_re.astype(F32), c_im.astype(F32)], axis=1).reshape(2, 2, O, gpt, H, P)
    cct = jnp.transpose(cs, (2, 0, 1, 3, 5, 4)).reshape(O, 4, gpt * P, 1, H)
    cct = jnp.broadcast_to(cct, (O, 4, gpt * P, gpt, H)).reshape(O, 4, gpt * P, LANES)
    a4 = jnp.transpose(jnp.stack([a_re, a_im], axis=1).reshape(2, 2, O, gpt * P), (2, 0, 1, 3)).reshape(O, 4, gpt * P)
    arow = jnp.concatenate([a4, jnp.zeros_like(a4)], axis=1)
    acol = jnp.broadcast_to(a4[..., None], (O, 4, gpt * P, LANES))

    def apower(k):
        k = jnp.asarray(k, F32).reshape((-1, 1, 1, 1))
        m = jnp.exp(k * lr)
        return m * jnp.cos(k * li), m * jnp.sin(k * li)

    def cols(re, im):
        k = re.shape[0]
        z = jnp.stack([re, im], axis=2).reshape(k, 2, 2, O, gpt * P)
        return jnp.transpose(z, (3, 0, 1, 2, 4)).reshape(O, k, 4 * gpt * P)

    ap = cols(*apower(jnp.array([L, L * seg_len])))
    apow = jnp.concatenate([ap, jnp.zeros((O, 6, ap.shape[-1]), F32)], axis=1)
    ii = jnp.arange(seg_len)
    pf_re, pf_im = apower(L * ii)
    pb_re, pb_im = apower(L * (seg_len - 1 - ii))
    pw = cols(jnp.stack([pf_re[:, 0], pb_re[:, 1]], axis=1), jnp.stack([pf_im[:, 0], pb_im[:, 1]], axis=1))
    return bc, cct, arow, acol, apow, pw


def _block_transpose8(v):
    bw = LANES // 8
    blk = lax.broadcasted_iota(jnp.int32, v[0].shape, 1) // bw
    for dist in (4, 2, 1):
        upper = (blk & dist) != 0
        nxt = list(v)
        for a in range(8):
            if a & dist:
                continue
            b = a + dist
            nxt[a] = jnp.where(upper, pltpu.roll(v[b], dist * bw, 1), v[a])
            nxt[b] = jnp.where(upper, v[b], pltpu.roll(v[a], LANES - dist * bw, 1))
        v = nxt
    return v


def _ssm_build_operators(bc_ref, cct_ref, arow_ref, acol_ref, ws, ws_lo, w3):
    L = SSM_CHUNK
    hh = bc_ref.shape[2]
    sw = bc_ref.shape[3]
    pp = sw // (LANES // hh)
    npair = sw // LANES
    assert L * hh == LANES and 2 * pp == LANES
    lane = lax.broadcasted_iota(jnp.int32, (LANES, LANES), 1)
    row = lax.broadcasted_iota(jnp.int32, (LANES, LANES), 0)
    lane_g0, row_g0 = lane < pp, row < pp
    lane_blk = lane // hh
    zeros = jnp.zeros((LANES, LANES), F32)
    zeros_b = jnp.zeros((LANES, LANES), BF16)

    def powers(re, im, n):
        out = [(jnp.ones_like(re), jnp.zeros_like(re))]
        for _ in range(n):
            out.append(_cmul(out[-1][0], out[-1][1], re, im))
        return out

    def split(v):
        hi = v.astype(BF16)
        return hi, (v - hi.astype(F32)).astype(BF16)

    def shift_rows(t, blocks):
        n = abs(blocks) * hh
        if n == 0:
            return t
        pad = jnp.zeros((n, LANES), F32)
        return jnp.concatenate([pad, t[:LANES - n]] if blocks > 0 else [t[n:], pad], axis=0)

    for q in range(npair):
        ql = slice(q * LANES, (q + 1) * LANES)
        taps = []
        for d in range(2):
            b_re = jnp.concatenate([bc_ref[0, 2 * d, :, ql]] * L, axis=0)
            b_im = jnp.concatenate([bc_ref[0, 2 * d + 1, :, ql]] * L, axis=0)
            prow = powers(arow_ref[0, 2 * d:2 * d + 1, ql], arow_ref[0, 2 * d + 1:2 * d + 2, ql], L - 1)
            exps = [L - 1 - j if d == 0 else j for j in range(L)]
            a_re = jnp.concatenate([jnp.broadcast_to(prow[e][0], (hh, LANES)) for e in exps], axis=0)
            a_im = jnp.concatenate([jnp.broadcast_to(prow[e][1], (hh, LANES)) for e in exps], axis=0)
            for part, v in zip((2 * d, 2 * d + 1), _cmul(b_re, b_im, a_re, a_im)):
                cols = slice(part * LANES, (part + 1) * LANES)
                for dst, piece in zip((ws, ws_lo), split(v)):
                    dst[q, 0:LANES, cols] = jnp.where(lane_g0, piece, zeros_b)
                    dst[q, LANES:2 * LANES, cols] = jnp.where(lane_g0, zeros_b, piece)
            c_re = cct_ref[0, 2 * d, ql, :]
            c_im = cct_ref[0, 2 * d + 1, ql, :]
            (cr_hi, cr_lo), (ci_hi, ci_lo) = split(c_re), split(-c_im)
            c_hi = jnp.concatenate([cr_hi, ci_hi], axis=0)
            c_lo = jnp.concatenate([cr_lo, ci_lo], axis=0)
            dcols = slice(2 * d * LANES, (2 * d + 2) * LANES)
            taps.append(jnp.dot(jnp.concatenate([ws[q, :, dcols], ws[q, :, dcols], ws_lo[q, :, dcols]], axis=1),
                                jnp.concatenate([c_hi, c_lo, c_hi], axis=0), preferred_element_type=F32))
            pcol = powers(acol_ref[0, 2 * d, ql, :], acol_ref[0, 2 * d + 1, ql, :], L)
            ap_re, ap_im = zeros, zeros
            for jo in range(L):
                e = jo + 1 if d == 0 else L - jo
                ap_re = jnp.where(lane_blk == jo, pcol[e][0], ap_re)
                ap_im = jnp.where(lane_blk == jo, pcol[e][1], ap_im)
            ca_re, ca_im = _cmul(c_re, c_im, ap_re, ap_im)
            for part, v in ((2 * d, ca_re), (2 * d + 1, -ca_im)):
                r0 = 2 * LANES + part * LANES
                w3[q, r0:r0 + LANES, 0:LANES] = jnp.where(row_g0, v, zeros).astype(BF16)
                w3[q, r0:r0 + LANES, LANES:2 * LANES] = jnp.where(row_g0, zeros, v).astype(BF16)
        for g2 in range(2):
            tf = taps[0][g2 * LANES:(g2 + 1) * LANES]
            tb = taps[1][g2 * LANES:(g2 + 1) * LANES]
            m = zeros
            for jo in range(L):
                m = jnp.where(lane_blk == jo, shift_rows(tf, jo - (L - 1)) + shift_rows(tb, jo), m)
            rows = slice(g2 * LANES, (g2 + 1) * LANES)
            w3[q, rows, g2 * LANES:(g2 + 1) * LANES] = m.astype(BF16)
            w3[q, rows, (1 - g2) * LANES:(2 - g2) * LANES] = zeros_b


def _seg_pitch(seg_len):
    tiles = seg_len // 8 + 1
    return 8 * (tiles + 1 - tiles % 2)


def _ssm_kernel(*refs, nc, nx, seg_len, ctx_out):
    if ctx_out:
        (ux_ref, uc_ref, bc_ref, cct_ref, arow_ref, acol_ref, apow_ref, pw_ref,
         yx_ref, yc_ref, ws, ws_lo, w3, lhs, st, uxf, ucf) = refs
    else:
        (ux_ref, uc_ref, bc_ref, cct_ref, arow_ref, acol_ref, apow_ref, pw_ref,
         yx_ref, ws, ws_lo, w3, lhs, st, uxf, ucf) = refs
        yc_ref = None
    L = SSM_CHUNK
    nlt = st.shape[0] // 4
    sw = nlt * LANES
    xw = 2 * LANES
    FR, FI, BR, BI = range(4)
    nseg = SSM_SEGS
    pitch = _seg_pitch(seg_len)

    @pl.when(pl.program_id(1) == 0)
    def _():
        _ssm_build_operators(bc_ref, cct_ref, arow_ref, acol_ref, ws, ws_lo, w3)

    def ld(part, rows):
        return jnp.concatenate([st[part * nlt + q, rows, :] for q in range(nlt)], axis=1)

    def sto(part, rows, val):
        for q in range(nlt):
            st[part * nlt + q, rows, :] = val[:, q * LANES:(q + 1) * LANES]

    def part_cols(part):
        return slice(part * sw, (part + 1) * sw)

    ucf[...] = uc_ref[...].astype(F32)
    uxf[...] = ux_ref[...].astype(F32)
    by_pos = [jnp.concatenate([ucf[pl.ds(j, nc, stride=L), :], uxf[pl.ds(j, nx, stride=L), :]], axis=0)
              for j in range(L)]
    for g, tile_g in enumerate(_block_transpose8(by_pos)):
        lhs[g // 2, :, (g % 2) * LANES:(g % 2 + 1) * LANES] = tile_g.astype(BF16)

    for q in range(nlt):
        s = jnp.dot(lhs[q, :, 0:xw], ws[q], preferred_element_type=F32)
        for part in range(4):
            k = part * nlt + q
            st[k, 0:nc, :] = s[0:nc, part * LANES:(part + 1) * LANES]
            for g in range(nseg):
                st[k, nc + g * pitch:nc + g * pitch + seg_len, :] = (
                    s[nc + g * seg_len:nc + (g + 1) * seg_len, part * LANES:(part + 1) * LANES])

    a_l = apow_ref[0, 0:1, :]
    a_seg = apow_ref[0, 1:2, :]
    al = [a_l[:, part_cols(p)] for p in range(4)]
    aseg = [a_seg[:, part_cols(p)] for p in range(4)]

    zero = jnp.zeros((1, sw), F32)

    def ctx_step(pr, pi):
        def step(i, carry):
            er, ei = carry
            row = pl.ds(i, 1)
            sr, si = ld(pr, row), ld(pi, row)
            sto(pr, row, er)
            sto(pi, row, ei)
            return _cmul_add(al[pr], al[pi], er, ei, sr, si)
        return step

    h0f = lax.fori_loop(0, nc, ctx_step(FR, FI), (zero, zero))
    bwd_step = ctx_step(BR, BI)
    h0b = lax.fori_loop(0, nc, lambda k, carry: bwd_step(nc - 1 - k, carry), (zero, zero))
    for part in range(4):
        for q in range(nlt):
            lhs[q, 0:nc, xw + part * LANES:xw + (part + 1) * LANES] = st[part * nlt + q, 0:nc, :].astype(BF16)

    zseg = jnp.zeros((nseg, sw), F32)

    def seg_pass(pr, pi, order):
        zr, zi = zseg, zseg
        for i in order:
            rows = pl.ds(nc + i, nseg, stride=pitch)
            sr, si = ld(pr, rows), ld(pi, rows)
            sto(pr, rows, zr)
            sto(pi, rows, zi)
            zr, zi = _cmul_add(al[pr], al[pi], zr, zi, sr, si)
        return zr, zi

    zfr, zfi = seg_pass(FR, FI, range(seg_len))
    zbr, zbi = seg_pass(BR, BI, range(seg_len - 1, -1, -1))

    ef = [h0f]
    for s in range(nseg - 1):
        ef.append(_cmul_add(aseg[FR], aseg[FI], ef[s][0], ef[s][1], zfr[s:s + 1], zfi[s:s + 1]))
    eb = [None] * nseg
    eb[nseg - 1] = h0b
    for s in range(nseg - 1, 0, -1):
        eb[s - 1] = _cmul_add(aseg[BR], aseg[BI], eb[s][0], eb[s][1], zbr[s:s + 1], zbi[s:s + 1])
    for s in range(nseg):
        srows = slice(nc + s * pitch, nc + s * pitch + seg_len)
        rows = slice(nc + s * seg_len, nc + (s + 1) * seg_len)
        for (pr, pi, e) in ((FR, FI, ef[s]), (BR, BI, eb[s])):
            xr, xi = _cmul_add(pw_ref[0, :, part_cols(pr)], pw_ref[0, :, part_cols(pi)], e[0], e[1],
                               ld(pr, srows), ld(pi, srows))
            for q in range(nlt):
                ql = slice(q * LANES, (q + 1) * LANES)
                lhs[q, rows, xw + pr * LANES:xw + (pr + 1) * LANES] = xr[:, ql].astype(BF16)
                lhs[q, rows, xw + pi * LANES:xw + (pi + 1) * LANES] = xi[:, ql].astype(BF16)

    by_group = []
    for q in range(nlt):
        y = jnp.dot(lhs[q], w3[q], preferred_element_type=F32)
        by_group += [y[:, 0:LANES], y[:, LANES:2 * LANES]]
    for j, tile_j in enumerate(_block_transpose8(by_group)):
        if ctx_out:
            yc_ref[pl.ds(j, nc, stride=L), :] = tile_j[0:nc]
        yx_ref[pl.ds(j, nx, stride=L), :] = tile_j[nc:nc + nx]


def _ssm_scan(px, pc, col_x, col_c, w, params, batch, ctx_out):
    bc, cct, arow, acol, apow, pw = params
    L = SSM_CHUNK
    t = px.shape[0] // batch
    lc = pc.shape[0] // batch
    nx, nc = t // L, lc // L
    seg_len = nx // SSM_SEGS
    o = w // LANES
    sw = bc.shape[-1]
    sdim = 4 * sw
    npair = sw // LANES
    out_shape = [jax.ShapeDtypeStruct((batch * t, w), F32)]
    out_specs = [pl.BlockSpec((t, LANES), lambda oi, b: (b, oi))]
    if ctx_out:
        out_shape.append(jax.ShapeDtypeStruct((batch * lc, w), F32))
        out_specs.append(pl.BlockSpec((lc, LANES), lambda oi, b: (b, oi)))
    tile = lambda a: pl.BlockSpec((1,) + a.shape[1:], lambda oi, b: (oi,) + (0,) * (a.ndim - 1))
    res = pl.pallas_call(
        functools.partial(_ssm_kernel, nc=nc, nx=nx, seg_len=seg_len, ctx_out=ctx_out),
        grid=(o, batch),
        in_specs=[
            pl.BlockSpec((t, LANES), lambda oi, b: (b, col_x + oi)),
            pl.BlockSpec((lc, LANES), lambda oi, b: (b, col_c + oi)),
            tile(bc), tile(cct), tile(arow), tile(acol), tile(apow), tile(pw),
        ],
        out_specs=out_specs,
        out_shape=out_shape,
        scratch_shapes=[
            pltpu.VMEM((npair, 2 * LANES, 4 * LANES), BF16),
            pltpu.VMEM((npair, 2 * LANES, 4 * LANES), BF16),
            pltpu.VMEM((npair, 6 * LANES, 2 * LANES), BF16),
            pltpu.VMEM((npair, nc + nx, 6 * LANES), BF16),
            pltpu.VMEM((sdim // LANES, nc + SSM_SEGS * _seg_pitch(seg_len), LANES), F32),
            pltpu.VMEM((t, LANES), F32),
            pltpu.VMEM((lc, LANES), F32),
        ],
        compiler_params=_cparams("arbitrary", "arbitrary"),
        name="ssm_scan",
    )(px, pc, bc, cct, arow, acol, apow, pw)
    return (res[0], res[1]) if ctx_out else (res[0], None)


def _dft_table_kernel(e1_ref, e2_ref, o_ref, *, t):
    e1c, e1s = e1_ref[0, :, 0:t], e1_ref[0, :, t:2 * t]
    e2c, e2s = e2_ref[:, 0:t], e2_ref[:, t:2 * t]
    c, s = _cmul(e2c, e2s, e1c, e1s)
    o_ref[:, 0:t] = c.astype(BF16)
    o_ref[:, t:2 * t] = (-s).astype(BF16)


def _dft_tables(t, gw):
    def cis(rows_mult, nrows):
        k = jnp.arange(nrows, dtype=jnp.int32)[:, None] * rows_mult
        n = jnp.arange(t, dtype=jnp.int32)[None, :]
        ang = ((k * n) % t).astype(F32) * (2.0 * math.pi / t)
        return jnp.concatenate([jnp.cos(ang), jnp.sin(ang)], axis=1)
    r = DFT_ROWS
    e1 = cis(r, t // r).reshape(t // r, 1, 2 * t)
    e2 = cis(1, r)
    tab_t = pl.pallas_call(
        functools.partial(_dft_table_kernel, t=t),
        grid=(t // r,),
        in_specs=[
            pl.BlockSpec((1, 1, 2 * t), lambda i: (i, 0, 0)),
            pl.BlockSpec((r, 2 * t), lambda i: (0, 0)),
        ],
        out_specs=pl.BlockSpec((r, 2 * t), lambda i: (i, 0)),
        out_shape=jax.ShapeDtypeStruct((t, 2 * t), BF16),
        compiler_params=_cparams("arbitrary"),
        name="dft_table",
    )(e1, e2)
    kc = jnp.arange(gw, dtype=jnp.int32)
    ang = ((kc[:, None] * kc[None, :]) % gw).astype(F32) * (2.0 * math.pi / gw)
    tab_c = jnp.concatenate([jnp.cos(ang), jnp.sin(ang)], axis=1).astype(BF16)
    return tab_t, tab_c


def _fft_kernel(f_ref, cs_ref, tab_ref, o_ref, data, *, t, gw, groups, scale):
    i = pl.program_id(1)

    @pl.when(i == 0)
    def _():
        rb = min(512, t)
        for r in range(t // rb):
            for g in range(groups):
                fg = f_ref[r * rb:(r + 1) * rb, g * gw:(g + 1) * gw]
                z = jnp.dot(fg, cs_ref[...], preferred_element_type=F32)
                data[r * rb:(r + 1) * rb, g * gw:(g + 1) * gw] = z[:, :gw].astype(BF16)
                data[t + r * rb:t + (r + 1) * rb, g * gw:(g + 1) * gw] = z[:, gw:].astype(BF16)

    y = jnp.dot(tab_ref[...], data[...], preferred_element_type=F32)
    o_ref[...] = (y * scale).astype(BF16)


def _fourier(p, col_tile, t, batch, tabs):
    tab_t, tab_c = tabs
    wf = 1024
    gw = wf // FFT_GROUPS
    tm = min(256, t)
    return pl.pallas_call(
        functools.partial(_fft_kernel, t=t, gw=gw, groups=FFT_GROUPS, scale=1.0 / math.sqrt(t * gw)),
        grid=(batch, t // tm),
        in_specs=[
            pl.BlockSpec((t, wf), lambda b, i: (b, col_tile)),
            pl.BlockSpec((gw, 2 * gw), lambda b, i: (0, 0)),
            pl.BlockSpec((tm, 2 * t), lambda b, i: (i, 0)),
        ],
        out_specs=pl.BlockSpec((tm, wf), lambda b, i: (b * (t // tm) + i, 0)),
        out_shape=jax.ShapeDtypeStruct((batch * t, wf), BF16),
        scratch_shapes=[pltpu.VMEM((2 * t, wf), BF16)],
        compiler_params=_cparams("arbitrary", "arbitrary"),
        name="fourier",
    )(p, tab_c, tab_t)


def _cmul_const(xr, xi, c, s):
    def scaled(v, k):
        if abs(k) < 1e-12:
            return None
        return v if abs(k - 1.0) < 1e-12 else -v if abs(k + 1.0) < 1e-12 else v * k

    def add(a, b):
        return b if a is None else a if b is None else a + b

    return add(scaled(xr, c), scaled(xi, -s)), add(scaled(xi, c), scaled(xr, s))


def _dft4(y):
    (ar, ai), (br, bi), (cr, ci), (dr, di) = y
    sr, si, tr, ti = ar + cr, ai + ci, ar - cr, ai - ci
    ur, ui, vr, vi = br + dr, bi + di, br - dr, bi - di
    return [(sr + ur, si + ui), (tr + vi, ti - vr), (sr - ur, si - ui), (tr - vi, ti + vr)]


def _dft16(x):
    out = [None] * 16
    p = [_dft4([x[4 * a + b] for a in range(4)]) for b in range(4)]
    for ka in range(4):
        q = []
        for b in range(4):
            ang = -2.0 * math.pi * ka * b / 16.0
            q.append(_cmul_const(p[b][ka][0], p[b][ka][1], math.cos(ang), math.sin(ang)))
        r = _dft4(q)
        for kb in range(4):
            out[ka + 4 * kb] = r[kb]
    return out


def _fft_fast_kernel(f_ref, csc_ref, tw_ref, cst_ref, o_ref, z, o_scr, *, n1, n2, gw, scale):
    slabs_per_dot = 4
    for q in range(n2 // slabs_per_dot):
        r0 = q * slabs_per_dot * n1
        zz = jnp.dot(f_ref[r0:r0 + slabs_per_dot * n1, :], csc_ref[...], preferred_element_type=F32)
        for s in range(slabs_per_dot):
            z[0, q * slabs_per_dot + s] = zz[s * n1:(s + 1) * n1, 0:gw]
            z[1, q * slabs_per_dot + s] = zz[s * n1:(s + 1) * n1, gw:2 * gw]

    def tile_body(r, carry):
        rows = pl.ds(pl.multiple_of(r * 8, 8), 8)
        for c in range(gw // LANES):
            cols = slice(c * LANES, (c + 1) * LANES)
            h = _dft16([(z[0, t, rows, cols], z[1, t, rows, cols]) for t in range(n2)])
            for k2 in range(n2):
                hr, hi = h[k2]
                if k2 > 0:
                    hr, hi = _cmul(hr, hi, tw_ref[0, k2, rows, :], tw_ref[1, k2, rows, :])
                z[0, k2, rows, cols] = hr
                z[1, k2, rows, cols] = hi
        return carry

    lax.fori_loop(0, n1 // 8, tile_body, 0)

    for k2 in range(n2):
        rhs = jnp.concatenate([z[0, k2], z[1, k2]], axis=0).astype(BF16)
        y = jnp.dot(cst_ref[...], rhs, preferred_element_type=F32) * scale
        for c in range(gw // LANES):
            o_scr[c, pl.ds(k2, n1, stride=n2), :] = y[:, c * LANES:(c + 1) * LANES]
    o_ref[...] = jnp.concatenate([o_scr[c] for c in range(gw // LANES)], axis=1).astype(BF16)


def _fourier_fast(p, col_tile, t, batch):
    n1, n2 = 256, 16
    assert t == n1 * n2
    wf = 1024
    gw = wf // FFT_GROUPS
    ang = lambda a, b, n: ((a[:, None] * b[None, :]) % n).astype(F32) * (2.0 * math.pi / n)
    kc = jnp.arange(gw, dtype=jnp.int32)
    a_c = ang(kc, kc, gw)
    csc = jnp.concatenate([jnp.cos(a_c), -jnp.sin(a_c)], axis=1).astype(BF16)
    k1 = jnp.arange(n1, dtype=jnp.int32)
    a_t = ang(k1, k1, n1)
    cst = jnp.concatenate([jnp.cos(a_t), jnp.sin(a_t)], axis=1).astype(BF16)
    a_w = ang(jnp.arange(n2, dtype=jnp.int32), k1, t)
    tw = jnp.stack([jnp.cos(a_w), -jnp.sin(a_w)], axis=0)
    tw = jnp.broadcast_to(tw[..., None], (2, n2, n1, LANES))
    ngrp = wf // gw
    return pl.pallas_call(
        functools.partial(_fft_fast_kernel, n1=n1, n2=n2, gw=gw, scale=1.0 / math.sqrt(t * gw)),
        grid=(batch, ngrp),
        in_specs=[
            pl.BlockSpec((t, gw), lambda b, g: (b, col_tile * ngrp + g)),
            pl.BlockSpec((gw, 2 * gw), lambda b, g: (0, 0)),
            pl.BlockSpec((2, n2, n1, LANES), lambda b, g: (0, 0, 0, 0), pipeline_mode=pl.Buffered(1)),
            pl.BlockSpec((n1, 2 * n1), lambda b, g: (0, 0)),
        ],
        out_specs=pl.BlockSpec((t, gw), lambda b, g: (b, g)),
        out_shape=jax.ShapeDtypeStruct((batch * t, wf), BF16),
        scratch_shapes=[
            pltpu.VMEM((2, n2, n1, gw), F32),
            pltpu.VMEM((gw // LANES, t, LANES), F32),
        ],
        compiler_params=_cparams("arbitrary", "arbitrary"),
        name="fourier_fast",
    )(p, csc, tw, cst)


def _branch_kernel(xa_ref, ba_ref, ca_ref, za_ref, zb_ref, zc_ref, g0_ref, g1_ref, g2_ref,
                   u_ref, ys_ref, yf_ref, cw_ref, sd_ref, wa_ref, wb_ref, fw_ref,
                   pa_ref, pb_ref, pc_ref, m_ref, *, row_len):
    tm = xa_ref.shape[0]
    f = lambda r: r[...].astype(F32)
    v = f(ca_ref) * f(xa_ref)
    pos = lax.broadcasted_iota(jnp.int32, v.shape, 0) % row_len
    v_prev = jnp.where(pos == 0, 0.0, pltpu.roll(v, 1, 0))
    v_next = jnp.where(pos == row_len - 1, 0.0, pltpu.roll(v, tm - 1, 0))
    cw = cw_ref[0]
    conv = v_prev * cw[0:1, :] + v * cw[1:2, :] + v_next * cw[2:3, :]
    a = f(ba_ref) * conv * _silu(f(za_ref))
    ya = jnp.dot(a.astype(BF16), pa_ref[0], preferred_element_type=F32)
    acc = _sigmoid(f(g0_ref)) * ya
    y = _gelu_tanh(ys_ref[...] + sd_ref[0] * f(u_ref)).astype(BF16)
    glu = (jnp.dot(y, wa_ref[0], preferred_element_type=F32)
           * _sigmoid(jnp.dot(y, wb_ref[0], preferred_element_type=F32)) * _silu(f(zb_ref)))
    yb = jnp.dot(glu.astype(BF16), pb_ref[0], preferred_element_type=F32)
    acc = acc + _sigmoid(f(g1_ref)) * yb
    c = jnp.dot(yf_ref[...], fw_ref[0], preferred_element_type=F32) * _silu(f(zc_ref))
    yc = jnp.dot(c.astype(BF16), pc_ref[0], preferred_element_type=F32)
    acc = acc + _sigmoid(f(g2_ref)) * yc
    m_ref[...] = acc.astype(BF16)


def _branches(p, ys, yf, conv_w, ssm_d, wa, wb, fw, pa, pb, pc, layer, row_len):
    n = p.shape[0]
    wc = conv_w.shape[-1]
    d = pa.shape[-1]
    tm = 256
    tn = 1024
    col = lambda k: pl.BlockSpec((tm, tn), lambda i: (i, k))
    const = lambda shape: pl.BlockSpec(shape, lambda i: (layer,) + (0,) * (len(shape) - 1),
                                       pipeline_mode=pl.Buffered(1))
    g_tile0 = 8
    return pl.pallas_call(
        functools.partial(_branch_kernel, row_len=row_len),
        grid=(n // tm,),
        in_specs=[
            col(0), col(1), col(2), col(3), col(5), col(7),
            pl.BlockSpec((tm, d), lambda i: (i, g_tile0 * tn // d)),
            pl.BlockSpec((tm, d), lambda i: (i, g_tile0 * tn // d + 1)),
            pl.BlockSpec((tm, d), lambda i: (i, g_tile0 * tn // d + 2)),
            col(4),
            pl.BlockSpec((tm, tn), lambda i: (i, 0)),
            pl.BlockSpec((tm, tn), lambda i: (i, 0)),
            const((1, 3, wc)), const((1, 1, tn)),
            const((1, tn, tn)), const((1, tn, tn)), const((1, tn, tn)),
            const((1, wc, d)), const((1, tn, d)), const((1, tn, d)),
        ],
        out_specs=pl.BlockSpec((tm, d), lambda i: (i, 0)),
        out_shape=jax.ShapeDtypeStruct((n, d), BF16),
        compiler_params=_cparams("arbitrary"),
        name="branches",
    )(p, p, p, p, p, p, p, p, p, p, ys, yf, conv_w, ssm_d.reshape(ssm_d.shape[0], 1, -1),
      wa, wb, fw, pa, pb, pc)


def _out_kernel(*refs, with_next):
    if with_next:
        m_ref, x_ref, gt_ref, g_ref, w_ref, scn_ref, shn_ref, gn_ref, o_ref, h_ref = refs
    else:
        m_ref, x_ref, gt_ref, g_ref, w_ref, o_ref = refs
    o = jnp.dot(m_ref[...], w_ref[0], preferred_element_type=F32)
    ms = jnp.mean(o * o, axis=-1, keepdims=True)
    y = o * lax.rsqrt(ms + RMS_EPS) * g_ref[0]
    xn = x_ref[...] + gt_ref[0] * y
    o_ref[...] = xn
    if with_next:
        h_ref[...] = _modnorm(xn, gn_ref[0], scn_ref[0], shn_ref[0]).astype(BF16)


def _outproj(m, x2, gt, g_post, w_out_b, layer, rows_per_batch, next_mod=None):
    n, d = x2.shape
    tm = min(512, rows_per_batch)
    tpb = rows_per_batch // tm
    per_batch = pl.BlockSpec((1, 1, d), lambda i: (i // tpb, 0, 0))
    in_specs = [
        pl.BlockSpec((tm, d), lambda i: (i, 0)),
        pl.BlockSpec((tm, d), lambda i: (i, 0)),
        per_batch,
        pl.BlockSpec((1, 1, d), lambda i: (layer, 0, 0)),
        pl.BlockSpec((1, d, d), lambda i: (layer, 0, 0), pipeline_mode=pl.Buffered(1)),
    ]
    args = [m, x2, gt, g_post.reshape(g_post.shape[0], 1, d), w_out_b]
    out_specs = [pl.BlockSpec((tm, d), lambda i: (i, 0))]
    out_shape = [jax.ShapeDtypeStruct((n, d), F32)]
    if next_mod is not None:
        sc_n, sh_n, g_pre = next_mod
        in_specs += [per_batch, per_batch, pl.BlockSpec((1, 1, d), lambda i: (layer + 1, 0, 0))]
        args += [sc_n, sh_n, g_pre.reshape(g_pre.shape[0], 1, d)]
        out_specs.append(pl.BlockSpec((tm, d), lambda i: (i, 0)))
        out_shape.append(jax.ShapeDtypeStruct((n, d), BF16))
    res = pl.pallas_call(
        functools.partial(_out_kernel, with_next=next_mod is not None),
        grid=(n // tm,),
        in_specs=in_specs,
        out_specs=out_specs,
        out_shape=out_shape,
        compiler_params=_cparams("arbitrary"),
        name="outproj",
    )(*args)
    return (res[0], res[1]) if next_mod is not None else (res[0], None)


def kernel(x, c, ctx, c_ctx, w_ada, b_ada, g_pre, g_post, w_in, conv_w, ssm_lam_re, ssm_lam_im, ssm_log_dt,
           ssm_b_re, ssm_b_im, ssm_c_re, ssm_c_im, ssm_d, glu_wa, glu_wb, fourier_w, proj_a, proj_b, proj_c,
           w_out):
    bn, t, d = x.shape
    lc = ctx.shape[1]
    depth = w_ada.shape[0]
    w_conv = conv_w.shape[-1]
    w_ssm = ssm_d.shape[-1]
    tn = 1024
    assert w_conv == tn and w_ssm == tn and fourier_w.shape[-1] == tn and d % tn == 0
    assert bn + 1 <= 8 and t % GRID_W == 0
    u_tile = 4 * w_conv // tn
    f_tile = u_tile + 2
    n_tiles = w_in.shape[-1] // tn
    seg_len = (t // SSM_CHUNK) // SSM_SEGS

    cc = jnp.concatenate([c, c_ctx[None, :], jnp.zeros((8 - bn - 1, d), F32)], axis=0)
    mods = _ada_mods(cc, w_ada, b_ada)

    to_b = lambda w: w.astype(BF16)
    wa_b, wb_b, fw_b = to_b(glu_wa), to_b(glu_wb), to_b(fourier_w)
    pa_b, pb_b, pc_b, wo_b = to_b(proj_a), to_b(proj_b), to_b(proj_c), to_b(w_out)
    fast_dft = t == 256 * 16
    tabs_x = None if fast_dft else _dft_tables(t, tn // FFT_GROUPS)
    tabs_c = _dft_tables(lc, tn // FFT_GROUPS)

    x2 = x.reshape(bn * t, d)
    c2 = ctx.reshape(bn * lc, d)
    def mod_x(l):
        return tuple(mods[l, :bn, k * d:(k + 1) * d].reshape(bn, 1, d) for k in range(3))

    def mod_c(l):
        return tuple(jnp.broadcast_to(mods[l, bn, k * d:(k + 1) * d], (bn, 1, d)) for k in range(3))

    hx = _prenorm(x2, mod_x(0)[1], mod_x(0)[0], g_pre, 0, t)
    hc = _prenorm(c2, mod_c(0)[1], mod_c(0)[0], g_pre, 0, lc)
    for l in range(depth):
        last = l == depth - 1
        sh_x, sc_x, gt_x = mod_x(l)
        sh_c, sc_c, gt_c = mod_c(l)
        next_x = None if last else (mod_x(l + 1)[1], mod_x(l + 1)[0], g_pre)
        next_c = None if last else (mod_c(l + 1)[1], mod_c(l + 1)[0], g_pre)
        ssm_p = _ssm_params(ssm_lam_re[l], ssm_lam_im[l], ssm_log_dt[l], ssm_b_re[l], ssm_b_im[l],
                            ssm_c_re[l], ssm_c_im[l], seg_len)

        px = _inproj(hx, w_in, l, 0, n_tiles)
        pc_ = _inproj(hc, w_in, l, u_tile, 1) if last else _inproj(hc, w_in, l, 0, n_tiles)
        u_lane_tile = u_tile * (tn // LANES)
        ys_x, ys_c = _ssm_scan(px, pc_, u_lane_tile, 0 if last else u_lane_tile, w_ssm, ssm_p, bn,
                               ctx_out=not last)

        yf_x = _fourier_fast(px, f_tile, t, bn) if fast_dft else _fourier(px, f_tile, t, bn, tabs_x)
        mx = _branches(px, ys_x, yf_x, conv_w, ssm_d, wa_b, wb_b, fw_b, pa_b, pb_b, pc_b, l, GRID_W)
        new_x2, hx = _outproj(mx, x2, gt_x, g_post, wo_b, l, t, next_x)
        if not last:
            yf_c = _fourier(pc_, f_tile, lc, bn, tabs_c)
            mc = _branches(pc_, ys_c, yf_c, conv_w, ssm_d, wa_b, wb_b, fw_b, pa_b, pb_b, pc_b, l, lc)
            c2, hc = _outproj(mc, c2, gt_c, g_post, wo_b, l, lc, next_c)
        x2 = new_x2
    return x2.reshape(bn, t, d)
```

```python
import functools
import math

import jax
import jax.numpy as jnp
from jax import lax
from jax.experimental import pallas as pl
from jax.experimental.pallas import tpu as pltpu

F32 = jnp.float32
BF16 = jnp.bfloat16

GRID_W = 64
FFT_GROUPS = 4
N_BRANCH = 3
RMS_EPS = 1e-6
LANES = 128
SSM_CHUNK = 8
SSM_SEGS = 8
DFT_ROWS = 64
VMEM_LIMIT = 56 * 1024 * 1024


def _cparams(*sem):
    return pltpu.CompilerParams(dimension_semantics=sem, vmem_limit_bytes=VMEM_LIMIT)


def _sigmoid(v):
    return 0.5 * jnp.tanh(0.5 * v) + 0.5


def _silu(v):
    return v * _sigmoid(v)


def _gelu_tanh(v):
    return v * (0.5 * (1.0 + jnp.tanh(math.sqrt(2.0 / math.pi) * (v + 0.044715 * (v * v * v)))))


def _cmul(ar, ai, br, bi):
    return ar * br - ai * bi, ar * bi + ai * br


def _cmul_add(ar, ai, zr, zi, sr, si):
    return ar * zr - ai * zi + sr, ar * zi + ai * zr + si


def _ada_kernel(c_ref, w_ref, b_ref, o_ref):
    c = c_ref[...]
    s = _silu(c).astype(BF16)
    o_ref[0] = jnp.dot(s, w_ref[0].astype(BF16), preferred_element_type=F32) + b_ref[0]


def _ada_mods(cc, w_ada, b_ada):
    depth, d, n3 = w_ada.shape
    tn = min(1024, n3)
    return pl.pallas_call(
        _ada_kernel,
        grid=(depth, n3 // tn),
        in_specs=[
            pl.BlockSpec((8, d), lambda l, j: (0, 0)),
            pl.BlockSpec((1, d, tn), lambda l, j: (l, 0, j)),
            pl.BlockSpec((1, 1, tn), lambda l, j: (l, 0, j)),
        ],
        out_specs=pl.BlockSpec((1, 8, tn), lambda l, j: (l, 0, j)),
        out_shape=jax.ShapeDtypeStruct((depth, 8, n3), F32),
        compiler_params=_cparams("arbitrary", "arbitrary"),
        name="ada_mods",
    )(cc, w_ada, b_ada.reshape(depth, 1, n3))


def _modnorm(x, g, sc, sh):
    ms = jnp.mean(x * x, axis=-1, keepdims=True)
    return (x * lax.rsqrt(ms + RMS_EPS) * g) * (1.0 + sc) + sh


def _prenorm_kernel(x_ref, sc_ref, sh_ref, g_ref, h_ref):
    rb = 64

    def body(r, carry):
        rows = pl.ds(pl.multiple_of(r * rb, rb), rb)
        h_ref[rows, :] = _modnorm(x_ref[rows, :], g_ref[0], sc_ref[0], sh_ref[0]).astype(BF16)
        return carry

    lax.fori_loop(0, x_ref.shape[0] // rb, body, 0)


def _prenorm(x2, sc, sh, g_pre, layer, rows_per_batch):
    n, d = x2.shape
    tm = min(512, rows_per_batch)
    tpb = rows_per_batch // tm
    return pl.pallas_call(
        _prenorm_kernel,
        grid=(n // tm,),
        in_specs=[
            pl.BlockSpec((tm, d), lambda i: (i, 0)),
            pl.BlockSpec((1, 1, d), lambda i: (i // tpb, 0, 0)),
            pl.BlockSpec((1, 1, d), lambda i: (i // tpb, 0, 0)),
            pl.BlockSpec((1, 1, d), lambda i: (layer, 0, 0)),
        ],
        out_specs=pl.BlockSpec((tm, d), lambda i: (i, 0)),
        out_shape=jax.ShapeDtypeStruct((n, d), BF16),
        compiler_params=_cparams("arbitrary"),
        name="prenorm",
    )(x2, sc, sh, g_pre.reshape(g_pre.shape[0], 1, d))


def _inproj_kernel(h_ref, w_ref, p_ref, wb):
    @pl.when(pl.program_id(1) == 0)
    def _():
        wb[...] = w_ref[0].astype(BF16)

    p_ref[...] = jnp.dot(h_ref[...], wb[...], preferred_element_type=F32).astype(BF16)


def _inproj(h, w_in, layer, tile0, ntiles):
    n, d = h.shape
    tn = 1024
    tm = min(1024, n)
    return pl.pallas_call(
        _inproj_kernel,
        grid=(ntiles, n // tm),
        in_specs=[
            pl.BlockSpec((tm, d), lambda j, i: (i, 0)),
            pl.BlockSpec((1, d, tn), lambda j, i: (layer, 0, tile0 + j)),
        ],
        out_specs=pl.BlockSpec((tm, tn), lambda j, i: (i, j)),
        out_shape=jax.ShapeDtypeStruct((n, ntiles * tn), BF16),
        scratch_shapes=[pltpu.VMEM((d, tn), BF16)],
        compiler_params=_cparams("arbitrary", "arbitrary"),
        name="inproj",
    )(h, w_in)


def _ssm_params(lam_re, lam_im, log_dt, b_re, b_im, c_re, c_im):
    _, G, P = lam_re.shape
    H = b_re.shape[-1]
    gpt = LANES // H
    O = G // gpt
    lam_re = lam_re.astype(F32)
    lam_im = lam_im.astype(F32)
    dt = jnp.exp(log_dt.astype(F32))[..., None]
    lr = lam_re * dt
    li = lam_im * dt
    mag = jnp.exp(lr)
    a_re = mag * jnp.cos(li)
    a_im = mag * jnp.sin(li)
    n_re = a_re - 1.0
    n_im = a_im
    den = lam_re * lam_re + lam_im * lam_im
    q_re = (n_re * lam_re + n_im * lam_im) / den
    q_im = (n_im * lam_re - n_re * lam_im) / den
    bb_re = q_re[..., None] * b_re - q_im[..., None] * b_im
    bb_im = q_re[..., None] * b_im + q_im[..., None] * b_re

    bbs = jnp.stack([bb_re, bb_im], axis=1).reshape(2, 2, O, gpt, P, H)
    bc = jnp.transpose(bbs, (2, 0, 1, 5, 3, 4)).reshape(O, 4, H, gpt * P)
    cs = jnp.stack([c_re.astype(F32), c_im.astype(F32)], axis=1).reshape(2, 2, O, gpt, H, P)
    cct = jnp.transpose(cs, (2, 0, 1, 3, 5, 4)).reshape(O, 4, gpt * P, 1, H)
    cct = jnp.broadcast_to(cct, (O, 4, gpt * P, gpt, H)).reshape(O, 4, gpt * P, LANES)
    a4 = jnp.transpose(jnp.stack([a_re, a_im], axis=1).reshape(2, 2, O, gpt * P), (2, 0, 1, 3)).reshape(O, 4, gpt * P)
    arow = jnp.concatenate([a4, jnp.zeros_like(a4)], axis=1)
    acol = jnp.broadcast_to(a4[..., None], (O, 4, gpt * P, LANES))

    return bc, cct, arow, acol


def _block_transpose8(v):
    bw = LANES // 8
    blk = lax.broadcasted_iota(jnp.int32, v[0].shape, 1) // bw
    for dist in (4, 2, 1):
        upper = (blk & dist) != 0
        nxt = list(v)
        for a in range(8):
            if a & dist:
                continue
            b = a + dist
            nxt[a] = jnp.where(upper, pltpu.roll(v[b], dist * bw, 1), v[a])
            nxt[b] = jnp.where(upper, v[b], pltpu.roll(v[a], LANES - dist * bw, 1))
        v = nxt
    return v


def _ssm_build_operators(bc_ref, cct_ref, arow_ref, acol_ref, ws, ws_lo, w3):
    L = SSM_CHUNK
    hh = bc_ref.shape[2]
    sw = bc_ref.shape[3]
    pp = sw // (LANES // hh)
    npair = sw // LANES
    assert L * hh == LANES and 2 * pp == LANES
    lane = lax.broadcasted_iota(jnp.int32, (LANES, LANES), 1)
    row = lax.broadcasted_iota(jnp.int32, (LANES, LANES), 0)
    lane_g0, row_g0 = lane < pp, row < pp
    lane_blk = lane // hh
    zeros = jnp.zeros((LANES, LANES), F32)
    zeros_b = jnp.zeros((LANES, LANES), BF16)

    def powers(re, im, n):
        out = [(jnp.ones_like(re), jnp.zeros_like(re))]
        for _ in range(n):
            out.append(_cmul(out[-1][0], out[-1][1], re, im))
        return out

    def split(v):
        hi = v.astype(BF16)
        return hi, (v - hi.astype(F32)).astype(BF16)

    def shift_rows(t, blocks):
        n = abs(blocks) * hh
        if n == 0:
            return t
        pad = jnp.zeros((n, LANES), F32)
        return jnp.concatenate([pad, t[:LANES - n]] if blocks > 0 else [t[n:], pad], axis=0)

    for q in range(npair):
        ql = slice(q * LANES, (q + 1) * LANES)
        taps = []
        for d in range(2):
            b_re = jnp.concatenate([bc_ref[0, 2 * d, :, ql]] * L, axis=0)
            b_im = jnp.concatenate([bc_ref[0, 2 * d + 1, :, ql]] * L, axis=0)
            prow = powers(arow_ref[0, 2 * d:2 * d + 1, ql], arow_ref[0, 2 * d + 1:2 * d + 2, ql], L - 1)
            exps = [L - 1 - j if d == 0 else j for j in range(L)]
            a_re = jnp.concatenate([jnp.broadcast_to(prow[e][0], (hh, LANES)) for e in exps], axis=0)
            a_im = jnp.concatenate([jnp.broadcast_to(prow[e][1], (hh, LANES)) for e in exps], axis=0)
            for part, v in zip((2 * d, 2 * d + 1), _cmul(b_re, b_im, a_re, a_im)):
                cols = slice(part * LANES, (part + 1) * LANES)
                for dst, piece in zip((ws, ws_lo), split(v)):
                    dst[q, 0:LANES, cols] = jnp.where(lane_g0, piece, zeros_b)
                    dst[q, LANES:2 * LANES, cols] = jnp.where(lane_g0, zeros_b, piece)
            c_re = cct_ref[0, 2 * d, ql, :]
            c_im = cct_ref[0, 2 * d + 1, ql, :]
            (cr_hi, cr_lo), (ci_hi, ci_lo) = split(c_re), split(-c_im)
            c_hi = jnp.concatenate([cr_hi, ci_hi], axis=0)
            c_lo = jnp.concatenate([cr_lo, ci_lo], axis=0)
            dcols = slice(2 * d * LANES, (2 * d + 2) * LANES)
            taps.append(jnp.dot(jnp.concatenate([ws[q, :, dcols], ws[q, :, dcols], ws_lo[q, :, dcols]], axis=1),
                                jnp.concatenate([c_hi, c_lo, c_hi], axis=0), preferred_element_type=F32))
            pcol = powers(acol_ref[0, 2 * d, ql, :], acol_ref[0, 2 * d + 1, ql, :], L)
            ap_re, ap_im = zeros, zeros
            for jo in range(L):
                e = jo + 1 if d == 0 else L - jo
                ap_re = jnp.where(lane_blk == jo, pcol[e][0], ap_re)
                ap_im = jnp.where(lane_blk == jo, pcol[e][1], ap_im)
            ca_re, ca_im = _cmul(c_re, c_im, ap_re, ap_im)
            for part, v in ((2 * d, ca_re), (2 * d + 1, -ca_im)):
                r0 = 2 * LANES + part * LANES
                w3[q, r0:r0 + LANES, 0:LANES] = jnp.where(row_g0, v, zeros).astype(BF16)
                w3[q, r0:r0 + LANES, LANES:2 * LANES] = jnp.where(row_g0, zeros, v).astype(BF16)
        for g2 in range(2):
            tf = taps[0][g2 * LANES:(g2 + 1) * LANES]
            tb = taps[1][g2 * LANES:(g2 + 1) * LANES]
            m = zeros
            for jo in range(L):
                m = jnp.where(lane_blk == jo, shift_rows(tf, jo - (L - 1)) + shift_rows(tb, jo), m)
            rows = slice(g2 * LANES, (g2 + 1) * LANES)
            w3[q, rows, g2 * LANES:(g2 + 1) * LANES] = m.astype(BF16)
            w3[q, rows, (1 - g2) * LANES:(2 - g2) * LANES] = zeros_b


def _ssm_build_powers(arow_ref, ap, pw, seg_len):
    L = SSM_CHUNK
    sw = arow_ref.shape[2]
    assert L & (L - 1) == 0 and seg_len & (seg_len - 1) == 0 and seg_len >= 8
    for d in range(2):
        cr, ci = slice(2 * d * sw, (2 * d + 1) * sw), slice((2 * d + 1) * sw, (2 * d + 2) * sw)
        re, im = arow_ref[0, 2 * d:2 * d + 1, :], arow_ref[0, 2 * d + 1:2 * d + 2, :]
        for _ in range(L.bit_length() - 1):
            re, im = _cmul(re, im, re, im)
        ap[0:1, cr] = re
        ap[0:1, ci] = im
        p = [(jnp.ones_like(re), jnp.zeros_like(re))]
        for _ in range(7):
            p.append(_cmul(p[-1][0], p[-1][1], re, im))
        order = list(range(8)) if d == 0 else list(range(7, -1, -1))
        first = slice(0, 8) if d == 0 else slice(seg_len - 8, seg_len)
        pw[first, cr] = jnp.concatenate([p[r][0] for r in order], axis=0)
        pw[first, ci] = jnp.concatenate([p[r][1] for r in order], axis=0)
        sq = _cmul(p[4][0], p[4][1], p[4][0], p[4][1])
        m = 8
        while m < seg_len:
            src = slice(0, m) if d == 0 else slice(seg_len - m, seg_len)
            dst = slice(m, 2 * m) if d == 0 else slice(seg_len - 2 * m, seg_len - m)
            xr, xi = _cmul(pw[src, cr], pw[src, ci], sq[0], sq[1])
            pw[dst, cr] = xr
            pw[dst, ci] = xi
            sq = _cmul(sq[0], sq[1], sq[0], sq[1])
            m *= 2
        ap[1:2, cr] = sq[0]
        ap[1:2, ci] = sq[1]


def _seg_pitch(seg_len):
    tiles = seg_len // 8 + 1
    return 8 * (tiles + 1 - tiles % 2)


def _ssm_kernel(*refs, nc, nx, seg_len, ctx_out):
    if ctx_out:
        (ux_ref, uc_ref, bc_ref, cct_ref, arow_ref, acol_ref,
         yx_ref, yc_ref, ws, ws_lo, w3, lhs, st, uxf, ucf, ap, pw) = refs
    else:
        (ux_ref, uc_ref, bc_ref, cct_ref, arow_ref, acol_ref,
         yx_ref, ws, ws_lo, w3, lhs, st, uxf, ucf, ap, pw) = refs
        yc_ref = None
    L = SSM_CHUNK
    nlt = st.shape[0] // 4
    sw = nlt * LANES
    xw = 2 * LANES
    FR, FI, BR, BI = range(4)
    nseg = SSM_SEGS
    pitch = _seg_pitch(seg_len)

    @pl.when(pl.program_id(1) == 0)
    def _():
        _ssm_build_operators(bc_ref, cct_ref, arow_ref, acol_ref, ws, ws_lo, w3)
        _ssm_build_powers(arow_ref, ap, pw, seg_len)

    def ld(part, rows):
        return jnp.concatenate([st[part * nlt + q, rows, :] for q in range(nlt)], axis=1)

    def sto(part, rows, val):
        for q in range(nlt):
            st[part * nlt + q, rows, :] = val[:, q * LANES:(q + 1) * LANES]

    def part_cols(part):
        return slice(part * sw, (part + 1) * sw)

    ucf[...] = uc_ref[...].astype(F32)
    uxf[...] = ux_ref[...].astype(F32)
    by_pos = [jnp.concatenate([ucf[pl.ds(j, nc, stride=L), :], uxf[pl.ds(j, nx, stride=L), :]], axis=0)
              for j in range(L)]
    for g, tile_g in enumerate(_block_transpose8(by_pos)):
        lhs[g // 2, :, (g % 2) * LANES:(g % 2 + 1) * LANES] = tile_g.astype(BF16)

    for q in range(nlt):
        s = jnp.dot(lhs[q, :, 0:xw], ws[q], preferred_element_type=F32)
        for part in range(4):
            k = part * nlt + q
            st[k, 0:nc, :] = s[0:nc, part * LANES:(part + 1) * LANES]
            for g in range(nseg):
                st[k, nc + g * pitch:nc + g * pitch + seg_len, :] = (
                    s[nc + g * seg_len:nc + (g + 1) * seg_len, part * LANES:(part + 1) * LANES])

    a_l = ap[0:1, :]
    a_seg = ap[1:2, :]
    al = [a_l[:, part_cols(p)] for p in range(4)]
    aseg = [a_seg[:, part_cols(p)] for p in range(4)]

    zero = jnp.zeros((1, sw), F32)

    def ctx_step(pr, pi):
        def step(i, carry):
            er, ei = carry
            row = pl.ds(i, 1)
            sr, si = ld(pr, row), ld(pi, row)
            sto(pr, row, er)
            sto(pi, row, ei)
            return _cmul_add(al[pr], al[pi], er, ei, sr, si)
        return step

    h0f = lax.fori_loop(0, nc, ctx_step(FR, FI), (zero, zero))
    bwd_step = ctx_step(BR, BI)
    h0b = lax.fori_loop(0, nc, lambda k, carry: bwd_step(nc - 1 - k, carry), (zero, zero))
    for part in range(4):
        for q in range(nlt):
            lhs[q, 0:nc, xw + part * LANES:xw + (part + 1) * LANES] = st[part * nlt + q, 0:nc, :].astype(BF16)

    zseg = jnp.zeros((nseg, sw), F32)

    def seg_pass(pr, pi, order):
        zr, zi = zseg, zseg
        for i in order:
            rows = pl.ds(nc + i, nseg, stride=pitch)
            sr, si = ld(pr, rows), ld(pi, rows)
            sto(pr, rows, zr)
            sto(pi, rows, zi)
            zr, zi = _cmul_add(al[pr], al[pi], zr, zi, sr, si)
        return zr, zi

    zfr, zfi = seg_pass(FR, FI, range(seg_len))
    zbr, zbi = seg_pass(BR, BI, range(seg_len - 1, -1, -1))

    ef = [h0f]
    for s in range(nseg - 1):
        ef.append(_cmul_add(aseg[FR], aseg[FI], ef[s][0], ef[s][1], zfr[s:s + 1], zfi[s:s + 1]))
    eb = [None] * nseg
    eb[nseg - 1] = h0b
    for s in range(nseg - 1, 0, -1):
        eb[s - 1] = _cmul_add(aseg[BR], aseg[BI], eb[s][0], eb[s][1], zbr[s:s + 1], zbi[s:s + 1])
    for s in range(nseg):
        srows = slice(nc + s * pitch, nc + s * pitch + seg_len)
        rows = slice(nc + s * seg_len, nc + (s + 1) * seg_len)
        for (pr, pi, e) in ((FR, FI, ef[s]), (BR, BI, eb[s])):
            xr, xi = _cmul_add(pw[:, part_cols(pr)], pw[:, part_cols(pi)], e[0], e[1],
                               ld(pr, srows), ld(pi, srows))
            for q in range(nlt):
                ql = slice(q * LANES, (q + 1) * LANES)
                lhs[q, rows, xw + pr * LANES:xw + (pr + 1) * LANES] = xr[:, ql].astype(BF16)
                lhs[q, rows, xw + pi * LANES:xw + (pi + 1) * LANES] = xi[:, ql].astype(BF16)

    by_group = []
    for q in range(nlt):
        y = jnp.dot(lhs[q], w3[q], preferred_element_type=F32)
        by_group += [y[:, 0:LANES], y[:, LANES:2 * LANES]]
    for j, tile_j in enumerate(_block_transpose8(by_group)):
        if ctx_out:
            yc_ref[pl.ds(j, nc, stride=L), :] = tile_j[0:nc]
        yx_ref[pl.ds(j, nx, stride=L), :] = tile_j[nc:nc + nx]


def _ssm_scan(px, pc, col_x, col_c, w, params, batch, ctx_out):
    bc, cct, arow, acol = params
    L = SSM_CHUNK
    t = px.shape[0] // batch
    lc = pc.shape[0] // batch
    nx, nc = t // L, lc // L
    seg_len = nx // SSM_SEGS
    o = w // LANES
    sw = bc.shape[-1]
    sdim = 4 * sw
    npair = sw // LANES
    out_shape = [jax.ShapeDtypeStruct((batch * t, w), F32)]
    out_specs = [pl.BlockSpec((t, LANES), lambda oi, b: (b, oi))]
    if ctx_out:
        out_shape.append(jax.ShapeDtypeStruct((batch * lc, w), F32))
        out_specs.append(pl.BlockSpec((lc, LANES), lambda oi, b: (b, oi)))
    tile = lambda a: pl.BlockSpec((1,) + a.shape[1:], lambda oi, b: (oi,) + (0,) * (a.ndim - 1))
    res = pl.pallas_call(
        functools.partial(_ssm_kernel, nc=nc, nx=nx, seg_len=seg_len, ctx_out=ctx_out),
        grid=(o, batch),
        in_specs=[
            pl.BlockSpec((t, LANES), lambda oi, b: (b, col_x + oi)),
            pl.BlockSpec((lc, LANES), lambda oi, b: (b, col_c + oi)),
            tile(bc), tile(cct), tile(arow), tile(acol),
        ],
        out_specs=out_specs,
        out_shape=out_shape,
        scratch_shapes=[
            pltpu.VMEM((npair, 2 * LANES, 4 * LANES), BF16),
            pltpu.VMEM((npair, 2 * LANES, 4 * LANES), BF16),
            pltpu.VMEM((npair, 6 * LANES, 2 * LANES), BF16),
            pltpu.VMEM((npair, nc + nx, 6 * LANES), BF16),
            pltpu.VMEM((sdim // LANES, nc + SSM_SEGS * _seg_pitch(seg_len), LANES), F32),
            pltpu.VMEM((t, LANES), F32),
            pltpu.VMEM((lc, LANES), F32),
            pltpu.VMEM((8, sdim), F32),
            pltpu.VMEM((seg_len, sdim), F32),
        ],
        compiler_params=_cparams("arbitrary", "arbitrary"),
        name="ssm_scan",
    )(px, pc, bc, cct, arow, acol)
    return (res[0], res[1]) if ctx_out else (res[0], None)


def _dft_table_kernel(e1_ref, e2_ref, o_ref, *, t):
    e1c, e1s = e1_ref[0, :, 0:t], e1_ref[0, :, t:2 * t]
    e2c, e2s = e2_ref[:, 0:t], e2_ref[:, t:2 * t]
    c, s = _cmul(e2c, e2s, e1c, e1s)
    o_ref[:, 0:t] = c.astype(BF16)
    o_ref[:, t:2 * t] = (-s).astype(BF16)


def _dft_tables(t, gw):
    def cis(rows_mult, nrows):
        k = jnp.arange(nrows, dtype=jnp.int32)[:, None] * rows_mult
        n = jnp.arange(t, dtype=jnp.int32)[None, :]
        ang = ((k * n) % t).astype(F32) * (2.0 * math.pi / t)
        return jnp.concatenate([jnp.cos(ang), jnp.sin(ang)], axis=1)
    r = DFT_ROWS
    e1 = cis(r, t // r).reshape(t // r, 1, 2 * t)
    e2 = cis(1, r)
    tab_t = pl.pallas_call(
        functools.partial(_dft_table_kernel, t=t),
        grid=(t // r,),
        in_specs=[
            pl.BlockSpec((1, 1, 2 * t), lambda i: (i, 0, 0)),
            pl.BlockSpec((r, 2 * t), lambda i: (0, 0)),
        ],
        out_specs=pl.BlockSpec((r, 2 * t), lambda i: (i, 0)),
        out_shape=jax.ShapeDtypeStruct((t, 2 * t), BF16),
        compiler_params=_cparams("arbitrary"),
        name="dft_table",
    )(e1, e2)
    kc = jnp.arange(gw, dtype=jnp.int32)
    ang = ((kc[:, None] * kc[None, :]) % gw).astype(F32) * (2.0 * math.pi / gw)
    tab_c = jnp.concatenate([jnp.cos(ang), jnp.sin(ang)], axis=1).astype(BF16)
    return tab_t, tab_c


def _fft_kernel(f_ref, cs_ref, tab_ref, o_ref, data, *, t, gw, groups, scale):
    i = pl.program_id(1)

    @pl.when(i == 0)
    def _():
        rb = min(512, t)
        for r in range(t // rb):
            for g in range(groups):
                fg = f_ref[r * rb:(r + 1) * rb, g * gw:(g + 1) * gw]
                z = jnp.dot(fg, cs_ref[...], preferred_element_type=F32)
                data[r * rb:(r + 1) * rb, g * gw:(g + 1) * gw] = z[:, :gw].astype(BF16)
                data[t + r * rb:t + (r + 1) * rb, g * gw:(g + 1) * gw] = z[:, gw:].astype(BF16)

    y = jnp.dot(tab_ref[...], data[...], preferred_element_type=F32)
    o_ref[...] = (y * scale).astype(BF16)


def _fourier(p, col_tile, t, batch, tabs):
    tab_t, tab_c = tabs
    wf = 1024
    gw = wf // FFT_GROUPS
    tm = min(256, t)
    return pl.pallas_call(
        functools.partial(_fft_kernel, t=t, gw=gw, groups=FFT_GROUPS, scale=1.0 / math.sqrt(t * gw)),
        grid=(batch, t // tm),
        in_specs=[
            pl.BlockSpec((t, wf), lambda b, i: (b, col_tile)),
            pl.BlockSpec((gw, 2 * gw), lambda b, i: (0, 0)),
            pl.BlockSpec((tm, 2 * t), lambda b, i: (i, 0)),
        ],
        out_specs=pl.BlockSpec((tm, wf), lambda b, i: (b * (t // tm) + i, 0)),
        out_shape=jax.ShapeDtypeStruct((batch * t, wf), BF16),
        scratch_shapes=[pltpu.VMEM((2 * t, wf), BF16)],
        compiler_params=_cparams("arbitrary", "arbitrary"),
        name="fourier",
    )(p, tab_c, tab_t)


def _cmul_const(xr, xi, c, s):
    def scaled(v, k):
        if abs(k) < 1e-12:
            return None
        return v if abs(k - 1.0) < 1e-12 else -v if abs(k + 1.0) < 1e-12 else v * k

    def add(a, b):
        return b if a is None else a if b is None else a + b

    return add(scaled(xr, c), scaled(xi, -s)), add(scaled(xi, c), scaled(xr, s))


def _dft4(y):
    (ar, ai), (br, bi), (cr, ci), (dr, di) = y
    sr, si, tr, ti = ar + cr, ai + ci, ar - cr, ai - ci
    ur, ui, vr, vi = br + dr, bi + di, br - dr, bi - di
    return [(sr + ur, si + ui), (tr + vi, ti - vr), (sr - ur, si - ui), (tr - vi, ti + vr)]


def _dft16(x):
    out = [None] * 16
    p = [_dft4([x[4 * a + b] for a in range(4)]) for b in range(4)]
    for ka in range(4):
        q = []
        for b in range(4):
            ang = -2.0 * math.pi * ka * b / 16.0
            q.append(_cmul_const(p[b][ka][0], p[b][ka][1], math.cos(ang), math.sin(ang)))
        r = _dft4(q)
        for kb in range(4):
            out[ka + 4 * kb] = r[kb]
    return out


def _fft_fast_kernel(f_ref, csc_ref, tw_ref, cst_ref, o_ref, z, o_scr, *, n1, n2, gw, scale):
    slabs_per_dot = 4
    for q in range(n2 // slabs_per_dot):
        r0 = q * slabs_per_dot * n1
        zz = jnp.dot(f_ref[r0:r0 + slabs_per_dot * n1, :], csc_ref[...], preferred_element_type=F32)
        for s in range(slabs_per_dot):
            z[0, q * slabs_per_dot + s] = zz[s * n1:(s + 1) * n1, 0:gw]
            z[1, q * slabs_per_dot + s] = zz[s * n1:(s + 1) * n1, gw:2 * gw]

    def tile_body(r, carry):
        rows = pl.ds(pl.multiple_of(r * 8, 8), 8)
        for c in range(gw // LANES):
            cols = slice(c * LANES, (c + 1) * LANES)
            h = _dft16([(z[0, t, rows, cols], z[1, t, rows, cols]) for t in range(n2)])
            for k2 in range(n2):
                hr, hi = h[k2]
                if k2 > 0:
                    hr, hi = _cmul(hr, hi, tw_ref[0, k2, rows, :], tw_ref[1, k2, rows, :])
                z[0, k2, rows, cols] = hr
                z[1, k2, rows, cols] = hi
        return carry

    lax.fori_loop(0, n1 // 8, tile_body, 0)

    for k2 in range(n2):
        rhs = jnp.concatenate([z[0, k2], z[1, k2]], axis=0).astype(BF16)
        y = jnp.dot(cst_ref[...], rhs, preferred_element_type=F32) * scale
        for c in range(gw // LANES):
            o_scr[c, pl.ds(k2, n1, stride=n2), :] = y[:, c * LANES:(c + 1) * LANES]
    o_ref[...] = jnp.concatenate([o_scr[c] for c in range(gw // LANES)], axis=1).astype(BF16)


def _fourier_fast(p, col_tile, t, batch):
    n1, n2 = 256, 16
    assert t == n1 * n2
    wf = 1024
    gw = wf // FFT_GROUPS
    ang = lambda a, b, n: ((a[:, None] * b[None, :]) % n).astype(F32) * (2.0 * math.pi / n)
    kc = jnp.arange(gw, dtype=jnp.int32)
    a_c = ang(kc, kc, gw)
    csc = jnp.concatenate([jnp.cos(a_c), -jnp.sin(a_c)], axis=1).astype(BF16)
    k1 = jnp.arange(n1, dtype=jnp.int32)
    a_t = ang(k1, k1, n1)
    cst = jnp.concatenate([jnp.cos(a_t), jnp.sin(a_t)], axis=1).astype(BF16)
    a_w = ang(jnp.arange(n2, dtype=jnp.int32), k1, t)
    tw = jnp.stack([jnp.cos(a_w), -jnp.sin(a_w)], axis=0)
    tw = jnp.broadcast_to(tw[..., None], (2, n2, n1, LANES))
    ngrp = wf // gw
    return pl.pallas_call(
        functools.partial(_fft_fast_kernel, n1=n1, n2=n2, gw=gw, scale=1.0 / math.sqrt(t * gw)),
        grid=(batch, ngrp),
        in_specs=[
            pl.BlockSpec((t, gw), lambda b, g: (b, col_tile * ngrp + g)),
            pl.BlockSpec((gw, 2 * gw), lambda b, g: (0, 0)),
            pl.BlockSpec((2, n2, n1, LANES), lambda b, g: (0, 0, 0, 0), pipeline_mode=pl.Buffered(1)),
            pl.BlockSpec((n1, 2 * n1), lambda b, g: (0, 0)),
        ],
        out_specs=pl.BlockSpec((t, gw), lambda b, g: (b, g)),
        out_shape=jax.ShapeDtypeStruct((batch * t, wf), BF16),
        scratch_shapes=[
            pltpu.VMEM((2, n2, n1, gw), F32),
            pltpu.VMEM((gw // LANES, t, LANES), F32),
        ],
        compiler_params=_cparams("arbitrary", "arbitrary"),
        name="fourier_fast",
    )(p, csc, tw, cst)


def _branch_kernel(xa_ref, ba_ref, ca_ref, za_ref, zb_ref, zc_ref, g0_ref, g1_ref, g2_ref,
                   u_ref, ys_ref, yf_ref, cw_ref, sd_ref, wa_ref, wb_ref, fw_ref,
                   pa_ref, pb_ref, pc_ref, m_ref, *, row_len):
    tm = xa_ref.shape[0]
    f = lambda r: r[...].astype(F32)
    v = f(ca_ref) * f(xa_ref)
    pos = lax.broadcasted_iota(jnp.int32, v.shape, 0) % row_len
    v_prev = jnp.where(pos == 0, 0.0, pltpu.roll(v, 1, 0))
    v_next = jnp.where(pos == row_len - 1, 0.0, pltpu.roll(v, tm - 1, 0))
    cw = cw_ref[0]
    conv = v_prev * cw[0:1, :] + v * cw[1:2, :] + v_next * cw[2:3, :]
    a = f(ba_ref) * conv * _silu(f(za_ref))
    ya = jnp.dot(a.astype(BF16), pa_ref[0], preferred_element_type=F32)
    acc = _sigmoid(f(g0_ref)) * ya
    y = _gelu_tanh(ys_ref[...] + sd_ref[0] * f(u_ref)).astype(BF16)
    glu = (jnp.dot(y, wa_ref[0], preferred_element_type=F32)
           * _sigmoid(jnp.dot(y, wb_ref[0], preferred_element_type=F32)) * _silu(f(zb_ref)))
    yb = jnp.dot(glu.astype(BF16), pb_ref[0], preferred_element_type=F32)
    acc = acc + _sigmoid(f(g1_ref)) * yb
    c = jnp.dot(yf_ref[...], fw_ref[0], preferred_element_type=F32) * _silu(f(zc_ref))
    yc = jnp.dot(c.astype(BF16), pc_ref[0], preferred_element_type=F32)
    acc = acc + _sigmoid(f(g2_ref)) * yc
    m_ref[...] = acc.astype(BF16)


def _branches(p, ys, yf, conv_w, ssm_d, wa, wb, fw, pa, pb, pc, layer, row_len):
    n = p.shape[0]
    wc = conv_w.shape[-1]
    d = pa.shape[-1]
    tm = 256
    tn = 1024
    col = lambda k: pl.BlockSpec((tm, tn), lambda i: (i, k))
    const = lambda shape: pl.BlockSpec(shape, lambda i: (layer,) + (0,) * (len(shape) - 1),
                                       pipeline_mode=pl.Buffered(1))
    g_tile0 = 8
    return pl.pallas_call(
        functools.partial(_branch_kernel, row_len=row_len),
        grid=(n // tm,),
        in_specs=[
            col(0), col(1), col(2), col(3), col(5), col(7),
            pl.BlockSpec((tm, d), lambda i: (i, g_tile0 * tn // d)),
            pl.BlockSpec((tm, d), lambda i: (i, g_tile0 * tn // d + 1)),
            pl.BlockSpec((tm, d), lambda i: (i, g_tile0 * tn // d + 2)),
            col(4),
            pl.BlockSpec((tm, tn), lambda i: (i, 0)),
            pl.BlockSpec((tm, tn), lambda i: (i, 0)),
            const((1, 3, wc)), const((1, 1, tn)),
            const((1, tn, tn)), const((1, tn, tn)), const((1, tn, tn)),
            const((1, wc, d)), const((1, tn, d)), const((1, tn, d)),
        ],
        out_specs=pl.BlockSpec((tm, d), lambda i: (i, 0)),
        out_shape=jax.ShapeDtypeStruct((n, d), BF16),
        compiler_params=_cparams("arbitrary"),
        name="branches",
    )(p, p, p, p, p, p, p, p, p, p, ys, yf, conv_w, ssm_d.reshape(ssm_d.shape[0], 1, -1),
      wa, wb, fw, pa, pb, pc)


def _out_kernel(*refs, with_next):
    if with_next:
        m_ref, x_ref, gt_ref, g_ref, w_ref, scn_ref, shn_ref, gn_ref, o_ref, h_ref = refs
    else:
        m_ref, x_ref, gt_ref, g_ref, w_ref, o_ref = refs
    o = jnp.dot(m_ref[...], w_ref[0], preferred_element_type=F32)
    ms = jnp.mean(o * o, axis=-1, keepdims=True)
    y = o * lax.rsqrt(ms + RMS_EPS) * g_ref[0]
    xn = x_ref[...] + gt_ref[0] * y
    o_ref[...] = xn
    if with_next:
        h_ref[...] = _modnorm(xn, gn_ref[0], scn_ref[0], shn_ref[0]).astype(BF16)


def _outproj(m, x2, gt, g_post, w_out_b, layer, rows_per_batch, next_mod=None):
    n, d = x2.shape
    tm = min(512, rows_per_batch)
    tpb = rows_per_batch // tm
    per_batch = pl.BlockSpec((1, 1, d), lambda i: (i // tpb, 0, 0))
    in_specs = [
        pl.BlockSpec((tm, d), lambda i: (i, 0)),
        pl.BlockSpec((tm, d), lambda i: (i, 0)),
        per_batch,
        pl.BlockSpec((1, 1, d), lambda i: (layer, 0, 0)),
        pl.BlockSpec((1, d, d), lambda i: (layer, 0, 0), pipeline_mode=pl.Buffered(1)),
    ]
    args = [m, x2, gt, g_post.reshape(g_post.shape[0], 1, d), w_out_b]
    out_specs = [pl.BlockSpec((tm, d), lambda i: (i, 0))]
    out_shape = [jax.ShapeDtypeStruct((n, d), F32)]
    if next_mod is not None:
        sc_n, sh_n, g_pre = next_mod
        in_specs += [per_batch, per_batch, pl.BlockSpec((1, 1, d), lambda i: (layer + 1, 0, 0))]
        args += [sc_n, sh_n, g_pre.reshape(g_pre.shape[0], 1, d)]
        out_specs.append(pl.BlockSpec((tm, d), lambda i: (i, 0)))
        out_shape.append(jax.ShapeDtypeStruct((n, d), BF16))
    res = pl.pallas_call(
        functools.partial(_out_kernel, with_next=next_mod is not None),
        grid=(n // tm,),
        in_specs=in_specs,
        out_specs=out_specs,
        out_shape=out_shape,
        compiler_params=_cparams("arbitrary"),
        name="outproj",
    )(*args)
    return (res[0], res[1]) if next_mod is not None else (res[0], None)


def kernel(x, c, ctx, c_ctx, w_ada, b_ada, g_pre, g_post, w_in, conv_w, ssm_lam_re, ssm_lam_im, ssm_log_dt,
           ssm_b_re, ssm_b_im, ssm_c_re, ssm_c_im, ssm_d, glu_wa, glu_wb, fourier_w, proj_a, proj_b, proj_c,
           w_out):
    bn, t, d = x.shape
    lc = ctx.shape[1]
    depth = w_ada.shape[0]
    w_conv = conv_w.shape[-1]
    w_ssm = ssm_d.shape[-1]
    tn = 1024
    assert w_conv == tn and w_ssm == tn and fourier_w.shape[-1] == tn and d % tn == 0
    assert bn + 1 <= 8 and t % GRID_W == 0
    u_tile = 4 * w_conv // tn
    f_tile = u_tile + 2
    n_tiles = w_in.shape[-1] // tn

    cc = jnp.concatenate([c, c_ctx[None, :], jnp.zeros((8 - bn - 1, d), F32)], axis=0)
    mods = _ada_mods(cc, w_ada, b_ada)

    to_b = lambda w: w.astype(BF16)
    wa_b, wb_b, fw_b = to_b(glu_wa), to_b(glu_wb), to_b(fourier_w)
    pa_b, pb_b, pc_b, wo_b = to_b(proj_a), to_b(proj_b), to_b(proj_c), to_b(w_out)
    fast_dft = t == 256 * 16
    tabs_x = None if fast_dft else _dft_tables(t, tn // FFT_GROUPS)
    tabs_c = _dft_tables(lc, tn // FFT_GROUPS)

    x2 = x.reshape(bn * t, d)
    c2 = ctx.reshape(bn * lc, d)
    def mod_x(l):
        return tuple(mods[l, :bn, k * d:(k + 1) * d].reshape(bn, 1, d) for k in range(3))

    def mod_c(l):
        return tuple(jnp.broadcast_to(mods[l, bn, k * d:(k + 1) * d], (bn, 1, d)) for k in range(3))

    hx = _prenorm(x2, mod_x(0)[1], mod_x(0)[0], g_pre, 0, t)
    hc = _prenorm(c2, mod_c(0)[1], mod_c(0)[0], g_pre, 0, lc)
    for l in range(depth):
        last = l == depth - 1
        sh_x, sc_x, gt_x = mod_x(l)
        sh_c, sc_c, gt_c = mod_c(l)
        next_x = None if last else (mod_x(l + 1)[1], mod_x(l + 1)[0], g_pre)
        next_c = None if last else (mod_c(l + 1)[1], mod_c(l + 1)[0], g_pre)
        ssm_p = _ssm_params(ssm_lam_re[l], ssm_lam_im[l], ssm_log_dt[l], ssm_b_re[l], ssm_b_im[l],
                            ssm_c_re[l], ssm_c_im[l])

        px = _inproj(hx, w_in, l, 0, n_tiles)
        pc_ = _inproj(hc, w_in, l, u_tile, 1) if last else _inproj(hc, w_in, l, 0, n_tiles)
        u_lane_tile = u_tile * (tn // LANES)
        ys_x, ys_c = _ssm_scan(px, pc_, u_lane_tile, 0 if last else u_lane_tile, w_ssm, ssm_p, bn,
                               ctx_out=not last)

        yf_x = _fourier_fast(px, f_tile, t, bn) if fast_dft else _fourier(px, f_tile, t, bn, tabs_x)
        mx = _branches(px, ys_x, yf_x, conv_w, ssm_d, wa_b, wb_b, fw_b, pa_b, pb_b, pc_b, l, GRID_W)
        new_x2, hx = _outproj(mx, x2, gt_x, g_post, wo_b, l, t, next_x)
        if not last:
            yf_c = _fourier(pc_, f_tile, lc, bn, tabs_c)
            mc = _branches(pc_, ys_c, yf_c, conv_w, ssm_d, wa_b, wb_b, fw_b, pa_b, pb_b, pc_b, l, lc)
            c2, hc = _outproj(mc, c2, gt_c, g_post, wo_b, l, lc, next_c)
        x2 = new_x2
    return x2.reshape(bn, t, d)
```

```python
import functools
import math

import jax
import jax.numpy as jnp
from jax import lax
from jax.experimental import pallas as pl
from jax.experimental.pallas import tpu as pltpu

F32 = jnp.float32
BF16 = jnp.bfloat16

GRID_W = 64
FFT_GROUPS = 4
N_BRANCH = 3
RMS_EPS = 1e-6
LANES = 128
SSM_CHUNK = 8
SSM_SEGS = 8
DFT_ROWS = 64
DFT_N1, DFT_N2 = 256, 16
VMEM_LIMIT = 56 * 1024 * 1024
COL_TILE = 1024
N_BRANCH_TILES = 8
TM_INPROJ, TM_PRENORM, TM_BRANCH, TM_OUT, TM_DFT = 1024, 1024, 256, 512, 256


def _cparams(*sem):
    return pltpu.CompilerParams(dimension_semantics=sem, vmem_limit_bytes=VMEM_LIMIT)


def _sigmoid(v):
    return 0.5 * jnp.tanh(0.5 * v) + 0.5


def _silu(v):
    return v * _sigmoid(v)


def _gelu_tanh(v):
    return v * (0.5 * (1.0 + jnp.tanh(math.sqrt(2.0 / math.pi) * (v + 0.044715 * (v * v * v)))))


def _cmul(ar, ai, br, bi):
    return ar * br - ai * bi, ar * bi + ai * br


def _cmul_add(ar, ai, zr, zi, sr, si):
    return ar * zr - ai * zi + sr, ar * zi + ai * zr + si


def _ada_kernel(c_ref, w_ref, b_ref, o_ref):
    c = c_ref[...]
    s = _silu(c).astype(BF16)
    o_ref[0] = jnp.dot(s, w_ref[0].astype(BF16), preferred_element_type=F32) + b_ref[0]


def _ada_mods(cc, w_ada, b_ada):
    depth, d, n3 = w_ada.shape
    tn = 2 * COL_TILE if n3 % (2 * COL_TILE) == 0 else COL_TILE
    assert n3 % tn == 0
    return pl.pallas_call(
        _ada_kernel,
        grid=(depth, n3 // tn),
        in_specs=[
            pl.BlockSpec((8, d), lambda l, j: (0, 0)),
            pl.BlockSpec((1, d, tn), lambda l, j: (l, 0, j)),
            pl.BlockSpec((1, 1, tn), lambda l, j: (l, 0, j)),
        ],
        out_specs=pl.BlockSpec((1, 8, tn), lambda l, j: (l, 0, j)),
        out_shape=jax.ShapeDtypeStruct((depth, 8, n3), F32),
        compiler_params=_cparams("arbitrary", "arbitrary"),
        name="ada_mods",
    )(cc, w_ada, b_ada.reshape(depth, 1, n3))


def _modnorm(x, g, sc, sh):
    ms = jnp.mean(x * x, axis=-1, keepdims=True)
    return (x * lax.rsqrt(ms + RMS_EPS) * g) * (1.0 + sc) + sh


def _prenorm_kernel(x_ref, sc_ref, sh_ref, g_ref, h_ref):
    rb = 64

    def body(r, carry):
        rows = pl.ds(pl.multiple_of(r * rb, rb), rb)
        h_ref[rows, :] = _modnorm(x_ref[rows, :], g_ref[0], sc_ref[0], sh_ref[0]).astype(BF16)
        return carry

    lax.fori_loop(0, x_ref.shape[0] // rb, body, 0)


def _prenorm(x2, sc, sh, g_pre, layer, rows_per_batch):
    n, d = x2.shape
    tm = min(TM_PRENORM, rows_per_batch)
    tpb = rows_per_batch // tm
    return pl.pallas_call(
        _prenorm_kernel,
        grid=(n // tm,),
        in_specs=[
            pl.BlockSpec((tm, d), lambda i: (i, 0)),
            pl.BlockSpec((1, 1, d), lambda i: (i // tpb, 0, 0)),
            pl.BlockSpec((1, 1, d), lambda i: (i // tpb, 0, 0)),
            pl.BlockSpec((1, 1, d), lambda i: (layer, 0, 0)),
        ],
        out_specs=pl.BlockSpec((tm, d), lambda i: (i, 0)),
        out_shape=jax.ShapeDtypeStruct((n, d), BF16),
        compiler_params=_cparams("arbitrary"),
        name="prenorm",
    )(x2, sc, sh, g_pre.reshape(g_pre.shape[0], 1, d))


def _inproj_kernel(h_ref, w_ref, p_ref, wb):
    @pl.when(pl.program_id(1) == 0)
    def _():
        wb[...] = w_ref[0].astype(BF16)

    p_ref[...] = jnp.dot(h_ref[...], wb[...], preferred_element_type=F32).astype(BF16)


def _inproj(h, w_in, layer, tile0, ntiles):
    n, d = h.shape
    tn = COL_TILE
    tm = min(TM_INPROJ, n)
    return pl.pallas_call(
        _inproj_kernel,
        grid=(ntiles, n // tm),
        in_specs=[
            pl.BlockSpec((tm, d), lambda j, i: (i, 0)),
            pl.BlockSpec((1, d, tn), lambda j, i: (layer, 0, tile0 + j)),
        ],
        out_specs=pl.BlockSpec((tm, tn), lambda j, i: (i, j)),
        out_shape=jax.ShapeDtypeStruct((n, ntiles * tn), BF16),
        scratch_shapes=[pltpu.VMEM((d, tn), BF16)],
        compiler_params=_cparams("arbitrary", "arbitrary"),
        name="inproj",
    )(h, w_in)


def _ssm_params(lam_re, lam_im, log_dt, b_re, b_im, c_re, c_im):
    _, G, P = lam_re.shape
    H = b_re.shape[-1]
    gpt = LANES // H
    O = G // gpt
    lam_re = lam_re.astype(F32)
    lam_im = lam_im.astype(F32)
    dt = jnp.exp(log_dt.astype(F32))[..., None]
    lr = lam_re * dt
    li = lam_im * dt
    mag = jnp.exp(lr)
    a_re = mag * jnp.cos(li)
    a_im = mag * jnp.sin(li)
    n_re = a_re - 1.0
    n_im = a_im
    den = lam_re * lam_re + lam_im * lam_im
    q_re = (n_re * lam_re + n_im * lam_im) / den
    q_im = (n_im * lam_re - n_re * lam_im) / den
    bb_re = q_re[..., None] * b_re - q_im[..., None] * b_im
    bb_im = q_re[..., None] * b_im + q_im[..., None] * b_re

    bbs = jnp.stack([bb_re, bb_im], axis=1).reshape(2, 2, O, gpt, P, H)
    bc = jnp.transpose(bbs, (2, 0, 1, 5, 3, 4)).reshape(O, 4, H, gpt * P)
    cs = jnp.stack([c_re.astype(F32), c_im.astype(F32)], axis=1).reshape(2, 2, O, gpt, H, P)
    cct = jnp.transpose(cs, (2, 0, 1, 3, 5, 4)).reshape(O, 4, gpt * P, 1, H)
    cct = jnp.broadcast_to(cct, (O, 4, gpt * P, gpt, H)).reshape(O, 4, gpt * P, LANES)
    a4 = jnp.transpose(jnp.stack([a_re, a_im], axis=1).reshape(2, 2, O, gpt * P), (2, 0, 1, 3)).reshape(O, 4, gpt * P)
    arow = jnp.concatenate([a4, jnp.zeros_like(a4)], axis=1)
    acol = jnp.broadcast_to(a4[..., None], (O, 4, gpt * P, LANES))

    return bc, cct, arow, acol


def _block_transpose8(v):
    bw = LANES // 8
    blk = lax.broadcasted_iota(jnp.int32, v[0].shape, 1) // bw
    for dist in (4, 2, 1):
        upper = (blk & dist) != 0
        nxt = list(v)
        for a in range(8):
            if a & dist:
                continue
            b = a + dist
            nxt[a] = jnp.where(upper, pltpu.roll(v[b], dist * bw, 1), v[a])
            nxt[b] = jnp.where(upper, v[b], pltpu.roll(v[a], LANES - dist * bw, 1))
        v = nxt
    return v


def _ssm_build_operators(bc_ref, cct_ref, arow_ref, acol_ref, ws, ws_lo, w3):
    L = SSM_CHUNK
    hh = bc_ref.shape[2]
    sw = bc_ref.shape[3]
    pp = sw // (LANES // hh)
    npair = sw // LANES
    assert L * hh == LANES and 2 * pp == LANES
    lane = lax.broadcasted_iota(jnp.int32, (LANES, LANES), 1)
    row = lax.broadcasted_iota(jnp.int32, (LANES, LANES), 0)
    lane_g0, row_g0 = lane < pp, row < pp
    lane_blk = lane // hh
    zeros = jnp.zeros((LANES, LANES), F32)
    zeros_b = jnp.zeros((LANES, LANES), BF16)

    def powers(re, im, n):
        out = [(jnp.ones_like(re), jnp.zeros_like(re))]
        for _ in range(n):
            out.append(_cmul(out[-1][0], out[-1][1], re, im))
        return out

    def split(v):
        hi = v.astype(BF16)
        return hi, (v - hi.astype(F32)).astype(BF16)

    def shift_rows(t, blocks):
        n = abs(blocks) * hh
        if n == 0:
            return t
        pad = jnp.zeros((n, LANES), F32)
        return jnp.concatenate([pad, t[:LANES - n]] if blocks > 0 else [t[n:], pad], axis=0)

    for q in range(npair):
        ql = slice(q * LANES, (q + 1) * LANES)
        taps = []
        for d in range(2):
            b_re = jnp.concatenate([bc_ref[0, 2 * d, :, ql]] * L, axis=0)
            b_im = jnp.concatenate([bc_ref[0, 2 * d + 1, :, ql]] * L, axis=0)
            prow = powers(arow_ref[0, 2 * d:2 * d + 1, ql], arow_ref[0, 2 * d + 1:2 * d + 2, ql], L - 1)
            exps = [L - 1 - j if d == 0 else j for j in range(L)]
            a_re = jnp.concatenate([jnp.broadcast_to(prow[e][0], (hh, LANES)) for e in exps], axis=0)
            a_im = jnp.concatenate([jnp.broadcast_to(prow[e][1], (hh, LANES)) for e in exps], axis=0)
            for part, v in zip((2 * d, 2 * d + 1), _cmul(b_re, b_im, a_re, a_im)):
                cols = slice(part * LANES, (part + 1) * LANES)
                for dst, piece in zip((ws, ws_lo), split(v)):
                    dst[q, 0:LANES, cols] = jnp.where(lane_g0, piece, zeros_b)
                    dst[q, LANES:2 * LANES, cols] = jnp.where(lane_g0, zeros_b, piece)
            c_re = cct_ref[0, 2 * d, ql, :]
            c_im = cct_ref[0, 2 * d + 1, ql, :]
            (cr_hi, cr_lo), (ci_hi, ci_lo) = split(c_re), split(-c_im)
            c_hi = jnp.concatenate([cr_hi, ci_hi], axis=0)
            c_lo = jnp.concatenate([cr_lo, ci_lo], axis=0)
            dcols = slice(2 * d * LANES, (2 * d + 2) * LANES)
            taps.append(jnp.dot(jnp.concatenate([ws[q, :, dcols], ws[q, :, dcols], ws_lo[q, :, dcols]], axis=1),
                                jnp.concatenate([c_hi, c_lo, c_hi], axis=0), preferred_element_type=F32))
            pcol = powers(acol_ref[0, 2 * d, ql, :], acol_ref[0, 2 * d + 1, ql, :], L)
            ap_re, ap_im = zeros, zeros
            for jo in range(L):
                e = jo + 1 if d == 0 else L - jo
                ap_re = jnp.where(lane_blk == jo, pcol[e][0], ap_re)
                ap_im = jnp.where(lane_blk == jo, pcol[e][1], ap_im)
            ca_re, ca_im = _cmul(c_re, c_im, ap_re, ap_im)
            for part, v in ((2 * d, ca_re), (2 * d + 1, -ca_im)):
                r0 = 2 * LANES + part * LANES
                w3[q, r0:r0 + LANES, 0:LANES] = jnp.where(row_g0, v, zeros).astype(BF16)
                w3[q, r0:r0 + LANES, LANES:2 * LANES] = jnp.where(row_g0, zeros, v).astype(BF16)
        for g2 in range(2):
            tf = taps[0][g2 * LANES:(g2 + 1) * LANES]
            tb = taps[1][g2 * LANES:(g2 + 1) * LANES]
            m = zeros
            for jo in range(L):
                m = jnp.where(lane_blk == jo, shift_rows(tf, jo - (L - 1)) + shift_rows(tb, jo), m)
            rows = slice(g2 * LANES, (g2 + 1) * LANES)
            w3[q, rows, g2 * LANES:(g2 + 1) * LANES] = m.astype(BF16)
            w3[q, rows, (1 - g2) * LANES:(2 - g2) * LANES] = zeros_b


def _ssm_build_powers(arow_ref, ap, pw, seg_len):
    L = SSM_CHUNK
    sw = arow_ref.shape[2]
    assert L & (L - 1) == 0 and seg_len & (seg_len - 1) == 0 and seg_len >= 8
    for d in range(2):
        cr, ci = slice(2 * d * sw, (2 * d + 1) * sw), slice((2 * d + 1) * sw, (2 * d + 2) * sw)
        re, im = arow_ref[0, 2 * d:2 * d + 1, :], arow_ref[0, 2 * d + 1:2 * d + 2, :]
        for _ in range(L.bit_length() - 1):
            re, im = _cmul(re, im, re, im)
        ap[0:1, cr] = re
        ap[0:1, ci] = im
        p = [(jnp.ones_like(re), jnp.zeros_like(re))]
        for _ in range(7):
            p.append(_cmul(p[-1][0], p[-1][1], re, im))
        order = list(range(8)) if d == 0 else list(range(7, -1, -1))
        first = slice(0, 8) if d == 0 else slice(seg_len - 8, seg_len)
        pw[first, cr] = jnp.concatenate([p[r][0] for r in order], axis=0)
        pw[first, ci] = jnp.concatenate([p[r][1] for r in order], axis=0)
        sq = _cmul(p[4][0], p[4][1], p[4][0], p[4][1])
        m = 8
        while m < seg_len:
            src = slice(0, m) if d == 0 else slice(seg_len - m, seg_len)
            dst = slice(m, 2 * m) if d == 0 else slice(seg_len - 2 * m, seg_len - m)
            xr, xi = _cmul(pw[src, cr], pw[src, ci], sq[0], sq[1])
            pw[dst, cr] = xr
            pw[dst, ci] = xi
            sq = _cmul(sq[0], sq[1], sq[0], sq[1])
            m *= 2
        ap[1:2, cr] = sq[0]
        ap[1:2, ci] = sq[1]


def _seg_pitch(seg_len):
    tiles = seg_len // 8 + 1
    return 8 * (tiles + 1 - tiles % 2)


def _ssm_kernel(*refs, nc, nx, seg_len, ctx_out):
    if ctx_out:
        (ux_ref, uc_ref, bc_ref, cct_ref, arow_ref, acol_ref,
         yx_ref, yc_ref, ws, ws_lo, w3, lhs, st, uxf, ucf, ap, pw) = refs
    else:
        (ux_ref, uc_ref, bc_ref, cct_ref, arow_ref, acol_ref,
         yx_ref, ws, ws_lo, w3, lhs, st, uxf, ucf, ap, pw) = refs
        yc_ref = None
    L = SSM_CHUNK
    nlt = st.shape[0] // 4
    sw = nlt * LANES
    xw = 2 * LANES
    FR, FI, BR, BI = range(4)
    nseg = SSM_SEGS
    pitch = _seg_pitch(seg_len)

    @pl.when(pl.program_id(1) == 0)
    def _():
        _ssm_build_operators(bc_ref, cct_ref, arow_ref, acol_ref, ws, ws_lo, w3)
        _ssm_build_powers(arow_ref, ap, pw, seg_len)

    def ld(part, rows):
        return jnp.concatenate([st[part * nlt + q, rows, :] for q in range(nlt)], axis=1)

    def sto(part, rows, val):
        for q in range(nlt):
            st[part * nlt + q, rows, :] = val[:, q * LANES:(q + 1) * LANES]

    def part_cols(part):
        return slice(part * sw, (part + 1) * sw)

    ucf[...] = uc_ref[...].astype(F32)
    uxf[...] = ux_ref[...].astype(F32)
    by_pos = [jnp.concatenate([ucf[pl.ds(j, nc, stride=L), :], uxf[pl.ds(j, nx, stride=L), :]], axis=0)
              for j in range(L)]
    for g, tile_g in enumerate(_block_transpose8(by_pos)):
        lhs[g // 2, :, (g % 2) * LANES:(g % 2 + 1) * LANES] = tile_g.astype(BF16)

    for q in range(nlt):
        s = jnp.dot(lhs[q, :, 0:xw], ws[q], preferred_element_type=F32)
        for part in range(4):
            k = part * nlt + q
            st[k, 0:nc, :] = s[0:nc, part * LANES:(part + 1) * LANES]
            for g in range(nseg):
                st[k, nc + g * pitch:nc + g * pitch + seg_len, :] = (
                    s[nc + g * seg_len:nc + (g + 1) * seg_len, part * LANES:(part + 1) * LANES])

    a_l = ap[0:1, :]
    a_seg = ap[1:2, :]
    al = [a_l[:, part_cols(p)] for p in range(4)]
    aseg = [a_seg[:, part_cols(p)] for p in range(4)]

    zero = jnp.zeros((1, sw), F32)

    def ctx_step(pr, pi):
        def step(i, carry):
            er, ei = carry
            row = pl.ds(i, 1)
            sr, si = ld(pr, row), ld(pi, row)
            sto(pr, row, er)
            sto(pi, row, ei)
            return _cmul_add(al[pr], al[pi], er, ei, sr, si)
        return step

    h0f = lax.fori_loop(0, nc, ctx_step(FR, FI), (zero, zero))
    bwd_step = ctx_step(BR, BI)
    h0b = lax.fori_loop(0, nc, lambda k, carry: bwd_step(nc - 1 - k, carry), (zero, zero))
    for part in range(4):
        for q in range(nlt):
            lhs[q, 0:nc, xw + part * LANES:xw + (part + 1) * LANES] = st[part * nlt + q, 0:nc, :].astype(BF16)

    zseg = jnp.zeros((nseg, sw), F32)

    def seg_pass(pr, pi, order):
        zr, zi = zseg, zseg
        for i in order:
            rows = pl.ds(nc + i, nseg, stride=pitch)
            sr, si = ld(pr, rows), ld(pi, rows)
            sto(pr, rows, zr)
            sto(pi, rows, zi)
            zr, zi = _cmul_add(al[pr], al[pi], zr, zi, sr, si)
        return zr, zi

    zfr, zfi = seg_pass(FR, FI, range(seg_len))
    zbr, zbi = seg_pass(BR, BI, range(seg_len - 1, -1, -1))

    ef = [h0f]
    for s in range(nseg - 1):
        ef.append(_cmul_add(aseg[FR], aseg[FI], ef[s][0], ef[s][1], zfr[s:s + 1], zfi[s:s + 1]))
    eb = [None] * nseg
    eb[nseg - 1] = h0b
    for s in range(nseg - 1, 0, -1):
        eb[s - 1] = _cmul_add(aseg[BR], aseg[BI], eb[s][0], eb[s][1], zbr[s:s + 1], zbi[s:s + 1])
    for s in range(nseg):
        srows = slice(nc + s * pitch, nc + s * pitch + seg_len)
        rows = slice(nc + s * seg_len, nc + (s + 1) * seg_len)
        for (pr, pi, e) in ((FR, FI, ef[s]), (BR, BI, eb[s])):
            xr, xi = _cmul_add(pw[:, part_cols(pr)], pw[:, part_cols(pi)], e[0], e[1],
                               ld(pr, srows), ld(pi, srows))
            for q in range(nlt):
                ql = slice(q * LANES, (q + 1) * LANES)
                lhs[q, rows, xw + pr * LANES:xw + (pr + 1) * LANES] = xr[:, ql].astype(BF16)
                lhs[q, rows, xw + pi * LANES:xw + (pi + 1) * LANES] = xi[:, ql].astype(BF16)

    by_group = []
    for q in range(nlt):
        y = jnp.dot(lhs[q], w3[q], preferred_element_type=F32)
        by_group += [y[:, 0:LANES], y[:, LANES:2 * LANES]]
    for j, tile_j in enumerate(_block_transpose8(by_group)):
        if ctx_out:
            yc_ref[pl.ds(j, nc, stride=L), :] = tile_j[0:nc]
        yx_ref[pl.ds(j, nx, stride=L), :] = tile_j[nc:nc + nx]


def _ssm_scan(px, pc, col_x, col_c, w, params, batch, ctx_out):
    bc, cct, arow, acol = params
    L = SSM_CHUNK
    t = px.shape[0] // batch
    lc = pc.shape[0] // batch
    nx, nc = t // L, lc // L
    seg_len = nx // SSM_SEGS
    o = w // LANES
    sw = bc.shape[-1]
    sdim = 4 * sw
    npair = sw // LANES
    out_shape = [jax.ShapeDtypeStruct((batch * t, w), F32)]
    out_specs = [pl.BlockSpec((t, LANES), lambda oi, b: (b, oi))]
    if ctx_out:
        out_shape.append(jax.ShapeDtypeStruct((batch * lc, w), F32))
        out_specs.append(pl.BlockSpec((lc, LANES), lambda oi, b: (b, oi)))
    tile = lambda a: pl.BlockSpec((1,) + a.shape[1:], lambda oi, b: (oi,) + (0,) * (a.ndim - 1))
    res = pl.pallas_call(
        functools.partial(_ssm_kernel, nc=nc, nx=nx, seg_len=seg_len, ctx_out=ctx_out),
        grid=(o, batch),
        in_specs=[
            pl.BlockSpec((t, LANES), lambda oi, b: (b, col_x + oi)),
            pl.BlockSpec((lc, LANES), lambda oi, b: (b, col_c + oi)),
            tile(bc), tile(cct), tile(arow), tile(acol),
        ],
        out_specs=out_specs,
        out_shape=out_shape,
        scratch_shapes=[
            pltpu.VMEM((npair, 2 * LANES, 4 * LANES), BF16),
            pltpu.VMEM((npair, 2 * LANES, 4 * LANES), BF16),
            pltpu.VMEM((npair, 6 * LANES, 2 * LANES), BF16),
            pltpu.VMEM((npair, nc + nx, 6 * LANES), BF16),
            pltpu.VMEM((sdim // LANES, nc + SSM_SEGS * _seg_pitch(seg_len), LANES), F32),
            pltpu.VMEM((t, LANES), F32),
            pltpu.VMEM((lc, LANES), F32),
            pltpu.VMEM((8, sdim), F32),
            pltpu.VMEM((seg_len, sdim), F32),
        ],
        compiler_params=_cparams("arbitrary", "arbitrary"),
        name="ssm_scan",
    )(px, pc, bc, cct, arow, acol)
    return (res[0], res[1]) if ctx_out else (res[0], None)


def _dft_table_kernel(e1_ref, e2_ref, o_ref, *, t):
    e1c, e1s = e1_ref[0, :, 0:t], e1_ref[0, :, t:2 * t]
    e2c, e2s = e2_ref[:, 0:t], e2_ref[:, t:2 * t]
    c, s = _cmul(e2c, e2s, e1c, e1s)
    o_ref[:, 0:t] = c.astype(BF16)
    o_ref[:, t:2 * t] = (-s).astype(BF16)


def _dft_tables(t, gw):
    def cis(rows_mult, nrows):
        k = jnp.arange(nrows, dtype=jnp.int32)[:, None] * rows_mult
        n = jnp.arange(t, dtype=jnp.int32)[None, :]
        ang = ((k * n) % t).astype(F32) * (2.0 * math.pi / t)
        return jnp.concatenate([jnp.cos(ang), jnp.sin(ang)], axis=1)
    r = DFT_ROWS
    e1 = cis(r, t // r).reshape(t // r, 1, 2 * t)
    e2 = cis(1, r)
    tab_t = pl.pallas_call(
        functools.partial(_dft_table_kernel, t=t),
        grid=(t // r,),
        in_specs=[
            pl.BlockSpec((1, 1, 2 * t), lambda i: (i, 0, 0)),
            pl.BlockSpec((r, 2 * t), lambda i: (0, 0)),
        ],
        out_specs=pl.BlockSpec((r, 2 * t), lambda i: (i, 0)),
        out_shape=jax.ShapeDtypeStruct((t, 2 * t), BF16),
        compiler_params=_cparams("arbitrary"),
        name="dft_table",
    )(e1, e2)
    kc = jnp.arange(gw, dtype=jnp.int32)
    ang = ((kc[:, None] * kc[None, :]) % gw).astype(F32) * (2.0 * math.pi / gw)
    tab_c = jnp.concatenate([jnp.cos(ang), jnp.sin(ang)], axis=1).astype(BF16)
    return tab_t, tab_c


def _fft_kernel(f_ref, cs_ref, tab_ref, o_ref, data, *, t, gw, groups, scale):
    i = pl.program_id(1)

    @pl.when(i == 0)
    def _():
        rb = min(512, t)
        for r in range(t // rb):
            for g in range(groups):
                fg = f_ref[r * rb:(r + 1) * rb, g * gw:(g + 1) * gw]
                z = jnp.dot(fg, cs_ref[...], preferred_element_type=F32)
                data[r * rb:(r + 1) * rb, g * gw:(g + 1) * gw] = z[:, :gw].astype(BF16)
                data[t + r * rb:t + (r + 1) * rb, g * gw:(g + 1) * gw] = z[:, gw:].astype(BF16)

    y = jnp.dot(tab_ref[...], data[...], preferred_element_type=F32)
    o_ref[...] = (y * scale).astype(BF16)


def _fourier(p, col_tile, t, batch, tabs):
    tab_t, tab_c = tabs
    wf = COL_TILE
    gw = wf // FFT_GROUPS
    tm = min(TM_DFT, t)
    return pl.pallas_call(
        functools.partial(_fft_kernel, t=t, gw=gw, groups=FFT_GROUPS, scale=1.0 / math.sqrt(t * gw)),
        grid=(batch, t // tm),
        in_specs=[
            pl.BlockSpec((t, wf), lambda b, i: (b, col_tile)),
            pl.BlockSpec((gw, 2 * gw), lambda b, i: (0, 0)),
            pl.BlockSpec((tm, 2 * t), lambda b, i: (i, 0)),
        ],
        out_specs=pl.BlockSpec((tm, wf), lambda b, i: (b * (t // tm) + i, 0)),
        out_shape=jax.ShapeDtypeStruct((batch * t, wf), BF16),
        scratch_shapes=[pltpu.VMEM((2 * t, wf), BF16)],
        compiler_params=_cparams("arbitrary", "arbitrary"),
        name="fourier",
    )(p, tab_c, tab_t)


def _cmul_const(xr, xi, c, s):
    def scaled(v, k):
        if abs(k) < 1e-12:
            return None
        return v if abs(k - 1.0) < 1e-12 else -v if abs(k + 1.0) < 1e-12 else v * k

    def add(a, b):
        return b if a is None else a if b is None else a + b

    return add(scaled(xr, c), scaled(xi, -s)), add(scaled(xi, c), scaled(xr, s))


def _dft4(y):
    (ar, ai), (br, bi), (cr, ci), (dr, di) = y
    sr, si, tr, ti = ar + cr, ai + ci, ar - cr, ai - ci
    ur, ui, vr, vi = br + dr, bi + di, br - dr, bi - di
    return [(sr + ur, si + ui), (tr + vi, ti - vr), (sr - ur, si - ui), (tr - vi, ti + vr)]


def _dft16(x):
    out = [None] * 16
    p = [_dft4([x[4 * a + b] for a in range(4)]) for b in range(4)]
    for ka in range(4):
        q = []
        for b in range(4):
            ang = -2.0 * math.pi * ka * b / 16.0
            q.append(_cmul_const(p[b][ka][0], p[b][ka][1], math.cos(ang), math.sin(ang)))
        r = _dft4(q)
        for kb in range(4):
            out[ka + 4 * kb] = r[kb]
    return out


def _fft_fast_kernel(f_ref, csc_ref, tw_ref, cst_ref, o_ref, z, o_scr, *, n1, n2, gw, scale):
    slabs_per_dot = 4
    for q in range(n2 // slabs_per_dot):
        r0 = q * slabs_per_dot * n1
        zz = jnp.dot(f_ref[r0:r0 + slabs_per_dot * n1, :], csc_ref[...], preferred_element_type=F32)
        for s in range(slabs_per_dot):
            z[0, q * slabs_per_dot + s] = zz[s * n1:(s + 1) * n1, 0:gw]
            z[1, q * slabs_per_dot + s] = zz[s * n1:(s + 1) * n1, gw:2 * gw]

    def tile_body(r, carry):
        rows = pl.ds(pl.multiple_of(r * 8, 8), 8)
        for c in range(gw // LANES):
            cols = slice(c * LANES, (c + 1) * LANES)
            h = _dft16([(z[0, t, rows, cols], z[1, t, rows, cols]) for t in range(n2)])
            for k2 in range(n2):
                hr, hi = h[k2]
                if k2 > 0:
                    hr, hi = _cmul(hr, hi, tw_ref[0, k2, rows, :], tw_ref[1, k2, rows, :])
                z[0, k2, rows, cols] = hr
                z[1, k2, rows, cols] = hi
        return carry

    lax.fori_loop(0, n1 // 8, tile_body, 0)

    for k2 in range(n2):
        rhs = jnp.concatenate([z[0, k2], z[1, k2]], axis=0).astype(BF16)
        y = jnp.dot(cst_ref[...], rhs, preferred_element_type=F32) * scale
        for c in range(gw // LANES):
            o_scr[c, pl.ds(k2, n1, stride=n2), :] = y[:, c * LANES:(c + 1) * LANES]
    o_ref[...] = jnp.concatenate([o_scr[c] for c in range(gw // LANES)], axis=1).astype(BF16)


def _fourier_fast(p, col_tile, t, batch):
    n1, n2 = DFT_N1, DFT_N2
    assert t == n1 * n2
    wf = COL_TILE
    gw = wf // FFT_GROUPS
    ang = lambda a, b, n: ((a[:, None] * b[None, :]) % n).astype(F32) * (2.0 * math.pi / n)
    kc = jnp.arange(gw, dtype=jnp.int32)
    a_c = ang(kc, kc, gw)
    csc = jnp.concatenate([jnp.cos(a_c), -jnp.sin(a_c)], axis=1).astype(BF16)
    k1 = jnp.arange(n1, dtype=jnp.int32)
    a_t = ang(k1, k1, n1)
    cst = jnp.concatenate([jnp.cos(a_t), jnp.sin(a_t)], axis=1).astype(BF16)
    a_w = ang(jnp.arange(n2, dtype=jnp.int32), k1, t)
    tw = jnp.stack([jnp.cos(a_w), -jnp.sin(a_w)], axis=0)
    tw = jnp.broadcast_to(tw[..., None], (2, n2, n1, LANES))
    ngrp = wf // gw
    return pl.pallas_call(
        functools.partial(_fft_fast_kernel, n1=n1, n2=n2, gw=gw, scale=1.0 / math.sqrt(t * gw)),
        grid=(batch, ngrp),
        in_specs=[
            pl.BlockSpec((t, gw), lambda b, g: (b, col_tile * ngrp + g)),
            pl.BlockSpec((gw, 2 * gw), lambda b, g: (0, 0)),
            pl.BlockSpec((2, n2, n1, LANES), lambda b, g: (0, 0, 0, 0), pipeline_mode=pl.Buffered(1)),
            pl.BlockSpec((n1, 2 * n1), lambda b, g: (0, 0)),
        ],
        out_specs=pl.BlockSpec((t, gw), lambda b, g: (b, g)),
        out_shape=jax.ShapeDtypeStruct((batch * t, wf), BF16),
        scratch_shapes=[
            pltpu.VMEM((2, n2, n1, gw), F32),
            pltpu.VMEM((gw // LANES, t, LANES), F32),
        ],
        compiler_params=_cparams("arbitrary", "arbitrary"),
        name="fourier_fast",
    )(p, csc, tw, cst)


def _branch_kernel(xa_ref, ba_ref, ca_ref, za_ref, zb_ref, zc_ref, g0_ref, g1_ref, g2_ref,
                   u_ref, ys_ref, yf_ref, cw_ref, sd_ref, wa_ref, wb_ref, fw_ref,
                   pa_ref, pb_ref, pc_ref, m_ref, *, row_len):
    tm = xa_ref.shape[0]
    f = lambda r: r[...].astype(F32)
    v = f(ca_ref) * f(xa_ref)
    pos = lax.broadcasted_iota(jnp.int32, v.shape, 0) % row_len
    v_prev = jnp.where(pos == 0, 0.0, pltpu.roll(v, 1, 0))
    v_next = jnp.where(pos == row_len - 1, 0.0, pltpu.roll(v, tm - 1, 0))
    cw = cw_ref[0]
    conv = v_prev * cw[0:1, :] + v * cw[1:2, :] + v_next * cw[2:3, :]
    a = f(ba_ref) * conv * _silu(f(za_ref))
    ya = jnp.dot(a.astype(BF16), pa_ref[0], preferred_element_type=F32)
    acc = _sigmoid(f(g0_ref)) * ya
    y = _gelu_tanh(ys_ref[...] + sd_ref[0] * f(u_ref)).astype(BF16)
    glu = (jnp.dot(y, wa_ref[0], preferred_element_type=F32)
           * _sigmoid(jnp.dot(y, wb_ref[0], preferred_element_type=F32)) * _silu(f(zb_ref)))
    yb = jnp.dot(glu.astype(BF16), pb_ref[0], preferred_element_type=F32)
    acc = acc + _sigmoid(f(g1_ref)) * yb
    c = jnp.dot(yf_ref[...], fw_ref[0], preferred_element_type=F32) * _silu(f(zc_ref))
    yc = jnp.dot(c.astype(BF16), pc_ref[0], preferred_element_type=F32)
    acc = acc + _sigmoid(f(g2_ref)) * yc
    m_ref[...] = acc.astype(BF16)


def _branches(p, ys, yf, conv_w, ssm_d, wa, wb, fw, pa, pb, pc, layer, row_len):
    n = p.shape[0]
    wc = conv_w.shape[-1]
    d = pa.shape[-1]
    tm = TM_BRANCH
    tn = COL_TILE
    col = lambda k: pl.BlockSpec((tm, tn), lambda i: (i, k))
    const = lambda shape: pl.BlockSpec(shape, lambda i: (layer,) + (0,) * (len(shape) - 1),
                                       pipeline_mode=pl.Buffered(1))
    g_tile0 = N_BRANCH_TILES
    return pl.pallas_call(
        functools.partial(_branch_kernel, row_len=row_len),
        grid=(n // tm,),
        in_specs=[
            col(0), col(1), col(2), col(3), col(5), col(7),
            pl.BlockSpec((tm, d), lambda i: (i, g_tile0 * tn // d)),
            pl.BlockSpec((tm, d), lambda i: (i, g_tile0 * tn // d + 1)),
            pl.BlockSpec((tm, d), lambda i: (i, g_tile0 * tn // d + 2)),
            col(4),
            pl.BlockSpec((tm, tn), lambda i: (i, 0)),
            pl.BlockSpec((tm, tn), lambda i: (i, 0)),
            const((1, 3, wc)), const((1, 1, tn)),
            const((1, tn, tn)), const((1, tn, tn)), const((1, tn, tn)),
            const((1, wc, d)), const((1, tn, d)), const((1, tn, d)),
        ],
        out_specs=pl.BlockSpec((tm, d), lambda i: (i, 0)),
        out_shape=jax.ShapeDtypeStruct((n, d), BF16),
        compiler_params=_cparams("arbitrary"),
        name="branches",
    )(p, p, p, p, p, p, p, p, p, p, ys, yf, conv_w, ssm_d.reshape(ssm_d.shape[0], 1, -1),
      wa, wb, fw, pa, pb, pc)


def _out_kernel(*refs, with_next):
    if with_next:
        m_ref, x_ref, gt_ref, g_ref, w_ref, scn_ref, shn_ref, gn_ref, o_ref, h_ref = refs
    else:
        m_ref, x_ref, gt_ref, g_ref, w_ref, o_ref = refs
    o = jnp.dot(m_ref[...], w_ref[0], preferred_element_type=F32)
    ms = jnp.mean(o * o, axis=-1, keepdims=True)
    y = o * lax.rsqrt(ms + RMS_EPS) * g_ref[0]
    xn = x_ref[...] + gt_ref[0] * y
    o_ref[...] = xn
    if with_next:
        h_ref[...] = _modnorm(xn, gn_ref[0], scn_ref[0], shn_ref[0]).astype(BF16)


def _outproj(m, x2, gt, g_post, w_out_b, layer, rows_per_batch, next_mod=None):
    n, d = x2.shape
    tm = min(TM_OUT, rows_per_batch)
    tpb = rows_per_batch // tm
    per_batch =pl.BlockSpec((1, 1, d), lambda i: (i // tpb, 0, 0))
    in_specs = [
        pl.BlockSpec((tm, d), lambda i: (i, 0)),
        pl.BlockSpec((tm, d), lambda i: (i, 0)),
        per_batch,
        pl.BlockSpec((1, 1, d), lambda i: (layer, 0, 0)),
        pl.BlockSpec((1, d, d), lambda i: (layer, 0, 0), pipeline_mode=pl.Buffered(1)),
    ]
    args = [m, x2, gt, g_post.reshape(g_post.shape[0], 1, d), w_out_b]
    out_specs = [pl.BlockSpec((tm, d), lambda i: (i, 0))]
    out_shape = [jax.ShapeDtypeStruct((n, d), F32)]
    if next_mod is not None:
        sc_n, sh_n, g_pre = next_mod
        in_specs += [per_batch, per_batch, pl.BlockSpec((1, 1, d), lambda i: (layer + 1, 0, 0))]
        args += [sc_n, sh_n, g_pre.reshape(g_pre.shape[0], 1, d)]
        out_specs.append(pl.BlockSpec((tm, d), lambda i: (i, 0)))
        out_shape.append(jax.ShapeDtypeStruct((n, d), BF16))
    res = pl.pallas_call(
        functools.partial(_out_kernel, with_next=next_mod is not None),
        grid=(n // tm,),
        in_specs=in_specs,
        out_specs=out_specs,
        out_shape=out_shape,
        compiler_params=_cparams("arbitrary"),
        name="outproj",
    )(*args)
    return (res[0], res[1]) if next_mod is not None else (res[0], None)


def kernel(x, c, ctx, c_ctx, w_ada, b_ada, g_pre, g_post, w_in, conv_w, ssm_lam_re, ssm_lam_im, ssm_log_dt,
           ssm_b_re, ssm_b_im, ssm_c_re, ssm_c_im, ssm_d, glu_wa, glu_wb, fourier_w, proj_a, proj_b, proj_c,
           w_out):
    bn, t, d = x.shape
    lc = ctx.shape[1]
    depth = w_ada.shape[0]
    w_conv = conv_w.shape[-1]
    w_ssm = ssm_d.shape[-1]
    tn = COL_TILE
    assert w_conv == tn and w_ssm == tn and fourier_w.shape[-1] == tn and d % tn == 0
    assert bn + 1 <= 8 and t % GRID_W == 0
    u_tile = 4 * w_conv // tn
    f_tile = u_tile + 2
    n_tiles = w_in.shape[-1] // tn
    assert f_tile + 2 == N_BRANCH_TILES and (n_tiles - N_BRANCH_TILES) * tn == N_BRANCH * d

    cc = jnp.concatenate([c, c_ctx[None, :], jnp.zeros((8 - bn - 1, d), F32)], axis=0)
    mods = _ada_mods(cc, w_ada, b_ada)

    to_b = lambda w: w.astype(BF16)
    wa_b, wb_b, fw_b = to_b(glu_wa), to_b(glu_wb), to_b(fourier_w)
    pa_b, pb_b, pc_b, wo_b = to_b(proj_a), to_b(proj_b), to_b(proj_c), to_b(w_out)
    fast_dft = t == DFT_N1 * DFT_N2
    tabs_x = None if fast_dft else _dft_tables(t, tn // FFT_GROUPS)
    tabs_c = _dft_tables(lc, tn // FFT_GROUPS)

    x2 = x.reshape(bn * t, d)
    c2 = ctx.reshape(bn * lc, d)
    def mod_x(l):
        return tuple(mods[l, :bn, k * d:(k + 1) * d].reshape(bn, 1, d) for k in range(3))

    def mod_c(l):
        return tuple(jnp.broadcast_to(mods[l, bn, k * d:(k + 1) * d], (bn, 1, d)) for k in range(3))

    hx = _prenorm(x2, mod_x(0)[1], mod_x(0)[0], g_pre, 0, t)
    hc = _prenorm(c2, mod_c(0)[1], mod_c(0)[0], g_pre, 0, lc)
    for l in range(depth):
        last = l == depth - 1
        sh_x, sc_x, gt_x = mod_x(l)
        sh_c, sc_c, gt_c = mod_c(l)
        next_x = None if last else (mod_x(l + 1)[1], mod_x(l + 1)[0], g_pre)
        next_c = None if last else (mod_c(l + 1)[1], mod_c(l + 1)[0], g_pre)
        ssm_p = _ssm_params(ssm_lam_re[l], ssm_lam_im[l], ssm_log_dt[l], ssm_b_re[l], ssm_b_im[l],
                            ssm_c_re[l], ssm_c_im[l])

        px = _inproj(hx, w_in, l, 0, n_tiles)
        pc_ = _inproj(hc, w_in, l, u_tile, 1) if last else _inproj(hc, w_in, l, 0, n_tiles)
        u_lane_tile = u_tile * (tn // LANES)
        ys_x, ys_c = _ssm_scan(px, pc_, u_lane_tile, 0 if last else u_lane_tile, w_ssm, ssm_p, bn,
                               ctx_out=not last)

        yf_x = _fourier_fast(px, f_tile, t, bn) if fast_dft else _fourier(px, f_tile, t, bn, tabs_x)
        mx = _branches(px, ys_x, yf_x, conv_w, ssm_d, wa_b, wb_b, fw_b, pa_b, pb_b, pc_b, l, GRID_W)
        new_x2, hx = _outproj(mx, x2, gt_x, g_post, wo_b, l, t, next_x)
        if not last:
            yf_c = _fourier(pc_, f_tile, lc, bn, tabs_c)
            mc = _branches(pc_, ys_c, yf_c, conv_w, ssm_d, wa_b, wb_b, fw_b, pa_b, pb_b, pc_b, l, lc)
            c2, hc = _outproj(mc, c2, gt_c, g_post, wo_b, l, lc, next_c)
        x2 = new_x2
    return x2.reshape(bn, t, d)
```

```python
import functools
import math

import jax
import jax.numpy as jnp
from jax import lax
from jax.experimental import pallas as pl
from jax.experimental.pallas import tpu as pltpu

F32 = jnp.float32
BF16 = jnp.bfloat16

GRID_W = 64
FFT_GROUPS = 4
N_BRANCH = 3
RMS_EPS = 1e-6
LANES = 128
SSM_CHUNK = 8
SSM_SEGS = 8
DFT_ROWS = 64
DFT_N1, DFT_N2 = 256, 16
VMEM_LIMIT = 56 * 1024 * 1024
COL_TILE = 1024
N_BRANCH_TILES = 8
U_TILE = 4
TM_INPROJ, TM_PRENORM, TM_BRANCH, TM_OUT, TM_DFT = 2048, 1024, 256, 512, 256
TM_INPROJ_DOT = 1024


def _cparams(*sem):
    return pltpu.CompilerParams(dimension_semantics=sem, vmem_limit_bytes=VMEM_LIMIT)


def _sigmoid(v):
    return 0.5 * jnp.tanh(0.5 * v) + 0.5


def _silu(v):
    return v * _sigmoid(v)


def _gelu_tanh(v):
    return v * (0.5 * (1.0 + jnp.tanh(math.sqrt(2.0 / math.pi) * (v + 0.044715 * (v * v * v)))))


def _cmul(ar, ai, br, bi):
    return ar * br - ai * bi, ar * bi + ai * br


def _cmul_add(ar, ai, zr, zi, sr, si):
    return ar * zr - ai * zi + sr, ar * zi + ai * zr + si


def _ada_kernel(c_ref, w_ref, b_ref, o_ref):
    c = c_ref[...]
    s = _silu(c).astype(BF16)
    o_ref[0] = jnp.dot(s, w_ref[0].astype(BF16), preferred_element_type=F32) + b_ref[0]


def _ada_mods(cc, w_ada, b_ada):
    depth, d, n3 = w_ada.shape
    rows = cc.shape[0]
    tn = 2 * COL_TILE if n3 % (2 * COL_TILE) == 0 else COL_TILE
    assert n3 % tn == 0
    return pl.pallas_call(
        _ada_kernel,
        grid=(depth, n3 // tn),
        in_specs=[
            pl.BlockSpec((rows, d), lambda l, j: (0, 0)),
            pl.BlockSpec((1, d, tn), lambda l, j: (l, 0, j)),
            pl.BlockSpec((1, 1, tn), lambda l, j: (l, 0, j)),
        ],
        out_specs=pl.BlockSpec((1, rows, tn), lambda l, j: (l, 0, j)),
        out_shape=jax.ShapeDtypeStruct((depth, rows, n3), F32),
        compiler_params=_cparams("arbitrary", "arbitrary"),
        name="ada_mods",
    )(cc, w_ada, b_ada.reshape(depth, 1, n3))


def _modnorm(x, g, sc, sh):
    ms = jnp.mean(x * x, axis=-1, keepdims=True)
    return (x * lax.rsqrt(ms + RMS_EPS) * g) * (1.0 + sc) + sh


def _prenorm_kernel(x_ref, sc_ref, sh_ref, g_ref, h_ref):
    rb = 64

    def body(r, carry):
        rows = pl.ds(pl.multiple_of(r * rb, rb), rb)
        h_ref[rows, :] = _modnorm(x_ref[rows, :], g_ref[0], sc_ref[0], sh_ref[0]).astype(BF16)
        return carry

    lax.fori_loop(0, x_ref.shape[0] // rb, body, 0)


MOD_SHIFT, MOD_SCALE, MOD_GATE = 0, 1, 2
MOD_ROWS = 8


def _mod_spec(d, layer, which, tiles_per_batch, ctx_row):
    def index(i):
        row = ctx_row if ctx_row is not None else i // tiles_per_batch
        return ((layer * MOD_ROWS + row) * 3 + which, 0, 0)
    return pl.BlockSpec((1, 1, d), index)


def _prenorm(x2, mods_r, g_pre, layer, rows_per_batch, ctx_row=None):
    n, d = x2.shape
    tm = min(TM_PRENORM, rows_per_batch)
    tpb = rows_per_batch // tm
    return pl.pallas_call(
        _prenorm_kernel,
        grid=(n // tm,),
        in_specs=[
            pl.BlockSpec((tm, d), lambda i: (i, 0)),
            _mod_spec(d, layer, MOD_SCALE, tpb, ctx_row),
            _mod_spec(d, layer, MOD_SHIFT, tpb, ctx_row),
            pl.BlockSpec((1, 1, d), lambda i: (layer, 0, 0)),
        ],
        out_specs=pl.BlockSpec((tm, d), lambda i: (i, 0)),
        out_shape=jax.ShapeDtypeStruct((n, d), BF16),
        compiler_params=_cparams("arbitrary"),
        name="prenorm",
    )(x2, mods_r, mods_r, g_pre.reshape(g_pre.shape[0], 1, d))


def _inproj_kernel(h_ref, w_ref, p_ref, wb):
    @pl.when(pl.program_id(1) == 0)
    def _():
        wb[...] = w_ref[0].astype(BF16)

    sub = min(TM_INPROJ_DOT, h_ref.shape[0])
    for r in range(h_ref.shape[0] // sub):
        rows = slice(r * sub, (r + 1) * sub)
        p_ref[rows, :] = jnp.dot(h_ref[rows, :], wb[...], preferred_element_type=F32).astype(BF16)


def _inproj(h, w_in, layer, tile0, ntiles):
    n, d = h.shape
    tn = COL_TILE
    tm = min(TM_INPROJ, n)
    return pl.pallas_call(
        _inproj_kernel,
        grid=(ntiles, n // tm),
        in_specs=[
            pl.BlockSpec((tm, d), lambda j, i: (i, 0)),
            pl.BlockSpec((1, d, tn), lambda j, i: (layer, 0, tile0 + j)),
        ],
        out_specs=pl.BlockSpec((tm, tn), lambda j, i: (i, j)),
        out_shape=jax.ShapeDtypeStruct((n, ntiles * tn), BF16),
        scratch_shapes=[pltpu.VMEM((d, tn), BF16)],
        compiler_params=_cparams("arbitrary", "arbitrary"),
        name="inproj",
    )(h, w_in)


def _ssm_params(lam_re, lam_im, log_dt, b_re, b_im, c_re, c_im):
    nl, _, G, P = lam_re.shape
    H = b_re.shape[-1]
    gpt = LANES // H
    O = G // gpt
    lam_re = lam_re.astype(F32)
    lam_im = lam_im.astype(F32)
    dt = jnp.exp(log_dt.astype(F32))[..., None]
    lr = lam_re * dt
    li = lam_im * dt
    mag = jnp.exp(lr)
    a_re = mag * jnp.cos(li)
    a_im = mag * jnp.sin(li)
    n_re = a_re - 1.0
    n_im = a_im
    den = lam_re * lam_re + lam_im * lam_im
    q_re = (n_re * lam_re + n_im * lam_im) / den
    q_im = (n_im * lam_re - n_re * lam_im) / den
    bb_re = q_re[..., None] * b_re - q_im[..., None] * b_im
    bb_im = q_re[..., None] * b_im + q_im[..., None] * b_re

    bbs = jnp.stack([bb_re, bb_im], axis=2).reshape(nl, 2, 2, O, gpt, P, H)
    bc = jnp.transpose(bbs, (0, 3, 1, 2, 6, 4, 5)).reshape(nl, O, 4, H, gpt * P)
    cs = jnp.stack([c_re.astype(F32), c_im.astype(F32)], axis=2).reshape(nl, 2, 2, O, gpt, H, P)
    cct = jnp.transpose(cs, (0, 3, 1, 2, 4, 6, 5)).reshape(nl, O, 4, gpt * P, 1, H)
    cct = jnp.broadcast_to(cct, (nl, O, 4, gpt * P, gpt, H)).reshape(nl, O, 4, gpt * P, LANES)
    a4 = jnp.stack([a_re, a_im], axis=2).reshape(nl, 2, 2, O, gpt * P)
    a4 = jnp.transpose(a4, (0, 3, 1, 2, 4)).reshape(nl, O, 4, gpt * P)
    arow = jnp.concatenate([a4, jnp.zeros_like(a4)], axis=2)
    acol = jnp.broadcast_to(a4[..., None], (nl, O, 4, gpt * P, LANES))

    return bc, cct, arow, acol


def _block_transpose8(v):
    bw = LANES // 8
    blk = lax.broadcasted_iota(jnp.int32, v[0].shape, 1) // bw
    for dist in (4, 2, 1):
        upper = (blk & dist) != 0
        nxt = list(v)
        for a in range(8):
            if a & dist:
                continue
            b = a + dist
            nxt[a] = jnp.where(upper, pltpu.roll(v[b], dist * bw, 1), v[a])
            nxt[b] = jnp.where(upper, v[b], pltpu.roll(v[a], LANES - dist * bw, 1))
        v = nxt
    return v


def _ssm_build_operators(bc_ref, cct_ref, arow_ref, acol_ref, ws, ws_lo, w3):
    L = SSM_CHUNK
    hh = bc_ref.shape[2]
    sw = bc_ref.shape[3]
    pp = sw // (LANES // hh)
    npair = sw // LANES
    assert L * hh == LANES and 2 * pp == LANES
    lane = lax.broadcasted_iota(jnp.int32, (LANES, LANES), 1)
    row = lax.broadcasted_iota(jnp.int32, (LANES, LANES), 0)
    lane_g0, row_g0 = lane < pp, row < pp
    lane_blk = lane // hh
    zeros = jnp.zeros((LANES, LANES), F32)
    zeros_b = jnp.zeros((LANES, LANES), BF16)

    def powers(re, im, n):
        out = [(jnp.ones_like(re), jnp.zeros_like(re))]
        for _ in range(n):
            out.append(_cmul(out[-1][0], out[-1][1], re, im))
        return out

    def split(v):
        hi = v.astype(BF16)
        return hi, (v - hi.astype(F32)).astype(BF16)

    def shift_rows(t, blocks):
        n = abs(blocks) * hh
        if n == 0:
            return t
        pad = jnp.zeros((n, LANES), F32)
        return jnp.concatenate([pad, t[:LANES - n]] if blocks > 0 else [t[n:], pad], axis=0)

    for q in range(npair):
        ql = slice(q * LANES, (q + 1) * LANES)
        taps = []
        for d in range(2):
            b_re = jnp.concatenate([bc_ref[0, 2 * d, :, ql]] * L, axis=0)
            b_im = jnp.concatenate([bc_ref[0, 2 * d + 1, :, ql]] * L, axis=0)
            prow = powers(arow_ref[0, 2 * d:2 * d + 1, ql], arow_ref[0, 2 * d + 1:2 * d + 2, ql], L - 1)
            exps = [L - 1 - j if d == 0 else j for j in range(L)]
            a_re = jnp.concatenate([jnp.broadcast_to(prow[e][0], (hh, LANES)) for e in exps], axis=0)
            a_im = jnp.concatenate([jnp.broadcast_to(prow[e][1], (hh, LANES)) for e in exps], axis=0)
            for part, v in zip((2 * d, 2 * d + 1), _cmul(b_re, b_im, a_re, a_im)):
                cols = slice(part * LANES, (part + 1) * LANES)
                for dst, piece in zip((ws, ws_lo), split(v)):
                    dst[q, 0:LANES, cols] = jnp.where(lane_g0, piece, zeros_b)
                    dst[q, LANES:2 * LANES, cols] = jnp.where(lane_g0, zeros_b, piece)
            c_re = cct_ref[0, 2 * d, ql, :]
            c_im = cct_ref[0, 2 * d + 1, ql, :]
            (cr_hi, cr_lo), (ci_hi, ci_lo) = split(c_re), split(-c_im)
            c_hi = jnp.concatenate([cr_hi, ci_hi], axis=0)
            c_lo = jnp.concatenate([cr_lo, ci_lo], axis=0)
            dcols = slice(2 * d * LANES, (2 * d + 2) * LANES)
            taps.append(jnp.dot(jnp.concatenate([ws[q, :, dcols], ws[q, :, dcols], ws_lo[q, :, dcols]], axis=1),
                                jnp.concatenate([c_hi, c_lo, c_hi], axis=0), preferred_element_type=F32))
            pcol = powers(acol_ref[0, 2 * d, ql, :], acol_ref[0, 2 * d + 1, ql, :], L)
            ap_re, ap_im = zeros, zeros
            for jo in range(L):
                e = jo + 1 if d == 0 else L - jo
                ap_re = jnp.where(lane_blk == jo, pcol[e][0], ap_re)
                ap_im = jnp.where(lane_blk == jo, pcol[e][1], ap_im)
            ca_re, ca_im = _cmul(c_re, c_im, ap_re, ap_im)
            for part, v in ((2 * d, ca_re), (2 * d + 1, -ca_im)):
                r0 = 2 * LANES + part * LANES
                w3[q, r0:r0 + LANES, 0:LANES] = jnp.where(row_g0, v, zeros).astype(BF16)
                w3[q, r0:r0 + LANES, LANES:2 * LANES] = jnp.where(row_g0, zeros, v).astype(BF16)
        for g2 in range(2):
            tf = taps[0][g2 * LANES:(g2 + 1) * LANES]
            tb = taps[1][g2 * LANES:(g2 + 1) * LANES]
            m = zeros
            for jo in range(L):
                m = jnp.where(lane_blk == jo, shift_rows(tf, jo - (L - 1)) + shift_rows(tb, jo), m)
            rows = slice(g2 * LANES, (g2 + 1) * LANES)
            w3[q, rows, g2 * LANES:(g2 + 1) * LANES] = m.astype(BF16)
            w3[q, rows, (1 - g2) * LANES:(2 - g2) * LANES] = zeros_b


def _ssm_build_powers(arow_ref, ap, pw, seg_len):
    L = SSM_CHUNK
    sw = arow_ref.shape[2]
    assert L & (L - 1) == 0 and seg_len & (seg_len - 1) == 0 and seg_len >= 8
    for d in range(2):
        cr, ci = slice(2 * d * sw, (2 * d + 1) * sw), slice((2 * d + 1) * sw, (2 * d + 2) * sw)
        re, im = arow_ref[0, 2 * d:2 * d + 1, :], arow_ref[0, 2 * d + 1:2 * d + 2, :]
        for _ in range(L.bit_length() - 1):
            re, im = _cmul(re, im, re, im)
        ap[0:1, cr] = re
        ap[0:1, ci] = im
        p = [(jnp.ones_like(re), jnp.zeros_like(re))]
        for _ in range(7):
            p.append(_cmul(p[-1][0], p[-1][1], re, im))
        order = list(range(8)) if d == 0 else list(range(7, -1, -1))
        first = slice(0, 8) if d == 0 else slice(seg_len - 8, seg_len)
        pw[first, cr] = jnp.concatenate([p[r][0] for r in order], axis=0)
        pw[first, ci] = jnp.concatenate([p[r][1] for r in order], axis=0)
        sq = _cmul(p[4][0], p[4][1], p[4][0], p[4][1])
        m = 8
        while m < seg_len:
            src = slice(0, m) if d == 0 else slice(seg_len - m, seg_len)
            dst = slice(m, 2 * m) if d == 0 else slice(seg_len - 2 * m, seg_len - m)
            xr, xi = _cmul(pw[src, cr], pw[src, ci], sq[0], sq[1])
            pw[dst, cr] = xr
            pw[dst, ci] = xi
            sq = _cmul(sq[0], sq[1], sq[0], sq[1])
            m *= 2
        ap[1:2, cr] = sq[0]
        ap[1:2, ci] = sq[1]


def _seg_pitch(seg_len):
    tiles = seg_len // 8 + 1
    return 8 * (tiles + 1 - tiles % 2)


def _ssm_kernel(*refs, nc, nx, seg_len, ctx_out):
    if ctx_out:
        (ux_ref, uc_ref, bc_ref, cct_ref, arow_ref, acol_ref,
         yx_ref, yc_ref, ws, ws_lo, w3, lhs, st, uxf, ucf, ap, pw) = refs
    else:
        (ux_ref, uc_ref, bc_ref, cct_ref, arow_ref, acol_ref,
         yx_ref, ws, ws_lo, w3, lhs, st, uxf, ucf, ap, pw) = refs
        yc_ref = None
    L = SSM_CHUNK
    nlt = st.shape[0] // 4
    sw = nlt * LANES
    xw = 2 * LANES
    FR, FI, BR, BI = range(4)
    nseg = SSM_SEGS
    pitch = _seg_pitch(seg_len)

    @pl.when(pl.program_id(1) == 0)
    def _():
        _ssm_build_operators(bc_ref, cct_ref, arow_ref, acol_ref, ws, ws_lo, w3)
        _ssm_build_powers(arow_ref, ap, pw, seg_len)

    def ld(part, rows):
        return jnp.concatenate([st[part * nlt + q, rows, :] for q in range(nlt)], axis=1)

    def sto(part, rows, val):
        for q in range(nlt):
            st[part * nlt + q, rows, :] = val[:, q * LANES:(q + 1) * LANES]

    def part_cols(part):
        return slice(part * sw, (part + 1) * sw)

    ucf[...] = uc_ref[...].astype(F32)
    uxf[...] = ux_ref[...].astype(F32)
    by_pos = [jnp.concatenate([ucf[pl.ds(j, nc, stride=L), :], uxf[pl.ds(j, nx, stride=L), :]], axis=0)
              for j in range(L)]
    for g, tile_g in enumerate(_block_transpose8(by_pos)):
        lhs[g // 2, :, (g % 2) * LANES:(g % 2 + 1) * LANES] = tile_g.astype(BF16)

    for q in range(nlt):
        s = jnp.dot(lhs[q, :, 0:xw], ws[q], preferred_element_type=F32)
        for part in range(4):
            k = part * nlt + q
            st[k, 0:nc, :] = s[0:nc, part * LANES:(part + 1) * LANES]
            for g in range(nseg):
                st[k, nc + g * pitch:nc + g * pitch + seg_len, :] = (
                    s[nc + g * seg_len:nc + (g + 1) * seg_len, part * LANES:(part + 1) * LANES])

    a_l = ap[0:1, :]
    a_seg = ap[1:2, :]
    al = [a_l[:, part_cols(p)] for p in range(4)]
    aseg = [a_seg[:, part_cols(p)] for p in range(4)]

    zero = jnp.zeros((1, sw), F32)

    def ctx_step(pr, pi):
        def step(i, carry):
            er, ei = carry
            row = pl.ds(i, 1)
            sr, si = ld(pr, row), ld(pi, row)
            sto(pr, row, er)
            sto(pi, row, ei)
            return _cmul_add(al[pr], al[pi], er, ei, sr, si)
        return step

    h0f = lax.fori_loop(0, nc, ctx_step(FR, FI), (zero, zero))
    bwd_step = ctx_step(BR, BI)
    h0b = lax.fori_loop(0, nc, lambda k, carry: bwd_step(nc - 1 - k, carry), (zero, zero))
    for part in range(4):
        for q in range(nlt):
            lhs[q, 0:nc, xw + part * LANES:xw + (part + 1) * LANES] = st[part * nlt + q, 0:nc, :].astype(BF16)

    zseg = jnp.zeros((nseg, sw), F32)

    def seg_pass(pr, pi, order):
        zr, zi = zseg, zseg
        for i in order:
            rows = pl.ds(nc + i, nseg, stride=pitch)
            sr, si = ld(pr, rows), ld(pi, rows)
            sto(pr, rows, zr)
            sto(pi, rows, zi)
            zr, zi = _cmul_add(al[pr], al[pi], zr, zi, sr, si)
        return zr, zi

    zfr, zfi = seg_pass(FR, FI, range(seg_len))
    zbr, zbi = seg_pass(BR, BI, range(seg_len - 1, -1, -1))

    ef = [h0f]
    for s in range(nseg - 1):
        ef.append(_cmul_add(aseg[FR], aseg[FI], ef[s][0], ef[s][1], zfr[s:s + 1], zfi[s:s + 1]))
    eb = [None] * nseg
    eb[nseg - 1] = h0b
    for s in range(nseg - 1, 0, -1):
        eb[s - 1] = _cmul_add(aseg[BR], aseg[BI], eb[s][0], eb[s][1], zbr[s:s + 1], zbi[s:s + 1])
    for s in range(nseg):
        srows = slice(nc + s * pitch, nc + s * pitch + seg_len)
        rows = slice(nc + s * seg_len, nc + (s + 1) * seg_len)
        for (pr, pi, e) in ((FR, FI, ef[s]), (BR, BI, eb[s])):
            xr, xi = _cmul_add(pw[:, part_cols(pr)], pw[:, part_cols(pi)], e[0], e[1],
                               ld(pr, srows), ld(pi, srows))
            for q in range(nlt):
                ql = slice(q * LANES, (q + 1) * LANES)
                lhs[q, rows, xw + pr * LANES:xw + (pr + 1) * LANES] = xr[:, ql].astype(BF16)
                lhs[q, rows, xw + pi * LANES:xw + (pi + 1) * LANES] = xi[:, ql].astype(BF16)

    by_group = []
    for q in range(nlt):
        y = jnp.dot(lhs[q], w3[q], preferred_element_type=F32)
        by_group += [y[:, 0:LANES], y[:, LANES:2 * LANES]]
    for j, tile_j in enumerate(_block_transpose8(by_group)):
        if ctx_out:
            yc_ref[pl.ds(j, nc, stride=L), :] = tile_j[0:nc]
        yx_ref[pl.ds(j, nx, stride=L), :] = tile_j[nc:nc + nx]


def _ssm_scan(px, pc, col_x, col_c, w, params, layer, batch, ctx_out):
    bc, cct, arow, acol = params
    L = SSM_CHUNK
    t = px.shape[0] // batch
    lc = pc.shape[0] // batch
    nx, nc = t // L, lc // L
    seg_len = nx // SSM_SEGS
    o = w // LANES
    sw = bc.shape[-1]
    sdim = 4 * sw
    npair = sw // LANES
    out_shape = [jax.ShapeDtypeStruct((batch * t, w), F32)]
    out_specs = [pl.BlockSpec((t, LANES), lambda oi, b: (b, oi))]
    if ctx_out:
        out_shape.append(jax.ShapeDtypeStruct((batch * lc, w), F32))
        out_specs.append(pl.BlockSpec((lc, LANES), lambda oi, b: (b, oi)))
    tile = lambda a: pl.BlockSpec((None, 1) + a.shape[2:], lambda oi, b: (layer, oi) + (0,) * (a.ndim - 2))
    res = pl.pallas_call(
        functools.partial(_ssm_kernel, nc=nc, nx=nx, seg_len=seg_len, ctx_out=ctx_out),
        grid=(o, batch),
        in_specs=[
            pl.BlockSpec((t, LANES), lambda oi, b: (b, col_x + oi)),
            pl.BlockSpec((lc, LANES), lambda oi, b: (b, col_c + oi)),
            tile(bc), tile(cct), tile(arow), tile(acol),
        ],
        out_specs=out_specs,
        out_shape=out_shape,
        scratch_shapes=[
            pltpu.VMEM((npair, 2 * LANES, 4 * LANES), BF16),
            pltpu.VMEM((npair, 2 * LANES, 4 * LANES), BF16),
            pltpu.VMEM((npair, 6 * LANES, 2 * LANES), BF16),
            pltpu.VMEM((npair, nc + nx, 6 * LANES), BF16),
            pltpu.VMEM((sdim // LANES, nc + SSM_SEGS * _seg_pitch(seg_len), LANES), F32),
            pltpu.VMEM((t, LANES), F32),
            pltpu.VMEM((lc, LANES), F32),
            pltpu.VMEM((8, sdim), F32),
            pltpu.VMEM((seg_len, sdim), F32),
        ],
        compiler_params=_cparams("arbitrary", "arbitrary"),
        name="ssm_scan",
    )(px, pc, bc, cct, arow, acol)
    return (res[0], res[1]) if ctx_out else (res[0], None)


def _dft_table_kernel(e1_ref, e2_ref, o_ref, *, t):
    e1c, e1s = e1_ref[0, :, 0:t], e1_ref[0, :, t:2 * t]
    e2c, e2s = e2_ref[:, 0:t], e2_ref[:, t:2 * t]
    c, s = _cmul(e2c, e2s, e1c, e1s)
    o_ref[:, 0:t] = c.astype(BF16)
    o_ref[:, t:2 * t] = (-s).astype(BF16)


def _dft_tables(t, gw):
    def cis(rows_mult, nrows):
        k = jnp.arange(nrows, dtype=jnp.int32)[:, None] * rows_mult
        n = jnp.arange(t, dtype=jnp.int32)[None, :]
        ang = ((k * n) % t).astype(F32) * (2.0 * math.pi / t)
        return jnp.concatenate([jnp.cos(ang), jnp.sin(ang)], axis=1)
    r = DFT_ROWS
    e1 = cis(r, t // r).reshape(t // r, 1, 2 * t)
    e2 = cis(1, r)
    tab_t = pl.pallas_call(
        functools.partial(_dft_table_kernel, t=t),
        grid=(t // r,),
        in_specs=[
            pl.BlockSpec((1, 1, 2 * t), lambda i: (i, 0, 0)),
            pl.BlockSpec((r, 2 * t), lambda i: (0, 0)),
        ],
        out_specs=pl.BlockSpec((r, 2 * t), lambda i: (i, 0)),
        out_shape=jax.ShapeDtypeStruct((t, 2 * t), BF16),
        compiler_params=_cparams("arbitrary"),
        name="dft_table",
    )(e1, e2)
    kc = jnp.arange(gw, dtype=jnp.int32)
    ang = ((kc[:, None] * kc[None, :]) % gw).astype(F32) * (2.0 * math.pi / gw)
    tab_c = jnp.concatenate([jnp.cos(ang), jnp.sin(ang)], axis=1).astype(BF16)
    return tab_t, tab_c


def _fft_kernel(f_ref, cs_ref, tab_ref, o_ref, data, *, t, gw, groups, scale):
    i = pl.program_id(1)

    @pl.when(i == 0)
    def _():
        rb = min(512, t)
        for r in range(t // rb):
            for g in range(groups):
                fg = f_ref[r * rb:(r + 1) * rb, g * gw:(g + 1) * gw]
                z = jnp.dot(fg, cs_ref[...], preferred_element_type=F32)
                data[r * rb:(r + 1) * rb, g * gw:(g + 1) * gw] = z[:, :gw].astype(BF16)
                data[t + r * rb:t + (r + 1) * rb, g * gw:(g + 1) * gw] = z[:, gw:].astype(BF16)

    y = jnp.dot(tab_ref[...], data[...], preferred_element_type=F32)
    o_ref[...] = (y * scale).astype(BF16)


def _fourier(p, col_tile, t, batch, tabs):
    tab_t, tab_c = tabs
    wf = COL_TILE
    gw = wf // FFT_GROUPS
    tm = min(TM_DFT, t)
    return pl.pallas_call(
        functools.partial(_fft_kernel, t=t, gw=gw, groups=FFT_GROUPS, scale=1.0 / math.sqrt(t * gw)),
        grid=(batch, t // tm),
        in_specs=[
            pl.BlockSpec((t, wf), lambda b, i: (b, col_tile)),
            pl.BlockSpec((gw, 2 * gw), lambda b, i: (0, 0)),
            pl.BlockSpec((tm, 2 * t), lambda b, i: (i, 0)),
        ],
        out_specs=pl.BlockSpec((tm, wf), lambda b, i: (b * (t // tm) + i, 0)),
        out_shape=jax.ShapeDtypeStruct((batch * t, wf), BF16),
        scratch_shapes=[pltpu.VMEM((2 * t, wf), BF16)],
        compiler_params=_cparams("arbitrary", "arbitrary"),
        name="fourier",
    )(p, tab_c, tab_t)


def _cmul_const(xr, xi, c, s):
    def scaled(v, k):
        if abs(k) < 1e-12:
            return None
        return v if abs(k - 1.0) < 1e-12 else -v if abs(k + 1.0) < 1e-12 else v * k

    def add(a, b):
        return b if a is None else a if b is None else a + b

    return add(scaled(xr, c), scaled(xi, -s)), add(scaled(xi, c), scaled(xr, s))


def _dft4(y):
    (ar, ai), (br, bi), (cr, ci), (dr, di) = y
    sr, si, tr, ti = ar + cr, ai + ci, ar - cr, ai - ci
    ur, ui, vr, vi = br + dr, bi + di, br - dr, bi - di
    return [(sr + ur, si + ui), (tr + vi, ti - vr), (sr - ur, si - ui), (tr - vi, ti + vr)]


def _dft16(x):
    out = [None] * 16
    p = [_dft4([x[4 * a + b] for a in range(4)]) for b in range(4)]
    for ka in range(4):
        q = []
        for b in range(4):
            ang = -2.0 * math.pi * ka * b / 16.0
            q.append(_cmul_const(p[b][ka][0], p[b][ka][1], math.cos(ang), math.sin(ang)))
        r = _dft4(q)
        for kb in range(4):
            out[ka + 4 * kb] = r[kb]
    return out


def _fft_fast_kernel(f_ref, csc_ref, tw_ref, cst_ref, o_ref, z, o_scr, *, n1, n2, gw, scale):
    slabs_per_dot = 4
    for q in range(n2 // slabs_per_dot):
        r0 = q * slabs_per_dot * n1
        zz = jnp.dot(f_ref[r0:r0 + slabs_per_dot * n1, :], csc_ref[...], preferred_element_type=F32)
        for s in range(slabs_per_dot):
            z[0, q * slabs_per_dot + s] = zz[s * n1:(s + 1) * n1, 0:gw]
            z[1, q * slabs_per_dot + s] = zz[s * n1:(s + 1) * n1, gw:2 * gw]

    def tile_body(r, carry):
        rows = pl.ds(pl.multiple_of(r * 8, 8), 8)
        for c in range(gw // LANES):
            cols = slice(c * LANES, (c + 1) * LANES)
            h = _dft16([(z[0, t, rows, cols], z[1, t, rows, cols]) for t in range(n2)])
            for k2 in range(n2):
                hr, hi = h[k2]
                if k2 > 0:
                    hr, hi = _cmul(hr, hi, tw_ref[0, k2, rows, :], tw_ref[1, k2, rows, :])
                z[0, k2, rows, cols] = hr
                z[1, k2, rows, cols] = hi
        return carry

    lax.fori_loop(0, n1 // 8, tile_body, 0)

    for k2 in range(n2):
        rhs = jnp.concatenate([z[0, k2], z[1, k2]], axis=0).astype(BF16)
        y = jnp.dot(cst_ref[...], rhs, preferred_element_type=F32) * scale
        for c in range(gw // LANES):
            o_scr[c, pl.ds(k2, n1, stride=n2), :] = y[:, c * LANES:(c + 1) * LANES]
    o_ref[...] = jnp.concatenate([o_scr[c] for c in range(gw // LANES)], axis=1).astype(BF16)


def _fourier_fast(p, col_tile, t, batch):
    n1, n2 = DFT_N1, DFT_N2
    assert t == n1 * n2
    wf = COL_TILE
    gw = wf // FFT_GROUPS
    ang = lambda a, b, n: ((a[:, None] * b[None, :]) % n).astype(F32) * (2.0 * math.pi / n)
    kc = jnp.arange(gw, dtype=jnp.int32)
    a_c = ang(kc, kc, gw)
    csc = jnp.concatenate([jnp.cos(a_c), -jnp.sin(a_c)], axis=1).astype(BF16)
    k1 = jnp.arange(n1, dtype=jnp.int32)
    a_t = ang(k1, k1, n1)
    cst = jnp.concatenate([jnp.cos(a_t), jnp.sin(a_t)], axis=1).astype(BF16)
    a_w = ang(jnp.arange(n2, dtype=jnp.int32), k1, t)
    tw = jnp.stack([jnp.cos(a_w), -jnp.sin(a_w)], axis=0)
    tw = jnp.broadcast_to(tw[..., None], (2, n2, n1, LANES))
    ngrp = wf // gw
    return pl.pallas_call(
        functools.partial(_fft_fast_kernel, n1=n1, n2=n2, gw=gw, scale=1.0 / math.sqrt(t * gw)),
        grid=(batch, ngrp),
        in_specs=[
            pl.BlockSpec((t, gw), lambda b, g: (b, col_tile * ngrp + g)),
            pl.BlockSpec((gw, 2 * gw), lambda b, g: (0, 0)),
            pl.BlockSpec((2, n2, n1, LANES), lambda b, g: (0, 0, 0, 0), pipeline_mode=pl.Buffered(1)),
            pl.BlockSpec((n1, 2 * n1), lambda b, g: (0, 0)),
        ],
        out_specs=pl.BlockSpec((t, gw), lambda b, g: (b, g)),
        out_shape=jax.ShapeDtypeStruct((batch * t, wf), BF16),
        scratch_shapes=[
            pltpu.VMEM((2, n2, n1, gw), F32),
            pltpu.VMEM((gw // LANES, t, LANES), F32),
        ],
        compiler_params=_cparams("arbitrary", "arbitrary"),
        name="fourier_fast",
    )(p, csc, tw, cst)


def _branch_kernel(conv_ref, uz_ref, zc_ref, g01_ref, g2_ref, ys_ref, yf_ref, cw_ref, sd_ref, wa_ref, wb_ref,
                   fw_ref, pa_ref, pb_ref, pc_ref, m_ref, *, row_len):
    tm = conv_ref.shape[0]
    tn = zc_ref.shape[1]
    d = g2_ref.shape[1]
    xa_ref, ba_ref, ca_ref, za_ref = (conv_ref.at[:, k * tn:(k + 1) * tn] for k in range(4))
    u_ref, zb_ref = (uz_ref.at[:, k * tn:(k + 1) * tn] for k in range(2))
    g0_ref, g1_ref = (g01_ref.at[:, k * d:(k + 1) * d] for k in range(2))
    f = lambda r: r[...].astype(F32)
    v = f(ca_ref) * f(xa_ref)
    pos = lax.broadcasted_iota(jnp.int32, v.shape, 0) % row_len
    v_prev = jnp.where(pos == 0, 0.0, pltpu.roll(v, 1, 0))
    v_next = jnp.where(pos == row_len - 1, 0.0, pltpu.roll(v, tm - 1, 0))
    cw = cw_ref[0]
    conv = v_prev * cw[0:1, :] + v * cw[1:2, :] + v_next * cw[2:3, :]
    a = f(ba_ref) * conv * _silu(f(za_ref))
    ya = jnp.dot(a.astype(BF16), pa_ref[0], preferred_element_type=F32)
    acc = _sigmoid(f(g0_ref)) * ya
    y = _gelu_tanh(ys_ref[...] + sd_ref[0] * f(u_ref)).astype(BF16)
    glu = (jnp.dot(y, wa_ref[0], preferred_element_type=F32)
           * _sigmoid(jnp.dot(y, wb_ref[0], preferred_element_type=F32)) * _silu(f(zb_ref)))
    yb = jnp.dot(glu.astype(BF16), pb_ref[0], preferred_element_type=F32)
    acc = acc + _sigmoid(f(g1_ref)) * yb
    c = jnp.dot(yf_ref[...], fw_ref[0], preferred_element_type=F32) * _silu(f(zc_ref))
    yc = jnp.dot(c.astype(BF16), pc_ref[0], preferred_element_type=F32)
    acc = acc + _sigmoid(f(g2_ref)) * yc
    m_ref[...] = acc.astype(BF16)


def _branches(p, ys, yf, conv_w, ssm_d, wa, wb, fw, pa, pb, pc, layer, row_len):
    n = p.shape[0]
    wc = conv_w.shape[-1]
    d = pa.shape[-1]
    tm = TM_BRANCH
    tn = COL_TILE
    col = lambda k: pl.BlockSpec((tm, tn), lambda i: (i, k))
    const = lambda shape: pl.BlockSpec(shape, lambda i: (layer,) + (0,) * (len(shape) - 1),
                                       pipeline_mode=pl.Buffered(1))
    g_col0 = N_BRANCH_TILES * tn
    assert g_col0 % (2 * d) == 0 and U_TILE % 2 == 0
    return pl.pallas_call(
        functools.partial(_branch_kernel, row_len=row_len),
        grid=(n // tm,),
        in_specs=[
            pl.BlockSpec((tm, U_TILE * tn), lambda i: (i, 0)),
            pl.BlockSpec((tm, 2 * tn), lambda i: (i, U_TILE // 2)),
            col(7),
            pl.BlockSpec((tm, 2 * d), lambda i: (i, g_col0 // (2 * d))),
            pl.BlockSpec((tm, d), lambda i: (i, g_col0 // d + 2)),
            pl.BlockSpec((tm, tn), lambda i: (i, 0)),
            pl.BlockSpec((tm, tn), lambda i: (i, 0)),
            const((1, 3, wc)), const((1, 1, tn)),
            const((1, tn, tn)), const((1, tn, tn)), const((1, tn, tn)),
            const((1, wc, d)), const((1, tn, d)), const((1, tn, d)),
        ],
        out_specs=pl.BlockSpec((tm, d), lambda i: (i, 0)),
        out_shape=jax.ShapeDtypeStruct((n, d), BF16),
        compiler_params=_cparams("arbitrary"),
        name="branches",
    )(p, p, p, p, p, ys, yf, conv_w, ssm_d.reshape(ssm_d.shape[0], 1, -1),
      wa, wb, fw, pa, pb, pc)


def _out_kernel(*refs, with_next):
    if with_next:
        m_ref, x_ref, gt_ref, g_ref, w_ref, scn_ref, shn_ref, gn_ref, o_ref, h_ref = refs
    else:
        m_ref, x_ref, gt_ref, g_ref, w_ref, o_ref = refs
    o = jnp.dot(m_ref[...], w_ref[0], preferred_element_type=F32)
    ms = jnp.mean(o * o, axis=-1, keepdims=True)
    y = o * lax.rsqrt(ms + RMS_EPS) * g_ref[0]
    xn = x_ref[...] + gt_ref[0] * y
    o_ref[...] = xn
    if with_next:
        h_ref[...] = _modnorm(xn, gn_ref[0], scn_ref[0], shn_ref[0]).astype(BF16)


def _outproj(m, x2, mods_r, g_post, w_out_b, layer, rows_per_batch, ctx_row=None, g_pre_next=None):
    n, d = x2.shape
    tm = min(TM_OUT, rows_per_batch)
    tpb = rows_per_batch // tm
    in_specs = [
        pl.BlockSpec((tm, d), lambda i: (i, 0)),
        pl.BlockSpec((tm, d), lambda i: (i, 0)),
        _mod_spec(d, layer, MOD_GATE, tpb, ctx_row),
        pl.BlockSpec((1, 1, d), lambda i: (layer, 0, 0)),
        pl.BlockSpec((1, d, d), lambda i: (layer, 0, 0), pipeline_mode=pl.Buffered(1)),
    ]
    args = [m, x2, mods_r, g_post.reshape(g_post.shape[0], 1, d), w_out_b]
    out_specs = [pl.BlockSpec((tm, d), lambda i: (i, 0))]
    out_shape = [jax.ShapeDtypeStruct((n, d), F32)]
    if g_pre_next is not None:
        in_specs += [_mod_spec(d, layer + 1, MOD_SCALE, tpb, ctx_row), _mod_spec(d, layer + 1, MOD_SHIFT, tpb, ctx_row),
                     pl.BlockSpec((1, 1, d), lambda i: (layer + 1, 0, 0))]
        args += [mods_r, mods_r, g_pre_next.reshape(g_pre_next.shape[0], 1, d)]
        out_specs.append(pl.BlockSpec((tm, d), lambda i: (i, 0)))
        out_shape.append(jax.ShapeDtypeStruct((n, d), BF16))
    res = pl.pallas_call(
        functools.partial(_out_kernel, with_next=g_pre_next is not None),
        grid=(n // tm,),
        in_specs=in_specs,
        out_specs=out_specs,
        out_shape=out_shape,
        compiler_params=_cparams("arbitrary"),
        name="outproj",
    )(*args)
    return (res[0], res[1]) if g_pre_next is not None else (res[0], None)


def kernel(x, c, ctx, c_ctx, w_ada, b_ada, g_pre, g_post, w_in, conv_w, ssm_lam_re, ssm_lam_im, ssm_log_dt,
           ssm_b_re, ssm_b_im, ssm_c_re, ssm_c_im, ssm_d, glu_wa, glu_wb, fourier_w, proj_a, proj_b, proj_c,
           w_out):
    bn, t, d = x.shape
    lc = ctx.shape[1]
    depth = w_ada.shape[0]
    w_conv = conv_w.shape[-1]
    w_ssm = ssm_d.shape[-1]
    tn = COL_TILE
    assert w_conv == tn and w_ssm == tn and fourier_w.shape[-1] == tn and d % tn == 0
    assert bn + 1 <= MOD_ROWS and t % GRID_W == 0
    u_tile = 4 * w_conv // tn
    f_tile = u_tile + 2
    n_tiles = w_in.shape[-1] // tn
    assert u_tile == U_TILE and f_tile + 2 == N_BRANCH_TILES and (n_tiles - N_BRANCH_TILES) * tn == N_BRANCH * d

    cc = jnp.concatenate([c, c_ctx[None, :], jnp.zeros((MOD_ROWS - bn - 1, d), F32)], axis=0)
    mods = _ada_mods(cc, w_ada, b_ada)

    to_b = lambda w: w.astype(BF16)
    wa_b, wb_b, fw_b = to_b(glu_wa), to_b(glu_wb), to_b(fourier_w)
    pa_b, pb_b, pc_b, wo_b = to_b(proj_a), to_b(proj_b), to_b(proj_c), to_b(w_out)
    fast_dft = t == DFT_N1 * DFT_N2
    tabs_x = None if fast_dft else _dft_tables(t, tn // FFT_GROUPS)
    tabs_c = _dft_tables(lc, tn // FFT_GROUPS)

    x2 = x.reshape(bn * t, d)
    c2 = ctx.reshape(bn * lc, d)
    mods_r = mods.reshape(depth * MOD_ROWS * 3, 1, d)
    ctx_row = bn
    ssm_p = _ssm_params(ssm_lam_re, ssm_lam_im, ssm_log_dt, ssm_b_re, ssm_b_im, ssm_c_re, ssm_c_im)
    hx = _prenorm(x2, mods_r, g_pre, 0, t)
    hc = _prenorm(c2, mods_r, g_pre, 0, lc, ctx_row)
    for l in range(depth):
        last = l == depth - 1
        g_pre_next = None if last else g_pre

        px = _inproj(hx, w_in, l, 0, n_tiles)
        pc_ = _inproj(hc, w_in, l, u_tile, 1) if last else _inproj(hc, w_in, l, 0, n_tiles)
        u_lane_tile = u_tile * (tn // LANES)
        ys_x, ys_c = _ssm_scan(px, pc_, u_lane_tile, 0 if last else u_lane_tile, w_ssm, ssm_p, l, bn,
                               ctx_out=not last)

        yf_x = _fourier_fast(px, f_tile, t, bn) if fast_dft else _fourier(px, f_tile, t, bn, tabs_x)
        mx = _branches(px, ys_x, yf_x, conv_w, ssm_d, wa_b, wb_b, fw_b, pa_b, pb_b, pc_b, l, GRID_W)
        new_x2, hx = _outproj(mx, x2, mods_r, g_post, wo_b, l, t, None, g_pre_next)
        if not last:
            yf_c = _fourier(pc_, f_tile, lc, bn, tabs_c)
            mc = _branches(pc_, ys_c, yf_c, conv_w, ssm_d, wa_b, wb_b, fw_b, pa_b, pb_b, pc_b, l, lc)
            c2, hc = _outproj(mc, c2, mods_r, g_post, wo_b, l, lc, ctx_row, g_pre_next)
        x2 = new_x2
    return x2.reshape(bn, t, d)
```

```python
import functools
import math

import jax
import jax.numpy as jnp
from jax import lax
from jax.experimental import pallas as pl
from jax.experimental.pallas import tpu as pltpu

F32 = jnp.float32
BF16 = jnp.bfloat16

GRID_W = 64
FFT_GROUPS = 4
N_BRANCH = 3
RMS_EPS = 1e-6
LANES = 128
SSM_CHUNK = 8
SSM_SEGS = 8
DFT_ROWS = 64
DFT_N1, DFT_N2 = 256, 16
VMEM_LIMIT = 56 * 1024 * 1024
COL_TILE = 1024
N_BRANCH_TILES = 8
U_TILE = 4
TM_INPROJ, TM_PRENORM, TM_BRANCH, TM_OUT, TM_DFT = 2048, 1024, 256, 512, 256
TM_INPROJ_DOT = 1024


def _cparams(*sem):
    return pltpu.CompilerParams(dimension_semantics=sem, vmem_limit_bytes=VMEM_LIMIT)


def _sigmoid(v):
    return 0.5 * jnp.tanh(0.5 * v) + 0.5


def _silu(v):
    return v * _sigmoid(v)


def _gelu_tanh(v):
    return v * (0.5 * (1.0 + jnp.tanh(math.sqrt(2.0 / math.pi) * (v + 0.044715 * (v * v * v)))))


def _cmul(ar, ai, br, bi):
    return ar * br - ai * bi, ar * bi + ai * br


def _cmul_add(ar, ai, zr, zi, sr, si):
    return ar * zr - ai * zi + sr, ar * zi + ai * zr + si


def _ada_kernel(c_ref, w_ref, b_ref, o_ref):
    c = c_ref[...]
    s = _silu(c).astype(BF16)
    o_ref[0] = jnp.dot(s, w_ref[0].astype(BF16), preferred_element_type=F32) + b_ref[0]


def _ada_mods(cc, w_ada, b_ada):
    depth, d, n3 = w_ada.shape
    rows = cc.shape[0]
    tn = 2 * COL_TILE if n3 % (2 * COL_TILE) == 0 else COL_TILE
    assert n3 % tn == 0
    return pl.pallas_call(
        _ada_kernel,
        grid=(depth, n3 // tn),
        in_specs=[
            pl.BlockSpec((rows, d), lambda l, j: (0, 0)),
            pl.BlockSpec((1, d, tn), lambda l, j: (l, 0, j)),
            pl.BlockSpec((1, 1, tn), lambda l, j: (l, 0, j)),
        ],
        out_specs=pl.BlockSpec((1, rows, tn), lambda l, j: (l, 0, j)),
        out_shape=jax.ShapeDtypeStruct((depth, rows, n3), F32),
        compiler_params=_cparams("arbitrary", "arbitrary"),
        name="ada_mods",
    )(cc, w_ada, b_ada.reshape(depth, 1, n3))


def _modnorm(x, gs, sh):
    ms = jnp.mean(x * x, axis=-1, keepdims=True)
    return x * lax.rsqrt(ms + RMS_EPS) * gs + sh


def _prenorm_kernel(x_ref, sc_ref, sh_ref, g_ref, h_ref):
    rb = 16
    gs = g_ref[0] * (1.0 + sc_ref[0])
    sh = sh_ref[0]

    def body(r, carry):
        rows = pl.ds(pl.multiple_of(r * rb, rb), rb)
        h_ref[rows, :] = _modnorm(x_ref[rows, :], gs, sh).astype(BF16)
        return carry

    lax.fori_loop(0, x_ref.shape[0] // rb, body, 0, unroll=4)


MOD_SHIFT, MOD_SCALE, MOD_GATE = 0, 1, 2
MOD_ROWS = 8


def _mod_spec(d, layer, which, tiles_per_batch, ctx_row):
    def index(i):
        row = ctx_row if ctx_row is not None else i // tiles_per_batch
        return ((layer * MOD_ROWS + row) * 3 + which, 0, 0)
    return pl.BlockSpec((1, 1, d), index)


def _prenorm(x2, mods_r, g_pre, layer, rows_per_batch, ctx_row=None):
    n, d = x2.shape
    tm = min(TM_PRENORM, rows_per_batch)
    tpb = rows_per_batch // tm
    return pl.pallas_call(
        _prenorm_kernel,
        grid=(n // tm,),
        in_specs=[
            pl.BlockSpec((tm, d), lambda i: (i, 0)),
            _mod_spec(d, layer, MOD_SCALE, tpb, ctx_row),
            _mod_spec(d, layer, MOD_SHIFT, tpb, ctx_row),
            pl.BlockSpec((1, 1, d), lambda i: (layer, 0, 0)),
        ],
        out_specs=pl.BlockSpec((tm, d), lambda i: (i, 0)),
        out_shape=jax.ShapeDtypeStruct((n, d), BF16),
        compiler_params=_cparams("arbitrary"),
        name="prenorm",
    )(x2, mods_r, mods_r, g_pre.reshape(g_pre.shape[0], 1, d))


def _inproj_kernel(h_ref, w_ref, p_ref, wb):
    @pl.when(pl.program_id(1) == 0)
    def _():
        wb[...] = w_ref[0].astype(BF16)

    sub = min(TM_INPROJ_DOT, h_ref.shape[0])
    for r in range(h_ref.shape[0] // sub):
        rows = slice(r * sub, (r + 1) * sub)
        p_ref[rows, :] = jnp.dot(h_ref[rows, :], wb[...], preferred_element_type=F32).astype(BF16)


def _inproj(h, w_in, layer, tile0, ntiles):
    n, d = h.shape
    tn = COL_TILE
    tm = min(TM_INPROJ, n)
    return pl.pallas_call(
        _inproj_kernel,
        grid=(ntiles, n // tm),
        in_specs=[
            pl.BlockSpec((tm, d), lambda j, i: (i, 0)),
            pl.BlockSpec((1, d, tn), lambda j, i: (layer, 0, tile0 + j)),
        ],
        out_specs=pl.BlockSpec((tm, tn), lambda j, i: (i, j)),
        out_shape=jax.ShapeDtypeStruct((n, ntiles * tn), BF16),
        scratch_shapes=[pltpu.VMEM((d, tn), BF16)],
        compiler_params=_cparams("arbitrary", "arbitrary"),
        name="inproj",
    )(h, w_in)


def _ssm_params(lam_re, lam_im, log_dt, b_re, b_im, c_re, c_im):
    nl, _, G, P = lam_re.shape
    H = b_re.shape[-1]
    gpt = LANES // H
    O = G // gpt
    lam_re = lam_re.astype(F32)
    lam_im = lam_im.astype(F32)
    dt = jnp.exp(log_dt.astype(F32))[..., None]
    lr = lam_re * dt
    li = lam_im * dt
    mag = jnp.exp(lr)
    a_re = mag * jnp.cos(li)
    a_im = mag * jnp.sin(li)
    n_re = a_re - 1.0
    n_im = a_im
    den = lam_re * lam_re + lam_im * lam_im
    q_re = (n_re * lam_re + n_im * lam_im) / den
    q_im = (n_im * lam_re - n_re * lam_im) / den
    bb_re = q_re[..., None] * b_re - q_im[..., None] * b_im
    bb_im = q_re[..., None] * b_im + q_im[..., None] * b_re

    bbs = jnp.stack([bb_re, bb_im], axis=2).reshape(nl, 2, 2, O, gpt, P, H)
    bc = jnp.transpose(bbs, (0, 3, 1, 2, 6, 4, 5)).reshape(nl, O, 4, H, gpt * P)
    cs = jnp.stack([c_re.astype(F32), c_im.astype(F32)], axis=2).reshape(nl, 2, 2, O, gpt, H, P)
    cct = jnp.transpose(cs, (0, 3, 1, 2, 4, 6, 5)).reshape(nl, O, 4, gpt * P, 1, H)
    cct = jnp.broadcast_to(cct, (nl, O, 4, gpt * P, gpt, H)).reshape(nl, O, 4, gpt * P, LANES)
    a4 = jnp.stack([a_re, a_im], axis=2).reshape(nl, 2, 2, O, gpt * P)
    a4 = jnp.transpose(a4, (0, 3, 1, 2, 4)).reshape(nl, O, 4, gpt * P)
    arow = jnp.concatenate([a4, jnp.zeros_like(a4)], axis=2)
    acol = jnp.broadcast_to(a4[..., None], (nl, O, 4, gpt * P, LANES))

    return bc, cct, arow, acol


def _block_transpose8(v):
    bw = LANES // 8
    blk = lax.broadcasted_iota(jnp.int32, v[0].shape, 1) // bw
    for dist in (4, 2, 1):
        upper = (blk & dist) != 0
        nxt = list(v)
        for a in range(8):
            if a & dist:
                continue
            b = a + dist
            nxt[a] = jnp.where(upper, pltpu.roll(v[b], dist * bw, 1), v[a])
            nxt[b] = jnp.where(upper, v[b], pltpu.roll(v[a], LANES - dist * bw, 1))
        v = nxt
    return v


def _ssm_build_operators(bc_ref, cct_ref, arow_ref, acol_ref, ws, ws_lo, w3):
    L = SSM_CHUNK
    hh = bc_ref.shape[2]
    sw = bc_ref.shape[3]
    pp = sw // (LANES // hh)
    npair = sw // LANES
    assert L * hh == LANES and 2 * pp == LANES
    lane = lax.broadcasted_iota(jnp.int32, (LANES, LANES), 1)
    row = lax.broadcasted_iota(jnp.int32, (LANES, LANES), 0)
    lane_g0, row_g0 = lane < pp, row < pp
    lane_blk = lane // hh
    zeros = jnp.zeros((LANES, LANES), F32)
    zeros_b = jnp.zeros((LANES, LANES), BF16)

    def powers(re, im, n):
        out = [(jnp.ones_like(re), jnp.zeros_like(re))]
        for _ in range(n):
            out.append(_cmul(out[-1][0], out[-1][1], re, im))
        return out

    def split(v):
        hi = v.astype(BF16)
        return hi, (v - hi.astype(F32)).astype(BF16)

    def shift_rows(t, blocks):
        n = abs(blocks) * hh
        if n == 0:
            return t
        pad = jnp.zeros((n, LANES), F32)
        return jnp.concatenate([pad, t[:LANES - n]] if blocks > 0 else [t[n:], pad], axis=0)

    for q in range(npair):
        ql = slice(q * LANES, (q + 1) * LANES)
        taps = []
        for d in range(2):
            b_re = jnp.concatenate([bc_ref[0, 2 * d, :, ql]] * L, axis=0)
            b_im = jnp.concatenate([bc_ref[0, 2 * d + 1, :, ql]] * L, axis=0)
            prow = powers(arow_ref[0, 2 * d:2 * d + 1, ql], arow_ref[0, 2 * d + 1:2 * d + 2, ql], L - 1)
            exps = [L - 1 - j if d == 0 else j for j in range(L)]
            a_re = jnp.concatenate([jnp.broadcast_to(prow[e][0], (hh, LANES)) for e in exps], axis=0)
            a_im = jnp.concatenate([jnp.broadcast_to(prow[e][1], (hh, LANES)) for e in exps], axis=0)
            for part, v in zip((2 * d, 2 * d + 1), _cmul(b_re, b_im, a_re, a_im)):
                cols = slice(part * LANES, (part + 1) * LANES)
                for dst, piece in zip((ws, ws_lo), split(v)):
                    dst[q, 0:LANES, cols] = jnp.where(lane_g0, piece, zeros_b)
                    dst[q, LANES:2 * LANES, cols] = jnp.where(lane_g0, zeros_b, piece)
            c_re = cct_ref[0, 2 * d, ql, :]
            c_im = cct_ref[0, 2 * d + 1, ql, :]
            (cr_hi, cr_lo), (ci_hi, ci_lo) = split(c_re), split(-c_im)
            c_hi = jnp.concatenate([cr_hi, ci_hi], axis=0)
            c_lo = jnp.concatenate([cr_lo, ci_lo], axis=0)
            dcols = slice(2 * d * LANES, (2 * d + 2) * LANES)
            taps.append(jnp.dot(jnp.concatenate([ws[q, :, dcols], ws[q, :, dcols], ws_lo[q, :, dcols]], axis=1),
                                jnp.concatenate([c_hi, c_lo, c_hi], axis=0), preferred_element_type=F32))
            pcol = powers(acol_ref[0, 2 * d, ql, :], acol_ref[0, 2 * d + 1, ql, :], L)
            ap_re, ap_im = zeros, zeros
            for jo in range(L):
                e = jo + 1 if d == 0 else L - jo
                ap_re = jnp.where(lane_blk == jo, pcol[e][0], ap_re)
                ap_im = jnp.where(lane_blk == jo, pcol[e][1], ap_im)
            ca_re, ca_im = _cmul(c_re, c_im, ap_re, ap_im)
            for part, v in ((2 * d, ca_re), (2 * d + 1, -ca_im)):
                r0 = 2 * LANES + part * LANES
                w3[q, r0:r0 + LANES, 0:LANES] = jnp.where(row_g0, v, zeros).astype(BF16)
                w3[q, r0:r0 + LANES, LANES:2 * LANES] = jnp.where(row_g0, zeros, v).astype(BF16)
        for g2 in range(2):
            tf = taps[0][g2 * LANES:(g2 + 1) * LANES]
            tb = taps[1][g2 * LANES:(g2 + 1) * LANES]
            m = zeros
            for jo in range(L):
                m = jnp.where(lane_blk == jo, shift_rows(tf, jo - (L - 1)) + shift_rows(tb, jo), m)
            rows = slice(g2 * LANES, (g2 + 1) * LANES)
            w3[q, rows, g2 * LANES:(g2 + 1) * LANES] = m.astype(BF16)
            w3[q, rows, (1 - g2) * LANES:(2 - g2) * LANES] = zeros_b


def _ssm_build_powers(arow_ref, ap, pw, seg_len):
    L = SSM_CHUNK
    sw = arow_ref.shape[2]
    assert L & (L - 1) == 0 and seg_len & (seg_len - 1) == 0 and seg_len >= 8
    for d in range(2):
        cr, ci = slice(2 * d * sw, (2 * d + 1) * sw), slice((2 * d + 1) * sw, (2 * d + 2) * sw)
        re, im = arow_ref[0, 2 * d:2 * d + 1, :], arow_ref[0, 2 * d + 1:2 * d + 2, :]
        for _ in range(L.bit_length() - 1):
            re, im = _cmul(re, im, re, im)
        ap[0:1, cr] = re
        ap[0:1, ci] = im
        p = [(jnp.ones_like(re), jnp.zeros_like(re))]
        for _ in range(7):
            p.append(_cmul(p[-1][0], p[-1][1], re, im))
        order = list(range(8)) if d == 0 else list(range(7, -1, -1))
        first = slice(0, 8) if d == 0 else slice(seg_len - 8, seg_len)
        pw[first, cr] = jnp.concatenate([p[r][0] for r in order], axis=0)
        pw[first, ci] = jnp.concatenate([p[r][1] for r in order], axis=0)
        sq = _cmul(p[4][0], p[4][1], p[4][0], p[4][1])
        m = 8
        while m < seg_len:
            src = slice(0, m) if d == 0 else slice(seg_len - m, seg_len)
            dst = slice(m, 2 * m) if d == 0 else slice(seg_len - 2 * m, seg_len - m)
            xr, xi = _cmul(pw[src, cr], pw[src, ci], sq[0], sq[1])
            pw[dst, cr] = xr
            pw[dst, ci] = xi
            sq = _cmul(sq[0], sq[1], sq[0], sq[1])
            m *= 2
        ap[1:2, cr] = sq[0]
        ap[1:2, ci] = sq[1]


def _seg_pitch(seg_len):
    tiles = seg_len // 8 + 1
    return 8 * (tiles + 1 - tiles % 2)


def _ssm_kernel(*refs, nc, nx, seg_len, ctx_out):
    if ctx_out:
        (ux_ref, uc_ref, bc_ref, cct_ref, arow_ref, acol_ref,
         yx_ref, yc_ref, ws, ws_lo, w3, lhs, st, uxf, ucf, ap, pw) = refs
    else:
        (ux_ref, uc_ref, bc_ref, cct_ref, arow_ref, acol_ref,
         yx_ref, ws, ws_lo, w3, lhs, st, uxf, ucf, ap, pw) = refs
        yc_ref = None
    L = SSM_CHUNK
    nlt = st.shape[0] // 4
    sw = nlt * LANES
    xw = 2 * LANES
    FR, FI, BR, BI = range(4)
    nseg = SSM_SEGS
    pitch = _seg_pitch(seg_len)

    @pl.when(pl.program_id(1) == 0)
    def _():
        _ssm_build_operators(bc_ref, cct_ref, arow_ref, acol_ref, ws, ws_lo, w3)
        _ssm_build_powers(arow_ref, ap, pw, seg_len)

    def ld(part, rows):
        return jnp.concatenate([st[part * nlt + q, rows, :] for q in range(nlt)], axis=1)

    def sto(part, rows, val):
        for q in range(nlt):
            st[part * nlt + q, rows, :] = val[:, q * LANES:(q + 1) * LANES]

    def part_cols(part):
        return slice(part * sw, (part + 1) * sw)

    ucf[...] = uc_ref[...].astype(F32)
    uxf[...] = ux_ref[...].astype(F32)
    by_pos = [jnp.concatenate([ucf[pl.ds(j, nc, stride=L), :], uxf[pl.ds(j, nx, stride=L), :]], axis=0)
              for j in range(L)]
    for g, tile_g in enumerate(_block_transpose8(by_pos)):
        lhs[g // 2, :, (g % 2) * LANES:(g % 2 + 1) * LANES] = tile_g.astype(BF16)

    for q in range(nlt):
        s = jnp.dot(lhs[q, :, 0:xw], ws[q], preferred_element_type=F32)
        for part in range(4):
            k = part * nlt + q
            st[k, 0:nc, :] = s[0:nc, part * LANES:(part + 1) * LANES]
            for g in range(nseg):
                st[k, nc + g * pitch:nc + g * pitch + seg_len, :] = (
                    s[nc + g * seg_len:nc + (g + 1) * seg_len, part * LANES:(part + 1) * LANES])

    a_l = ap[0:1, :]
    a_seg = ap[1:2, :]
    al = [a_l[:, part_cols(p)] for p in range(4)]
    aseg = [a_seg[:, part_cols(p)] for p in range(4)]

    zero = jnp.zeros((1, sw), F32)

    def ctx_step(pr, pi):
        def step(i, carry):
            er, ei = carry
            row = pl.ds(i, 1)
            sr, si = ld(pr, row), ld(pi, row)
            sto(pr, row, er)
            sto(pi, row, ei)
            return _cmul_add(al[pr], al[pi], er, ei, sr, si)
        return step

    h0f = lax.fori_loop(0, nc, ctx_step(FR, FI), (zero, zero))
    bwd_step = ctx_step(BR, BI)
    h0b = lax.fori_loop(0, nc, lambda k, carry: bwd_step(nc - 1 - k, carry), (zero, zero))
    for part in range(4):
        for q in range(nlt):
            lhs[q, 0:nc, xw + part * LANES:xw + (part + 1) * LANES] = st[part * nlt + q, 0:nc, :].astype(BF16)

    zseg = jnp.zeros((nseg, sw), F32)

    def seg_pass(pr, pi, order):
        zr, zi = zseg, zseg
        for i in order:
            rows = pl.ds(nc + i, nseg, stride=pitch)
            sr, si = ld(pr, rows), ld(pi, rows)
            sto(pr, rows, zr)
            sto(pi, rows, zi)
            zr, zi = _cmul_add(al[pr], al[pi], zr, zi, sr, si)
        return zr, zi

    zfr, zfi = seg_pass(FR, FI, range(seg_len))
    zbr, zbi = seg_pass(BR, BI, range(seg_len - 1, -1, -1))

    ef = [h0f]
    for s in range(nseg - 1):
        ef.append(_cmul_add(aseg[FR], aseg[FI], ef[s][0], ef[s][1], zfr[s:s + 1], zfi[s:s + 1]))
    eb = [None] * nseg
    eb[nseg - 1] = h0b
    for s in range(nseg - 1, 0, -1):
        eb[s - 1] = _cmul_add(aseg[BR], aseg[BI], eb[s][0], eb[s][1], zbr[s:s + 1], zbi[s:s + 1])
    for s in range(nseg):
        srows = slice(nc + s * pitch, nc + s * pitch + seg_len)
        rows = slice(nc + s * seg_len, nc + (s + 1) * seg_len)
        for (pr, pi, e) in ((FR, FI, ef[s]), (BR, BI, eb[s])):
            xr, xi = _cmul_add(pw[:, part_cols(pr)], pw[:, part_cols(pi)], e[0], e[1],
                               ld(pr, srows), ld(pi, srows))
            for q in range(nlt):
                ql = slice(q * LANES, (q + 1) * LANES)
                lhs[q, rows, xw + pr * LANES:xw + (pr + 1) * LANES] = xr[:, ql].astype(BF16)
                lhs[q, rows, xw + pi * LANES:xw + (pi + 1) * LANES] = xi[:, ql].astype(BF16)

    by_group = []
    for q in range(nlt):
        y = jnp.dot(lhs[q], w3[q], preferred_element_type=F32)
        by_group += [y[:, 0:LANES], y[:, LANES:2 * LANES]]
    for j, tile_j in enumerate(_block_transpose8(by_group)):
        if ctx_out:
            yc_ref[pl.ds(j, nc, stride=L), :] = tile_j[0:nc]
        yx_ref[pl.ds(j, nx, stride=L), :] = tile_j[nc:nc + nx]


def _ssm_scan(px, pc, col_x, col_c, w, params, layer, batch, ctx_out):
    bc, cct, arow, acol = params
    L = SSM_CHUNK
    t = px.shape[0] // batch
    lc = pc.shape[0] // batch
    nx, nc = t // L, lc // L
    seg_len = nx // SSM_SEGS
    o = w // LANES
    sw = bc.shape[-1]
    sdim = 4 * sw
    npair = sw // LANES
    out_shape = [jax.ShapeDtypeStruct((batch * t, w), F32)]
    out_specs = [pl.BlockSpec((t, LANES), lambda oi, b: (b, oi))]
    if ctx_out:
        out_shape.append(jax.ShapeDtypeStruct((batch * lc, w), F32))
        out_specs.append(pl.BlockSpec((lc, LANES), lambda oi, b: (b, oi)))
    tile = lambda a: pl.BlockSpec((None, 1) + a.shape[2:], lambda oi, b: (layer, oi) + (0,) * (a.ndim - 2))
    res = pl.pallas_call(
        functools.partial(_ssm_kernel, nc=nc, nx=nx, seg_len=seg_len, ctx_out=ctx_out),
        grid=(o, batch),
        in_specs=[
            pl.BlockSpec((t, LANES), lambda oi, b: (b, col_x + oi)),
            pl.BlockSpec((lc, LANES), lambda oi, b: (b, col_c + oi)),
            tile(bc), tile(cct), tile(arow), tile(acol),
        ],
        out_specs=out_specs,
        out_shape=out_shape,
        scratch_shapes=[
            pltpu.VMEM((npair, 2 * LANES, 4 * LANES), BF16),
            pltpu.VMEM((npair, 2 * LANES, 4 * LANES), BF16),
            pltpu.VMEM((npair, 6 * LANES, 2 * LANES), BF16),
            pltpu.VMEM((npair, nc + nx, 6 * LANES), BF16),
            pltpu.VMEM((sdim // LANES, nc + SSM_SEGS * _seg_pitch(seg_len), LANES), F32),
            pltpu.VMEM((t, LANES), F32),
            pltpu.VMEM((lc, LANES), F32),
            pltpu.VMEM((8, sdim), F32),
            pltpu.VMEM((seg_len, sdim), F32),
        ],
        compiler_params=_cparams("arbitrary", "arbitrary"),
        name="ssm_scan",
    )(px, pc, bc, cct, arow, acol)
    return (res[0], res[1]) if ctx_out else (res[0], None)


def _dft_table_kernel(e1_ref, e2_ref, o_ref, *, t):
    e1c, e1s = e1_ref[0, :, 0:t], e1_ref[0, :, t:2 * t]
    e2c, e2s = e2_ref[:, 0:t], e2_ref[:, t:2 * t]
    c, s = _cmul(e2c, e2s, e1c, e1s)
    o_ref[:, 0:t] = c.astype(BF16)
    o_ref[:, t:2 * t] = (-s).astype(BF16)


def _dft_tables(t, gw):
    def cis(rows_mult, nrows):
        k = jnp.arange(nrows, dtype=jnp.int32)[:, None] * rows_mult
        n = jnp.arange(t, dtype=jnp.int32)[None, :]
        ang = ((k * n) % t).astype(F32) * (2.0 * math.pi / t)
        return jnp.concatenate([jnp.cos(ang), jnp.sin(ang)], axis=1)
    r = DFT_ROWS
    e1 = cis(r, t // r).reshape(t // r, 1, 2 * t)
    e2 = cis(1, r)
    tab_t = pl.pallas_call(
        functools.partial(_dft_table_kernel, t=t),
        grid=(t // r,),
        in_specs=[
            pl.BlockSpec((1, 1, 2 * t), lambda i: (i, 0, 0)),
            pl.BlockSpec((r, 2 * t), lambda i: (0, 0)),
        ],
        out_specs=pl.BlockSpec((r, 2 * t), lambda i: (i, 0)),
        out_shape=jax.ShapeDtypeStruct((t, 2 * t), BF16),
        compiler_params=_cparams("arbitrary"),
        name="dft_table",
    )(e1, e2)
    kc = jnp.arange(gw, dtype=jnp.int32)
    ang = ((kc[:, None] * kc[None, :]) % gw).astype(F32) * (2.0 * math.pi / gw)
    tab_c = jnp.concatenate([jnp.cos(ang), jnp.sin(ang)], axis=1).astype(BF16)
    return tab_t, tab_c


def _fft_kernel(f_ref, cs_ref, tab_ref, o_ref, data, *, t, gw, groups, scale):
    i = pl.program_id(1)

    @pl.when(i == 0)
    def _():
        rb = min(512, t)
        for r in range(t // rb):
            for g in range(groups):
                fg = f_ref[r * rb:(r + 1) * rb, g * gw:(g + 1) * gw]
                z = jnp.dot(fg, cs_ref[...], preferred_element_type=F32)
                data[r * rb:(r + 1) * rb, g * gw:(g + 1) * gw] = z[:, :gw].astype(BF16)
                data[t + r * rb:t + (r + 1) * rb, g * gw:(g + 1) * gw] = z[:, gw:].astype(BF16)

    y = jnp.dot(tab_ref[...], data[...], preferred_element_type=F32)
    o_ref[...] = (y * scale).astype(BF16)


def _fourier(p, col_tile, t, batch, tabs):
    tab_t, tab_c = tabs
    wf = COL_TILE
    gw = wf // FFT_GROUPS
    tm = min(TM_DFT, t)
    return pl.pallas_call(
        functools.partial(_fft_kernel, t=t, gw=gw, groups=FFT_GROUPS, scale=1.0 / math.sqrt(t * gw)),
        grid=(batch, t // tm),
        in_specs=[
            pl.BlockSpec((t, wf), lambda b, i: (b, col_tile)),
            pl.BlockSpec((gw, 2 * gw), lambda b, i: (0, 0)),
            pl.BlockSpec((tm, 2 * t), lambda b, i: (i, 0)),
        ],
        out_specs=pl.BlockSpec((tm, wf), lambda b, i: (b * (t // tm) + i, 0)),
        out_shape=jax.ShapeDtypeStruct((batch * t, wf), BF16),
        scratch_shapes=[pltpu.VMEM((2 * t, wf), BF16)],
        compiler_params=_cparams("arbitrary", "arbitrary"),
        name="fourier",
    )(p, tab_c, tab_t)


def _cmul_const(xr, xi, c, s):
    def scaled(v, k):
        if abs(k) < 1e-12:
            return None
        return v if abs(k - 1.0) < 1e-12 else -v if abs(k + 1.0) < 1e-12 else v * k

    def add(a, b):
        return b if a is None else a if b is None else a + b

    return add(scaled(xr, c), scaled(xi, -s)), add(scaled(xi, c), scaled(xr, s))


def _dft4(y):
    (ar, ai), (br, bi), (cr, ci), (dr, di) = y
    sr, si, tr, ti = ar + cr, ai + ci, ar - cr, ai - ci
    ur, ui, vr, vi = br + dr, bi + di, br - dr, bi - di
    return [(sr + ur, si + ui), (tr + vi, ti - vr), (sr - ur, si - ui), (tr - vi, ti + vr)]


def _dft16(x):
    out = [None] * 16
    p = [_dft4([x[4 * a + b] for a in range(4)]) for b in range(4)]
    for ka in range(4):
        q = []
        for b in range(4):
            ang = -2.0 * math.pi * ka * b / 16.0
            q.append(_cmul_const(p[b][ka][0], p[b][ka][1], math.cos(ang), math.sin(ang)))
        r = _dft4(q)
        for kb in range(4):
            out[ka + 4 * kb] = r[kb]
    return out


def _fft_fast_kernel(f_ref, csc_ref, tw_ref, cst_ref, o_ref, z, o_scr, *, n1, n2, gw, scale):
    slabs_per_dot = 4
    for q in range(n2 // slabs_per_dot):
        r0 = q * slabs_per_dot * n1
        zz = jnp.dot(f_ref[r0:r0 + slabs_per_dot * n1, :], csc_ref[...], preferred_element_type=F32)
        for s in range(slabs_per_dot):
            z[0, q * slabs_per_dot + s] = zz[s * n1:(s + 1) * n1, 0:gw]
            z[1, q * slabs_per_dot + s] = zz[s * n1:(s + 1) * n1, gw:2 * gw]

    def tile_body(r, carry):
        rows = pl.ds(pl.multiple_of(r * 8, 8), 8)
        for c in range(gw // LANES):
            cols = slice(c * LANES, (c + 1) * LANES)
            h = _dft16([(z[0, t, rows, cols], z[1, t, rows, cols]) for t in range(n2)])
            for k2 in range(n2):
                hr, hi = h[k2]
                if k2 > 0:
                    hr, hi = _cmul(hr, hi, tw_ref[0, k2, rows, :], tw_ref[1, k2, rows, :])
                z[0, k2, rows, cols] = hr
                z[1, k2, rows, cols] = hi
        return carry

    lax.fori_loop(0, n1 // 8, tile_body, 0)

    for k2 in range(n2):
        rhs = jnp.concatenate([z[0, k2], z[1, k2]], axis=0).astype(BF16)
        y = jnp.dot(cst_ref[...], rhs, preferred_element_type=F32) * scale
        for c in range(gw // LANES):
            o_scr[c, pl.ds(k2, n1, stride=n2), :] = y[:, c * LANES:(c + 1) * LANES]
    o_ref[...] = jnp.concatenate([o_scr[c] for c in range(gw // LANES)], axis=1).astype(BF16)


def _fourier_fast(p, col_tile, t, batch):
    n1, n2 = DFT_N1, DFT_N2
    assert t == n1 * n2
    wf = COL_TILE
    gw = wf // FFT_GROUPS
    ang = lambda a, b, n: ((a[:, None] * b[None, :]) % n).astype(F32) * (2.0 * math.pi / n)
    kc = jnp.arange(gw, dtype=jnp.int32)
    a_c = ang(kc, kc, gw)
    csc = jnp.concatenate([jnp.cos(a_c), -jnp.sin(a_c)], axis=1).astype(BF16)
    k1 = jnp.arange(n1, dtype=jnp.int32)
    a_t = ang(k1, k1, n1)
    cst = jnp.concatenate([jnp.cos(a_t), jnp.sin(a_t)], axis=1).astype(BF16)
    a_w = ang(jnp.arange(n2, dtype=jnp.int32), k1, t)
    tw = jnp.stack([jnp.cos(a_w), -jnp.sin(a_w)], axis=0)
    tw = jnp.broadcast_to(tw[..., None], (2, n2, n1, LANES))
    ngrp = wf // gw
    return pl.pallas_call(
        functools.partial(_fft_fast_kernel, n1=n1, n2=n2, gw=gw, scale=1.0 / math.sqrt(t * gw)),
        grid=(batch, ngrp),
        in_specs=[
            pl.BlockSpec((t, gw), lambda b, g: (b, col_tile * ngrp + g)),
            pl.BlockSpec((gw, 2 * gw), lambda b, g: (0, 0)),
            pl.BlockSpec((2, n2, n1, LANES), lambda b, g: (0, 0, 0, 0), pipeline_mode=pl.Buffered(1)),
            pl.BlockSpec((n1, 2 * n1), lambda b, g: (0, 0)),
        ],
        out_specs=pl.BlockSpec((t, gw), lambda b, g: (b, g)),
        out_shape=jax.ShapeDtypeStruct((batch * t, wf), BF16),
        scratch_shapes=[
            pltpu.VMEM((2, n2, n1, gw), F32),
            pltpu.VMEM((gw // LANES, t, LANES), F32),
        ],
        compiler_params=_cparams("arbitrary", "arbitrary"),
        name="fourier_fast",
    )(p, csc, tw, cst)


def _branch_kernel(conv_ref, uz_ref, zc_ref, g01_ref, g2_ref, ys_ref, yf_ref, cw_ref, sd_ref, wa_ref, wb_ref,
                   fw_ref, pa_ref, pb_ref, pc_ref, m_ref, *, row_len):
    tm = conv_ref.shape[0]
    tn = zc_ref.shape[1]
    d = g2_ref.shape[1]
    xa_ref, ba_ref, ca_ref, za_ref = (conv_ref.at[:, k * tn:(k + 1) * tn] for k in range(4))
    u_ref, zb_ref = (uz_ref.at[:, k * tn:(k + 1) * tn] for k in range(2))
    g0_ref, g1_ref = (g01_ref.at[:, k * d:(k + 1) * d] for k in range(2))
    f = lambda r: r[...].astype(F32)
    v = f(ca_ref) * f(xa_ref)
    pos = lax.broadcasted_iota(jnp.int32, v.shape, 0) % row_len
    v_prev = jnp.where(pos == 0, 0.0, pltpu.roll(v, 1, 0))
    v_next = jnp.where(pos == row_len - 1, 0.0, pltpu.roll(v, tm - 1, 0))
    cw = cw_ref[0]
    conv = v_prev * cw[0:1, :] + v * cw[1:2, :] + v_next * cw[2:3, :]
    a = f(ba_ref) * conv * _silu(f(za_ref))
    ya = jnp.dot(a.astype(BF16), pa_ref[0], preferred_element_type=F32)
    acc = _sigmoid(f(g0_ref)) * ya
    y = _gelu_tanh(ys_ref[...] + sd_ref[0] * f(u_ref)).astype(BF16)
    glu = (jnp.dot(y, wa_ref[0], preferred_element_type=F32)
           * _sigmoid(jnp.dot(y, wb_ref[0], preferred_element_type=F32)) * _silu(f(zb_ref)))
    yb = jnp.dot(glu.astype(BF16), pb_ref[0], preferred_element_type=F32)
    acc = acc + _sigmoid(f(g1_ref)) * yb
    c = jnp.dot(yf_ref[...], fw_ref[0], preferred_element_type=F32) * _silu(f(zc_ref))
    yc = jnp.dot(c.astype(BF16), pc_ref[0], preferred_element_type=F32)
    acc = acc + _sigmoid(f(g2_ref)) * yc
    m_ref[...] = acc.astype(BF16)


def _branches(p, ys, yf, conv_w, ssm_d, wa, wb, fw, pa, pb, pc, layer, row_len):
    n = p.shape[0]
    wc = conv_w.shape[-1]
    d = pa.shape[-1]
    tm = TM_BRANCH
    tn = COL_TILE
    col = lambda k: pl.BlockSpec((tm, tn), lambda i: (i, k))
    const = lambda shape: pl.BlockSpec(shape, lambda i: (layer,) + (0,) * (len(shape) - 1),
                                       pipeline_mode=pl.Buffered(1))
    g_col0 = N_BRANCH_TILES * tn
    assert g_col0 % (2 * d) == 0 and U_TILE % 2 == 0
    return pl.pallas_call(
        functools.partial(_branch_kernel, row_len=row_len),
        grid=(n // tm,),
        in_specs=[
            pl.BlockSpec((tm, U_TILE * tn), lambda i: (i, 0)),
            pl.BlockSpec((tm, 2 * tn), lambda i: (i, U_TILE // 2)),
            col(7),
            pl.BlockSpec((tm, 2 * d), lambda i: (i, g_col0 // (2 * d))),
            pl.BlockSpec((tm, d), lambda i: (i, g_col0 // d + 2)),
            pl.BlockSpec((tm, tn), lambda i: (i, 0)),
            pl.BlockSpec((tm, tn), lambda i: (i, 0)),
            const((1, 3, wc)), const((1, 1, tn)),
            const((1, tn, tn)), const((1, tn, tn)), const((1, tn, tn)),
            const((1, wc, d)), const((1, tn, d)), const((1, tn, d)),
        ],
        out_specs=pl.BlockSpec((tm, d), lambda i: (i, 0)),
        out_shape=jax.ShapeDtypeStruct((n, d), BF16),
        compiler_params=_cparams("arbitrary"),
        name="branches",
    )(p, p, p, p, p, ys, yf, conv_w, ssm_d.reshape(ssm_d.shape[0], 1, -1),
      wa, wb, fw, pa, pb, pc)


def _out_kernel(*refs, with_next):
    if with_next:
        m_ref, x_ref, gt_ref, g_ref, w_ref, scn_ref, shn_ref, gn_ref, o_ref, h_ref, wb = refs
    else:
        m_ref, x_ref, gt_ref, g_ref, w_ref, o_ref, wb = refs

    @pl.when(pl.program_id(0) == 0)
    def _():
        wb[...] = w_ref[0].astype(BF16)

    o = jnp.dot(m_ref[...], wb[...], preferred_element_type=F32)
    ms = jnp.mean(o * o, axis=-1, keepdims=True)
    xn = x_ref[...] + o * lax.rsqrt(ms + RMS_EPS) * (g_ref[0] * gt_ref[0])
    o_ref[...] = xn
    if with_next:
        h_ref[...] = _modnorm(xn, gn_ref[0] * (1.0 + scn_ref[0]), shn_ref[0]).astype(BF16)


def _outproj(m, x2, mods_r, g_post, w_out, layer, rows_per_batch, ctx_row=None, g_pre_next=None):
    n, d = x2.shape
    tm = min(TM_OUT, rows_per_batch)
    tpb = rows_per_batch // tm
    in_specs = [
        pl.BlockSpec((tm, d), lambda i: (i, 0)),
        pl.BlockSpec((tm, d), lambda i: (i, 0)),
        _mod_spec(d, layer, MOD_GATE, tpb, ctx_row),
        pl.BlockSpec((1, 1, d), lambda i: (layer, 0, 0)),
        pl.BlockSpec((1, d, d), lambda i: (layer, 0, 0), pipeline_mode=pl.Buffered(1)),
    ]
    args = [m, x2, mods_r, g_post.reshape(g_post.shape[0], 1, d), w_out]
    out_specs = [pl.BlockSpec((tm, d), lambda i: (i, 0))]
    out_shape = [jax.ShapeDtypeStruct((n, d), F32)]
    if g_pre_next is not None:
        in_specs += [_mod_spec(d, layer + 1, MOD_SCALE, tpb, ctx_row), _mod_spec(d, layer + 1, MOD_SHIFT, tpb, ctx_row),
                     pl.BlockSpec((1, 1, d), lambda i: (layer + 1, 0, 0))]
        args += [mods_r, mods_r, g_pre_next.reshape(g_pre_next.shape[0], 1, d)]
        out_specs.append(pl.BlockSpec((tm, d), lambda i: (i, 0)))
        out_shape.append(jax.ShapeDtypeStruct((n, d), BF16))
    res = pl.pallas_call(
        functools.partial(_out_kernel, with_next=g_pre_next is not None),
        grid=(n // tm,),
        in_specs=in_specs,
        out_specs=out_specs,
        out_shape=out_shape,
        scratch_shapes=[pltpu.VMEM((d, d), BF16)],
        compiler_params=_cparams("arbitrary"),
        name="outproj",
    )(*args)
    return (res[0], res[1]) if g_pre_next is not None else (res[0], None)


def kernel(x, c, ctx, c_ctx, w_ada, b_ada, g_pre, g_post, w_in, conv_w, ssm_lam_re, ssm_lam_im, ssm_log_dt,
           ssm_b_re, ssm_b_im, ssm_c_re, ssm_c_im, ssm_d, glu_wa, glu_wb, fourier_w, proj_a, proj_b, proj_c,
           w_out):
    bn, t, d = x.shape
    lc = ctx.shape[1]
    depth = w_ada.shape[0]
    w_conv = conv_w.shape[-1]
    w_ssm = ssm_d.shape[-1]
    tn = COL_TILE
    assert w_conv == tn and w_ssm == tn and fourier_w.shape[-1] == tn and d % tn == 0
    assert bn + 1 <= MOD_ROWS and t % GRID_W == 0
    u_tile = 4 * w_conv // tn
    f_tile = u_tile + 2
    n_tiles = w_in.shape[-1] // tn
    assert u_tile == U_TILE and f_tile + 2 == N_BRANCH_TILES and (n_tiles - N_BRANCH_TILES) * tn == N_BRANCH * d

    cc = jnp.concatenate([c, c_ctx[None, :], jnp.zeros((MOD_ROWS - bn - 1, d), F32)], axis=0)
    mods = _ada_mods(cc, w_ada, b_ada)

    to_b = lambda w: w.astype(BF16)
    wa_b, wb_b, fw_b = to_b(glu_wa), to_b(glu_wb), to_b(fourier_w)
    pa_b, pb_b, pc_b = to_b(proj_a), to_b(proj_b), to_b(proj_c)
    fast_dft = t == DFT_N1 * DFT_N2
    tabs_x = None if fast_dft else _dft_tables(t, tn // FFT_GROUPS)
    tabs_c = _dft_tables(lc, tn // FFT_GROUPS)

    x2 = x.reshape(bn * t, d)
    c2 = ctx.reshape(bn * lc, d)
    mods_r = mods.reshape(depth * MOD_ROWS * 3, 1, d)
    ctx_row = bn
    ssm_p = _ssm_params(ssm_lam_re, ssm_lam_im, ssm_log_dt, ssm_b_re, ssm_b_im, ssm_c_re, ssm_c_im)
    hx = _prenorm(x2, mods_r, g_pre, 0, t)
    hc = _prenorm(c2, mods_r, g_pre, 0, lc, ctx_row)
    for l in range(depth):
        last = l == depth - 1
        g_pre_next = None if last else g_pre

        px = _inproj(hx, w_in, l, 0, n_tiles)
        pc_ = _inproj(hc, w_in, l, u_tile, 1) if last else _inproj(hc, w_in, l, 0, n_tiles)
        u_lane_tile = u_tile * (tn // LANES)
        ys_x, ys_c = _ssm_scan(px, pc_, u_lane_tile, 0 if last else u_lane_tile, w_ssm, ssm_p, l, bn,
                               ctx_out=not last)

        yf_x = _fourier_fast(px, f_tile, t, bn) if fast_dft else _fourier(px, f_tile, t, bn, tabs_x)
        mx = _branches(px, ys_x, yf_x, conv_w, ssm_d, wa_b, wb_b, fw_b, pa_b, pb_b, pc_b, l, GRID_W)
        new_x2, hx = _outproj(mx, x2, mods_r, g_post, w_out, l, t, None, g_pre_next)
        if not last:
            yf_c = _fourier(pc_, f_tile, lc, bn, tabs_c)
            mc = _branches(pc_, ys_c, yf_c, conv_w, ssm_d, wa_b, wb_b, fw_b, pa_b, pb_b, pc_b, l, lc)
            c2, hc = _outproj(mc, c2, mods_r, g_post, w_out, l, lc, ctx_row, g_pre_next)
        x2 = new_x2
    return x2.reshape(bn, t, d)
```

```python
import functools
import math

import jax
import jax.numpy as jnp
from jax import lax
from jax.experimental import pallas as pl
from jax.experimental.pallas import tpu as pltpu

F32 = jnp.float32
BF16 = jnp.bfloat16

GRID_W = 64
FFT_GROUPS = 4
N_BRANCH = 3
RMS_EPS = 1e-6
LANES = 128
SSM_CHUNK = 8
SSM_SEGS = 8
DFT_ROWS = 64
DFT_N1, DFT_N2 = 256, 16
VMEM_LIMIT = 56 * 1024 * 1024
COL_TILE = 1024
N_BRANCH_TILES = 8
U_TILE = 4
TM_INPROJ, TM_PRENORM, TM_BRANCH, TM_OUT, TM_DFT = 2048, 1024, 256, 512, 256
TM_INPROJ_DOT = 1024


def _cparams(*sem):
    return pltpu.CompilerParams(dimension_semantics=sem, vmem_limit_bytes=VMEM_LIMIT)


def _sigmoid(v):
    return 0.5 * jnp.tanh(0.5 * v) + 0.5


def _silu(v):
    return v * _sigmoid(v)


def _gelu_tanh(v):
    return v * (0.5 * (1.0 + jnp.tanh(math.sqrt(2.0 / math.pi) * (v + 0.044715 * (v * v * v)))))


def _cmul(ar, ai, br, bi):
    return ar * br - ai * bi, ar * bi + ai * br


def _cmul_add(ar, ai, zr, zi, sr, si):
    return ar * zr - ai * zi + sr, ar * zi + ai * zr + si


def _ada_kernel(c_ref, w_ref, b_ref, o_ref):
    c = c_ref[...]
    s = _silu(c).astype(BF16)
    o_ref[0] = jnp.dot(s, w_ref[0].astype(BF16), preferred_element_type=F32) + b_ref[0]


def _ada_mods(cc, w_ada, b_ada):
    depth, d, n3 = w_ada.shape
    rows = cc.shape[0]
    tn = 2 * COL_TILE if n3 % (2 * COL_TILE) == 0 else COL_TILE
    assert n3 % tn == 0
    return pl.pallas_call(
        _ada_kernel,
        grid=(depth, n3 // tn),
        in_specs=[
            pl.BlockSpec((rows, d), lambda l, j: (0, 0)),
            pl.BlockSpec((1, d, tn), lambda l, j: (l, 0, j)),
            pl.BlockSpec((1, 1, tn), lambda l, j: (l, 0, j)),
        ],
        out_specs=pl.BlockSpec((1, rows, tn), lambda l, j: (l, 0, j)),
        out_shape=jax.ShapeDtypeStruct((depth, rows, n3), F32),
        compiler_params=_cparams("arbitrary", "arbitrary"),
        name="ada_mods",
    )(cc, w_ada, b_ada.reshape(depth, 1, n3))


def _modnorm(x, gs, sh):
    ms = jnp.mean(x * x, axis=-1, keepdims=True)
    return x * lax.rsqrt(ms + RMS_EPS) * gs + sh


def _prenorm_kernel(x_ref, sc_ref, sh_ref, g_ref, h_ref):
    rb = 16
    gs = g_ref[0] * (1.0 + sc_ref[0])
    sh = sh_ref[0]

    def body(r, carry):
        rows = pl.ds(pl.multiple_of(r * rb, rb), rb)
        h_ref[rows, :] = _modnorm(x_ref[rows, :], gs, sh).astype(BF16)
        return carry

    lax.fori_loop(0, x_ref.shape[0] // rb, body, 0, unroll=4)


MOD_SHIFT, MOD_SCALE, MOD_GATE = 0, 1, 2
MOD_ROWS = 8


def _mod_spec(d, layer, which, tiles_per_batch, ctx_row):
    def index(i):
        row = ctx_row if ctx_row is not None else i // tiles_per_batch
        return ((layer * MOD_ROWS + row) * 3 + which, 0, 0)
    return pl.BlockSpec((1, 1, d), index)


def _prenorm(x2, mods_r, g_pre, layer, rows_per_batch, ctx_row=None):
    n, d = x2.shape
    tm = min(TM_PRENORM, rows_per_batch)
    tpb = rows_per_batch // tm
    return pl.pallas_call(
        _prenorm_kernel,
        grid=(n // tm,),
        in_specs=[
            pl.BlockSpec((tm, d), lambda i: (i, 0)),
            _mod_spec(d, layer, MOD_SCALE, tpb, ctx_row),
            _mod_spec(d, layer, MOD_SHIFT, tpb, ctx_row),
            pl.BlockSpec((1, 1, d), lambda i: (layer, 0, 0)),
        ],
        out_specs=pl.BlockSpec((tm, d), lambda i: (i, 0)),
        out_shape=jax.ShapeDtypeStruct((n, d), BF16),
        compiler_params=_cparams("arbitrary"),
        name="prenorm",
    )(x2, mods_r, mods_r, g_pre.reshape(g_pre.shape[0], 1, d))


def _inproj_kernel(h_ref, w_ref, p_ref, wb):
    @pl.when(pl.program_id(1) == 0)
    def _():
        wb[...] = w_ref[0].astype(BF16)

    sub = min(TM_INPROJ_DOT, h_ref.shape[0])
    for r in range(h_ref.shape[0] // sub):
        rows = slice(r * sub, (r + 1) * sub)
        p_ref[rows, :] = jnp.dot(h_ref[rows, :], wb[...], preferred_element_type=F32).astype(BF16)


def _inproj(h, w_in, layer, tile0, ntiles):
    n, d = h.shape
    tn = COL_TILE
    tm = min(TM_INPROJ, n)
    return pl.pallas_call(
        _inproj_kernel,
        grid=(ntiles, n // tm),
        in_specs=[
            pl.BlockSpec((tm, d), lambda j, i: (i, 0)),
            pl.BlockSpec((1, d, tn), lambda j, i: (layer, 0, tile0 + j)),
        ],
        out_specs=pl.BlockSpec((tm, tn), lambda j, i: (i, j)),
        out_shape=jax.ShapeDtypeStruct((n, ntiles * tn), BF16),
        scratch_shapes=[pltpu.VMEM((d, tn), BF16)],
        compiler_params=_cparams("arbitrary", "arbitrary"),
        name="inproj",
    )(h, w_in)


def _ssm_params(lam_re, lam_im, log_dt, b_re, b_im, c_re, c_im):
    nl, _, G, P = lam_re.shape
    H = b_re.shape[-1]
    gpt = LANES // H
    O = G // gpt
    lam_re = lam_re.astype(F32)
    lam_im = lam_im.astype(F32)
    dt = jnp.exp(log_dt.astype(F32))[..., None]
    lr = lam_re * dt
    li = lam_im * dt
    mag = jnp.exp(lr)
    a_re = mag * jnp.cos(li)
    a_im = mag * jnp.sin(li)
    n_re = a_re - 1.0
    n_im = a_im
    den = lam_re * lam_re + lam_im * lam_im
    q_re = (n_re * lam_re + n_im * lam_im) / den
    q_im = (n_im * lam_re - n_re * lam_im) / den
    bb_re = q_re[..., None] * b_re - q_im[..., None] * b_im
    bb_im = q_re[..., None] * b_im + q_im[..., None] * b_re

    bbs = jnp.stack([bb_re, bb_im], axis=2).reshape(nl, 2, 2, O, gpt, P, H)
    bc = jnp.transpose(bbs, (0, 3, 1, 2, 6, 4, 5)).reshape(nl, O, 4, H, gpt * P)
    cs = jnp.stack([c_re.astype(F32), c_im.astype(F32)], axis=2).reshape(nl, 2, 2, O, gpt, H, P)
    cct = jnp.transpose(cs, (0, 3, 1, 2, 4, 6, 5)).reshape(nl, O, 4, gpt * P, 1, H)
    cct = jnp.broadcast_to(cct, (nl, O, 4, gpt * P, gpt, H)).reshape(nl, O, 4, gpt * P, LANES)
    a4 = jnp.stack([a_re, a_im], axis=2).reshape(nl, 2, 2, O, gpt * P)
    a4 = jnp.transpose(a4, (0, 3, 1, 2, 4)).reshape(nl, O, 4, gpt * P)
    arow = jnp.concatenate([a4, jnp.zeros_like(a4)], axis=2)
    acol = jnp.broadcast_to(a4[..., None], (nl, O, 4, gpt * P, LANES))

    return bc, cct, arow, acol


def _block_transpose8(v):
    bw = LANES // 8
    blk = lax.broadcasted_iota(jnp.int32, v[0].shape, 1) // bw
    for dist in (4, 2, 1):
        upper = (blk & dist) != 0
        nxt = list(v)
        for a in range(8):
            if a & dist:
                continue
            b = a + dist
            nxt[a] = jnp.where(upper, pltpu.roll(v[b], dist * bw, 1), v[a])
            nxt[b] = jnp.where(upper, v[b], pltpu.roll(v[a], LANES - dist * bw, 1))
        v = nxt
    return v


def _ssm_build_operators(bc_ref, cct_ref, arow_ref, acol_ref, ws, ws_lo, w3):
    L = SSM_CHUNK
    hh = bc_ref.shape[2]
    sw = bc_ref.shape[3]
    pp = sw // (LANES // hh)
    npair = sw // LANES
    assert L * hh == LANES and 2 * pp == LANES
    lane = lax.broadcasted_iota(jnp.int32, (LANES, LANES), 1)
    row = lax.broadcasted_iota(jnp.int32, (LANES, LANES), 0)
    lane_g0, row_g0 = lane < pp, row < pp
    lane_blk = lane // hh
    zeros = jnp.zeros((LANES, LANES), F32)
    zeros_b = jnp.zeros((LANES, LANES), BF16)

    def powers(re, im, n):
        out = [(jnp.ones_like(re), jnp.zeros_like(re))]
        for _ in range(n):
            out.append(_cmul(out[-1][0], out[-1][1], re, im))
        return out

    def split(v):
        hi = v.astype(BF16)
        return hi, (v - hi.astype(F32)).astype(BF16)

    def shift_rows(t, blocks):
        n = abs(blocks) * hh
        if n == 0:
            return t
        pad = jnp.zeros((n, LANES), F32)
        return jnp.concatenate([pad, t[:LANES - n]] if blocks > 0 else [t[n:], pad], axis=0)

    for q in range(npair):
        ql = slice(q * LANES, (q + 1) * LANES)
        taps = []
        for d in range(2):
            b_re = jnp.concatenate([bc_ref[0, 2 * d, :, ql]] * L, axis=0)
            b_im = jnp.concatenate([bc_ref[0, 2 * d + 1, :, ql]] * L, axis=0)
            prow = powers(arow_ref[0, 2 * d:2 * d + 1, ql], arow_ref[0, 2 * d + 1:2 * d + 2, ql], L - 1)
            exps = [L - 1 - j if d == 0 else j for j in range(L)]
            a_re = jnp.concatenate([jnp.broadcast_to(prow[e][0], (hh, LANES)) for e in exps], axis=0)
            a_im = jnp.concatenate([jnp.broadcast_to(prow[e][1], (hh, LANES)) for e in exps], axis=0)
            for part, v in zip((2 * d, 2 * d + 1), _cmul(b_re, b_im, a_re, a_im)):
                cols = slice(part * LANES, (part + 1) * LANES)
                for dst, piece in zip((ws, ws_lo), split(v)):
                    dst[q, 0:LANES, cols] = jnp.where(lane_g0, piece, zeros_b)
                    dst[q, LANES:2 * LANES, cols] = jnp.where(lane_g0, zeros_b, piece)
            c_re = cct_ref[0, 2 * d, ql, :]
            c_im = cct_ref[0, 2 * d + 1, ql, :]
            (cr_hi, cr_lo), (ci_hi, ci_lo) = split(c_re), split(-c_im)
            c_hi = jnp.concatenate([cr_hi, ci_hi], axis=0)
            c_lo = jnp.concatenate([cr_lo, ci_lo], axis=0)
            dcols = slice(2 * d * LANES, (2 * d + 2) * LANES)
            taps.append(jnp.dot(jnp.concatenate([ws[q, :, dcols], ws[q, :, dcols], ws_lo[q, :, dcols]], axis=1),
                                jnp.concatenate([c_hi, c_lo, c_hi], axis=0), preferred_element_type=F32))
            pcol = powers(acol_ref[0, 2 * d, ql, :], acol_ref[0, 2 * d + 1, ql, :], L)
            ap_re, ap_im = zeros, zeros
            for jo in range(L):
                e = jo + 1 if d == 0 else L - jo
                ap_re = jnp.where(lane_blk == jo, pcol[e][0], ap_re)
                ap_im = jnp.where(lane_blk == jo, pcol[e][1], ap_im)
            ca_re, ca_im = _cmul(c_re, c_im, ap_re, ap_im)
            for part, v in ((2 * d, ca_re), (2 * d + 1, -ca_im)):
                r0 = 2 * LANES + part * LANES
                w3[q, r0:r0 + LANES, 0:LANES] = jnp.where(row_g0, v, zeros).astype(BF16)
                w3[q, r0:r0 + LANES, LANES:2 * LANES] = jnp.where(row_g0, zeros, v).astype(BF16)
        for g2 in range(2):
            tf = taps[0][g2 * LANES:(g2 + 1) * LANES]
            tb = taps[1][g2 * LANES:(g2 + 1) * LANES]
            m = zeros
            for jo in range(L):
                m = jnp.where(lane_blk == jo, shift_rows(tf, jo - (L - 1)) + shift_rows(tb, jo), m)
            rows = slice(g2 * LANES, (g2 + 1) * LANES)
            w3[q, rows, g2 * LANES:(g2 + 1) * LANES] = m.astype(BF16)
            w3[q, rows, (1 - g2) * LANES:(2 - g2) * LANES] = zeros_b


def _ssm_build_powers(arow_ref, ap, pw, seg_len):
    L = SSM_CHUNK
    sw = arow_ref.shape[2]
    assert L & (L - 1) == 0 and seg_len & (seg_len - 1) == 0 and seg_len >= 8
    for d in range(2):
        cr, ci = slice(2 * d * sw, (2 * d + 1) * sw), slice((2 * d + 1) * sw, (2 * d + 2) * sw)
        re, im = arow_ref[0, 2 * d:2 * d + 1, :], arow_ref[0, 2 * d + 1:2 * d + 2, :]
        for _ in range(L.bit_length() - 1):
            re, im = _cmul(re, im, re, im)
        ap[0:1, cr] = re
        ap[0:1, ci] = im
        p = [(jnp.ones_like(re), jnp.zeros_like(re))]
        for _ in range(7):
            p.append(_cmul(p[-1][0], p[-1][1], re, im))
        order = list(range(8)) if d == 0 else list(range(7, -1, -1))
        first = slice(0, 8) if d == 0 else slice(seg_len - 8, seg_len)
        pw[first, cr] = jnp.concatenate([p[r][0] for r in order], axis=0)
        pw[first, ci] = jnp.concatenate([p[r][1] for r in order], axis=0)
        sq = _cmul(p[4][0], p[4][1], p[4][0], p[4][1])
        m = 8
        while m < seg_len:
            src = slice(0, m) if d == 0 else slice(seg_len - m, seg_len)
            dst = slice(m, 2 * m) if d == 0 else slice(seg_len - 2 * m, seg_len - m)
            xr, xi = _cmul(pw[src, cr], pw[src, ci], sq[0], sq[1])
            pw[dst, cr] = xr
            pw[dst, ci] = xi
            sq = _cmul(sq[0], sq[1], sq[0], sq[1])
            m *= 2
        ap[1:2, cr] = sq[0]
        ap[1:2, ci] = sq[1]


def _seg_pitch(seg_len):
    tiles = seg_len // 8 + 1
    return 8 * (tiles + 1 - tiles % 2)


def _ssm_kernel(*refs, nc, nx, seg_len, batch, ctx_out):
    if ctx_out:
        (ux_ref, uc_ref, bc_ref, cct_ref, arow_ref, acol_ref,
         yx_ref, yc_ref, ws, ws_lo, w3, lhs, st, uxf, ucf, ap, pw) = refs
    else:
        (ux_ref, uc_ref, bc_ref, cct_ref, arow_ref, acol_ref,
         yx_ref, ws, ws_lo, w3, lhs, st, uxf, ucf, ap, pw) = refs
        yc_ref = None
    L = SSM_CHUNK
    nlt = st.shape[0] // 4
    sw = nlt * LANES
    xw = 2 * LANES
    FR, FI, BR, BI = range(4)
    nseg = SSM_SEGS
    pitch = _seg_pitch(seg_len)
    t, lc = nx * L, nc * L

    _ssm_build_operators(bc_ref, cct_ref, arow_ref, acol_ref, ws, ws_lo, w3)
    _ssm_build_powers(arow_ref, ap, pw, seg_len)

    def ld(part, rows):
        return jnp.concatenate([st[part * nlt + q, rows, :] for q in range(nlt)], axis=1)

    def sto(part, rows, val):
        for q in range(nlt):
            st[part * nlt + q, rows, :] = val[:, q * LANES:(q + 1) * LANES]

    def part_cols(part):
        return slice(part * sw, (part + 1) * sw)

    def one_sequence(b, carry):
        ucf[...] = uc_ref[pl.ds(pl.multiple_of(b * lc, lc), lc), :].astype(F32)
        uxf[...] = ux_ref[pl.ds(pl.multiple_of(b * t, t), t), :].astype(F32)
        by_pos = [jnp.concatenate([ucf[pl.ds(j, nc, stride=L), :], uxf[pl.ds(j, nx, stride=L), :]], axis=0)
                  for j in range(L)]
        for g, tile_g in enumerate(_block_transpose8(by_pos)):
            lhs[g // 2, :, (g % 2) * LANES:(g % 2 + 1) * LANES] = tile_g.astype(BF16)

        for q in range(nlt):
            s = jnp.dot(lhs[q, :, 0:xw], ws[q], preferred_element_type=F32)
            for part in range(4):
                k = part * nlt + q
                st[k, 0:nc, :] = s[0:nc, part * LANES:(part + 1) * LANES]
                for g in range(nseg):
                    st[k, nc + g * pitch:nc + g * pitch + seg_len, :] = (
                        s[nc + g * seg_len:nc + (g + 1) * seg_len, part * LANES:(part + 1) * LANES])

        a_l = ap[0:1, :]
        a_seg = ap[1:2, :]
        al = [a_l[:, part_cols(p)] for p in range(4)]
        aseg = [a_seg[:, part_cols(p)] for p in range(4)]

        zero = jnp.zeros((1, sw), F32)

        def ctx_step(pr, pi):
            def step(i, carry):
                er, ei = carry
                row = pl.ds(i, 1)
                sr, si = ld(pr, row), ld(pi, row)
                sto(pr, row, er)
                sto(pi, row, ei)
                return _cmul_add(al[pr], al[pi], er, ei, sr, si)
            return step

        h0f = lax.fori_loop(0, nc, ctx_step(FR, FI), (zero, zero))
        bwd_step = ctx_step(BR, BI)
        h0b = lax.fori_loop(0, nc, lambda k, carry: bwd_step(nc - 1 - k, carry), (zero, zero))
        for part in range(4):
            for q in range(nlt):
                lhs[q, 0:nc, xw + part * LANES:xw + (part + 1) * LANES] = st[part * nlt + q, 0:nc, :].astype(BF16)

        zseg = jnp.zeros((nseg, sw), F32)

        def seg_pass(pr, pi, order):
            zr, zi = zseg, zseg
            for i in order:
                rows = pl.ds(nc + i, nseg, stride=pitch)
                sr, si = ld(pr, rows), ld(pi, rows)
                sto(pr, rows, zr)
                sto(pi, rows, zi)
                zr, zi = _cmul_add(al[pr], al[pi], zr, zi, sr, si)
            return zr, zi

        zfr, zfi = seg_pass(FR, FI, range(seg_len))
        zbr, zbi = seg_pass(BR, BI, range(seg_len - 1, -1, -1))

        ef = [h0f]
        for s in range(nseg - 1):
            ef.append(_cmul_add(aseg[FR], aseg[FI], ef[s][0], ef[s][1], zfr[s:s + 1], zfi[s:s + 1]))
        eb = [None] * nseg
        eb[nseg - 1] = h0b
        for s in range(nseg - 1, 0, -1):
            eb[s - 1] = _cmul_add(aseg[BR], aseg[BI], eb[s][0], eb[s][1], zbr[s:s + 1], zbi[s:s + 1])
        for s in range(nseg):
            srows = slice(nc + s * pitch, nc + s * pitch + seg_len)
            rows = slice(nc + s * seg_len, nc + (s + 1) * seg_len)
            for (pr, pi, e) in ((FR, FI, ef[s]), (BR, BI, eb[s])):
                xr, xi = _cmul_add(pw[:, part_cols(pr)], pw[:, part_cols(pi)], e[0], e[1],
                                   ld(pr, srows), ld(pi, srows))
                for q in range(nlt):
                    ql = slice(q * LANES, (q + 1) * LANES)
                    lhs[q, rows, xw + pr * LANES:xw + (pr + 1) * LANES] = xr[:, ql].astype(BF16)
                    lhs[q, rows, xw + pi * LANES:xw + (pi + 1) * LANES] = xi[:, ql].astype(BF16)

        by_group = []
        for q in range(nlt):
            y = jnp.dot(lhs[q], w3[q], preferred_element_type=F32)
            by_group += [y[:, 0:LANES], y[:, LANES:2 * LANES]]
        for j, tile_j in enumerate(_block_transpose8(by_group)):
            if ctx_out:
                yc_ref[pl.ds(b * lc + j, nc, stride=L), :] = tile_j[0:nc]
            yx_ref[pl.ds(b * t + j, nx, stride=L), :] = tile_j[nc:nc + nx]
        return carry

    lax.fori_loop(0, batch, one_sequence, 0)


def _ssm_scan(px, pc, col_x, col_c, w, params, layer, batch, ctx_out):
    bc, cct, arow, acol = params
    L = SSM_CHUNK
    t = px.shape[0] // batch
    lc = pc.shape[0] // batch
    nx, nc = t // L, lc // L
    seg_len = nx // SSM_SEGS
    o = w // LANES
    sw = bc.shape[-1]
    sdim = 4 * sw
    npair = sw // LANES
    out_shape = [jax.ShapeDtypeStruct((batch * t, w), F32)]
    out_specs = [pl.BlockSpec((batch * t, LANES), lambda oi: (0, oi))]
    if ctx_out:
        out_shape.append(jax.ShapeDtypeStruct((batch * lc, w), F32))
        out_specs.append(pl.BlockSpec((batch * lc, LANES), lambda oi: (0, oi)))
    tile = lambda a: pl.BlockSpec((None, 1) + a.shape[2:], lambda oi: (layer, oi) + (0,) * (a.ndim - 2))
    res = pl.pallas_call(
        functools.partial(_ssm_kernel, nc=nc, nx=nx, seg_len=seg_len, batch=batch, ctx_out=ctx_out),
        grid=(o,),
        in_specs=[
            pl.BlockSpec((batch * t, LANES), lambda oi: (0, col_x + oi)),
            pl.BlockSpec((batch * lc, LANES), lambda oi: (0, col_c + oi)),
            tile(bc), tile(cct), tile(arow), tile(acol),
        ],
        out_specs=out_specs,
        out_shape=out_shape,
        scratch_shapes=[
            pltpu.VMEM((npair, 2 * LANES, 4 * LANES), BF16),
            pltpu.VMEM((npair, 2 * LANES, 4 * LANES), BF16),
            pltpu.VMEM((npair, 6 * LANES, 2 * LANES), BF16),
            pltpu.VMEM((npair, nc + nx, 6 * LANES), BF16),
            pltpu.VMEM((sdim // LANES, nc + SSM_SEGS * _seg_pitch(seg_len), LANES), F32),
            pltpu.VMEM((t, LANES), F32),
            pltpu.VMEM((lc, LANES), F32),
            pltpu.VMEM((8, sdim), F32),
            pltpu.VMEM((seg_len, sdim), F32),
        ],
        compiler_params=_cparams("arbitrary"),
        name="ssm_scan",
    )(px, pc, bc, cct, arow, acol)
    return (res[0], res[1]) if ctx_out else (res[0], None)


def _dft_table_kernel(e1_ref, e2_ref, o_ref, *, t):
    e1c, e1s = e1_ref[0, :, 0:t], e1_ref[0, :, t:2 * t]
    e2c, e2s = e2_ref[:, 0:t], e2_ref[:, t:2 * t]
    c, s = _cmul(e2c, e2s, e1c, e1s)
    o_ref[:, 0:t] = c.astype(BF16)
    o_ref[:, t:2 * t] = (-s).astype(BF16)


def _dft_tables(t, gw):
    def cis(rows_mult, nrows):
        k = jnp.arange(nrows, dtype=jnp.int32)[:, None] * rows_mult
        n = jnp.arange(t, dtype=jnp.int32)[None, :]
        ang = ((k * n) % t).astype(F32) * (2.0 * math.pi / t)
        return jnp.concatenate([jnp.cos(ang), jnp.sin(ang)], axis=1)
    r = DFT_ROWS
    e1 = cis(r, t // r).reshape(t // r, 1, 2 * t)
    e2 = cis(1, r)
    tab_t = pl.pallas_call(
        functools.partial(_dft_table_kernel, t=t),
        grid=(t // r,),
        in_specs=[
            pl.BlockSpec((1, 1, 2 * t), lambda i: (i, 0, 0)),
            pl.BlockSpec((r, 2 * t), lambda i: (0, 0)),
        ],
        out_specs=pl.BlockSpec((r, 2 * t), lambda i: (i, 0)),
        out_shape=jax.ShapeDtypeStruct((t, 2 * t), BF16),
        compiler_params=_cparams("arbitrary"),
        name="dft_table",
    )(e1, e2)
    kc = jnp.arange(gw, dtype=jnp.int32)
    ang = ((kc[:, None] * kc[None, :]) % gw).astype(F32) * (2.0 * math.pi / gw)
    tab_c = jnp.concatenate([jnp.cos(ang), jnp.sin(ang)], axis=1).astype(BF16)
    return tab_t, tab_c


def _fft_kernel(f_ref, cs_ref, tab_ref, o_ref, data, *, t, gw, groups, scale):
    i = pl.program_id(1)

    @pl.when(i == 0)
    def _():
        rb = min(512, t)
        for r in range(t // rb):
            for g in range(groups):
                fg = f_ref[r * rb:(r + 1) * rb, g * gw:(g + 1) * gw]
                z = jnp.dot(fg, cs_ref[...], preferred_element_type=F32)
                data[r * rb:(r + 1) * rb, g * gw:(g + 1) * gw] = z[:, :gw].astype(BF16)
                data[t + r * rb:t + (r + 1) * rb, g * gw:(g + 1) * gw] = z[:, gw:].astype(BF16)

    y = jnp.dot(tab_ref[...], data[...], preferred_element_type=F32)
    o_ref[...] = (y * scale).astype(BF16)


def _fourier(p, col_tile, t, batch, tabs):
    tab_t, tab_c = tabs
    wf = COL_TILE
    gw = wf // FFT_GROUPS
    tm = min(TM_DFT, t)
    return pl.pallas_call(
        functools.partial(_fft_kernel, t=t, gw=gw, groups=FFT_GROUPS, scale=1.0 / math.sqrt(t * gw)),
        grid=(batch, t // tm),
        in_specs=[
            pl.BlockSpec((t, wf), lambda b, i: (b, col_tile)),
            pl.BlockSpec((gw, 2 * gw), lambda b, i: (0, 0)),
            pl.BlockSpec((tm, 2 * t), lambda b, i: (i, 0)),
        ],
        out_specs=pl.BlockSpec((tm, wf), lambda b, i: (b * (t // tm) + i, 0)),
        out_shape=jax.ShapeDtypeStruct((batch * t, wf), BF16),
        scratch_shapes=[pltpu.VMEM((2 * t, wf), BF16)],
        compiler_params=_cparams("arbitrary", "arbitrary"),
        name="fourier",
    )(p, tab_c, tab_t)


def _cmul_const(xr, xi, c, s):
    def scaled(v, k):
        if abs(k) < 1e-12:
            return None
        return v if abs(k - 1.0) < 1e-12 else -v if abs(k + 1.0) < 1e-12 else v * k

    def add(a, b):
        return b if a is None else a if b is None else a + b

    return add(scaled(xr, c), scaled(xi, -s)), add(scaled(xi, c), scaled(xr, s))


def _dft4(y):
    (ar, ai), (br, bi), (cr, ci), (dr, di) = y
    sr, si, tr, ti = ar + cr, ai + ci, ar - cr, ai - ci
    ur, ui, vr, vi = br + dr, bi + di, br - dr, bi - di
    return [(sr + ur, si + ui), (tr + vi, ti - vr), (sr - ur, si - ui), (tr - vi, ti + vr)]


def _dft16(x):
    out = [None] * 16
    p = [_dft4([x[4 * a + b] for a in range(4)]) for b in range(4)]
    for ka in range(4):
        q = []
        for b in range(4):
            ang = -2.0 * math.pi * ka * b / 16.0
            q.append(_cmul_const(p[b][ka][0], p[b][ka][1], math.cos(ang), math.sin(ang)))
        r = _dft4(q)
        for kb in range(4):
            out[ka + 4 * kb] = r[kb]
    return out


def _fft_fast_kernel(f_ref, csc_ref, tw_ref, cst_ref, o_ref, z, o_scr, *, n1, n2, gw, scale):
    slabs_per_dot = 4
    for q in range(n2 // slabs_per_dot):
        r0 = q * slabs_per_dot * n1
        zz = jnp.dot(f_ref[r0:r0 + slabs_per_dot * n1, :], csc_ref[...], preferred_element_type=F32)
        for s in range(slabs_per_dot):
            z[0, q * slabs_per_dot + s] = zz[s * n1:(s + 1) * n1, 0:gw]
            z[1, q * slabs_per_dot + s] = zz[s * n1:(s + 1) * n1, gw:2 * gw]

    def tile_body(r, carry):
        rows = pl.ds(pl.multiple_of(r * 8, 8), 8)
        for c in range(gw // LANES):
            cols = slice(c * LANES, (c + 1) * LANES)
            h = _dft16([(z[0, t, rows, cols], z[1, t, rows, cols]) for t in range(n2)])
            for k2 in range(n2):
                hr, hi = h[k2]
                if k2 > 0:
                    hr, hi = _cmul(hr, hi, tw_ref[0, k2, rows, :], tw_ref[1, k2, rows, :])
                z[0, k2, rows, cols] = hr
                z[1, k2, rows, cols] = hi
        return carry

    lax.fori_loop(0, n1 // 8, tile_body, 0)

    for k2 in range(n2):
        rhs = jnp.concatenate([z[0, k2], z[1, k2]], axis=0).astype(BF16)
        y = jnp.dot(cst_ref[...], rhs, preferred_element_type=F32) * scale
        for c in range(gw // LANES):
            o_scr[c, pl.ds(k2, n1, stride=n2), :] = y[:, c * LANES:(c + 1) * LANES]
    o_ref[...] = jnp.concatenate([o_scr[c] for c in range(gw // LANES)], axis=1).astype(BF16)


def _fourier_fast(p, col_tile, t, batch):
    n1, n2 = DFT_N1, DFT_N2
    assert t == n1 * n2
    wf = COL_TILE
    gw = wf // FFT_GROUPS
    ang = lambda a, b, n: ((a[:, None] * b[None, :]) % n).astype(F32) * (2.0 * math.pi / n)
    kc = jnp.arange(gw, dtype=jnp.int32)
    a_c = ang(kc, kc, gw)
    csc = jnp.concatenate([jnp.cos(a_c), -jnp.sin(a_c)], axis=1).astype(BF16)
    k1 = jnp.arange(n1, dtype=jnp.int32)
    a_t = ang(k1, k1, n1)
    cst = jnp.concatenate([jnp.cos(a_t), jnp.sin(a_t)], axis=1).astype(BF16)
    a_w = ang(jnp.arange(n2, dtype=jnp.int32), k1, t)
    tw = jnp.stack([jnp.cos(a_w), -jnp.sin(a_w)], axis=0)
    tw = jnp.broadcast_to(tw[..., None], (2, n2, n1, LANES))
    ngrp = wf // gw
    return pl.pallas_call(
        functools.partial(_fft_fast_kernel, n1=n1, n2=n2, gw=gw, scale=1.0 / math.sqrt(t * gw)),
        grid=(batch, ngrp),
        in_specs=[
            pl.BlockSpec((t, gw), lambda b, g: (b, col_tile * ngrp + g)),
            pl.BlockSpec((gw, 2 * gw), lambda b, g: (0, 0)),
            pl.BlockSpec((2, n2, n1, LANES), lambda b, g: (0, 0, 0, 0), pipeline_mode=pl.Buffered(1)),
            pl.BlockSpec((n1, 2 * n1), lambda b, g: (0, 0)),
        ],
        out_specs=pl.BlockSpec((t, gw), lambda b, g: (b, g)),
        out_shape=jax.ShapeDtypeStruct((batch * t, wf), BF16),
        scratch_shapes=[
            pltpu.VMEM((2, n2, n1, gw), F32),
            pltpu.VMEM((gw // LANES, t, LANES), F32),
        ],
        compiler_params=_cparams("arbitrary", "arbitrary"),
        name="fourier_fast",
    )(p, csc, tw, cst)


def _branch_kernel(conv_ref, uz_ref, zc_ref, g01_ref, g2_ref, ys_ref, yf_ref, cw_ref, sd_ref, wa_ref, wb_ref,
                   fw_ref, pa_ref, pb_ref, pc_ref, m_ref, *, row_len):
    tm = conv_ref.shape[0]
    tn = zc_ref.shape[1]
    d = g2_ref.shape[1]
    xa_ref, ba_ref, ca_ref, za_ref = (conv_ref.at[:, k * tn:(k + 1) * tn] for k in range(4))
    u_ref, zb_ref = (uz_ref.at[:, k * tn:(k + 1) * tn] for k in range(2))
    g0_ref, g1_ref = (g01_ref.at[:, k * d:(k + 1) * d] for k in range(2))
    f = lambda r: r[...].astype(F32)
    v = f(ca_ref) * f(xa_ref)
    pos = lax.broadcasted_iota(jnp.int32, v.shape, 0) % row_len
    v_prev = jnp.where(pos == 0, 0.0, pltpu.roll(v, 1, 0))
    v_next = jnp.where(pos == row_len - 1, 0.0, pltpu.roll(v, tm - 1, 0))
    cw = cw_ref[0]
    conv = v_prev * cw[0:1, :] + v * cw[1:2, :] + v_next * cw[2:3, :]
    a = f(ba_ref) * conv * _silu(f(za_ref))
    ya = jnp.dot(a.astype(BF16), pa_ref[0], preferred_element_type=F32)
    acc = _sigmoid(f(g0_ref)) * ya
    y = _gelu_tanh(ys_ref[...] + sd_ref[0] * f(u_ref)).astype(BF16)
    glu = (jnp.dot(y, wa_ref[0], preferred_element_type=F32)
           * _sigmoid(jnp.dot(y, wb_ref[0], preferred_element_type=F32)) * _silu(f(zb_ref)))
    yb = jnp.dot(glu.astype(BF16), pb_ref[0], preferred_element_type=F32)
    acc = acc + _sigmoid(f(g1_ref)) * yb
    c = jnp.dot(yf_ref[...], fw_ref[0], preferred_element_type=F32) * _silu(f(zc_ref))
    yc = jnp.dot(c.astype(BF16), pc_ref[0], preferred_element_type=F32)
    acc = acc + _sigmoid(f(g2_ref)) * yc
    m_ref[...] = acc.astype(BF16)


def _branches(p, ys, yf, conv_w, ssm_d, wa, wb, fw, pa, pb, pc, layer, row_len):
    n = p.shape[0]
    wc = conv_w.shape[-1]
    d = pa.shape[-1]
    tm = TM_BRANCH
    tn = COL_TILE
    col = lambda k: pl.BlockSpec((tm, tn), lambda i: (i, k))
    const = lambda shape: pl.BlockSpec(shape, lambda i: (layer,) + (0,) * (len(shape) - 1),
                                       pipeline_mode=pl.Buffered(1))
    g_col0 = N_BRANCH_TILES * tn
    assert g_col0 % (2 * d) == 0 and U_TILE % 2 == 0
    return pl.pallas_call(
        functools.partial(_branch_kernel, row_len=row_len),
        grid=(n // tm,),
        in_specs=[
            pl.BlockSpec((tm, U_TILE * tn), lambda i: (i, 0)),
            pl.BlockSpec((tm, 2 * tn), lambda i: (i, U_TILE // 2)),
            col(7),
            pl.BlockSpec((tm, 2 * d), lambda i: (i, g_col0 // (2 * d))),
            pl.BlockSpec((tm, d), lambda i: (i, g_col0 // d + 2)),
            pl.BlockSpec((tm, tn), lambda i: (i, 0)),
            pl.BlockSpec((tm, tn), lambda i: (i, 0)),
            const((1, 3, wc)), const((1, 1, tn)),
            const((1, tn, tn)), const((1, tn, tn)), const((1, tn, tn)),
            const((1, wc, d)), const((1, tn, d)), const((1, tn, d)),
        ],
        out_specs=pl.BlockSpec((tm, d), lambda i: (i, 0)),
        out_shape=jax.ShapeDtypeStruct((n, d), BF16),
        compiler_params=_cparams("arbitrary"),
        name="branches",
    )(p, p, p, p, p, ys, yf, conv_w, ssm_d.reshape(ssm_d.shape[0], 1, -1),
      wa, wb, fw, pa, pb, pc)


def _out_kernel(*refs, with_next):
    if with_next:
        m_ref, x_ref, gt_ref, g_ref, w_ref, scn_ref, shn_ref, gn_ref, o_ref, h_ref, wb = refs
    else:
        m_ref, x_ref, gt_ref, g_ref, w_ref, o_ref, wb = refs

    @pl.when(pl.program_id(0) == 0)
    def _():
        wb[...] = w_ref[0].astype(BF16)

    o = jnp.dot(m_ref[...], wb[...], preferred_element_type=F32)
    ms = jnp.mean(o * o, axis=-1, keepdims=True)
    xn = x_ref[...] + o * lax.rsqrt(ms + RMS_EPS) * (g_ref[0] * gt_ref[0])
    o_ref[...] = xn
    if with_next:
        h_ref[...] = _modnorm(xn, gn_ref[0] * (1.0 + scn_ref[0]), shn_ref[0]).astype(BF16)


def _outproj(m, x2, mods_r, g_post, w_out, layer, rows_per_batch, ctx_row=None, g_pre_next=None):
    n, d = x2.shape
    tm = min(TM_OUT, rows_per_batch)
    tpb = rows_per_batch // tm
    in_specs = [
        pl.BlockSpec((tm, d), lambda i: (i, 0)),
        pl.BlockSpec((tm, d), lambda i: (i, 0)),
        _mod_spec(d, layer, MOD_GATE, tpb, ctx_row),
        pl.BlockSpec((1, 1, d), lambda i: (layer, 0, 0)),
        pl.BlockSpec((1, d, d), lambda i: (layer, 0, 0), pipeline_mode=pl.Buffered(1)),
    ]
    args = [m, x2, mods_r, g_post.reshape(g_post.shape[0], 1, d), w_out]
    out_specs = [pl.BlockSpec((tm, d), lambda i: (i, 0))]
    out_shape = [jax.ShapeDtypeStruct((n, d), F32)]
    if g_pre_next is not None:
        in_specs += [_mod_spec(d, layer + 1, MOD_SCALE, tpb, ctx_row), _mod_spec(d, layer + 1, MOD_SHIFT, tpb, ctx_row),
                     pl.BlockSpec((1, 1, d), lambda i: (layer + 1, 0, 0))]
        args += [mods_r, mods_r, g_pre_next.reshape(g_pre_next.shape[0], 1, d)]
        out_specs.append(pl.BlockSpec((tm, d), lambda i: (i, 0)))
        out_shape.append(jax.ShapeDtypeStruct((n, d), BF16))
    res = pl.pallas_call(
        functools.partial(_out_kernel, with_next=g_pre_next is not None),
        grid=(n // tm,),
        in_specs=in_specs,
        out_specs=out_specs,
        out_shape=out_shape,
        scratch_shapes=[pltpu.VMEM((d, d), BF16)],
        compiler_params=_cparams("arbitrary"),
        name="outproj",
    )(*args)
    return (res[0], res[1]) if g_pre_next is not None else (res[0], None)


def kernel(x, c, ctx, c_ctx, w_ada, b_ada, g_pre, g_post, w_in, conv_w, ssm_lam_re, ssm_lam_im, ssm_log_dt,
           ssm_b_re, ssm_b_im, ssm_c_re, ssm_c_im, ssm_d, glu_wa, glu_wb, fourier_w, proj_a, proj_b, proj_c,
           w_out):
    bn, t, d = x.shape
    lc = ctx.shape[1]
    depth = w_ada.shape[0]
    w_conv = conv_w.shape[-1]
    w_ssm = ssm_d.shape[-1]
    tn = COL_TILE
    assert w_conv == tn and w_ssm == tn and fourier_w.shape[-1] == tn and d % tn == 0
    assert bn + 1 <= MOD_ROWS and t % GRID_W == 0
    u_tile = 4 * w_conv // tn
    f_tile = u_tile + 2
    n_tiles = w_in.shape[-1] // tn
    assert u_tile == U_TILE and f_tile + 2 == N_BRANCH_TILES and (n_tiles - N_BRANCH_TILES) * tn == N_BRANCH * d

    cc = jnp.concatenate([c, c_ctx[None, :], jnp.zeros((MOD_ROWS - bn - 1, d), F32)], axis=0)
    mods = _ada_mods(cc, w_ada, b_ada)

    to_b = lambda w: w.astype(BF16)
    wa_b, wb_b, fw_b = to_b(glu_wa), to_b(glu_wb), to_b(fourier_w)
    pa_b, pb_b, pc_b = to_b(proj_a), to_b(proj_b), to_b(proj_c)
    fast_dft = t == DFT_N1 * DFT_N2
    tabs_x = None if fast_dft else _dft_tables(t, tn // FFT_GROUPS)
    tabs_c = _dft_tables(lc, tn // FFT_GROUPS)

    x2 = x.reshape(bn * t, d)
    c2 = ctx.reshape(bn * lc, d)
    mods_r = mods.reshape(depth * MOD_ROWS * 3, 1, d)
    ctx_row = bn
    ssm_p = _ssm_params(ssm_lam_re, ssm_lam_im, ssm_log_dt, ssm_b_re, ssm_b_im, ssm_c_re, ssm_c_im)
    hx = _prenorm(x2, mods_r, g_pre, 0, t)
    hc = _prenorm(c2, mods_r, g_pre, 0, lc, ctx_row)
    for l in range(depth):
        last = l == depth - 1
        g_pre_next = None if last else g_pre

        px = _inproj(hx, w_in, l, 0, n_tiles)
        pc_ = _inproj(hc, w_in, l, u_tile, 1) if last else _inproj(hc, w_in, l, 0, n_tiles)
        u_lane_tile = u_tile * (tn // LANES)
        ys_x, ys_c = _ssm_scan(px, pc_, u_lane_tile, 0 if last else u_lane_tile, w_ssm, ssm_p, l, bn,
                               ctx_out=not last)

        yf_x = _fourier_fast(px, f_tile, t, bn) if fast_dft else _fourier(px, f_tile, t, bn, tabs_x)
        mx = _branches(px, ys_x, yf_x, conv_w, ssm_d, wa_b, wb_b, fw_b, pa_b, pb_b, pc_b, l, GRID_W)
        new_x2, hx = _outproj(mx, x2, mods_r, g_post, w_out, l, t, None, g_pre_next)
        if not last:
            yf_c = _fourier(pc_, f_tile, lc, bn, tabs_c)
            mc = _branches(pc_, ys_c, yf_c, conv_w, ssm_d, wa_b, wb_b, fw_b, pa_b, pb_b, pc_b, l, lc)
            c2, hc = _outproj(mc, c2, mods_r, g_post, w_out, l, lc, ctx_row, g_pre_next)
        x2 = new_x2
    return x2.reshape(bn, t, d)
```

```python
import functools
import math

import jax
import jax.numpy as jnp
from jax import lax
from jax.experimental import pallas as pl
from jax.experimental.pallas import tpu as pltpu

F32 = jnp.float32
BF16 = jnp.bfloat16

GRID_W = 64
FFT_GROUPS = 4
N_BRANCH = 3
RMS_EPS = 1e-6
LANES = 128
SSM_CHUNK = 8
SSM_SEGS = 8
DFT_ROWS = 64
DFT_N1, DFT_N2 = 256, 16
VMEM_LIMIT = 56 * 1024 * 1024
COL_TILE = 1024
N_BRANCH_TILES = 8
U_TILE = 4
TM_INPROJ, TM_PRENORM, TM_BRANCH, TM_OUT, TM_DFT = 2048, 1024, 256, 512, 256
TM_INPROJ_DOT = 1024


def _cparams(*sem):
    return pltpu.CompilerParams(dimension_semantics=sem, vmem_limit_bytes=VMEM_LIMIT)


def _sigmoid(v):
    return 0.5 * jnp.tanh(0.5 * v) + 0.5


def _silu(v):
    return v * _sigmoid(v)


def _gelu_tanh(v):
    return v * (0.5 * (1.0 + jnp.tanh(math.sqrt(2.0 / math.pi) * (v + 0.044715 * (v * v * v)))))


def _cmul(ar, ai, br, bi):
    return ar * br - ai * bi, ar * bi + ai * br


def _cmul_add(ar, ai, zr, zi, sr, si):
    return ar * zr - ai * zi + sr, ar * zi + ai * zr + si


def _ada_kernel(c_ref, w_ref, b_ref, o_ref):
    c = c_ref[...]
    s = _silu(c).astype(BF16)
    o_ref[0] = jnp.dot(s, w_ref[0].astype(BF16), preferred_element_type=F32) + b_ref[0]


def _ada_mods(cc, w_ada, b_ada):
    depth, d, n3 = w_ada.shape
    rows = cc.shape[0]
    tn = 2 * COL_TILE if n3 % (2 * COL_TILE) == 0 else COL_TILE
    assert n3 % tn == 0
    return pl.pallas_call(
        _ada_kernel,
        grid=(depth, n3 // tn),
        in_specs=[
            pl.BlockSpec((rows, d), lambda l, j: (0, 0)),
            pl.BlockSpec((1, d, tn), lambda l, j: (l, 0, j)),
            pl.BlockSpec((1, 1, tn), lambda l, j: (l, 0, j)),
        ],
        out_specs=pl.BlockSpec((1, rows, tn), lambda l, j: (l, 0, j)),
        out_shape=jax.ShapeDtypeStruct((depth, rows, n3), F32),
        compiler_params=_cparams("arbitrary", "arbitrary"),
        name="ada_mods",
    )(cc, w_ada, b_ada.reshape(depth, 1, n3))


def _modnorm(x, gs, sh):
    ms = jnp.mean(x * x, axis=-1, keepdims=True)
    return x * lax.rsqrt(ms + RMS_EPS) * gs + sh


def _prenorm_kernel(x_ref, sc_ref, sh_ref, g_ref, h_ref):
    rb = 16
    gs = g_ref[0] * (1.0 + sc_ref[0])
    sh = sh_ref[0]

    def body(r, carry):
        rows = pl.ds(pl.multiple_of(r * rb, rb), rb)
        h_ref[rows, :] = _modnorm(x_ref[rows, :], gs, sh).astype(BF16)
        return carry

    lax.fori_loop(0, x_ref.shape[0] // rb, body, 0, unroll=4)


MOD_SHIFT, MOD_SCALE, MOD_GATE = 0, 1, 2
MOD_ROWS = 8


def _mod_spec(d, layer, which, tiles_per_batch, ctx_row):
    def index(i):
        row = ctx_row if ctx_row is not None else i // tiles_per_batch
        return ((layer * MOD_ROWS + row) * 3 + which, 0, 0)
    return pl.BlockSpec((1, 1, d), index)


def _prenorm(x2, mods_r, g_pre, layer, rows_per_batch, ctx_row=None):
    n, d = x2.shape
    tm = min(TM_PRENORM, rows_per_batch)
    tpb = rows_per_batch // tm
    return pl.pallas_call(
        _prenorm_kernel,
        grid=(n // tm,),
        in_specs=[
            pl.BlockSpec((tm, d), lambda i: (i, 0)),
            _mod_spec(d, layer, MOD_SCALE, tpb, ctx_row),
            _mod_spec(d, layer, MOD_SHIFT, tpb, ctx_row),
            pl.BlockSpec((1, 1, d), lambda i: (layer, 0, 0)),
        ],
        out_specs=pl.BlockSpec((tm, d), lambda i: (i, 0)),
        out_shape=jax.ShapeDtypeStruct((n, d), BF16),
        compiler_params=_cparams("arbitrary"),
        name="prenorm",
    )(x2, mods_r, mods_r, g_pre.reshape(g_pre.shape[0], 1, d))


def _inproj_kernel(h_ref, w_ref, p_ref, wb):
    @pl.when(pl.program_id(1) == 0)
    def _():
        wb[...] = w_ref[0].astype(BF16)

    sub = min(TM_INPROJ_DOT, h_ref.shape[0])
    for r in range(h_ref.shape[0] // sub):
        rows = slice(r * sub, (r + 1) * sub)
        p_ref[rows, :] = jnp.dot(h_ref[rows, :], wb[...], preferred_element_type=F32).astype(BF16)


def _inproj(h, w_in, layer, tile0, ntiles):
    n, d = h.shape
    tn = COL_TILE
    tm = min(TM_INPROJ, n)
    return pl.pallas_call(
        _inproj_kernel,
        grid=(ntiles, n // tm),
        in_specs=[
            pl.BlockSpec((tm, d), lambda j, i: (i, 0)),
            pl.BlockSpec((1, d, tn), lambda j, i: (layer, 0, tile0 + j)),
        ],
        out_specs=pl.BlockSpec((tm, tn), lambda j, i: (i, j)),
        out_shape=jax.ShapeDtypeStruct((n, ntiles * tn), BF16),
        scratch_shapes=[pltpu.VMEM((d, tn), BF16)],
        compiler_params=_cparams("arbitrary", "arbitrary"),
        name="inproj",
    )(h, w_in)


def _ssm_params(lam_re, lam_im, log_dt, b_re, b_im, c_re, c_im):
    nl, _, G, P = lam_re.shape
    H = b_re.shape[-1]
    gpt = LANES // H
    O = G // gpt
    lam_re = lam_re.astype(F32)
    lam_im = lam_im.astype(F32)
    dt = jnp.exp(log_dt.astype(F32))[..., None]
    lr = lam_re * dt
    li = lam_im * dt
    mag = jnp.exp(lr)
    a_re = mag * jnp.cos(li)
    a_im = mag * jnp.sin(li)
    n_re = a_re - 1.0
    n_im = a_im
    den = lam_re * lam_re + lam_im * lam_im
    q_re = (n_re * lam_re + n_im * lam_im) / den
    q_im = (n_im * lam_re - n_re * lam_im) / den
    bb_re = q_re[..., None] * b_re - q_im[..., None] * b_im
    bb_im = q_re[..., None] * b_im + q_im[..., None] * b_re

    bbs = jnp.stack([bb_re, bb_im], axis=2).reshape(nl, 2, 2, O, gpt, P, H)
    bc = jnp.transpose(bbs, (0, 3, 1, 2, 6, 4, 5)).reshape(nl, O, 4, H, gpt * P)
    cs = jnp.stack([c_re.astype(F32), c_im.astype(F32)], axis=2).reshape(nl, 2, 2, O, gpt, H, P)
    cc = jnp.transpose(cs, (0, 3, 1, 2, 5, 4, 6)).reshape(nl, O, 4, H, gpt * P)
    a4 = jnp.stack([a_re, a_im], axis=2).reshape(nl, 2, 2, O, gpt * P)
    a4 = jnp.transpose(a4, (0, 3, 1, 2, 4)).reshape(nl, O, 4, gpt * P)
    arow = jnp.concatenate([a4, jnp.zeros_like(a4)], axis=2)

    return bc, cc, arow


def _block_transpose8(v):
    bw = LANES // 8
    blk = lax.broadcasted_iota(jnp.int32, v[0].shape, 1) // bw
    for dist in (4, 2, 1):
        upper = (blk & dist) != 0
        nxt = list(v)
        for a in range(8):
            if a & dist:
                continue
            b = a + dist
            nxt[a] = jnp.where(upper, pltpu.roll(v[b], dist * bw, 1), v[a])
            nxt[b] = jnp.where(upper, v[b], pltpu.roll(v[a], LANES - dist * bw, 1))
        v = nxt
    return v


def _ssm_build_operators(bc_ref, cc_ref, arow_ref, ws, ws_lo, w3):
    L = SSM_CHUNK
    hh = bc_ref.shape[2]
    sw = bc_ref.shape[3]
    pp = sw // (LANES // hh)
    npair = sw // LANES
    assert L * hh == LANES and 2 * pp == LANES
    lane = lax.broadcasted_iota(jnp.int32, (LANES, LANES), 1)
    row = lax.broadcasted_iota(jnp.int32, (LANES, LANES), 0)
    lane_g0, row_g0 = lane < pp, row < pp
    lane_blk = lane // hh
    zeros = jnp.zeros((LANES, LANES), F32)
    zeros_b = jnp.zeros((LANES, LANES), BF16)

    def powers(re, im, n):
        out = [(jnp.ones_like(re), jnp.zeros_like(re))]
        for _ in range(n):
            out.append(_cmul(out[-1][0], out[-1][1], re, im))
        return out

    def split(v):
        hi = v.astype(BF16)
        return hi, (v - hi.astype(F32)).astype(BF16)

    def shift_rows(t, blocks):
        n = abs(blocks) * hh
        if n == 0:
            return t
        pad = jnp.zeros((n, LANES), F32)
        return jnp.concatenate([pad, t[:LANES - n]] if blocks > 0 else [t[n:], pad], axis=0)

    for q in range(npair):
        ql = slice(q * LANES, (q + 1) * LANES)
        taps = []
        for d in range(2):
            b_re = jnp.concatenate([bc_ref[0, 2 * d, :, ql]] * L, axis=0)
            b_im = jnp.concatenate([bc_ref[0, 2 * d + 1, :, ql]] * L, axis=0)
            prow = powers(arow_ref[0, 2 * d:2 * d + 1, ql], arow_ref[0, 2 * d + 1:2 * d + 2, ql], L)

            def power_rows(exps):
                return (jnp.concatenate([jnp.broadcast_to(prow[e][0], (hh, LANES)) for e in exps], axis=0),
                        jnp.concatenate([jnp.broadcast_to(prow[e][1], (hh, LANES)) for e in exps], axis=0))

            a_re, a_im = power_rows([L - 1 - j if d == 0 else j for j in range(L)])
            for part, v in zip((2 * d, 2 * d + 1), _cmul(b_re, b_im, a_re, a_im)):
                cols = slice(part * LANES, (part + 1) * LANES)
                for dst, piece in zip((ws, ws_lo), split(v)):
                    dst[q, 0:LANES, cols] = jnp.where(lane_g0, piece, zeros_b)
                    dst[q, LANES:2 * LANES, cols] = jnp.where(lane_g0, zeros_b, piece)
            ct_re = jnp.concatenate([cc_ref[0, 2 * d, :, ql]] * L, axis=0)
            ct_im = jnp.concatenate([cc_ref[0, 2 * d + 1, :, ql]] * L, axis=0)
            c_re, c_im = ct_re.T, ct_im.T
            (cr_hi, cr_lo), (ci_hi, ci_lo) = split(c_re), split(-c_im)
            c_hi = jnp.concatenate([cr_hi, ci_hi], axis=0)
            c_lo = jnp.concatenate([cr_lo, ci_lo], axis=0)
            dcols = slice(2 * d * LANES, (2 * d + 2) * LANES)
            taps.append(jnp.dot(jnp.concatenate([ws[q, :, dcols], ws[q, :, dcols], ws_lo[q, :, dcols]], axis=1),
                                jnp.concatenate([c_hi, c_lo, c_hi], axis=0), preferred_element_type=F32))
            cat_re, cat_im = _cmul(ct_re, ct_im, *power_rows([jo + 1 if d == 0 else L - jo for jo in range(L)]))
            for part, v in ((2 * d, cat_re.T), (2 * d + 1, -cat_im.T)):
                r0 = 2 * LANES + part * LANES
                w3[q, r0:r0 + LANES, 0:LANES] = jnp.where(row_g0, v, zeros).astype(BF16)
                w3[q, r0:r0 + LANES, LANES:2 * LANES] = jnp.where(row_g0, zeros, v).astype(BF16)
        for g2 in range(2):
            tf = taps[0][g2 * LANES:(g2 + 1) * LANES]
            tb = taps[1][g2 * LANES:(g2 + 1) * LANES]
            m = zeros
            for jo in range(L):
                m = jnp.where(lane_blk == jo, shift_rows(tf, jo - (L - 1)) + shift_rows(tb, jo), m)
            rows = slice(g2 * LANES, (g2 + 1) * LANES)
            w3[q, rows, g2 * LANES:(g2 + 1) * LANES] = m.astype(BF16)
            w3[q, rows, (1 - g2) * LANES:(2 - g2) * LANES] = zeros_b


def _ssm_build_powers(arow_ref, ap, pw, seg_len):
    L = SSM_CHUNK
    sw = arow_ref.shape[2]
    assert L & (L - 1) == 0 and seg_len & (seg_len - 1) == 0 and seg_len >= 8
    for d in range(2):
        cr, ci = slice(2 * d * sw, (2 * d + 1) * sw), slice((2 * d + 1) * sw, (2 * d + 2) * sw)
        re, im = arow_ref[0, 2 * d:2 * d + 1, :], arow_ref[0, 2 * d + 1:2 * d + 2, :]
        for _ in range(L.bit_length() - 1):
            re, im = _cmul(re, im, re, im)
        ap[0:1, cr] = re
        ap[0:1, ci] = im
        p = [(jnp.ones_like(re), jnp.zeros_like(re))]
        for _ in range(7):
            p.append(_cmul(p[-1][0], p[-1][1], re, im))
        order = list(range(8)) if d == 0 else list(range(7, -1, -1))
        first = slice(0, 8) if d == 0 else slice(seg_len - 8, seg_len)
        pw[first, cr] = jnp.concatenate([p[r][0] for r in order], axis=0)
        pw[first, ci] = jnp.concatenate([p[r][1] for r in order], axis=0)
        sq = _cmul(p[4][0], p[4][1], p[4][0], p[4][1])
        m = 8
        while m < seg_len:
            src = slice(0, m) if d == 0 else slice(seg_len - m, seg_len)
            dst = slice(m, 2 * m) if d == 0 else slice(seg_len - 2 * m, seg_len - m)
            xr, xi = _cmul(pw[src, cr], pw[src, ci], sq[0], sq[1])
            pw[dst, cr] = xr
            pw[dst, ci] = xi
            sq = _cmul(sq[0], sq[1], sq[0], sq[1])
            m *= 2
        ap[1:2, cr] = sq[0]
        ap[1:2, ci] = sq[1]


def _seg_pitch(seg_len):
    tiles = seg_len // 8 + 1
    return 8 * (tiles + 1 - tiles % 2)


def _ssm_kernel(*refs, nc, nx, seg_len, batch, ctx_out):
    if ctx_out:
        (ux_ref, uc_ref, bc_ref, cc_ref, arow_ref,
         yx_ref, yc_ref, ws, ws_lo, w3, lhs, st, uxf, ucf, ap, pw) = refs
    else:
        (ux_ref, uc_ref, bc_ref, cc_ref, arow_ref,
         yx_ref, ws, ws_lo, w3, lhs, st, uxf, ucf, ap, pw) = refs
        yc_ref = None
    L = SSM_CHUNK
    nlt = st.shape[0] // 4
    sw = nlt * LANES
    xw = 2 * LANES
    FR, FI, BR, BI = range(4)
    nseg = SSM_SEGS
    pitch = _seg_pitch(seg_len)
    t, lc = nx * L, nc * L

    _ssm_build_operators(bc_ref, cc_ref, arow_ref, ws, ws_lo, w3)
    _ssm_build_powers(arow_ref, ap, pw, seg_len)

    def ld(part, rows):
        return jnp.concatenate([st[part * nlt + q, rows, :] for q in range(nlt)], axis=1)

    def sto(part, rows, val):
        for q in range(nlt):
            st[part * nlt + q, rows, :] = val[:, q * LANES:(q + 1) * LANES]

    def part_cols(part):
        return slice(part * sw, (part + 1) * sw)

    def one_sequence(b, carry):
        ucf[...] = uc_ref[pl.ds(pl.multiple_of(b * lc, lc), lc), :].astype(F32)
        uxf[...] = ux_ref[pl.ds(pl.multiple_of(b * t, t), t), :].astype(F32)
        by_pos = [jnp.concatenate([ucf[pl.ds(j, nc, stride=L), :], uxf[pl.ds(j, nx, stride=L), :]], axis=0)
                  for j in range(L)]
        for g, tile_g in enumerate(_block_transpose8(by_pos)):
            lhs[g // 2, :, (g % 2) * LANES:(g % 2 + 1) * LANES] = tile_g.astype(BF16)

        for q in range(nlt):
            s = jnp.dot(lhs[q, :, 0:xw], ws[q], preferred_element_type=F32)
            for part in range(4):
                k = part * nlt + q
                st[k, 0:nc, :] = s[0:nc, part * LANES:(part + 1) * LANES]
                for g in range(nseg):
                    st[k, nc + g * pitch:nc + g * pitch + seg_len, :] = (
                        s[nc + g * seg_len:nc + (g + 1) * seg_len, part * LANES:(part + 1) * LANES])

        a_l = ap[0:1, :]
        a_seg = ap[1:2, :]
        al = [a_l[:, part_cols(p)] for p in range(4)]
        aseg = [a_seg[:, part_cols(p)] for p in range(4)]

        zero = jnp.zeros((1, sw), F32)

        def ctx_step(pr, pi):
            def step(i, carry):
                er, ei = carry
                row = pl.ds(i, 1)
                sr, si = ld(pr, row), ld(pi, row)
                sto(pr, row, er)
                sto(pi, row, ei)
                return _cmul_add(al[pr], al[pi], er, ei, sr, si)
            return step

        h0f = lax.fori_loop(0, nc, ctx_step(FR, FI), (zero, zero))
        bwd_step = ctx_step(BR, BI)
        h0b = lax.fori_loop(0, nc, lambda k, carry: bwd_step(nc - 1 - k, carry), (zero, zero))
        for part in range(4):
            for q in range(nlt):
                lhs[q, 0:nc, xw + part * LANES:xw + (part + 1) * LANES] = st[part * nlt + q, 0:nc, :].astype(BF16)

        zseg = jnp.zeros((nseg, sw), F32)

        def seg_pass(pr, pi, order):
            zr, zi = zseg, zseg
            for i in order:
                rows = pl.ds(nc + i, nseg, stride=pitch)
                sr, si = ld(pr, rows), ld(pi, rows)
                sto(pr, rows, zr)
                sto(pi, rows, zi)
                zr, zi = _cmul_add(al[pr], al[pi], zr, zi, sr, si)
            return zr, zi

        zfr, zfi = seg_pass(FR, FI, range(seg_len))
        zbr, zbi = seg_pass(BR, BI, range(seg_len - 1, -1, -1))

        ef = [h0f]
        for s in range(nseg - 1):
            ef.append(_cmul_add(aseg[FR], aseg[FI], ef[s][0], ef[s][1], zfr[s:s + 1], zfi[s:s + 1]))
        eb = [None] * nseg
        eb[nseg - 1] = h0b
        for s in range(nseg - 1, 0, -1):
            eb[s - 1] = _cmul_add(aseg[BR], aseg[BI], eb[s][0], eb[s][1], zbr[s:s + 1], zbi[s:s + 1])
        for s in range(nseg):
            srows = slice(nc + s * pitch, nc + s * pitch + seg_len)
            rows = slice(nc + s * seg_len, nc + (s + 1) * seg_len)
            for (pr, pi, e) in ((FR, FI, ef[s]), (BR, BI, eb[s])):
                xr, xi = _cmul_add(pw[:, part_cols(pr)], pw[:, part_cols(pi)], e[0], e[1],
                                   ld(pr, srows), ld(pi, srows))
                for q in range(nlt):
                    ql = slice(q * LANES, (q + 1) * LANES)
                    lhs[q, rows, xw + pr * LANES:xw + (pr + 1) * LANES] = xr[:, ql].astype(BF16)
                    lhs[q, rows, xw + pi * LANES:xw + (pi + 1) * LANES] = xi[:, ql].astype(BF16)

        by_group = []
        for q in range(nlt):
            y = jnp.dot(lhs[q], w3[q], preferred_element_type=F32)
            by_group += [y[:, 0:LANES], y[:, LANES:2 * LANES]]
        for j, tile_j in enumerate(_block_transpose8(by_group)):
            if ctx_out:
                yc_ref[pl.ds(b * lc + j, nc, stride=L), :] = tile_j[0:nc]
            yx_ref[pl.ds(b * t + j, nx, stride=L), :] = tile_j[nc:nc + nx]
        return carry

    lax.fori_loop(0, batch, one_sequence, 0)


def _ssm_scan(px, pc, col_x, col_c, w, params, layer, batch, ctx_out):
    bc, cc, arow = params
    L = SSM_CHUNK
    t = px.shape[0] // batch
    lc = pc.shape[0] // batch
    nx, nc = t // L, lc // L
    seg_len = nx // SSM_SEGS
    o = w // LANES
    sw = bc.shape[-1]
    sdim = 4 * sw
    npair = sw // LANES
    out_shape = [jax.ShapeDtypeStruct((batch * t, w), F32)]
    out_specs = [pl.BlockSpec((batch * t, LANES), lambda oi: (0, oi))]
    if ctx_out:
        out_shape.append(jax.ShapeDtypeStruct((batch * lc, w), F32))
        out_specs.append(pl.BlockSpec((batch * lc, LANES), lambda oi: (0, oi)))
    tile = lambda a: pl.BlockSpec((None, 1) + a.shape[2:], lambda oi: (layer, oi) + (0,) * (a.ndim - 2))
    res = pl.pallas_call(
        functools.partial(_ssm_kernel, nc=nc, nx=nx, seg_len=seg_len, batch=batch, ctx_out=ctx_out),
        grid=(o,),
        in_specs=[
            pl.BlockSpec((batch * t, LANES), lambda oi: (0, col_x + oi)),
            pl.BlockSpec((batch * lc, LANES), lambda oi: (0, col_c + oi)),
            tile(bc), tile(cc), tile(arow),
        ],
        out_specs=out_specs,
        out_shape=out_shape,
        scratch_shapes=[
            pltpu.VMEM((npair, 2 * LANES, 4 * LANES), BF16),
            pltpu.VMEM((npair, 2 * LANES, 4 * LANES), BF16),
            pltpu.VMEM((npair, 6 * LANES, 2 * LANES), BF16),
            pltpu.VMEM((npair, nc + nx, 6 * LANES), BF16),
            pltpu.VMEM((sdim // LANES, nc + SSM_SEGS * _seg_pitch(seg_len), LANES), F32),
            pltpu.VMEM((t, LANES), F32),
            pltpu.VMEM((lc, LANES), F32),
            pltpu.VMEM((8, sdim), F32),
            pltpu.VMEM((seg_len, sdim), F32),
        ],
        compiler_params=_cparams("arbitrary"),
        name="ssm_scan",
    )(px, pc, bc, cc, arow)
    return (res[0], res[1]) if ctx_out else (res[0], None)


def _dft_table_kernel(e1_ref, e2_ref, o_ref, *, t):
    e1c, e1s = e1_ref[0, :, 0:t], e1_ref[0, :, t:2 * t]
    e2c, e2s = e2_ref[:, 0:t], e2_ref[:, t:2 * t]
    c, s = _cmul(e2c, e2s, e1c, e1s)
    o_ref[:, 0:t] = c.astype(BF16)
    o_ref[:, t:2 * t] = (-s).astype(BF16)


def _dft_tables(t, gw):
    def cis(rows_mult, nrows):
        k = jnp.arange(nrows, dtype=jnp.int32)[:, None] * rows_mult
        n = jnp.arange(t, dtype=jnp.int32)[None, :]
        ang = ((k * n) % t).astype(F32) * (2.0 * math.pi / t)
        return jnp.concatenate([jnp.cos(ang), jnp.sin(ang)], axis=1)
    r = DFT_ROWS
    e1 = cis(r, t // r).reshape(t // r, 1, 2 * t)
    e2 = cis(1, r)
    tab_t = pl.pallas_call(
        functools.partial(_dft_table_kernel, t=t),
        grid=(t // r,),
        in_specs=[
            pl.BlockSpec((1, 1, 2 * t), lambda i: (i, 0, 0)),
            pl.BlockSpec((r, 2 * t), lambda i: (0, 0)),
        ],
        out_specs=pl.BlockSpec((r, 2 * t), lambda i: (i, 0)),
        out_shape=jax.ShapeDtypeStruct((t, 2 * t), BF16),
        compiler_params=_cparams("arbitrary"),
        name="dft_table",
    )(e1, e2)
    kc = jnp.arange(gw, dtype=jnp.int32)
    ang = ((kc[:, None] * kc[None, :]) % gw).astype(F32) * (2.0 * math.pi / gw)
    tab_c = jnp.concatenate([jnp.cos(ang), jnp.sin(ang)], axis=1).astype(BF16)
    return tab_t, tab_c


def _fft_kernel(f_ref, cs_ref, tab_ref, o_ref, data, *, t, gw, groups, scale):
    i = pl.program_id(1)

    @pl.when(i == 0)
    def _():
        rb = min(512, t)
        for r in range(t // rb):
            for g in range(groups):
                fg = f_ref[r * rb:(r + 1) * rb, g * gw:(g + 1) * gw]
                z = jnp.dot(fg, cs_ref[...], preferred_element_type=F32)
                data[r * rb:(r + 1) * rb, g * gw:(g + 1) * gw] = z[:, :gw].astype(BF16)
                data[t + r * rb:t + (r + 1) * rb, g * gw:(g + 1) * gw] = z[:, gw:].astype(BF16)

    y = jnp.dot(tab_ref[...], data[...], preferred_element_type=F32)
    o_ref[...] = (y * scale).astype(BF16)


def _fourier(p, col_tile, t, batch, tabs):
    tab_t, tab_c = tabs
    wf = COL_TILE
    gw = wf // FFT_GROUPS
    tm = min(TM_DFT, t)
    return pl.pallas_call(
        functools.partial(_fft_kernel, t=t, gw=gw, groups=FFT_GROUPS, scale=1.0 / math.sqrt(t * gw)),
        grid=(batch, t // tm),
        in_specs=[
            pl.BlockSpec((t, wf), lambda b, i: (b, col_tile)),
            pl.BlockSpec((gw, 2 * gw), lambda b, i: (0, 0)),
            pl.BlockSpec((tm, 2 * t), lambda b, i: (i, 0)),
        ],
        out_specs=pl.BlockSpec((tm, wf), lambda b, i: (b * (t // tm) + i, 0)),
        out_shape=jax.ShapeDtypeStruct((batch * t, wf), BF16),
        scratch_shapes=[pltpu.VMEM((2 * t, wf), BF16)],
        compiler_params=_cparams("arbitrary", "arbitrary"),
        name="fourier",
    )(p, tab_c, tab_t)


def _cmul_const(xr, xi, c, s):
    def scaled(v, k):
        if abs(k) < 1e-12:
            return None
        return v if abs(k - 1.0) < 1e-12 else -v if abs(k + 1.0) < 1e-12 else v * k

    def add(a, b):
        return b if a is None else a if b is None else a + b

    return add(scaled(xr, c), scaled(xi, -s)), add(scaled(xi, c), scaled(xr, s))


def _dft4(y):
    (ar, ai), (br, bi), (cr, ci), (dr, di) = y
    sr, si, tr, ti = ar + cr, ai + ci, ar - cr, ai - ci
    ur, ui, vr, vi = br + dr, bi + di, br - dr, bi - di
    return [(sr + ur, si + ui), (tr + vi, ti - vr), (sr - ur, si - ui), (tr - vi, ti + vr)]


def _dft16(x):
    out = [None] * 16
    p = [_dft4([x[4 * a + b] for a in range(4)]) for b in range(4)]
    for ka in range(4):
        q = []
        for b in range(4):
            ang = -2.0 * math.pi * ka * b / 16.0
            q.append(_cmul_const(p[b][ka][0], p[b][ka][1], math.cos(ang), math.sin(ang)))
        r = _dft4(q)
        for kb in range(4):
            out[ka + 4 * kb] = r[kb]
    return out


def _fft_fast_kernel(f_ref, csc_ref, tw_ref, cst_ref, o_ref, z, o_scr, *, n1, n2, gw, scale):
    slabs_per_dot = 4
    for q in range(n2 // slabs_per_dot):
        r0 = q * slabs_per_dot * n1
        zz = jnp.dot(f_ref[r0:r0 + slabs_per_dot * n1, :], csc_ref[...], preferred_element_type=F32)
        for s in range(slabs_per_dot):
            z[0, q * slabs_per_dot + s] = zz[s * n1:(s + 1) * n1, 0:gw]
            z[1, q * slabs_per_dot + s] = zz[s * n1:(s + 1) * n1, gw:2 * gw]

    def tile_body(r, carry):
        rows = pl.ds(pl.multiple_of(r * 8, 8), 8)
        for c in range(gw // LANES):
            cols = slice(c * LANES, (c + 1) * LANES)
            h = _dft16([(z[0, t, rows, cols], z[1, t, rows, cols]) for t in range(n2)])
            for k2 in range(n2):
                hr, hi = h[k2]
                if k2 > 0:
                    hr, hi = _cmul(hr, hi, tw_ref[0, k2, rows, :], tw_ref[1, k2, rows, :])
                z[0, k2, rows, cols] = hr
                z[1, k2, rows, cols] = hi
        return carry

    lax.fori_loop(0, n1 // 8, tile_body, 0)

    for k2 in range(n2):
        rhs = jnp.concatenate([z[0, k2], z[1, k2]], axis=0).astype(BF16)
        y = jnp.dot(cst_ref[...], rhs, preferred_element_type=F32) * scale
        for c in range(gw // LANES):
            o_scr[c, pl.ds(k2, n1, stride=n2), :] = y[:, c * LANES:(c + 1) * LANES]
    o_ref[...] = jnp.concatenate([o_scr[c] for c in range(gw // LANES)], axis=1).astype(BF16)


def _fourier_fast(p, col_tile, t, batch):
    n1, n2 = DFT_N1, DFT_N2
    assert t == n1 * n2
    wf = COL_TILE
    gw = wf // FFT_GROUPS
    ang = lambda a, b, n: ((a[:, None] * b[None, :]) % n).astype(F32) * (2.0 * math.pi / n)
    kc = jnp.arange(gw, dtype=jnp.int32)
    a_c = ang(kc, kc, gw)
    csc = jnp.concatenate([jnp.cos(a_c), -jnp.sin(a_c)], axis=1).astype(BF16)
    k1 = jnp.arange(n1, dtype=jnp.int32)
    a_t = ang(k1, k1, n1)
    cst = jnp.concatenate([jnp.cos(a_t), jnp.sin(a_t)], axis=1).astype(BF16)
    a_w = ang(jnp.arange(n2, dtype=jnp.int32), k1, t)
    tw = jnp.stack([jnp.cos(a_w), -jnp.sin(a_w)], axis=0)
    tw = jnp.broadcast_to(tw[..., None], (2, n2, n1, LANES))
    ngrp = wf // gw
    return pl.pallas_call(
        functools.partial(_fft_fast_kernel, n1=n1, n2=n2, gw=gw, scale=1.0 / math.sqrt(t * gw)),
        grid=(batch, ngrp),
        in_specs=[
            pl.BlockSpec((t, gw), lambda b, g: (b, col_tile * ngrp + g)),
            pl.BlockSpec((gw, 2 * gw), lambda b, g: (0, 0)),
            pl.BlockSpec((2, n2, n1, LANES), lambda b, g: (0, 0, 0, 0), pipeline_mode=pl.Buffered(1)),
            pl.BlockSpec((n1, 2 * n1), lambda b, g: (0, 0)),
        ],
        out_specs=pl.BlockSpec((t, gw), lambda b, g: (b, g)),
        out_shape=jax.ShapeDtypeStruct((batch * t, wf), BF16),
        scratch_shapes=[
            pltpu.VMEM((2, n2, n1, gw), F32),
            pltpu.VMEM((gw // LANES, t, LANES), F32),
        ],
        compiler_params=_cparams("arbitrary", "arbitrary"),
        name="fourier_fast",
    )(p, csc, tw, cst)


def _branch_kernel(conv_ref, uz_ref, zc_ref, g01_ref, g2_ref, ys_ref, yf_ref, cw_ref, sd_ref, wa_ref, wb_ref,
                   fw_ref, pa_ref, pb_ref, pc_ref, m_ref, *, row_len):
    tm = conv_ref.shape[0]
    tn = zc_ref.shape[1]
    d = g2_ref.shape[1]
    xa_ref, ba_ref, ca_ref, za_ref = (conv_ref.at[:, k * tn:(k + 1) * tn] for k in range(4))
    u_ref, zb_ref = (uz_ref.at[:, k * tn:(k + 1) * tn] for k in range(2))
    g0_ref, g1_ref = (g01_ref.at[:, k * d:(k + 1) * d] for k in range(2))
    f = lambda r: r[...].astype(F32)
    v = f(ca_ref) * f(xa_ref)
    pos = lax.broadcasted_iota(jnp.int32, v.shape, 0) % row_len
    v_prev = jnp.where(pos == 0, 0.0, pltpu.roll(v, 1, 0))
    v_next = jnp.where(pos == row_len - 1, 0.0, pltpu.roll(v, tm - 1, 0))
    cw = cw_ref[0]
    conv = v_prev * cw[0:1, :] + v * cw[1:2, :] + v_next * cw[2:3, :]
    a = f(ba_ref) * conv * _silu(f(za_ref))
    ya = jnp.dot(a.astype(BF16), pa_ref[0], preferred_element_type=F32)
    acc = _sigmoid(f(g0_ref)) * ya
    y = _gelu_tanh(ys_ref[...] + sd_ref[0] * f(u_ref)).astype(BF16)
    glu = (jnp.dot(y, wa_ref[0], preferred_element_type=F32)
           * _sigmoid(jnp.dot(y, wb_ref[0], preferred_element_type=F32)) * _silu(f(zb_ref)))
    yb = jnp.dot(glu.astype(BF16), pb_ref[0], preferred_element_type=F32)
    acc = acc + _sigmoid(f(g1_ref)) * yb
    c = jnp.dot(yf_ref[...], fw_ref[0], preferred_element_type=F32) * _silu(f(zc_ref))
    yc = jnp.dot(c.astype(BF16), pc_ref[0], preferred_element_type=F32)
    acc = acc + _sigmoid(f(g2_ref)) * yc
    m_ref[...] = acc.astype(BF16)


def _branches(p, ys, yf, conv_w, ssm_d, wa, wb, fw, pa, pb, pc, layer, row_len):
    n = p.shape[0]
    wc = conv_w.shape[-1]
    d = pa.shape[-1]
    tm = TM_BRANCH
    tn = COL_TILE
    col = lambda k: pl.BlockSpec((tm, tn), lambda i: (i, k))
    const = lambda shape: pl.BlockSpec(shape, lambda i: (layer,) + (0,) * (len(shape) - 1),
                                       pipeline_mode=pl.Buffered(1))
    g_col0 = N_BRANCH_TILES * tn
    assert g_col0 % (2 * d) == 0 and U_TILE % 2 == 0
    return pl.pallas_call(
        functools.partial(_branch_kernel, row_len=row_len),
        grid=(n // tm,),
        in_specs=[
            pl.BlockSpec((tm, U_TILE * tn), lambda i: (i, 0)),
            pl.BlockSpec((tm, 2 * tn), lambda i: (i, U_TILE // 2)),
            col(7),
            pl.BlockSpec((tm, 2 * d), lambda i: (i, g_col0 // (2 * d))),
            pl.BlockSpec((tm, d), lambda i: (i, g_col0 // d + 2)),
            pl.BlockSpec((tm, tn), lambda i: (i, 0)),
            pl.BlockSpec((tm, tn), lambda i: (i, 0)),
            const((1, 3, wc)), const((1, 1, tn)),
            const((1, tn, tn)), const((1, tn, tn)), const((1, tn, tn)),
            const((1, wc, d)), const((1, tn, d)), const((1, tn, d)),
        ],
        out_specs=pl.BlockSpec((tm, d), lambda i: (i, 0)),
        out_shape=jax.ShapeDtypeStruct((n, d), BF16),
        compiler_params=_cparams("arbitrary"),
        name="branches",
    )(p, p, p, p, p, ys, yf, conv_w, ssm_d.reshape(ssm_d.shape[0], 1, -1),
      wa, wb, fw, pa, pb, pc)


def _out_kernel(*refs, with_next):
    if with_next:
        m_ref, x_ref, gt_ref, g_ref, w_ref, scn_ref, shn_ref, gn_ref, o_ref, h_ref, wb = refs
    else:
        m_ref, x_ref, gt_ref, g_ref, w_ref, o_ref, wb = refs

    @pl.when(pl.program_id(0) == 0)
    def _():
        wb[...] = w_ref[0].astype(BF16)

    o = jnp.dot(m_ref[...], wb[...], preferred_element_type=F32)
    ms = jnp.mean(o * o, axis=-1, keepdims=True)
    xn = x_ref[...] + o * lax.rsqrt(ms + RMS_EPS) * (g_ref[0] * gt_ref[0])
    o_ref[...] = xn
    if with_next:
        h_ref[...] = _modnorm(xn, gn_ref[0] * (1.0 + scn_ref[0]), shn_ref[0]).astype(BF16)


def _outproj(m, x2, mods_r, g_post, w_out, layer, rows_per_batch, ctx_row=None, g_pre_next=None):
    n, d = x2.shape
    tm = min(TM_OUT, rows_per_batch)
    tpb = rows_per_batch // tm
    in_specs = [
        pl.BlockSpec((tm, d), lambda i: (i, 0)),
        pl.BlockSpec((tm, d), lambda i: (i, 0)),
        _mod_spec(d, layer, MOD_GATE, tpb, ctx_row),
        pl.BlockSpec((1, 1, d), lambda i: (layer, 0, 0)),
        pl.BlockSpec((1, d, d), lambda i: (layer, 0, 0), pipeline_mode=pl.Buffered(1)),
    ]
    args = [m, x2, mods_r, g_post.reshape(g_post.shape[0], 1, d), w_out]
    out_specs = [pl.BlockSpec((tm, d), lambda i: (i, 0))]
    out_shape = [jax.ShapeDtypeStruct((n, d), F32)]
    if g_pre_next is not None:
        in_specs += [_mod_spec(d, layer + 1, MOD_SCALE, tpb, ctx_row), _mod_spec(d, layer + 1, MOD_SHIFT, tpb, ctx_row),
                     pl.BlockSpec((1, 1, d), lambda i: (layer + 1, 0, 0))]
        args += [mods_r, mods_r, g_pre_next.reshape(g_pre_next.shape[0], 1, d)]
        out_specs.append(pl.BlockSpec((tm, d), lambda i: (i, 0)))
        out_shape.append(jax.ShapeDtypeStruct((n, d), BF16))
    res = pl.pallas_call(
        functools.partial(_out_kernel, with_next=g_pre_next is not None),
        grid=(n // tm,),
        in_specs=in_specs,
        out_specs=out_specs,
        out_shape=out_shape,
        scratch_shapes=[pltpu.VMEM((d, d), BF16)],
        compiler_params=_cparams("arbitrary"),
        name="outproj",
    )(*args)
    return (res[0], res[1]) if g_pre_next is not None else (res[0], None)


def kernel(x, c, ctx, c_ctx, w_ada, b_ada, g_pre, g_post, w_in, conv_w, ssm_lam_re, ssm_lam_im, ssm_log_dt,
           ssm_b_re, ssm_b_im, ssm_c_re, ssm_c_im, ssm_d, glu_wa, glu_wb, fourier_w, proj_a, proj_b, proj_c,
           w_out):
    bn, t, d = x.shape
    lc = ctx.shape[1]
    depth = w_ada.shape[0]
    w_conv = conv_w.shape[-1]
    w_ssm = ssm_d.shape[-1]
    tn = COL_TILE
    assert w_conv == tn and w_ssm == tn and fourier_w.shape[-1] == tn and d % tn == 0
    assert bn + 1 <= MOD_ROWS and t % GRID_W == 0
    u_tile = 4 * w_conv // tn
    f_tile = u_tile + 2
    n_tiles = w_in.shape[-1] // tn
    assert u_tile == U_TILE and f_tile + 2 == N_BRANCH_TILES and (n_tiles - N_BRANCH_TILES) * tn == N_BRANCH * d

    cc = jnp.concatenate([c, c_ctx[None, :], jnp.zeros((MOD_ROWS - bn - 1, d), F32)], axis=0)
    mods = _ada_mods(cc, w_ada, b_ada)

    to_b = lambda w: w.astype(BF16)
    wa_b, wb_b, fw_b = to_b(glu_wa), to_b(glu_wb), to_b(fourier_w)
    pa_b, pb_b, pc_b = to_b(proj_a), to_b(proj_b), to_b(proj_c)
    fast_dft = t == DFT_N1 * DFT_N2
    tabs_x = None if fast_dft else _dft_tables(t, tn // FFT_GROUPS)
    tabs_c = _dft_tables(lc, tn // FFT_GROUPS)

    x2 = x.reshape(bn * t, d)
    c2 = ctx.reshape(bn * lc, d)
    mods_r = mods.reshape(depth * MOD_ROWS * 3, 1, d)
    ctx_row = bn
    ssm_p = _ssm_params(ssm_lam_re, ssm_lam_im, ssm_log_dt, ssm_b_re, ssm_b_im, ssm_c_re, ssm_c_im)
    hx = _prenorm(x2, mods_r, g_pre, 0, t)
    hc = _prenorm(c2, mods_r, g_pre, 0, lc, ctx_row)
    for l in range(depth):
        last = l == depth - 1
        g_pre_next = None if last else g_pre

        px = _inproj(hx, w_in, l, 0, n_tiles)
        pc_ = _inproj(hc, w_in, l, u_tile, 1) if last else _inproj(hc, w_in, l, 0, n_tiles)
        u_lane_tile = u_tile * (tn // LANES)
        ys_x, ys_c = _ssm_scan(px, pc_, u_lane_tile, 0 if last else u_lane_tile, w_ssm, ssm_p, l, bn,
                               ctx_out=not last)

        yf_x = _fourier_fast(px, f_tile, t, bn) if fast_dft else _fourier(px, f_tile, t, bn, tabs_x)
        mx = _branches(px, ys_x, yf_x, conv_w, ssm_d, wa_b, wb_b, fw_b, pa_b, pb_b, pc_b, l, GRID_W)
        new_x2, hx = _outproj(mx, x2, mods_r, g_post, w_out, l, t, None, g_pre_next)
        if not last:
            yf_c = _fourier(pc_, f_tile, lc, bn, tabs_c)
            mc = _branches(pc_, ys_c, yf_c, conv_w, ssm_d, wa_b, wb_b, fw_b, pa_b, pb_b, pc_b, l, lc)
            c2, hc = _outproj(mc, c2, mods_r, g_post, w_out, l, lc, ctx_row, g_pre_next)
        x2 = new_x2
    return x2.reshape(bn, t, d)
```

```python
import functools
import math

import jax
import jax.numpy as jnp
from jax import lax
from jax.experimental import pallas as pl
from jax.experimental.pallas import tpu as pltpu

F32 = jnp.float32
BF16 = jnp.bfloat16

GRID_W = 64
FFT_GROUPS = 4
N_BRANCH = 3
RMS_EPS = 1e-6
LANES = 128
SSM_CHUNK = 8
SSM_SEGS = 8
DFT_ROWS = 64
DFT_N1, DFT_N2 = 256, 16
VMEM_LIMIT = 56 * 1024 * 1024
COL_TILE = 1024
N_BRANCH_TILES = 8
U_TILE = 4
TM_INPROJ, TM_PRENORM, TM_BRANCH, TM_OUT, TM_DFT = 2048, 1024, 256, 512, 256
TM_INPROJ_DOT = 1024


def _cparams(*sem):
    return pltpu.CompilerParams(dimension_semantics=sem, vmem_limit_bytes=VMEM_LIMIT)


def _sigmoid(v):
    return 0.5 * jnp.tanh(0.5 * v) + 0.5


def _silu(v):
    return v * _sigmoid(v)


def _gelu_tanh(v):
    return v * (0.5 * (1.0 + jnp.tanh(math.sqrt(2.0 / math.pi) * (v + 0.044715 * (v * v * v)))))


def _cmul(ar, ai, br, bi):
    return ar * br - ai * bi, ar * bi + ai * br


def _cmul_add(ar, ai, zr, zi, sr, si):
    return ar * zr - ai * zi + sr, ar * zi + ai * zr + si


def _ada_kernel(c_ref, w_ref, b_ref, o_ref):
    c = c_ref[...]
    s = _silu(c).astype(BF16)
    o_ref[0] = jnp.dot(s, w_ref[0].astype(BF16), preferred_element_type=F32) + b_ref[0]


def _ada_mods(cc, w_ada, b_ada):
    depth, d, n3 = w_ada.shape
    rows = cc.shape[0]
    tn = 2 * COL_TILE if n3 % (2 * COL_TILE) == 0 else COL_TILE
    assert n3 % tn == 0
    return pl.pallas_call(
        _ada_kernel,
        grid=(depth, n3 // tn),
        in_specs=[
            pl.BlockSpec((rows, d), lambda l, j: (0, 0)),
            pl.BlockSpec((1, d, tn), lambda l, j: (l, 0, j)),
            pl.BlockSpec((1, 1, tn), lambda l, j: (l, 0, j)),
        ],
        out_specs=pl.BlockSpec((1, rows, tn), lambda l, j: (l, 0, j)),
        out_shape=jax.ShapeDtypeStruct((depth, rows, n3), F32),
        compiler_params=_cparams("arbitrary", "arbitrary"),
        name="ada_mods",
    )(cc, w_ada, b_ada.reshape(depth, 1, n3))


def _modnorm(x, gs, sh):
    ms = jnp.mean(x * x, axis=-1, keepdims=True)
    return x * lax.rsqrt(ms + RMS_EPS) * gs + sh


def _prenorm_kernel(x_ref, sc_ref, sh_ref, g_ref, h_ref):
    rb = 16
    gs = g_ref[0] * (1.0 + sc_ref[0])
    sh = sh_ref[0]

    def body(r, carry):
        rows = pl.ds(pl.multiple_of(r * rb, rb), rb)
        h_ref[rows, :] = _modnorm(x_ref[rows, :], gs, sh).astype(BF16)
        return carry

    lax.fori_loop(0, x_ref.shape[0] // rb, body, 0, unroll=4)


MOD_SHIFT, MOD_SCALE, MOD_GATE = 0, 1, 2
MOD_ROWS = 8


def _mod_spec(d, layer, which, tiles_per_batch, ctx_row):
    def index(i):
        row = ctx_row if ctx_row is not None else i // tiles_per_batch
        return ((layer * MOD_ROWS + row) * 3 + which, 0, 0)
    return pl.BlockSpec((1, 1, d), index)


def _prenorm(x2, mods_r, g_pre, layer, rows_per_batch, ctx_row=None):
    n, d = x2.shape
    tm = min(TM_PRENORM, rows_per_batch)
    tpb = rows_per_batch // tm
    return pl.pallas_call(
        _prenorm_kernel,
        grid=(n // tm,),
        in_specs=[
            pl.BlockSpec((tm, d), lambda i: (i, 0)),
            _mod_spec(d, layer, MOD_SCALE, tpb, ctx_row),
            _mod_spec(d, layer, MOD_SHIFT, tpb, ctx_row),
            pl.BlockSpec((1, 1, d), lambda i: (layer, 0, 0)),
        ],
        out_specs=pl.BlockSpec((tm, d), lambda i: (i, 0)),
        out_shape=jax.ShapeDtypeStruct((n, d), BF16),
        compiler_params=_cparams("arbitrary"),
        name="prenorm",
    )(x2, mods_r, mods_r, g_pre.reshape(g_pre.shape[0], 1, d))


def _inproj_kernel(h_ref, w_ref, p_ref, *bf16_weights, cast):
    if cast:
        (wb,) = bf16_weights

        @pl.when(pl.program_id(1) == 0)
        def _():
            wb[...] = w_ref[0].astype(BF16)
    else:
        wb = w_ref

    sub = min(TM_INPROJ_DOT, h_ref.shape[0])
    for r in range(h_ref.shape[0] // sub):
        rows = slice(r * sub, (r + 1) * sub)
        p_ref[rows, :] = jnp.dot(h_ref[rows, :], wb[...], preferred_element_type=F32).astype(BF16)


def _inproj(h, w_in, layer, tile0, ntiles, keep_weights=False):
    n, d = h.shape
    tn = COL_TILE
    tm = min(TM_INPROJ, n)
    cast = layer is not None
    w_spec = (pl.BlockSpec((1, d, tn), lambda j, i: (layer, 0, tile0 + j)) if cast
              else pl.BlockSpec((d, tn), lambda j, i: (0, tile0 + j)))
    out_specs = [pl.BlockSpec((tm, tn), lambda j, i: (i, j))]
    out_shape = [jax.ShapeDtypeStruct((n, ntiles * tn), BF16)]
    scratch = []
    if keep_weights:
        out_specs.append(pl.BlockSpec((d, tn), lambda j, i: (0, j)))
        out_shape.append(jax.ShapeDtypeStruct((d, ntiles * tn), BF16))
    elif cast:
        scratch = [pltpu.VMEM((d, tn), BF16)]
    res = pl.pallas_call(
        functools.partial(_inproj_kernel, cast=cast),
        grid=(ntiles, n // tm),
        in_specs=[pl.BlockSpec((tm, d), lambda j, i: (i, 0)), w_spec],
        out_specs=out_specs,
        out_shape=out_shape,
        scratch_shapes=scratch,
        compiler_params=_cparams("arbitrary", "arbitrary"),
        name="inproj",
    )(h, w_in)
    return (res[0], res[1]) if keep_weights else res[0]


def _ssm_params(lam_re, lam_im, log_dt, b_re, b_im, c_re, c_im):
    nl, _, G, P = lam_re.shape
    H = b_re.shape[-1]
    gpt = LANES // H
    O = G // gpt
    lam_re = lam_re.astype(F32)
    lam_im = lam_im.astype(F32)
    dt = jnp.exp(log_dt.astype(F32))[..., None]
    lr = lam_re * dt
    li = lam_im * dt
    mag = jnp.exp(lr)
    a_re = mag * jnp.cos(li)
    a_im = mag * jnp.sin(li)
    n_re = a_re - 1.0
    n_im = a_im
    den = lam_re * lam_re + lam_im * lam_im
    q_re = (n_re * lam_re + n_im * lam_im) / den
    q_im = (n_im * lam_re - n_re * lam_im) / den
    bb_re = q_re[..., None] * b_re - q_im[..., None] * b_im
    bb_im = q_re[..., None] * b_im + q_im[..., None] * b_re

    bbs = jnp.stack([bb_re, bb_im], axis=2).reshape(nl, 2, 2, O, gpt, P, H)
    bc = jnp.transpose(bbs, (0, 3, 1, 2, 6, 4, 5)).reshape(nl, O, 4, H, gpt * P)
    cs = jnp.stack([c_re.astype(F32), c_im.astype(F32)], axis=2).reshape(nl, 2, 2, O, gpt, H, P)
    cc = jnp.transpose(cs, (0, 3, 1, 2, 5, 4, 6)).reshape(nl, O, 4, H, gpt * P)
    a4 = jnp.stack([a_re, a_im], axis=2).reshape(nl, 2, 2, O, gpt * P)
    a4 = jnp.transpose(a4, (0, 3, 1, 2, 4)).reshape(nl, O, 4, gpt * P)
    arow = jnp.concatenate([a4, jnp.zeros_like(a4)], axis=2)

    return bc, cc, arow


def _block_transpose8(v):
    bw = LANES // 8
    blk = lax.broadcasted_iota(jnp.int32, v[0].shape, 1) // bw
    for dist in (4, 2, 1):
        upper = (blk & dist) != 0
        nxt = list(v)
        for a in range(8):
            if a & dist:
                continue
            b = a + dist
            nxt[a] = jnp.where(upper, pltpu.roll(v[b], dist * bw, 1), v[a])
            nxt[b] = jnp.where(upper, v[b], pltpu.roll(v[a], LANES - dist * bw, 1))
        v = nxt
    return v


def _ssm_build_operators(bc_ref, cc_ref, arow_ref, ws, ws_lo, w3):
    L = SSM_CHUNK
    hh = bc_ref.shape[2]
    sw = bc_ref.shape[3]
    pp = sw // (LANES // hh)
    npair = sw // LANES
    assert L * hh == LANES and 2 * pp == LANES
    lane = lax.broadcasted_iota(jnp.int32, (LANES, LANES), 1)
    row = lax.broadcasted_iota(jnp.int32, (LANES, LANES), 0)
    lane_g0, row_g0 = lane < pp, row < pp
    lane_blk = lane // hh
    zeros = jnp.zeros((LANES, LANES), F32)
    zeros_b = jnp.zeros((LANES, LANES), BF16)

    def powers(re, im, n):
        out = [(jnp.ones_like(re), jnp.zeros_like(re))]
        for _ in range(n):
            out.append(_cmul(out[-1][0], out[-1][1], re, im))
        return out

    def split(v):
        hi = v.astype(BF16)
        return hi, (v - hi.astype(F32)).astype(BF16)

    def shift_rows(t, blocks):
        n = abs(blocks) * hh
        if n == 0:
            return t
        pad = jnp.zeros((n, LANES), F32)
        return jnp.concatenate([pad, t[:LANES - n]] if blocks > 0 else [t[n:], pad], axis=0)

    for q in range(npair):
        ql = slice(q * LANES, (q + 1) * LANES)
        taps = []
        for d in range(2):
            b_re = jnp.concatenate([bc_ref[0, 2 * d, :, ql]] * L, axis=0)
            b_im = jnp.concatenate([bc_ref[0, 2 * d + 1, :, ql]] * L, axis=0)
            prow = powers(arow_ref[0, 2 * d:2 * d + 1, ql], arow_ref[0, 2 * d + 1:2 * d + 2, ql], L)

            def power_rows(exps):
                return (jnp.concatenate([jnp.broadcast_to(prow[e][0], (hh, LANES)) for e in exps], axis=0),
                        jnp.concatenate([jnp.broadcast_to(prow[e][1], (hh, LANES)) for e in exps], axis=0))

            a_re, a_im = power_rows([L - 1 - j if d == 0 else j for j in range(L)])
            for part, v in zip((2 * d, 2 * d + 1), _cmul(b_re, b_im, a_re, a_im)):
                cols = slice(part * LANES, (part + 1) * LANES)
                for dst, piece in zip((ws, ws_lo), split(v)):
                    dst[q, 0:LANES, cols] = jnp.where(lane_g0, piece, zeros_b)
                    dst[q, LANES:2 * LANES, cols] = jnp.where(lane_g0, zeros_b, piece)
            ct_re = jnp.concatenate([cc_ref[0, 2 * d, :, ql]] * L, axis=0)
            ct_im = jnp.concatenate([cc_ref[0, 2 * d + 1, :, ql]] * L, axis=0)
            c_re, c_im = ct_re.T, ct_im.T
            (cr_hi, cr_lo), (ci_hi, ci_lo) = split(c_re), split(-c_im)
            c_hi = jnp.concatenate([cr_hi, ci_hi], axis=0)
            c_lo = jnp.concatenate([cr_lo, ci_lo], axis=0)
            dcols = slice(2 * d * LANES, (2 * d + 2) * LANES)
            taps.append(jnp.dot(jnp.concatenate([ws[q, :, dcols], ws[q, :, dcols], ws_lo[q, :, dcols]], axis=1),
                                jnp.concatenate([c_hi, c_lo, c_hi], axis=0), preferred_element_type=F32))
            cat_re, cat_im = _cmul(ct_re, ct_im, *power_rows([jo + 1 if d == 0 else L - jo for jo in range(L)]))
            for part, v in ((2 * d, cat_re.T), (2 * d + 1, -cat_im.T)):
                r0 = 2 * LANES + part * LANES
                w3[q, r0:r0 + LANES, 0:LANES] = jnp.where(row_g0, v, zeros).astype(BF16)
                w3[q, r0:r0 + LANES, LANES:2 * LANES] = jnp.where(row_g0, zeros, v).astype(BF16)
        for g2 in range(2):
            tf = taps[0][g2 * LANES:(g2 + 1) * LANES]
            tb = taps[1][g2 * LANES:(g2 + 1) * LANES]
            m = zeros
            for jo in range(L):
                m = jnp.where(lane_blk == jo, shift_rows(tf, jo - (L - 1)) + shift_rows(tb, jo), m)
            rows = slice(g2 * LANES, (g2 + 1) * LANES)
            w3[q, rows, g2 * LANES:(g2 + 1) * LANES] = m.astype(BF16)
            w3[q, rows, (1 - g2) * LANES:(2 - g2) * LANES] = zeros_b


def _ssm_build_powers(arow_ref, ap, pw, seg_len):
    L = SSM_CHUNK
    sw = arow_ref.shape[2]
    assert L & (L - 1) == 0 and seg_len & (seg_len - 1) == 0 and seg_len >= 8
    for d in range(2):
        cr, ci = slice(2 * d * sw, (2 * d + 1) * sw), slice((2 * d + 1) * sw, (2 * d + 2) * sw)
        re, im = arow_ref[0, 2 * d:2 * d + 1, :], arow_ref[0, 2 * d + 1:2 * d + 2, :]
        for _ in range(L.bit_length() - 1):
            re, im = _cmul(re, im, re, im)
        ap[0:1, cr] = re
        ap[0:1, ci] = im
        p = [(jnp.ones_like(re), jnp.zeros_like(re))]
        for _ in range(7):
            p.append(_cmul(p[-1][0], p[-1][1], re, im))
        order = list(range(8)) if d == 0 else list(range(7, -1, -1))
        first = slice(0, 8) if d == 0 else slice(seg_len - 8, seg_len)
        pw[first, cr] = jnp.concatenate([p[r][0] for r in order], axis=0)
        pw[first, ci] = jnp.concatenate([p[r][1] for r in order], axis=0)
        sq = _cmul(p[4][0], p[4][1], p[4][0], p[4][1])
        m = 8
        while m < seg_len:
            src = slice(0, m) if d == 0 else slice(seg_len - m, seg_len)
            dst = slice(m, 2 * m) if d == 0 else slice(seg_len - 2 * m, seg_len - m)
            xr, xi = _cmul(pw[src, cr], pw[src, ci], sq[0], sq[1])
            pw[dst, cr] = xr
            pw[dst, ci] = xi
            sq = _cmul(sq[0], sq[1], sq[0], sq[1])
            m *= 2
        ap[1:2, cr] = sq[0]
        ap[1:2, ci] = sq[1]


def _seg_pitch(seg_len):
    tiles = seg_len // 8 + 1
    return 8 * (tiles + 1 - tiles % 2)


def _ssm_kernel(*refs, nc, nx, seg_len, batch, ctx_out):
    if ctx_out:
        (ux_ref, uc_ref, bc_ref, cc_ref, arow_ref,
         yx_ref, yc_ref, ws, ws_lo, w3, lhs, st, uxf, ucf, ap, pw) = refs
    else:
        (ux_ref, uc_ref, bc_ref, cc_ref, arow_ref,
         yx_ref, ws, ws_lo, w3, lhs, st, uxf, ucf, ap, pw) = refs
        yc_ref = None
    L = SSM_CHUNK
    nlt = st.shape[0] // 4
    sw = nlt * LANES
    xw = 2 * LANES
    FR, FI, BR, BI = range(4)
    nseg = SSM_SEGS
    pitch = _seg_pitch(seg_len)
    t, lc = nx * L, nc * L

    _ssm_build_operators(bc_ref, cc_ref, arow_ref, ws, ws_lo, w3)
    _ssm_build_powers(arow_ref, ap, pw, seg_len)

    def ld(part, rows):
        return jnp.concatenate([st[part * nlt + q, rows, :] for q in range(nlt)], axis=1)

    def sto(part, rows, val):
        for q in range(nlt):
            st[part * nlt + q, rows, :] = val[:, q * LANES:(q + 1) * LANES]

    def part_cols(part):
        return slice(part * sw, (part + 1) * sw)

    def one_sequence(b, carry):
        ucf[...] = uc_ref[pl.ds(pl.multiple_of(b * lc, lc), lc), :].astype(F32)
        uxf[...] = ux_ref[pl.ds(pl.multiple_of(b * t, t), t), :].astype(F32)
        by_pos = [jnp.concatenate([ucf[pl.ds(j, nc, stride=L), :], uxf[pl.ds(j, nx, stride=L), :]], axis=0)
                  for j in range(L)]
        for g, tile_g in enumerate(_block_transpose8(by_pos)):
            lhs[g // 2, :, (g % 2) * LANES:(g % 2 + 1) * LANES] = tile_g.astype(BF16)

        for q in range(nlt):
            s = jnp.dot(lhs[q, :, 0:xw], ws[q], preferred_element_type=F32)
            for part in range(4):
                k = part * nlt + q
                st[k, 0:nc, :] = s[0:nc, part * LANES:(part + 1) * LANES]
                for g in range(nseg):
                    st[k, nc + g * pitch:nc + g * pitch + seg_len, :] = (
                        s[nc + g * seg_len:nc + (g + 1) * seg_len, part * LANES:(part + 1) * LANES])

        a_l = ap[0:1, :]
        a_seg = ap[1:2, :]
        al = [a_l[:, part_cols(p)] for p in range(4)]
        aseg = [a_seg[:, part_cols(p)] for p in range(4)]

        zero = jnp.zeros((1, sw), F32)

        def ctx_step(pr, pi):
            def step(i, carry):
                er, ei = carry
                row = pl.ds(i, 1)
                sr, si = ld(pr, row), ld(pi, row)
                sto(pr, row, er)
                sto(pi, row, ei)
                return _cmul_add(al[pr], al[pi], er, ei, sr, si)
            return step

        h0f = lax.fori_loop(0, nc, ctx_step(FR, FI), (zero, zero))
        bwd_step = ctx_step(BR, BI)
        h0b = lax.fori_loop(0, nc, lambda k, carry: bwd_step(nc - 1 - k, carry), (zero, zero))
        for part in range(4):
            for q in range(nlt):
                lhs[q, 0:nc, xw + part * LANES:xw + (part + 1) * LANES] = st[part * nlt + q, 0:nc, :].astype(BF16)

        zseg = jnp.zeros((nseg, sw), F32)

        def seg_pass(pr, pi, order):
            zr, zi = zseg, zseg
            for i in order:
                rows = pl.ds(nc + i, nseg, stride=pitch)
                sr, si = ld(pr, rows), ld(pi, rows)
                sto(pr, rows, zr)
                sto(pi, rows, zi)
                zr, zi = _cmul_add(al[pr], al[pi], zr, zi, sr, si)
            return zr, zi

        zfr, zfi = seg_pass(FR, FI, range(seg_len))
        zbr, zbi = seg_pass(BR, BI, range(seg_len - 1, -1, -1))

        ef = [h0f]
        for s in range(nseg - 1):
            ef.append(_cmul_add(aseg[FR], aseg[FI], ef[s][0], ef[s][1], zfr[s:s + 1], zfi[s:s + 1]))
        eb = [None] * nseg
        eb[nseg - 1] = h0b
        for s in range(nseg - 1, 0, -1):
            eb[s - 1] = _cmul_add(aseg[BR], aseg[BI], eb[s][0], eb[s][1], zbr[s:s + 1], zbi[s:s + 1])
        for s in range(nseg):
            srows = slice(nc + s * pitch, nc + s * pitch + seg_len)
            rows = slice(nc + s * seg_len, nc + (s + 1) * seg_len)
            for (pr, pi, e) in ((FR, FI, ef[s]), (BR, BI, eb[s])):
                xr, xi = _cmul_add(pw[:, part_cols(pr)], pw[:, part_cols(pi)], e[0], e[1],
                                   ld(pr, srows), ld(pi, srows))
                for q in range(nlt):
                    ql = slice(q * LANES, (q + 1) * LANES)
                    lhs[q, rows, xw + pr * LANES:xw + (pr + 1) * LANES] = xr[:, ql].astype(BF16)
                    lhs[q, rows, xw + pi * LANES:xw + (pi + 1) * LANES] = xi[:, ql].astype(BF16)

        by_group = []
        for q in range(nlt):
            y = jnp.dot(lhs[q], w3[q], preferred_element_type=F32)
            by_group += [y[:, 0:LANES], y[:, LANES:2 * LANES]]
        for j, tile_j in enumerate(_block_transpose8(by_group)):
            if ctx_out:
                yc_ref[pl.ds(b * lc + j, nc, stride=L), :] = tile_j[0:nc]
            yx_ref[pl.ds(b * t + j, nx, stride=L), :] = tile_j[nc:nc + nx]
        return carry

    lax.fori_loop(0, batch, one_sequence, 0)


def _ssm_scan(px, pc, col_x, col_c, w, params, layer, batch, ctx_out):
    bc, cc, arow = params
    L = SSM_CHUNK
    t = px.shape[0] // batch
    lc = pc.shape[0] // batch
    nx, nc = t // L, lc // L
    seg_len = nx // SSM_SEGS
    o = w // LANES
    sw = bc.shape[-1]
    sdim = 4 * sw
    npair = sw // LANES
    out_shape = [jax.ShapeDtypeStruct((batch * t, w), F32)]
    out_specs = [pl.BlockSpec((batch * t, LANES), lambda oi: (0, oi))]
    if ctx_out:
        out_shape.append(jax.ShapeDtypeStruct((batch * lc, w), F32))
        out_specs.append(pl.BlockSpec((batch * lc, LANES), lambda oi: (0, oi)))
    tile = lambda a: pl.BlockSpec((None, 1) + a.shape[2:], lambda oi: (layer, oi) + (0,) * (a.ndim - 2))
    res = pl.pallas_call(
        functools.partial(_ssm_kernel, nc=nc, nx=nx, seg_len=seg_len, batch=batch, ctx_out=ctx_out),
        grid=(o,),
        in_specs=[
            pl.BlockSpec((batch * t, LANES), lambda oi: (0, col_x + oi)),
            pl.BlockSpec((batch * lc, LANES), lambda oi: (0, col_c + oi)),
            tile(bc), tile(cc), tile(arow),
        ],
        out_specs=out_specs,
        out_shape=out_shape,
        scratch_shapes=[
            pltpu.VMEM((npair, 2 * LANES, 4 * LANES), BF16),
            pltpu.VMEM((npair, 2 * LANES, 4 * LANES), BF16),
            pltpu.VMEM((npair, 6 * LANES, 2 * LANES), BF16),
            pltpu.VMEM((npair, nc + nx, 6 * LANES), BF16),
            pltpu.VMEM((sdim // LANES, nc + SSM_SEGS * _seg_pitch(seg_len), LANES), F32),
            pltpu.VMEM((t, LANES), F32),
            pltpu.VMEM((lc, LANES), F32),
            pltpu.VMEM((8, sdim), F32),
            pltpu.VMEM((seg_len, sdim), F32),
        ],
        compiler_params=_cparams("arbitrary"),
        name="ssm_scan",
    )(px, pc, bc, cc, arow)
    return (res[0], res[1]) if ctx_out else (res[0], None)


def _dft_table_kernel(e1_ref, e2_ref, o_ref, *, t):
    e1c, e1s = e1_ref[0, :, 0:t], e1_ref[0, :, t:2 * t]
    e2c, e2s = e2_ref[:, 0:t], e2_ref[:, t:2 * t]
    c, s = _cmul(e2c, e2s, e1c, e1s)
    o_ref[:, 0:t] = c.astype(BF16)
    o_ref[:, t:2 * t] = (-s).astype(BF16)


def _dft_tables(t, gw):
    def cis(rows_mult, nrows):
        k = jnp.arange(nrows, dtype=jnp.int32)[:, None] * rows_mult
        n = jnp.arange(t, dtype=jnp.int32)[None, :]
        ang = ((k * n) % t).astype(F32) * (2.0 * math.pi / t)
        return jnp.concatenate([jnp.cos(ang), jnp.sin(ang)], axis=1)
    r = DFT_ROWS
    e1 = cis(r, t // r).reshape(t // r, 1, 2 * t)
    e2 = cis(1, r)
    tab_t = pl.pallas_call(
        functools.partial(_dft_table_kernel, t=t),
        grid=(t // r,),
        in_specs=[
            pl.BlockSpec((1, 1, 2 * t), lambda i: (i, 0, 0)),
            pl.BlockSpec((r, 2 * t), lambda i: (0, 0)),
        ],
        out_specs=pl.BlockSpec((r, 2 * t), lambda i: (i, 0)),
        out_shape=jax.ShapeDtypeStruct((t, 2 * t), BF16),
        compiler_params=_cparams("arbitrary"),
        name="dft_table",
    )(e1, e2)
    kc = jnp.arange(gw, dtype=jnp.int32)
    ang = ((kc[:, None] * kc[None, :]) % gw).astype(F32) * (2.0 * math.pi / gw)
    tab_c = jnp.concatenate([jnp.cos(ang), jnp.sin(ang)], axis=1).astype(BF16)
    return tab_t, tab_c


def _fft_kernel(f_ref, cs_ref, tab_ref, o_ref, data, *, t, gw, groups, scale):
    i = pl.program_id(1)

    @pl.when(i == 0)
    def _():
        rb = min(512, t)
        for r in range(t // rb):
            for g in range(groups):
                fg = f_ref[r * rb:(r + 1) * rb, g * gw:(g + 1) * gw]
                z = jnp.dot(fg, cs_ref[...], preferred_element_type=F32)
                data[r * rb:(r + 1) * rb, g * gw:(g + 1) * gw] = z[:, :gw].astype(BF16)
                data[t + r * rb:t + (r + 1) * rb, g * gw:(g + 1) * gw] = z[:, gw:].astype(BF16)

    y = jnp.dot(tab_ref[...], data[...], preferred_element_type=F32)
    o_ref[...] = (y * scale).astype(BF16)


def _fourier(p, col_tile, t, batch, tabs):
    tab_t, tab_c = tabs
    wf = COL_TILE
    gw = wf // FFT_GROUPS
    tm = min(TM_DFT, t)
    return pl.pallas_call(
        functools.partial(_fft_kernel, t=t, gw=gw, groups=FFT_GROUPS, scale=1.0 / math.sqrt(t * gw)),
        grid=(batch, t // tm),
        in_specs=[
            pl.BlockSpec((t, wf), lambda b, i: (b, col_tile)),
            pl.BlockSpec((gw, 2 * gw), lambda b, i: (0, 0)),
            pl.BlockSpec((tm, 2 * t), lambda b, i: (i, 0)),
        ],
        out_specs=pl.BlockSpec((tm, wf), lambda b, i: (b * (t // tm) + i, 0)),
        out_shape=jax.ShapeDtypeStruct((batch * t, wf), BF16),
        scratch_shapes=[pltpu.VMEM((2 * t, wf), BF16)],
        compiler_params=_cparams("arbitrary", "arbitrary"),
        name="fourier",
    )(p, tab_c, tab_t)


def _cmul_const(xr, xi, c, s):
    def scaled(v, k):
        if abs(k) < 1e-12:
            return None
        return v if abs(k - 1.0) < 1e-12 else -v if abs(k + 1.0) < 1e-12 else v * k

    def add(a, b):
        return b if a is None else a if b is None else a + b

    return add(scaled(xr, c), scaled(xi, -s)), add(scaled(xi, c), scaled(xr, s))


def _dft4(y):
    (ar, ai), (br, bi), (cr, ci), (dr, di) = y
    sr, si, tr, ti = ar + cr, ai + ci, ar - cr, ai - ci
    ur, ui, vr, vi = br + dr, bi + di, br - dr, bi - di
    return [(sr + ur, si + ui), (tr + vi, ti - vr), (sr - ur, si - ui), (tr - vi, ti + vr)]


def _dft16(x):
    out = [None] * 16
    p = [_dft4([x[4 * a + b] for a in range(4)]) for b in range(4)]
    for ka in range(4):
        q = []
        for b in range(4):
            ang = -2.0 * math.pi * ka * b / 16.0
            q.append(_cmul_const(p[b][ka][0], p[b][ka][1], math.cos(ang), math.sin(ang)))
        r = _dft4(q)
        for kb in range(4):
            out[ka + 4 * kb] = r[kb]
    return out


def _fft_fast_kernel(f_ref, csc_ref, tw_ref, cst_ref, o_ref, z, o_scr, *, n1, n2, gw, scale):
    slabs_per_dot = 4
    for q in range(n2 // slabs_per_dot):
        r0 = q * slabs_per_dot * n1
        zz = jnp.dot(f_ref[r0:r0 + slabs_per_dot * n1, :], csc_ref[...], preferred_element_type=F32)
        for s in range(slabs_per_dot):
            z[0, q * slabs_per_dot + s] = zz[s * n1:(s + 1) * n1, 0:gw]
            z[1, q * slabs_per_dot + s] = zz[s * n1:(s + 1) * n1, gw:2 * gw]

    def tile_body(r, carry):
        rows = pl.ds(pl.multiple_of(r * 8, 8), 8)
        for c in range(gw // LANES):
            cols = slice(c * LANES, (c + 1) * LANES)
            h = _dft16([(z[0, t, rows, cols], z[1, t, rows, cols]) for t in range(n2)])
            for k2 in range(n2):
                hr, hi = h[k2]
                if k2 > 0:
                    hr, hi = _cmul(hr, hi, tw_ref[0, k2, rows, :], tw_ref[1, k2, rows, :])
                z[0, k2, rows, cols] = hr
                z[1, k2, rows, cols] = hi
        return carry

    lax.fori_loop(0, n1 // 8, tile_body, 0)

    for k2 in range(n2):
        rhs = jnp.concatenate([z[0, k2], z[1, k2]], axis=0).astype(BF16)
        y = jnp.dot(cst_ref[...], rhs, preferred_element_type=F32) * scale
        for c in range(gw // LANES):
            o_scr[c, pl.ds(k2, n1, stride=n2), :] = y[:, c * LANES:(c + 1) * LANES]
    o_ref[...] = jnp.concatenate([o_scr[c] for c in range(gw // LANES)], axis=1).astype(BF16)


def _fourier_fast(p, col_tile, t, batch):
    n1, n2 = DFT_N1, DFT_N2
    assert t == n1 * n2
    wf = COL_TILE
    gw = wf // FFT_GROUPS
    ang = lambda a, b, n: ((a[:, None] * b[None, :]) % n).astype(F32) * (2.0 * math.pi / n)
    kc = jnp.arange(gw, dtype=jnp.int32)
    a_c = ang(kc, kc, gw)
    csc = jnp.concatenate([jnp.cos(a_c), -jnp.sin(a_c)], axis=1).astype(BF16)
    k1 = jnp.arange(n1, dtype=jnp.int32)
    a_t = ang(k1, k1, n1)
    cst = jnp.concatenate([jnp.cos(a_t), jnp.sin(a_t)], axis=1).astype(BF16)
    a_w = ang(jnp.arange(n2, dtype=jnp.int32), k1, t)
    tw = jnp.stack([jnp.cos(a_w), -jnp.sin(a_w)], axis=0)
    tw = jnp.broadcast_to(tw[..., None], (2, n2, n1, LANES))
    ngrp = wf // gw
    return pl.pallas_call(
        functools.partial(_fft_fast_kernel, n1=n1, n2=n2, gw=gw, scale=1.0 / math.sqrt(t * gw)),
        grid=(batch, ngrp),
        in_specs=[
            pl.BlockSpec((t, gw), lambda b, g: (b, col_tile * ngrp + g)),
            pl.BlockSpec((gw, 2 * gw), lambda b, g: (0, 0)),
            pl.BlockSpec((2, n2, n1, LANES), lambda b, g: (0, 0, 0, 0), pipeline_mode=pl.Buffered(1)),
            pl.BlockSpec((n1, 2 * n1), lambda b, g: (0, 0)),
        ],
        out_specs=pl.BlockSpec((t, gw), lambda b, g: (b, g)),
        out_shape=jax.ShapeDtypeStruct((batch * t, wf), BF16),
        scratch_shapes=[
            pltpu.VMEM((2, n2, n1, gw), F32),
            pltpu.VMEM((gw // LANES, t, LANES), F32),
        ],
        compiler_params=_cparams("arbitrary", "arbitrary"),
        name="fourier_fast",
    )(p, csc, tw, cst)


def _branch_kernel(conv_ref, uz_ref, zc_ref, g01_ref, g2_ref, ys_ref, yf_ref, cw_ref, sd_ref, wa_ref, wb_ref,
                   fw_ref, pa_ref, pb_ref, pc_ref, m_ref, *, row_len):
    tm = conv_ref.shape[0]
    tn = zc_ref.shape[1]
    d = g2_ref.shape[1]
    xa_ref, ba_ref, ca_ref, za_ref = (conv_ref.at[:, k * tn:(k + 1) * tn] for k in range(4))
    u_ref, zb_ref = (uz_ref.at[:, k * tn:(k + 1) * tn] for k in range(2))
    g0_ref, g1_ref = (g01_ref.at[:, k * d:(k + 1) * d] for k in range(2))
    f = lambda r: r[...].astype(F32)
    v = f(ca_ref) * f(xa_ref)
    pos = lax.broadcasted_iota(jnp.int32, v.shape, 0) % row_len
    v_prev = jnp.where(pos == 0, 0.0, pltpu.roll(v, 1, 0))
    v_next = jnp.where(pos == row_len - 1, 0.0, pltpu.roll(v, tm - 1, 0))
    cw = cw_ref[0]
    conv = v_prev * cw[0:1, :] + v * cw[1:2, :] + v_next * cw[2:3, :]
    a = f(ba_ref) * conv * _silu(f(za_ref))
    ya = jnp.dot(a.astype(BF16), pa_ref[0], preferred_element_type=F32)
    acc = _sigmoid(f(g0_ref)) * ya
    y = _gelu_tanh(ys_ref[...] + sd_ref[0] * f(u_ref)).astype(BF16)
    glu = (jnp.dot(y, wa_ref[0], preferred_element_type=F32)
           * _sigmoid(jnp.dot(y, wb_ref[0], preferred_element_type=F32)) * _silu(f(zb_ref)))
    yb = jnp.dot(glu.astype(BF16), pb_ref[0], preferred_element_type=F32)
    acc = acc + _sigmoid(f(g1_ref)) * yb
    c = jnp.dot(yf_ref[...], fw_ref[0], preferred_element_type=F32) * _silu(f(zc_ref))
    yc = jnp.dot(c.astype(BF16), pc_ref[0], preferred_element_type=F32)
    acc = acc + _sigmoid(f(g2_ref)) * yc
    m_ref[...] = acc.astype(BF16)


def _branches(p, ys, yf, conv_w, ssm_d, wa, wb, fw, pa, pb, pc, layer, row_len):
    n = p.shape[0]
    wc = conv_w.shape[-1]
    d = pa.shape[-1]
    tm = TM_BRANCH
    tn = COL_TILE
    col = lambda k: pl.BlockSpec((tm, tn), lambda i: (i, k))
    const = lambda shape: pl.BlockSpec(shape, lambda i: (layer,) + (0,) * (len(shape) - 1),
                                       pipeline_mode=pl.Buffered(1))
    g_col0 = N_BRANCH_TILES * tn
    assert g_col0 % (2 * d) == 0 and U_TILE % 2 == 0
    return pl.pallas_call(
        functools.partial(_branch_kernel, row_len=row_len),
        grid=(n // tm,),
        in_specs=[
            pl.BlockSpec((tm, U_TILE * tn), lambda i: (i, 0)),
            pl.BlockSpec((tm, 2 * tn), lambda i: (i, U_TILE // 2)),
            col(7),
            pl.BlockSpec((tm, 2 * d), lambda i: (i, g_col0 // (2 * d))),
            pl.BlockSpec((tm, d), lambda i: (i, g_col0 // d + 2)),
            pl.BlockSpec((tm, tn), lambda i: (i, 0)),
            pl.BlockSpec((tm, tn), lambda i: (i, 0)),
            const((1, 3, wc)), const((1, 1, tn)),
            const((1, tn, tn)), const((1, tn, tn)), const((1, tn, tn)),
            const((1, wc, d)), const((1, tn, d)), const((1, tn, d)),
        ],
        out_specs=pl.BlockSpec((tm, d), lambda i: (i, 0)),
        out_shape=jax.ShapeDtypeStruct((n, d), BF16),
        compiler_params=_cparams("arbitrary"),
        name="branches",
    )(p, p, p, p, p, ys, yf, conv_w, ssm_d.reshape(ssm_d.shape[0], 1, -1),
      wa, wb, fw, pa, pb, pc)


def _out_kernel(*refs, with_next):
    if with_next:
        m_ref, x_ref, gt_ref, g_ref, w_ref, scn_ref, shn_ref, gn_ref, o_ref, h_ref, wb = refs
    else:
        m_ref, x_ref, gt_ref, g_ref, w_ref, o_ref, wb = refs

    @pl.when(pl.program_id(0) == 0)
    def _():
        wb[...] = w_ref[0].astype(BF16)

    o = jnp.dot(m_ref[...], wb[...], preferred_element_type=F32)
    ms = jnp.mean(o * o, axis=-1, keepdims=True)
    xn = x_ref[...] + o * lax.rsqrt(ms + RMS_EPS) * (g_ref[0] * gt_ref[0])
    o_ref[...] = xn
    if with_next:
        h_ref[...] = _modnorm(xn, gn_ref[0] * (1.0 + scn_ref[0]), shn_ref[0]).astype(BF16)


def _outproj(m, x2, mods_r, g_post, w_out, layer, rows_per_batch, ctx_row=None, g_pre_next=None):
    n, d = x2.shape
    tm = min(TM_OUT, rows_per_batch)
    tpb = rows_per_batch // tm
    in_specs = [
        pl.BlockSpec((tm, d), lambda i: (i, 0)),
        pl.BlockSpec((tm, d), lambda i: (i, 0)),
        _mod_spec(d, layer, MOD_GATE, tpb, ctx_row),
        pl.BlockSpec((1, 1, d), lambda i: (layer, 0, 0)),
        pl.BlockSpec((1, d, d), lambda i: (layer, 0, 0), pipeline_mode=pl.Buffered(1)),
    ]
    args = [m, x2, mods_r, g_post.reshape(g_post.shape[0], 1, d), w_out]
    out_specs = [pl.BlockSpec((tm, d), lambda i: (i, 0))]
    out_shape = [jax.ShapeDtypeStruct((n, d), F32)]
    if g_pre_next is not None:
        in_specs += [_mod_spec(d, layer + 1, MOD_SCALE, tpb, ctx_row), _mod_spec(d, layer + 1, MOD_SHIFT, tpb, ctx_row),
                     pl.BlockSpec((1, 1, d), lambda i: (layer + 1, 0, 0))]
        args += [mods_r, mods_r, g_pre_next.reshape(g_pre_next.shape[0], 1, d)]
        out_specs.append(pl.BlockSpec((tm, d), lambda i: (i, 0)))
        out_shape.append(jax.ShapeDtypeStruct((n, d), BF16))
    res = pl.pallas_call(
        functools.partial(_out_kernel, with_next=g_pre_next is not None),
        grid=(n // tm,),
        in_specs=in_specs,
        out_specs=out_specs,
        out_shape=out_shape,
        scratch_shapes=[pltpu.VMEM((d, d), BF16)],
        compiler_params=_cparams("arbitrary"),
        name="outproj",
    )(*args)
    return (res[0], res[1]) if g_pre_next is not None else (res[0], None)


def kernel(x, c, ctx, c_ctx, w_ada, b_ada, g_pre, g_post, w_in, conv_w, ssm_lam_re, ssm_lam_im, ssm_log_dt,
           ssm_b_re, ssm_b_im, ssm_c_re, ssm_c_im, ssm_d, glu_wa, glu_wb, fourier_w, proj_a, proj_b, proj_c,
           w_out):
    bn, t, d = x.shape
    lc = ctx.shape[1]
    depth = w_ada.shape[0]
    w_conv = conv_w.shape[-1]
    w_ssm = ssm_d.shape[-1]
    tn = COL_TILE
    assert w_conv == tn and w_ssm == tn and fourier_w.shape[-1] == tn and d % tn == 0
    assert bn + 1 <= MOD_ROWS and t % GRID_W == 0
    u_tile = 4 * w_conv // tn
    f_tile = u_tile + 2
    n_tiles = w_in.shape[-1] // tn
    assert u_tile == U_TILE and f_tile + 2 == N_BRANCH_TILES and (n_tiles - N_BRANCH_TILES) * tn == N_BRANCH * d

    cc = jnp.concatenate([c, c_ctx[None, :], jnp.zeros((MOD_ROWS - bn - 1, d), F32)], axis=0)
    mods = _ada_mods(cc, w_ada, b_ada)

    to_b = lambda w: w.astype(BF16)
    wa_b, wb_b, fw_b = to_b(glu_wa), to_b(glu_wb), to_b(fourier_w)
    pa_b, pb_b, pc_b = to_b(proj_a), to_b(proj_b), to_b(proj_c)
    fast_dft = t == DFT_N1 * DFT_N2
    tabs_x = None if fast_dft else _dft_tables(t, tn // FFT_GROUPS)
    tabs_c = _dft_tables(lc, tn // FFT_GROUPS)

    x2 = x.reshape(bn * t, d)
    c2 = ctx.reshape(bn * lc, d)
    mods_r = mods.reshape(depth * MOD_ROWS * 3, 1, d)
    ctx_row = bn
    ssm_p = _ssm_params(ssm_lam_re, ssm_lam_im, ssm_log_dt, ssm_b_re, ssm_b_im, ssm_c_re, ssm_c_im)
    hx = _prenorm(x2, mods_r, g_pre, 0, t)
    hc = _prenorm(c2, mods_r, g_pre, 0, lc, ctx_row)
    for l in range(depth):
        last = l == depth - 1
        g_pre_next = None if last else g_pre

        px, wq = _inproj(hx, w_in, l, 0, n_tiles, keep_weights=True)
        pc_ = _inproj(hc, wq, None, u_tile, 1) if last else _inproj(hc, wq, None, 0, n_tiles)
        u_lane_tile = u_tile * (tn // LANES)
        ys_x, ys_c = _ssm_scan(px, pc_, u_lane_tile, 0 if last else u_lane_tile, w_ssm, ssm_p, l, bn,
                               ctx_out=not last)

        yf_x = _fourier_fast(px, f_tile, t, bn) if fast_dft else _fourier(px, f_tile, t, bn, tabs_x)
        mx = _branches(px, ys_x, yf_x, conv_w, ssm_d, wa_b, wb_b, fw_b, pa_b, pb_b, pc_b, l, GRID_W)
        new_x2, hx = _outproj(mx, x2, mods_r, g_post, w_out, l, t, None, g_pre_next)
        if not last:
            yf_c = _fourier(pc_, f_tile, lc, bn, tabs_c)
            mc = _branches(pc_, ys_c, yf_c, conv_w, ssm_d, wa_b, wb_b, fw_b, pa_b, pb_b, pc_b, l, lc)
            c2, hc = _outproj(mc, c2, mods_r, g_post, w_out, l, lc, ctx_row, g_pre_next)
        x2 = new_x2
    return x2.reshape(bn, t, d)
```

```python
import functools
import math

import jax
import jax.numpy as jnp
from jax import lax
from jax.experimental import pallas as pl
from jax.experimental.pallas import tpu as pltpu

F32 = jnp.float32
BF16 = jnp.bfloat16

GRID_W = 64
FFT_GROUPS = 4
N_BRANCH = 3
RMS_EPS = 1e-6
LANES = 128
SSM_CHUNK = 8
SSM_SEGS = 8
DFT_ROWS = 64
DFT_N1, DFT_N2 = 256, 16
VMEM_LIMIT = 56 * 1024 * 1024
COL_TILE = 1024
N_BRANCH_TILES = 8
U_TILE = 4
TM_INPROJ, TM_PRENORM, TM_BRANCH, TM_OUT, TM_DFT = 2048, 1024, 256, 512, 256
TM_INPROJ_DOT = 1024


def _cparams(*sem):
    return pltpu.CompilerParams(dimension_semantics=sem, vmem_limit_bytes=VMEM_LIMIT)


def _sigmoid(v):
    return 0.5 * jnp.tanh(0.5 * v) + 0.5


def _silu(v):
    return v * _sigmoid(v)


def _gelu_tanh(v):
    return v * (0.5 * (1.0 + jnp.tanh(math.sqrt(2.0 / math.pi) * (v + 0.044715 * (v * v * v)))))


def _cmul(ar, ai, br, bi):
    return ar * br - ai * bi, ar * bi + ai * br


def _cmul_add(ar, ai, zr, zi, sr, si):
    return ar * zr - ai * zi + sr, ar * zi + ai * zr + si


def _ada_kernel(c_ref, w_ref, b_ref, o_ref):
    c = c_ref[...]
    s = _silu(c).astype(BF16)
    o_ref[0] = jnp.dot(s, w_ref[0].astype(BF16), preferred_element_type=F32) + b_ref[0]


def _ada_mods(cc, w_ada, b_ada):
    depth, d, n3 = w_ada.shape
    rows = cc.shape[0]
    tn = 2 * COL_TILE if n3 % (2 * COL_TILE) == 0 else COL_TILE
    assert n3 % tn == 0
    return pl.pallas_call(
        _ada_kernel,
        grid=(depth, n3 // tn),
        in_specs=[
            pl.BlockSpec((rows, d), lambda l, j: (0, 0)),
            pl.BlockSpec((1, d, tn), lambda l, j: (l, 0, j)),
            pl.BlockSpec((1, 1, tn), lambda l, j: (l, 0, j)),
        ],
        out_specs=pl.BlockSpec((1, rows, tn), lambda l, j: (l, 0, j)),
        out_shape=jax.ShapeDtypeStruct((depth, rows, n3), F32),
        compiler_params=_cparams("arbitrary", "arbitrary"),
        name="ada_mods",
    )(cc, w_ada, b_ada.reshape(depth, 1, n3))


def _modnorm(x, gs, sh):
    ms = jnp.mean(x * x, axis=-1, keepdims=True)
    return x * lax.rsqrt(ms + RMS_EPS) * gs + sh


def _prenorm_kernel(x_ref, sc_ref, sh_ref, g_ref, h_ref):
    rb = 16
    gs = g_ref[0] * (1.0 + sc_ref[0])
    sh = sh_ref[0]

    def body(r, carry):
        rows = pl.ds(pl.multiple_of(r * rb, rb), rb)
        h_ref[rows, :] = _modnorm(x_ref[rows, :], gs, sh).astype(BF16)
        return carry

    lax.fori_loop(0, x_ref.shape[0] // rb, body, 0, unroll=4)


MOD_SHIFT, MOD_SCALE, MOD_GATE = 0, 1, 2
MOD_ROWS = 8


def _mod_spec(d, layer, which, tiles_per_batch, ctx_row):
    def index(i):
        row = ctx_row if ctx_row is not None else i // tiles_per_batch
        return ((layer * MOD_ROWS + row) * 3 + which, 0, 0)
    return pl.BlockSpec((1, 1, d), index)


def _prenorm(x2, mods_r, g_pre, layer, rows_per_batch, ctx_row=None):
    n, d = x2.shape
    tm = min(TM_PRENORM, rows_per_batch)
    tpb = rows_per_batch // tm
    return pl.pallas_call(
        _prenorm_kernel,
        grid=(n // tm,),
        in_specs=[
            pl.BlockSpec((tm, d), lambda i: (i, 0)),
            _mod_spec(d, layer, MOD_SCALE, tpb, ctx_row),
            _mod_spec(d, layer, MOD_SHIFT, tpb, ctx_row),
            pl.BlockSpec((1, 1, d), lambda i: (layer, 0, 0)),
        ],
        out_specs=pl.BlockSpec((tm, d), lambda i: (i, 0)),
        out_shape=jax.ShapeDtypeStruct((n, d), BF16),
        compiler_params=_cparams("arbitrary"),
        name="prenorm",
    )(x2, mods_r, mods_r, g_pre.reshape(g_pre.shape[0], 1, d))


def _inproj_kernel(h_ref, w_ref, p_ref, *bf16_weights, cast):
    if cast:
        (wb,) = bf16_weights

        @pl.when(pl.program_id(1) == 0)
        def _():
            wb[...] = w_ref[0].astype(BF16)
    else:
        wb = w_ref

    sub = min(TM_INPROJ_DOT, h_ref.shape[0])
    for r in range(h_ref.shape[0] // sub):
        rows = slice(r * sub, (r + 1) * sub)
        p_ref[rows, :] = jnp.dot(h_ref[rows, :], wb[...], preferred_element_type=F32).astype(BF16)


def _inproj(h, w_in, layer, tile0, ntiles, keep_weights=False):
    n, d = h.shape
    tn = COL_TILE
    tm = min(TM_INPROJ, n)
    cast = layer is not None
    w_spec = (pl.BlockSpec((1, d, tn), lambda j, i: (layer, 0, tile0 + j)) if cast
              else pl.BlockSpec((d, tn), lambda j, i: (0, tile0 + j)))
    out_specs = [pl.BlockSpec((tm, tn), lambda j, i: (i, j))]
    out_shape = [jax.ShapeDtypeStruct((n, ntiles * tn), BF16)]
    scratch = []
    if keep_weights:
        out_specs.append(pl.BlockSpec((d, tn), lambda j, i: (0, j)))
        out_shape.append(jax.ShapeDtypeStruct((d, ntiles * tn), BF16))
    elif cast:
        scratch = [pltpu.VMEM((d, tn), BF16)]
    res = pl.pallas_call(
        functools.partial(_inproj_kernel, cast=cast),
        grid=(ntiles, n // tm),
        in_specs=[pl.BlockSpec((tm, d), lambda j, i: (i, 0)), w_spec],
        out_specs=out_specs,
        out_shape=out_shape,
        scratch_shapes=scratch,
        compiler_params=_cparams("arbitrary", "arbitrary"),
        name="inproj",
    )(h, w_in)
    return (res[0], res[1]) if keep_weights else res[0]


def _ssm_params(lam_re, lam_im, log_dt, b_re, b_im, c_re, c_im):
    nl, _, G, P = lam_re.shape
    H = b_re.shape[-1]
    gpt = LANES // H
    O = G // gpt
    lam_re = lam_re.astype(F32)
    lam_im = lam_im.astype(F32)
    dt = jnp.exp(log_dt.astype(F32))[..., None]
    lr = lam_re * dt
    li = lam_im * dt
    mag = jnp.exp(lr)
    a_re = mag * jnp.cos(li)
    a_im = mag * jnp.sin(li)
    n_re = a_re - 1.0
    n_im = a_im
    den = lam_re * lam_re + lam_im * lam_im
    q_re = (n_re * lam_re + n_im * lam_im) / den
    q_im = (n_im * lam_re - n_re * lam_im) / den
    bb_re = q_re[..., None] * b_re - q_im[..., None] * b_im
    bb_im = q_re[..., None] * b_im + q_im[..., None] * b_re

    bbs = jnp.stack([bb_re, bb_im], axis=2).reshape(nl, 2, 2, O, gpt, P, H)
    bc = jnp.transpose(bbs, (0, 3, 1, 2, 6, 4, 5)).reshape(nl, O, 4, H, gpt * P)
    cs = jnp.stack([c_re.astype(F32), c_im.astype(F32)], axis=2).reshape(nl, 2, 2, O, gpt, H, P)
    cc = jnp.transpose(cs, (0, 3, 1, 2, 5, 4, 6)).reshape(nl, O, 4, H, gpt * P)
    a4 = jnp.stack([a_re, a_im], axis=2).reshape(nl, 2, 2, O, gpt * P)
    a4 = jnp.transpose(a4, (0, 3, 1, 2, 4)).reshape(nl, O, 4, gpt * P)
    arow = jnp.concatenate([a4, jnp.zeros_like(a4)], axis=2)

    return bc, cc, arow


def _block_transpose8(v):
    bw = LANES // 8
    blk = lax.broadcasted_iota(jnp.int32, v[0].shape, 1) // bw
    for dist in (4, 2, 1):
        upper = (blk & dist) != 0
        nxt = list(v)
        for a in range(8):
            if a & dist:
                continue
            b = a + dist
            nxt[a] = jnp.where(upper, pltpu.roll(v[b], dist * bw, 1), v[a])
            nxt[b] = jnp.where(upper, v[b], pltpu.roll(v[a], LANES - dist * bw, 1))
        v = nxt
    return v


def _ssm_build_operators(bc_ref, cc_ref, arow_ref, ws, ws_lo, w3):
    L = SSM_CHUNK
    hh = bc_ref.shape[2]
    sw = bc_ref.shape[3]
    pp = sw // (LANES // hh)
    npair = sw // LANES
    assert L * hh == LANES and 2 * pp == LANES
    lane = lax.broadcasted_iota(jnp.int32, (LANES, LANES), 1)
    row = lax.broadcasted_iota(jnp.int32, (LANES, LANES), 0)
    lane_g0, row_g0 = lane < pp, row < pp
    lane_blk = lane // hh
    zeros = jnp.zeros((LANES, LANES), F32)
    zeros_b = jnp.zeros((LANES, LANES), BF16)

    def powers(re, im, n):
        out = [(jnp.ones_like(re), jnp.zeros_like(re))]
        for _ in range(n):
            out.append(_cmul(out[-1][0], out[-1][1], re, im))
        return out

    def split(v):
        hi = v.astype(BF16)
        return hi, (v - hi.astype(F32)).astype(BF16)

    def shift_rows(t, blocks):
        n = abs(blocks) * hh
        if n == 0:
            return t
        pad = jnp.zeros((n, LANES), F32)
        return jnp.concatenate([pad, t[:LANES - n]] if blocks > 0 else [t[n:], pad], axis=0)

    for q in range(npair):
        ql = slice(q * LANES, (q + 1) * LANES)
        taps = []
        for d in range(2):
            b_re = jnp.concatenate([bc_ref[0, 2 * d, :, ql]] * L, axis=0)
            b_im = jnp.concatenate([bc_ref[0, 2 * d + 1, :, ql]] * L, axis=0)
            prow = powers(arow_ref[0, 2 * d:2 * d + 1, ql], arow_ref[0, 2 * d + 1:2 * d + 2, ql], L)

            def power_rows(exps):
                return (jnp.concatenate([jnp.broadcast_to(prow[e][0], (hh, LANES)) for e in exps], axis=0),
                        jnp.concatenate([jnp.broadcast_to(prow[e][1], (hh, LANES)) for e in exps], axis=0))

            a_re, a_im = power_rows([L - 1 - j if d == 0 else j for j in range(L)])
            for part, v in zip((2 * d, 2 * d + 1), _cmul(b_re, b_im, a_re, a_im)):
                cols = slice(part * LANES, (part + 1) * LANES)
                for dst, piece in zip((ws, ws_lo), split(v)):
                    dst[q, 0:LANES, cols] = jnp.where(lane_g0, piece, zeros_b)
                    dst[q, LANES:2 * LANES, cols] = jnp.where(lane_g0, zeros_b, piece)
            ct_re = jnp.concatenate([cc_ref[0, 2 * d, :, ql]] * L, axis=0)
            ct_im = jnp.concatenate([cc_ref[0, 2 * d + 1, :, ql]] * L, axis=0)
            c_re, c_im = ct_re.T, ct_im.T
            (cr_hi, cr_lo), (ci_hi, ci_lo) = split(c_re), split(-c_im)
            c_hi = jnp.concatenate([cr_hi, ci_hi], axis=0)
            c_lo = jnp.concatenate([cr_lo, ci_lo], axis=0)
            dcols = slice(2 * d * LANES, (2 * d + 2) * LANES)
            taps.append(jnp.dot(jnp.concatenate([ws[q, :, dcols], ws[q, :, dcols], ws_lo[q, :, dcols]], axis=1),
                                jnp.concatenate([c_hi, c_lo, c_hi], axis=0), preferred_element_type=F32))
            cat_re, cat_im = _cmul(ct_re, ct_im, *power_rows([jo + 1 if d == 0 else L - jo for jo in range(L)]))
            for part, v in ((2 * d, cat_re.T), (2 * d + 1, -cat_im.T)):
                r0 = 2 * LANES + part * LANES
                w3[q, r0:r0 + LANES, 0:LANES] = jnp.where(row_g0, v, zeros).astype(BF16)
                w3[q, r0:r0 + LANES, LANES:2 * LANES] = jnp.where(row_g0, zeros, v).astype(BF16)
        for g2 in range(2):
            tf = taps[0][g2 * LANES:(g2 + 1) * LANES]
            tb = taps[1][g2 * LANES:(g2 + 1) * LANES]
            m = zeros
            for jo in range(L):
                m = jnp.where(lane_blk == jo, shift_rows(tf, jo - (L - 1)) + shift_rows(tb, jo), m)
            rows = slice(g2 * LANES, (g2 + 1) * LANES)
            w3[q, rows, g2 * LANES:(g2 + 1) * LANES] = m.astype(BF16)
            w3[q, rows, (1 - g2) * LANES:(2 - g2) * LANES] = zeros_b


def _ssm_build_powers(arow_ref, ap, pw, seg_len):
    L = SSM_CHUNK
    sw = arow_ref.shape[2]
    assert L & (L - 1) == 0 and seg_len & (seg_len - 1) == 0 and seg_len >= 8
    for d in range(2):
        cr, ci = slice(2 * d * sw, (2 * d + 1) * sw), slice((2 * d + 1) * sw, (2 * d + 2) * sw)
        re, im = arow_ref[0, 2 * d:2 * d + 1, :], arow_ref[0, 2 * d + 1:2 * d + 2, :]
        for _ in range(L.bit_length() - 1):
            re, im = _cmul(re, im, re, im)
        ap[0:1, cr] = re
        ap[0:1, ci] = im
        p = [(jnp.ones_like(re), jnp.zeros_like(re))]
        for _ in range(7):
            p.append(_cmul(p[-1][0], p[-1][1], re, im))
        order = list(range(8)) if d == 0 else list(range(7, -1, -1))
        first = slice(0, 8) if d == 0 else slice(seg_len - 8, seg_len)
        pw[first, cr] = jnp.concatenate([p[r][0] for r in order], axis=0)
        pw[first, ci] = jnp.concatenate([p[r][1] for r in order], axis=0)
        sq = _cmul(p[4][0], p[4][1], p[4][0], p[4][1])
        m = 8
        while m < seg_len:
            src = slice(0, m) if d == 0 else slice(seg_len - m, seg_len)
            dst = slice(m, 2 * m) if d == 0 else slice(seg_len - 2 * m, seg_len - m)
            xr, xi = _cmul(pw[src, cr], pw[src, ci], sq[0], sq[1])
            pw[dst, cr] = xr
            pw[dst, ci] = xi
            sq = _cmul(sq[0], sq[1], sq[0], sq[1])
            m *= 2
        ap[1:2, cr] = sq[0]
        ap[1:2, ci] = sq[1]


def _seg_pitch(seg_len):
    tiles = seg_len // 8 + 1
    return 8 * (tiles + 1 - tiles % 2)


def _ssm_kernel(*refs, nc, nx, seg_len, batch, ctx_out):
    if ctx_out:
        (ux_ref, uc_ref, bc_ref, cc_ref, arow_ref,
         yx_ref, yc_ref, ws, ws_lo, w3, lhs, st, uxf, ucf, ap, pw) = refs
    else:
        (ux_ref, uc_ref, bc_ref, cc_ref, arow_ref,
         yx_ref, ws, ws_lo, w3, lhs, st, uxf, ucf, ap, pw) = refs
        yc_ref = None
    L = SSM_CHUNK
    nlt = st.shape[0] // 4
    sw = nlt * LANES
    xw = 2 * LANES
    FR, FI, BR, BI = range(4)
    nseg = SSM_SEGS
    pitch = _seg_pitch(seg_len)
    t, lc = nx * L, nc * L

    _ssm_build_operators(bc_ref, cc_ref, arow_ref, ws, ws_lo, w3)
    _ssm_build_powers(arow_ref, ap, pw, seg_len)

    def ld(part, rows):
        return jnp.concatenate([st[part * nlt + q, rows, :] for q in range(nlt)], axis=1)

    def sto(part, rows, val):
        for q in range(nlt):
            st[part * nlt + q, rows, :] = val[:, q * LANES:(q + 1) * LANES]

    def part_cols(part):
        return slice(part * sw, (part + 1) * sw)

    def one_sequence(b, carry):
        ucf[...] = uc_ref[pl.ds(pl.multiple_of(b * lc, lc), lc), :].astype(F32)
        uxf[...] = ux_ref[pl.ds(pl.multiple_of(b * t, t), t), :].astype(F32)
        by_pos = [jnp.concatenate([ucf[pl.ds(j, nc, stride=L), :], uxf[pl.ds(j, nx, stride=L), :]], axis=0)
                  for j in range(L)]
        for g, tile_g in enumerate(_block_transpose8(by_pos)):
            lhs[g // 2, :, (g % 2) * LANES:(g % 2 + 1) * LANES] = tile_g.astype(BF16)

        for q in range(nlt):
            s = jnp.dot(lhs[q, :, 0:xw], ws[q], preferred_element_type=F32)
            for part in range(4):
                k = part * nlt + q
                st[k, 0:nc, :] = s[0:nc, part * LANES:(part + 1) * LANES]
                for g in range(nseg):
                    st[k, nc + g * pitch:nc + g * pitch + seg_len, :] = (
                        s[nc + g * seg_len:nc + (g + 1) * seg_len, part * LANES:(part + 1) * LANES])

        a_l = ap[0:1, :]
        a_seg = ap[1:2, :]
        al = [a_l[:, part_cols(p)] for p in range(4)]
        aseg = [a_seg[:, part_cols(p)] for p in range(4)]

        zero = jnp.zeros((1, sw), F32)

        def ctx_step(pr, pi):
            def step(i, carry):
                er, ei = carry
                row = pl.ds(i, 1)
                sr, si = ld(pr, row), ld(pi, row)
                sto(pr, row, er)
                sto(pi, row, ei)
                return _cmul_add(al[pr], al[pi], er, ei, sr, si)
            return step

        h0f = lax.fori_loop(0, nc, ctx_step(FR, FI), (zero, zero))
        bwd_step = ctx_step(BR, BI)
        h0b = lax.fori_loop(0, nc, lambda k, carry: bwd_step(nc - 1 - k, carry), (zero, zero))
        for part in range(4):
            for q in range(nlt):
                lhs[q, 0:nc, xw + part * LANES:xw + (part + 1) * LANES] = st[part * nlt + q, 0:nc, :].astype(BF16)

        zseg = jnp.zeros((nseg, sw), F32)

        def seg_pass(pr, pi, order):
            zr, zi = zseg, zseg
            for i in order:
                rows = pl.ds(nc + i, nseg, stride=pitch)
                sr, si = ld(pr, rows), ld(pi, rows)
                sto(pr, rows, zr)
                sto(pi, rows, zi)
                zr, zi = _cmul_add(al[pr], al[pi], zr, zi, sr, si)
            return zr, zi

        zfr, zfi = seg_pass(FR, FI, range(seg_len))
        zbr, zbi = seg_pass(BR, BI, range(seg_len - 1, -1, -1))

        ef = [h0f]
        for s in range(nseg - 1):
            ef.append(_cmul_add(aseg[FR], aseg[FI], ef[s][0], ef[s][1], zfr[s:s + 1], zfi[s:s + 1]))
        eb = [None] * nseg
        eb[nseg - 1] = h0b
        for s in range(nseg - 1, 0, -1):
            eb[s - 1] = _cmul_add(aseg[BR], aseg[BI], eb[s][0], eb[s][1], zbr[s:s + 1], zbi[s:s + 1])
        for s in range(nseg):
            srows = slice(nc + s * pitch, nc + s * pitch + seg_len)
            rows = slice(nc + s * seg_len, nc + (s + 1) * seg_len)
            for (pr, pi, e) in ((FR, FI, ef[s]), (BR, BI, eb[s])):
                xr, xi = _cmul_add(pw[:, part_cols(pr)], pw[:, part_cols(pi)], e[0], e[1],
                                   ld(pr, srows), ld(pi, srows))
                for q in range(nlt):
                    ql = slice(q * LANES, (q + 1) * LANES)
                    lhs[q, rows, xw + pr * LANES:xw + (pr + 1) * LANES] = xr[:, ql].astype(BF16)
                    lhs[q, rows, xw + pi * LANES:xw + (pi + 1) * LANES] = xi[:, ql].astype(BF16)

        by_group = []
        for q in range(nlt):
            y = jnp.dot(lhs[q], w3[q], preferred_element_type=F32)
            by_group += [y[:, 0:LANES], y[:, LANES:2 * LANES]]
        for j, tile_j in enumerate(_block_transpose8(by_group)):
            if ctx_out:
                yc_ref[pl.ds(b * lc + j, nc, stride=L), :] = tile_j[0:nc]
            yx_ref[pl.ds(b * t + j, nx, stride=L), :] = tile_j[nc:nc + nx]
        return carry

    lax.fori_loop(0, batch, one_sequence, 0)


def _ssm_scan(px, pc, col_x, col_c, w, params, layer, batch, ctx_out):
    bc, cc, arow = params
    L = SSM_CHUNK
    t = px.shape[0] // batch
    lc = pc.shape[0] // batch
    nx, nc = t // L, lc // L
    seg_len = nx // SSM_SEGS
    o = w // LANES
    sw = bc.shape[-1]
    sdim = 4 * sw
    npair = sw // LANES
    out_shape = [jax.ShapeDtypeStruct((batch * t, w), F32)]
    out_specs = [pl.BlockSpec((batch * t, LANES), lambda oi: (0, oi))]
    if ctx_out:
        out_shape.append(jax.ShapeDtypeStruct((batch * lc, w), F32))
        out_specs.append(pl.BlockSpec((batch * lc, LANES), lambda oi: (0, oi)))
    tile = lambda a: pl.BlockSpec((None, 1) + a.shape[2:], lambda oi: (layer, oi) + (0,) * (a.ndim - 2))
    res = pl.pallas_call(
        functools.partial(_ssm_kernel, nc=nc, nx=nx, seg_len=seg_len, batch=batch, ctx_out=ctx_out),
        grid=(o,),
        in_specs=[
            pl.BlockSpec((batch * t, LANES), lambda oi: (0, col_x + oi)),
            pl.BlockSpec((batch * lc, LANES), lambda oi: (0, col_c + oi)),
            tile(bc), tile(cc), tile(arow),
        ],
        out_specs=out_specs,
        out_shape=out_shape,
        scratch_shapes=[
            pltpu.VMEM((npair, 2 * LANES, 4 * LANES), BF16),
            pltpu.VMEM((npair, 2 * LANES, 4 * LANES), BF16),
            pltpu.VMEM((npair, 6 * LANES, 2 * LANES), BF16),
            pltpu.VMEM((npair, nc + nx, 6 * LANES), BF16),
            pltpu.VMEM((sdim // LANES, nc + SSM_SEGS * _seg_pitch(seg_len), LANES), F32),
            pltpu.VMEM((t, LANES), F32),
            pltpu.VMEM((lc, LANES), F32),
            pltpu.VMEM((8, sdim), F32),
            pltpu.VMEM((seg_len, sdim), F32),
        ],
        compiler_params=_cparams("arbitrary"),
        name="ssm_scan",
    )(px, pc, bc, cc, arow)
    return (res[0], res[1]) if ctx_out else (res[0], None)


def _dft_table_kernel(e1_ref, e2_ref, o_ref, *, t):
    e1c, e1s = e1_ref[0, :, 0:t], e1_ref[0, :, t:2 * t]
    e2c, e2s = e2_ref[:, 0:t], e2_ref[:, t:2 * t]
    c, s = _cmul(e2c, e2s, e1c, e1s)
    o_ref[:, 0:t] = c.astype(BF16)
    o_ref[:, t:2 * t] = (-s).astype(BF16)


def _dft_tables(t, gw):
    def cis(rows_mult, nrows):
        k = jnp.arange(nrows, dtype=jnp.int32)[:, None] * rows_mult
        n = jnp.arange(t, dtype=jnp.int32)[None, :]
        ang = ((k * n) % t).astype(F32) * (2.0 * math.pi / t)
        return jnp.concatenate([jnp.cos(ang), jnp.sin(ang)], axis=1)
    r = DFT_ROWS
    e1 = cis(r, t // r).reshape(t // r, 1, 2 * t)
    e2 = cis(1, r)
    tab_t = pl.pallas_call(
        functools.partial(_dft_table_kernel, t=t),
        grid=(t // r,),
        in_specs=[
            pl.BlockSpec((1, 1, 2 * t), lambda i: (i, 0, 0)),
            pl.BlockSpec((r, 2 * t), lambda i: (0, 0)),
        ],
        out_specs=pl.BlockSpec((r, 2 * t), lambda i: (i, 0)),
        out_shape=jax.ShapeDtypeStruct((t, 2 * t), BF16),
        compiler_params=_cparams("arbitrary"),
        name="dft_table",
    )(e1, e2)
    kc = jnp.arange(gw, dtype=jnp.int32)
    ang = ((kc[:, None] * kc[None, :]) % gw).astype(F32) * (2.0 * math.pi / gw)
    tab_c = jnp.concatenate([jnp.cos(ang), jnp.sin(ang)], axis=1).astype(BF16)
    return tab_t, tab_c


def _fft_kernel(f_ref, cs_ref, tab_ref, o_ref, data, *, t, gw, groups, scale):
    i = pl.program_id(1)

    @pl.when(i == 0)
    def _():
        rb = min(512, t)
        for r in range(t // rb):
            for g in range(groups):
                fg = f_ref[r * rb:(r + 1) * rb, g * gw:(g + 1) * gw]
                z = jnp.dot(fg, cs_ref[...], preferred_element_type=F32)
                data[r * rb:(r + 1) * rb, g * gw:(g + 1) * gw] = z[:, :gw].astype(BF16)
                data[t + r * rb:t + (r + 1) * rb, g * gw:(g + 1) * gw] = z[:, gw:].astype(BF16)

    y = jnp.dot(tab_ref[...], data[...], preferred_element_type=F32)
    o_ref[...] = (y * scale).astype(BF16)


def _fourier(p, col_tile, t, batch, tabs):
    tab_t, tab_c = tabs
    wf = COL_TILE
    gw = wf // FFT_GROUPS
    tm = min(TM_DFT, t)
    return pl.pallas_call(
        functools.partial(_fft_kernel, t=t, gw=gw, groups=FFT_GROUPS, scale=1.0 / math.sqrt(t * gw)),
        grid=(batch, t // tm),
        in_specs=[
            pl.BlockSpec((t, wf), lambda b, i: (b, col_tile)),
            pl.BlockSpec((gw, 2 * gw), lambda b, i: (0, 0)),
            pl.BlockSpec((tm, 2 * t), lambda b, i: (i, 0)),
        ],
        out_specs=pl.BlockSpec((tm, wf), lambda b, i: (b * (t // tm) + i, 0)),
        out_shape=jax.ShapeDtypeStruct((batch * t, wf), BF16),
        scratch_shapes=[pltpu.VMEM((2 * t, wf), BF16)],
        compiler_params=_cparams("arbitrary", "arbitrary"),
        name="fourier",
    )(p, tab_c, tab_t)


def _cmul_const(xr, xi, c, s):
    def scaled(v, k):
        if abs(k) < 1e-12:
            return None
        return v if abs(k - 1.0) < 1e-12 else -v if abs(k + 1.0) < 1e-12 else v * k

    def add(a, b):
        return b if a is None else a if b is None else a + b

    return add(scaled(xr, c), scaled(xi, -s)), add(scaled(xi, c), scaled(xr, s))


def _dft4(y):
    (ar, ai), (br, bi), (cr, ci), (dr, di) = y
    sr, si, tr, ti = ar + cr, ai + ci, ar - cr, ai - ci
    ur, ui, vr, vi = br + dr, bi + di, br - dr, bi - di
    return [(sr + ur, si + ui), (tr + vi, ti - vr), (sr - ur, si - ui), (tr - vi, ti + vr)]


def _dft16(x):
    out = [None] * 16
    p = [_dft4([x[4 * a + b] for a in range(4)]) for b in range(4)]
    for ka in range(4):
        q = []
        for b in range(4):
            ang = -2.0 * math.pi * ka * b / 16.0
            q.append(_cmul_const(p[b][ka][0], p[b][ka][1], math.cos(ang), math.sin(ang)))
        r = _dft4(q)
        for kb in range(4):
            out[ka + 4 * kb] = r[kb]
    return out


def _fft_fast_kernel(f_ref, csc_ref, tw_ref, cst_ref, o_ref, z, o_scr, tabs, *, n1, n2, gw, scale):
    @pl.when((pl.program_id(0) == 0) & (pl.program_id(1) == 0))
    def _():
        c, s = cst_ref[:, 0:n1], cst_ref[:, n1:2 * n1]
        for k2 in range(n2):
            twr, twi = tw_ref[0, k2:k2 + 1, :], tw_ref[1, k2:k2 + 1, :]
            tabs[k2, :, 0:n1] = (c * twr + s * twi).astype(BF16)
            tabs[k2, :, n1:2 * n1] = (s * twr - c * twi).astype(BF16)

    slabs_per_dot = 4
    for q in range(n2 // slabs_per_dot):
        r0 = q * slabs_per_dot * n1
        zz = jnp.dot(f_ref[r0:r0 + slabs_per_dot * n1, :], csc_ref[...], preferred_element_type=F32)
        for s in range(slabs_per_dot):
            z[0, q * slabs_per_dot + s] = zz[s * n1:(s + 1) * n1, 0:gw]
            z[1, q * slabs_per_dot + s] = zz[s * n1:(s + 1) * n1, gw:2 * gw]

    def tile_body(r, carry):
        rows = pl.ds(pl.multiple_of(r * 8, 8), 8)
        for c in range(gw // LANES):
            cols = slice(c * LANES, (c + 1) * LANES)
            h = _dft16([(z[0, t, rows, cols], z[1, t, rows, cols]) for t in range(n2)])
            for k2 in range(n2):
                z[0, k2, rows, cols] = h[k2][0]
                z[1, k2, rows, cols] = h[k2][1]
        return carry

    lax.fori_loop(0, n1 // 8, tile_body, 0)

    for k2 in range(n2):
        rhs = jnp.concatenate([z[0, k2], z[1, k2]], axis=0).astype(BF16)
        y = jnp.dot(tabs[k2], rhs, preferred_element_type=F32) * scale
        for c in range(gw // LANES):
            o_scr[c, pl.ds(k2, n1, stride=n2), :] = y[:, c * LANES:(c + 1) * LANES]
    o_ref[...] = jnp.concatenate([o_scr[c] for c in range(gw // LANES)], axis=1).astype(BF16)


def _fourier_fast(p, col_tile, t, batch):
    n1, n2 = DFT_N1, DFT_N2
    assert t == n1 * n2
    wf = COL_TILE
    gw = wf // FFT_GROUPS
    ang = lambda a, b, n: ((a[:, None] * b[None, :]) % n).astype(F32) * (2.0 * math.pi / n)
    kc = jnp.arange(gw, dtype=jnp.int32)
    a_c = ang(kc, kc, gw)
    csc = jnp.concatenate([jnp.cos(a_c), -jnp.sin(a_c)], axis=1).astype(BF16)
    k1 = jnp.arange(n1, dtype=jnp.int32)
    a_t = ang(k1, k1, n1)
    cst = jnp.concatenate([jnp.cos(a_t), jnp.sin(a_t)], axis=1)
    a_w = ang(jnp.arange(n2, dtype=jnp.int32), k1, t)
    tw = jnp.stack([jnp.cos(a_w), -jnp.sin(a_w)], axis=0)
    ngrp = wf // gw
    return pl.pallas_call(
        functools.partial(_fft_fast_kernel, n1=n1, n2=n2, gw=gw, scale=1.0 / math.sqrt(t * gw)),
        grid=(batch, ngrp),
        in_specs=[
            pl.BlockSpec((t, gw), lambda b, g: (b, col_tile * ngrp + g)),
            pl.BlockSpec((gw, 2 * gw), lambda b, g: (0, 0)),
            pl.BlockSpec((2, n2, n1), lambda b, g: (0, 0, 0)),
            pl.BlockSpec((n1, 2 * n1), lambda b, g: (0, 0)),
        ],
        out_specs=pl.BlockSpec((t, gw), lambda b, g: (b, g)),
        out_shape=jax.ShapeDtypeStruct((batch * t, wf), BF16),
        scratch_shapes=[
            pltpu.VMEM((2, n2, n1, gw), F32),
            pltpu.VMEM((gw // LANES, t, LANES), F32),
            pltpu.VMEM((n2, n1, 2 * n1), BF16),
        ],
        compiler_params=_cparams("arbitrary", "arbitrary"),
        name="fourier_fast",
    )(p, csc, tw, cst)


def _branch_kernel(conv_ref, uz_ref, zc_ref, g01_ref, g2_ref, ys_ref, yf_ref, cw_ref, sd_ref, wa_ref, wb_ref,
                   fw_ref, pa_ref, pb_ref, pc_ref, m_ref, *, row_len):
    tm = conv_ref.shape[0]
    tn = zc_ref.shape[1]
    d = g2_ref.shape[1]
    xa_ref, ba_ref, ca_ref, za_ref = (conv_ref.at[:, k * tn:(k + 1) * tn] for k in range(4))
    u_ref, zb_ref = (uz_ref.at[:, k * tn:(k + 1) * tn] for k in range(2))
    g0_ref, g1_ref = (g01_ref.at[:, k * d:(k + 1) * d] for k in range(2))
    f = lambda r: r[...].astype(F32)
    v = f(ca_ref) * f(xa_ref)
    pos = lax.broadcasted_iota(jnp.int32, v.shape, 0) % row_len
    v_prev = jnp.where(pos == 0, 0.0, pltpu.roll(v, 1, 0))
    v_next = jnp.where(pos == row_len - 1, 0.0, pltpu.roll(v, tm - 1, 0))
    cw = cw_ref[0]
    conv = v_prev * cw[0:1, :] + v * cw[1:2, :] + v_next * cw[2:3, :]
    a = f(ba_ref) * conv * _silu(f(za_ref))
    ya = jnp.dot(a.astype(BF16), pa_ref[0], preferred_element_type=F32)
    acc = _sigmoid(f(g0_ref)) * ya
    y = _gelu_tanh(ys_ref[...] + sd_ref[0] * f(u_ref)).astype(BF16)
    glu = (jnp.dot(y, wa_ref[0], preferred_element_type=F32)
           * _sigmoid(jnp.dot(y, wb_ref[0], preferred_element_type=F32)) * _silu(f(zb_ref)))
    yb = jnp.dot(glu.astype(BF16), pb_ref[0], preferred_element_type=F32)
    acc = acc + _sigmoid(f(g1_ref)) * yb
    c = jnp.dot(yf_ref[...], fw_ref[0], preferred_element_type=F32) * _silu(f(zc_ref))
    yc = jnp.dot(c.astype(BF16), pc_ref[0], preferred_element_type=F32)
    acc = acc + _sigmoid(f(g2_ref)) * yc
    m_ref[...] = acc.astype(BF16)


def _branches(p, ys, yf, conv_w, ssm_d, wa, wb, fw, pa, pb, pc, layer, row_len):
    n = p.shape[0]
    wc = conv_w.shape[-1]
    d = pa.shape[-1]
    tm = TM_BRANCH
    tn = COL_TILE
    col = lambda k: pl.BlockSpec((tm, tn), lambda i: (i, k))
    const = lambda shape: pl.BlockSpec(shape, lambda i: (layer,) + (0,) * (len(shape) - 1),
                                       pipeline_mode=pl.Buffered(1))
    g_col0 = N_BRANCH_TILES * tn
    assert g_col0 % (2 * d) == 0 and U_TILE % 2 == 0
    return pl.pallas_call(
        functools.partial(_branch_kernel, row_len=row_len),
        grid=(n // tm,),
        in_specs=[
            pl.BlockSpec((tm, U_TILE * tn), lambda i: (i, 0)),
            pl.BlockSpec((tm, 2 * tn), lambda i: (i, U_TILE // 2)),
            col(7),
            pl.BlockSpec((tm, 2 * d), lambda i: (i, g_col0 // (2 * d))),
            pl.BlockSpec((tm, d), lambda i: (i, g_col0 // d + 2)),
            pl.BlockSpec((tm, tn), lambda i: (i, 0)),
            pl.BlockSpec((tm, tn), lambda i: (i, 0)),
            const((1, 3, wc)), const((1, 1, tn)),
            const((1, tn, tn)), const((1, tn, tn)), const((1, tn, tn)),
            const((1, wc, d)), const((1, tn, d)), const((1, tn, d)),
        ],
        out_specs=pl.BlockSpec((tm, d), lambda i: (i, 0)),
        out_shape=jax.ShapeDtypeStruct((n, d), BF16),
        compiler_params=_cparams("arbitrary"),
        name="branches",
    )(p, p, p, p, p, ys, yf, conv_w, ssm_d.reshape(ssm_d.shape[0], 1, -1),
      wa, wb, fw, pa, pb, pc)


def _out_kernel(*refs, with_next):
    if with_next:
        m_ref, x_ref, gt_ref, g_ref, w_ref, scn_ref, shn_ref, gn_ref, o_ref, h_ref, wb = refs
    else:
        m_ref, x_ref, gt_ref, g_ref, w_ref, o_ref, wb = refs

    @pl.when(pl.program_id(0) == 0)
    def _():
        wb[...] = w_ref[0].astype(BF16)

    o = jnp.dot(m_ref[...], wb[...], preferred_element_type=F32)
    ms = jnp.mean(o * o, axis=-1, keepdims=True)
    xn = x_ref[...] + o * lax.rsqrt(ms + RMS_EPS) * (g_ref[0] * gt_ref[0])
    o_ref[...] = xn
    if with_next:
        h_ref[...] = _modnorm(xn, gn_ref[0] * (1.0 + scn_ref[0]), shn_ref[0]).astype(BF16)


def _outproj(m, x2, mods_r, g_post, w_out, layer, rows_per_batch, ctx_row=None, g_pre_next=None):
    n, d = x2.shape
    tm = min(TM_OUT, rows_per_batch)
    tpb = rows_per_batch // tm
    in_specs = [
        pl.BlockSpec((tm, d), lambda i: (i, 0)),
        pl.BlockSpec((tm, d), lambda i: (i, 0)),
        _mod_spec(d, layer, MOD_GATE, tpb, ctx_row),
        pl.BlockSpec((1, 1, d), lambda i: (layer, 0, 0)),
        pl.BlockSpec((1, d, d), lambda i: (layer, 0, 0), pipeline_mode=pl.Buffered(1)),
    ]
    args = [m, x2, mods_r, g_post.reshape(g_post.shape[0], 1, d), w_out]
    out_specs = [pl.BlockSpec((tm, d), lambda i: (i, 0))]
    out_shape = [jax.ShapeDtypeStruct((n, d), F32)]
    if g_pre_next is not None:
        in_specs += [_mod_spec(d, layer + 1, MOD_SCALE, tpb, ctx_row), _mod_spec(d, layer + 1, MOD_SHIFT, tpb, ctx_row),
                     pl.BlockSpec((1, 1, d), lambda i: (layer + 1, 0, 0))]
        args += [mods_r, mods_r, g_pre_next.reshape(g_pre_next.shape[0], 1, d)]
        out_specs.append(pl.BlockSpec((tm, d), lambda i: (i, 0)))
        out_shape.append(jax.ShapeDtypeStruct((n, d), BF16))
    res = pl.pallas_call(
        functools.partial(_out_kernel, with_next=g_pre_next is not None),
        grid=(n // tm,),
        in_specs=in_specs,
        out_specs=out_specs,
        out_shape=out_shape,
        scratch_shapes=[pltpu.VMEM((d, d), BF16)],
        compiler_params=_cparams("arbitrary"),
        name="outproj",
    )(*args)
    return (res[0], res[1]) if g_pre_next is not None else (res[0], None)


def kernel(x, c, ctx, c_ctx, w_ada, b_ada, g_pre, g_post, w_in, conv_w, ssm_lam_re, ssm_lam_im, ssm_log_dt,
           ssm_b_re, ssm_b_im, ssm_c_re, ssm_c_im, ssm_d, glu_wa, glu_wb, fourier_w, proj_a, proj_b, proj_c,
           w_out):
    bn, t, d = x.shape
    lc = ctx.shape[1]
    depth = w_ada.shape[0]
    w_conv = conv_w.shape[-1]
    w_ssm = ssm_d.shape[-1]
    tn = COL_TILE
    assert w_conv == tn and w_ssm == tn and fourier_w.shape[-1] == tn and d % tn == 0
    assert bn + 1 <= MOD_ROWS and t % GRID_W == 0
    u_tile = 4 * w_conv // tn
    f_tile = u_tile + 2
    n_tiles = w_in.shape[-1] // tn
    assert u_tile == U_TILE and f_tile + 2 == N_BRANCH_TILES and (n_tiles - N_BRANCH_TILES) * tn == N_BRANCH * d

    cc = jnp.concatenate([c, c_ctx[None, :], jnp.zeros((MOD_ROWS - bn - 1, d), F32)], axis=0)
    mods = _ada_mods(cc, w_ada, b_ada)

    to_b = lambda w: w.astype(BF16)
    wa_b, wb_b, fw_b = to_b(glu_wa), to_b(glu_wb), to_b(fourier_w)
    pa_b, pb_b, pc_b = to_b(proj_a), to_b(proj_b), to_b(proj_c)
    fast_dft = t == DFT_N1 * DFT_N2
    tabs_x = None if fast_dft else _dft_tables(t, tn // FFT_GROUPS)
    tabs_c = _dft_tables(lc, tn // FFT_GROUPS)

    x2 = x.reshape(bn * t, d)
    c2 = ctx.reshape(bn * lc, d)
    mods_r = mods.reshape(depth * MOD_ROWS * 3, 1, d)
    ctx_row = bn
    ssm_p = _ssm_params(ssm_lam_re, ssm_lam_im, ssm_log_dt, ssm_b_re, ssm_b_im, ssm_c_re, ssm_c_im)
    hx = _prenorm(x2, mods_r, g_pre, 0, t)
    hc = _prenorm(c2, mods_r, g_pre, 0, lc, ctx_row)
    for l in range(depth):
        last = l == depth - 1
        g_pre_next = None if last else g_pre

        px, wq = _inproj(hx, w_in, l, 0, n_tiles, keep_weights=True)
        pc_ = _inproj(hc, wq, None, u_tile, 1) if last else _inproj(hc, wq, None, 0, n_tiles)
        u_lane_tile = u_tile * (tn // LANES)
        ys_x, ys_c = _ssm_scan(px, pc_, u_lane_tile, 0 if last else u_lane_tile, w_ssm, ssm_p, l, bn,
                               ctx_out=not last)

        yf_x = _fourier_fast(px, f_tile, t, bn) if fast_dft else _fourier(px, f_tile, t, bn, tabs_x)
        mx = _branches(px, ys_x, yf_x, conv_w, ssm_d, wa_b, wb_b, fw_b, pa_b, pb_b, pc_b, l, GRID_W)
        new_x2, hx = _outproj(mx, x2, mods_r, g_post, w_out, l, t, None, g_pre_next)
        if not last:
            yf_c = _fourier(pc_, f_tile, lc, bn, tabs_c)
            mc = _branches(pc_, ys_c, yf_c, conv_w, ssm_d, wa_b, wb_b, fw_b, pa_b, pb_b, pc_b, l, lc)
            c2, hc = _outproj(mc, c2, mods_r, g_post, w_out, l, lc, ctx_row, g_pre_next)
        x2 = new_x2
    return x2.reshape(bn, t, d)
```

```python
import functools
import math

import jax
import jax.numpy as jnp
from jax import lax
from jax.experimental import pallas as pl
from jax.experimental.pallas import tpu as pltpu

F32 = jnp.float32
BF16 = jnp.bfloat16

GRID_W = 64
FFT_GROUPS = 4
N_BRANCH = 3
RMS_EPS = 1e-6
LANES = 128
SSM_CHUNK = 8
SSM_SEGS = 8
DFT_ROWS = 64
DFT_N1, DFT_N2 = 256, 16
VMEM_LIMIT = 56 * 1024 * 1024
VMEM_LIMIT_BRANCH = 61 * 1024 * 1024
COL_TILE = 1024
N_BRANCH_TILES = 8
U_TILE = 4
TM_INPROJ, TM_PRENORM, TM_BRANCH, TM_OUT, TM_DFT = 2048, 1024, 512, 512, 256
TM_INPROJ_DOT, TM_BRANCH_SUB = 1024, 256


def _cparams(*sem, vmem_limit=VMEM_LIMIT):
    return pltpu.CompilerParams(dimension_semantics=sem, vmem_limit_bytes=vmem_limit)


def _sigmoid(v):
    return 0.5 * jnp.tanh(0.5 * v) + 0.5


def _silu(v):
    return v * _sigmoid(v)


def _gelu_tanh(v):
    return v * (0.5 * (1.0 + jnp.tanh(math.sqrt(2.0 / math.pi) * (v + 0.044715 * (v * v * v)))))


def _cmul(ar, ai, br, bi):
    return ar * br - ai * bi, ar * bi + ai * br


def _cmul_add(ar, ai, zr, zi, sr, si):
    return ar * zr - ai * zi + sr, ar * zi + ai * zr + si


def _ada_kernel(c_ref, w_ref, b_ref, o_ref):
    c = c_ref[...]
    s = _silu(c).astype(BF16)
    o_ref[0] = jnp.dot(s, w_ref[0].astype(BF16), preferred_element_type=F32) + b_ref[0]


def _ada_mods(cc, w_ada, b_ada):
    depth, d, n3 = w_ada.shape
    rows = cc.shape[0]
    tn = 2 * COL_TILE if n3 % (2 * COL_TILE) == 0 else COL_TILE
    assert n3 % tn == 0
    return pl.pallas_call(
        _ada_kernel,
        grid=(depth, n3 // tn),
        in_specs=[
            pl.BlockSpec((rows, d), lambda l, j: (0, 0)),
            pl.BlockSpec((1, d, tn), lambda l, j: (l, 0, j)),
            pl.BlockSpec((1, 1, tn), lambda l, j: (l, 0, j)),
        ],
        out_specs=pl.BlockSpec((1, rows, tn), lambda l, j: (l, 0, j)),
        out_shape=jax.ShapeDtypeStruct((depth, rows, n3), F32),
        compiler_params=_cparams("arbitrary", "arbitrary"),
        name="ada_mods",
    )(cc, w_ada, b_ada.reshape(depth, 1, n3))


def _modnorm(x, gs, sh):
    ms = jnp.mean(x * x, axis=-1, keepdims=True)
    return x * lax.rsqrt(ms + RMS_EPS) * gs + sh


def _prenorm_kernel(x_ref, sc_ref, sh_ref, g_ref, h_ref):
    rb = 16
    gs = g_ref[0] * (1.0 + sc_ref[0])
    sh = sh_ref[0]

    def body(r, carry):
        rows = pl.ds(pl.multiple_of(r * rb, rb), rb)
        h_ref[rows, :] = _modnorm(x_ref[rows, :], gs, sh).astype(BF16)
        return carry

    lax.fori_loop(0, x_ref.shape[0] // rb, body, 0, unroll=4)


MOD_SHIFT, MOD_SCALE, MOD_GATE = 0, 1, 2
MOD_ROWS = 8


def _mod_spec(d, layer, which, tiles_per_batch, ctx_row):
    def index(i):
        row = ctx_row if ctx_row is not None else i // tiles_per_batch
        return ((layer * MOD_ROWS + row) * 3 + which, 0, 0)
    return pl.BlockSpec((1, 1, d), index)


def _prenorm(x2, mods_r, g_pre, layer, rows_per_batch, ctx_row=None):
    n, d = x2.shape
    tm = min(TM_PRENORM, rows_per_batch)
    tpb = rows_per_batch // tm
    return pl.pallas_call(
        _prenorm_kernel,
        grid=(n // tm,),
        in_specs=[
            pl.BlockSpec((tm, d), lambda i: (i, 0)),
            _mod_spec(d, layer, MOD_SCALE, tpb, ctx_row),
            _mod_spec(d, layer, MOD_SHIFT, tpb, ctx_row),
            pl.BlockSpec((1, 1, d), lambda i: (layer, 0, 0)),
        ],
        out_specs=pl.BlockSpec((tm, d), lambda i: (i, 0)),
        out_shape=jax.ShapeDtypeStruct((n, d), BF16),
        compiler_params=_cparams("arbitrary"),
        name="prenorm",
    )(x2, mods_r, mods_r, g_pre.reshape(g_pre.shape[0], 1, d))


def _inproj_kernel(h_ref, w_ref, p_ref, *bf16_weights, cast):
    if cast:
        (wb,) = bf16_weights

        @pl.when(pl.program_id(1) == 0)
        def _():
            wb[...] = w_ref[0].astype(BF16)
    else:
        wb = w_ref

    sub = min(TM_INPROJ_DOT, h_ref.shape[0])
    for r in range(h_ref.shape[0] // sub):
        rows = slice(r * sub, (r + 1) * sub)
        p_ref[rows, :] = jnp.dot(h_ref[rows, :], wb[...], preferred_element_type=F32).astype(BF16)


def _inproj(h, w_in, layer, tile0, ntiles, keep_weights=False):
    n, d = h.shape
    tn = COL_TILE
    tm = min(TM_INPROJ, n)
    cast = layer is not None
    w_spec = (pl.BlockSpec((1, d, tn), lambda j, i: (layer, 0, tile0 + j)) if cast
              else pl.BlockSpec((d, tn), lambda j, i: (0, tile0 + j)))
    out_specs = [pl.BlockSpec((tm, tn), lambda j, i: (i, j))]
    out_shape = [jax.ShapeDtypeStruct((n, ntiles * tn), BF16)]
    scratch = []
    if keep_weights:
        out_specs.append(pl.BlockSpec((d, tn), lambda j, i: (0, j)))
        out_shape.append(jax.ShapeDtypeStruct((d, ntiles * tn), BF16))
    elif cast:
        scratch = [pltpu.VMEM((d, tn), BF16)]
    res = pl.pallas_call(
        functools.partial(_inproj_kernel, cast=cast),
        grid=(ntiles, n // tm),
        in_specs=[pl.BlockSpec((tm, d), lambda j, i: (i, 0)), w_spec],
        out_specs=out_specs,
        out_shape=out_shape,
        scratch_shapes=scratch,
        compiler_params=_cparams("arbitrary", "arbitrary"),
        name="inproj",
    )(h, w_in)
    return (res[0], res[1]) if keep_weights else res[0]


def _ssm_params(lam_re, lam_im, log_dt, b_re, b_im, c_re, c_im):
    nl, _, G, P = lam_re.shape
    H = b_re.shape[-1]
    gpt = LANES // H
    O = G // gpt
    lam_re = lam_re.astype(F32)
    lam_im = lam_im.astype(F32)
    dt = jnp.exp(log_dt.astype(F32))[..., None]
    lr = lam_re * dt
    li = lam_im * dt
    mag = jnp.exp(lr)
    a_re = mag * jnp.cos(li)
    a_im = mag * jnp.sin(li)
    n_re = a_re - 1.0
    n_im = a_im
    den = lam_re * lam_re + lam_im * lam_im
    q_re = (n_re * lam_re + n_im * lam_im) / den
    q_im = (n_im * lam_re - n_re * lam_im) / den
    bb_re = q_re[..., None] * b_re - q_im[..., None] * b_im
    bb_im = q_re[..., None] * b_im + q_im[..., None] * b_re

    bbs = jnp.stack([bb_re, bb_im], axis=2).reshape(nl, 2, 2, O, gpt, P, H)
    bc = jnp.transpose(bbs, (0, 3, 1, 2, 6, 4, 5)).reshape(nl, O, 4, H, gpt * P)
    cs = jnp.stack([c_re.astype(F32), c_im.astype(F32)], axis=2).reshape(nl, 2, 2, O, gpt, H, P)
    cc = jnp.transpose(cs, (0, 3, 1, 2, 5, 4, 6)).reshape(nl, O, 4, H, gpt * P)
    a4 = jnp.stack([a_re, a_im], axis=2).reshape(nl, 2, 2, O, gpt * P)
    a4 = jnp.transpose(a4, (0, 3, 1, 2, 4)).reshape(nl, O, 4, gpt * P)
    arow = jnp.concatenate([a4, jnp.zeros_like(a4)], axis=2)

    return bc, cc, arow


def _block_transpose8(v):
    bw = LANES // 8
    blk = lax.broadcasted_iota(jnp.int32, v[0].shape, 1) // bw
    for dist in (4, 2, 1):
        upper = (blk & dist) != 0
        nxt = list(v)
        for a in range(8):
            if a & dist:
                continue
            b = a + dist
            nxt[a] = jnp.where(upper, pltpu.roll(v[b], dist * bw, 1), v[a])
            nxt[b] = jnp.where(upper, v[b], pltpu.roll(v[a], LANES - dist * bw, 1))
        v = nxt
    return v


def _ssm_build_operators(bc_ref, cc_ref, arow_ref, ws, ws_lo, w3):
    L = SSM_CHUNK
    hh = bc_ref.shape[2]
    sw = bc_ref.shape[3]
    pp = sw // (LANES // hh)
    npair = sw // LANES
    assert L * hh == LANES and 2 * pp == LANES
    lane = lax.broadcasted_iota(jnp.int32, (LANES, LANES), 1)
    row = lax.broadcasted_iota(jnp.int32, (LANES, LANES), 0)
    lane_g0, row_g0 = lane < pp, row < pp
    lane_blk = lane // hh
    zeros = jnp.zeros((LANES, LANES), F32)
    zeros_b = jnp.zeros((LANES, LANES), BF16)

    def powers(re, im, n):
        out = [(jnp.ones_like(re), jnp.zeros_like(re))]
        for _ in range(n):
            out.append(_cmul(out[-1][0], out[-1][1], re, im))
        return out

    def split(v):
        hi = v.astype(BF16)
        return hi, (v - hi.astype(F32)).astype(BF16)

    def shift_rows(t, blocks):
        n = abs(blocks) * hh
        if n == 0:
            return t
        pad = jnp.zeros((n, LANES), F32)
        return jnp.concatenate([pad, t[:LANES - n]] if blocks > 0 else [t[n:], pad], axis=0)

    for q in range(npair):
        ql = slice(q * LANES, (q + 1) * LANES)
        taps = []
        for d in range(2):
            b_re = jnp.concatenate([bc_ref[0, 2 * d, :, ql]] * L, axis=0)
            b_im = jnp.concatenate([bc_ref[0, 2 * d + 1, :, ql]] * L, axis=0)
            prow = powers(arow_ref[0, 2 * d:2 * d + 1, ql], arow_ref[0, 2 * d + 1:2 * d + 2, ql], L)

            def power_rows(exps):
                return (jnp.concatenate([jnp.broadcast_to(prow[e][0], (hh, LANES)) for e in exps], axis=0),
                        jnp.concatenate([jnp.broadcast_to(prow[e][1], (hh, LANES)) for e in exps], axis=0))

            a_re, a_im = power_rows([L - 1 - j if d == 0 else j for j in range(L)])
            for part, v in zip((2 * d, 2 * d + 1), _cmul(b_re, b_im, a_re, a_im)):
                cols = slice(part * LANES, (part + 1) * LANES)
                for dst, piece in zip((ws, ws_lo), split(v)):
                    dst[q, 0:LANES, cols] = jnp.where(lane_g0, piece, zeros_b)
                    dst[q, LANES:2 * LANES, cols] = jnp.where(lane_g0, zeros_b, piece)
            ct_re = jnp.concatenate([cc_ref[0, 2 * d, :, ql]] * L, axis=0)
            ct_im = jnp.concatenate([cc_ref[0, 2 * d + 1, :, ql]] * L, axis=0)
            c_re, c_im = ct_re.T, ct_im.T
            (cr_hi, cr_lo), (ci_hi, ci_lo) = split(c_re), split(-c_im)
            c_hi = jnp.concatenate([cr_hi, ci_hi], axis=0)
            c_lo = jnp.concatenate([cr_lo, ci_lo], axis=0)
            dcols = slice(2 * d * LANES, (2 * d + 2) * LANES)
            taps.append(jnp.dot(jnp.concatenate([ws[q, :, dcols], ws[q, :, dcols], ws_lo[q, :, dcols]], axis=1),
                                jnp.concatenate([c_hi, c_lo, c_hi], axis=0), preferred_element_type=F32))
            cat_re, cat_im = _cmul(ct_re, ct_im, *power_rows([jo + 1 if d == 0 else L - jo for jo in range(L)]))
            for part, v in ((2 * d, cat_re.T), (2 * d + 1, -cat_im.T)):
                r0 = 2 * LANES + part * LANES
                w3[q, r0:r0 + LANES, 0:LANES] = jnp.where(row_g0, v, zeros).astype(BF16)
                w3[q, r0:r0 + LANES, LANES:2 * LANES] = jnp.where(row_g0, zeros, v).astype(BF16)
        for g2 in range(2):
            tf = taps[0][g2 * LANES:(g2 + 1) * LANES]
            tb = taps[1][g2 * LANES:(g2 + 1) * LANES]
            m = zeros
            for jo in range(L):
                m = jnp.where(lane_blk == jo, shift_rows(tf, jo - (L - 1)) + shift_rows(tb, jo), m)
            rows = slice(g2 * LANES, (g2 + 1) * LANES)
            w3[q, rows, g2 * LANES:(g2 + 1) * LANES] = m.astype(BF16)
            w3[q, rows, (1 - g2) * LANES:(2 - g2) * LANES] = zeros_b


def _ssm_build_powers(arow_ref, ap, pw, seg_len):
    L = SSM_CHUNK
    sw = arow_ref.shape[2]
    assert L & (L - 1) == 0 and seg_len & (seg_len - 1) == 0 and seg_len >= 8
    for d in range(2):
        cr, ci = slice(2 * d * sw, (2 * d + 1) * sw), slice((2 * d + 1) * sw, (2 * d + 2) * sw)
        re, im = arow_ref[0, 2 * d:2 * d + 1, :], arow_ref[0, 2 * d + 1:2 * d + 2, :]
        for _ in range(L.bit_length() - 1):
            re, im = _cmul(re, im, re, im)
        ap[0:1, cr] = re
        ap[0:1, ci] = im
        p = [(jnp.ones_like(re), jnp.zeros_like(re))]
        for _ in range(7):
            p.append(_cmul(p[-1][0], p[-1][1], re, im))
        order = list(range(8)) if d == 0 else list(range(7, -1, -1))
        first = slice(0, 8) if d == 0 else slice(seg_len - 8, seg_len)
        pw[first, cr] = jnp.concatenate([p[r][0] for r in order], axis=0)
        pw[first, ci] = jnp.concatenate([p[r][1] for r in order], axis=0)
        sq = _cmul(p[4][0], p[4][1], p[4][0], p[4][1])
        m = 8
        while m < seg_len:
            src = slice(0, m) if d == 0 else slice(seg_len - m, seg_len)
            dst = slice(m, 2 * m) if d == 0 else slice(seg_len - 2 * m, seg_len - m)
            xr, xi = _cmul(pw[src, cr], pw[src, ci], sq[0], sq[1])
            pw[dst, cr] = xr
            pw[dst, ci] = xi
            sq = _cmul(sq[0], sq[1], sq[0], sq[1])
            m *= 2
        ap[1:2, cr] = sq[0]
        ap[1:2, ci] = sq[1]


def _seg_pitch(seg_len):
    tiles = seg_len // 8 + 1
    return 8 * (tiles + 1 - tiles % 2)


def _ssm_kernel(*refs, nc, nx, seg_len, batch, ctx_out):
    if ctx_out:
        (ux_ref, uc_ref, bc_ref, cc_ref, arow_ref,
         yx_ref, yc_ref, ws, ws_lo, w3, lhs, st, uxf, ucf, ap, pw) = refs
    else:
        (ux_ref, uc_ref, bc_ref, cc_ref, arow_ref,
         yx_ref, ws, ws_lo, w3, lhs, st, uxf, ucf, ap, pw) = refs
        yc_ref = None
    L = SSM_CHUNK
    nlt = st.shape[0] // 4
    sw = nlt * LANES
    xw = 2 * LANES
    FR, FI, BR, BI = range(4)
    nseg = SSM_SEGS
    pitch = _seg_pitch(seg_len)
    t, lc = nx * L, nc * L

    _ssm_build_operators(bc_ref, cc_ref, arow_ref, ws, ws_lo, w3)
    _ssm_build_powers(arow_ref, ap, pw, seg_len)

    def ld(part, rows):
        return jnp.concatenate([st[part * nlt + q, rows, :] for q in range(nlt)], axis=1)

    def sto(part, rows, val):
        for q in range(nlt):
            st[part * nlt + q, rows, :] = val[:, q * LANES:(q + 1) * LANES]

    def part_cols(part):
        return slice(part * sw, (part + 1) * sw)

    def one_sequence(b, carry):
        ucf[...] = uc_ref[pl.ds(pl.multiple_of(b * lc, lc), lc), :].astype(F32)
        uxf[...] = ux_ref[pl.ds(pl.multiple_of(b * t, t), t), :].astype(F32)
        by_pos = [jnp.concatenate([ucf[pl.ds(j, nc, stride=L), :], uxf[pl.ds(j, nx, stride=L), :]], axis=0)
                  for j in range(L)]
        for g, tile_g in enumerate(_block_transpose8(by_pos)):
            lhs[g // 2, :, (g % 2) * LANES:(g % 2 + 1) * LANES] = tile_g.astype(BF16)

        for q in range(nlt):
            s = jnp.dot(lhs[q, :, 0:xw], ws[q], preferred_element_type=F32)
            for part in range(4):
                k = part * nlt + q
                st[k, 0:nc, :] = s[0:nc, part * LANES:(part + 1) * LANES]
                for g in range(nseg):
                    st[k, nc + g * pitch:nc + g * pitch + seg_len, :] = (
                        s[nc + g * seg_len:nc + (g + 1) * seg_len, part * LANES:(part + 1) * LANES])

        a_l = ap[0:1, :]
        a_seg = ap[1:2, :]
        al = [a_l[:, part_cols(p)] for p in range(4)]
        aseg = [a_seg[:, part_cols(p)] for p in range(4)]

        zero = jnp.zeros((1, sw), F32)

        def ctx_step(pr, pi):
            def step(i, carry):
                er, ei = carry
                row = pl.ds(i, 1)
                sr, si = ld(pr, row), ld(pi, row)
                sto(pr, row, er)
                sto(pi, row, ei)
                return _cmul_add(al[pr], al[pi], er, ei, sr, si)
            return step

        h0f = lax.fori_loop(0, nc, ctx_step(FR, FI), (zero, zero))
        bwd_step = ctx_step(BR, BI)
        h0b = lax.fori_loop(0, nc, lambda k, carry: bwd_step(nc - 1 - k, carry), (zero, zero))
        for part in range(4):
            for q in range(nlt):
                lhs[q, 0:nc, xw + part * LANES:xw + (part + 1) * LANES] = st[part * nlt + q, 0:nc, :].astype(BF16)

        zseg = jnp.zeros((nseg, sw), F32)

        def seg_pass(pr, pi, order):
            zr, zi = zseg, zseg
            for i in order:
                rows = pl.ds(nc + i, nseg, stride=pitch)
                sr, si = ld(pr, rows), ld(pi, rows)
                sto(pr, rows, zr)
                sto(pi, rows, zi)
                zr, zi = _cmul_add(al[pr], al[pi], zr, zi, sr, si)
            return zr, zi

        zfr, zfi = seg_pass(FR, FI, range(seg_len))
        zbr, zbi = seg_pass(BR, BI, range(seg_len - 1, -1, -1))

        ef = [h0f]
        for s in range(nseg - 1):
            ef.append(_cmul_add(aseg[FR], aseg[FI], ef[s][0], ef[s][1], zfr[s:s + 1], zfi[s:s + 1]))
        eb = [None] * nseg
        eb[nseg - 1] = h0b
        for s in range(nseg - 1, 0, -1):
            eb[s - 1] = _cmul_add(aseg[BR], aseg[BI], eb[s][0], eb[s][1], zbr[s:s + 1], zbi[s:s + 1])
        for s in range(nseg):
            srows = slice(nc + s * pitch, nc + s * pitch + seg_len)
            rows = slice(nc + s * seg_len, nc + (s + 1) * seg_len)
            for (pr, pi, e) in ((FR, FI, ef[s]), (BR, BI, eb[s])):
                xr, xi = _cmul_add(pw[:, part_cols(pr)], pw[:, part_cols(pi)], e[0], e[1],
                                   ld(pr, srows), ld(pi, srows))
                for q in range(nlt):
                    ql = slice(q * LANES, (q + 1) * LANES)
                    lhs[q, rows, xw + pr * LANES:xw + (pr + 1) * LANES] = xr[:, ql].astype(BF16)
                    lhs[q, rows, xw + pi * LANES:xw + (pi + 1) * LANES] = xi[:, ql].astype(BF16)

        by_group = []
        for q in range(nlt):
            y = jnp.dot(lhs[q], w3[q], preferred_element_type=F32)
            by_group += [y[:, 0:LANES], y[:, LANES:2 * LANES]]
        for j, tile_j in enumerate(_block_transpose8(by_group)):
            if ctx_out:
                yc_ref[pl.ds(b * lc + j, nc, stride=L), :] = tile_j[0:nc]
            yx_ref[pl.ds(b * t + j, nx, stride=L), :] = tile_j[nc:nc + nx]
        return carry

    lax.fori_loop(0, batch, one_sequence, 0)


def _ssm_scan(px, pc, col_x, col_c, w, params, layer, batch, ctx_out):
    bc, cc, arow = params
    L = SSM_CHUNK
    t = px.shape[0] // batch
    lc = pc.shape[0] // batch
    nx, nc = t // L, lc // L
    seg_len = nx // SSM_SEGS
    o = w // LANES
    sw = bc.shape[-1]
    sdim = 4 * sw
    npair = sw // LANES
    out_shape = [jax.ShapeDtypeStruct((batch * t, w), F32)]
    out_specs = [pl.BlockSpec((batch * t, LANES), lambda oi: (0, oi))]
    if ctx_out:
        out_shape.append(jax.ShapeDtypeStruct((batch * lc, w), F32))
        out_specs.append(pl.BlockSpec((batch * lc, LANES), lambda oi: (0, oi)))
    tile = lambda a: pl.BlockSpec((None, 1) + a.shape[2:], lambda oi: (layer, oi) + (0,) * (a.ndim - 2))
    res = pl.pallas_call(
        functools.partial(_ssm_kernel, nc=nc, nx=nx, seg_len=seg_len, batch=batch, ctx_out=ctx_out),
        grid=(o,),
        in_specs=[
            pl.BlockSpec((batch * t, LANES), lambda oi: (0, col_x + oi)),
            pl.BlockSpec((batch * lc, LANES), lambda oi: (0, col_c + oi)),
            tile(bc), tile(cc), tile(arow),
        ],
        out_specs=out_specs,
        out_shape=out_shape,
        scratch_shapes=[
            pltpu.VMEM((npair, 2 * LANES, 4 * LANES), BF16),
            pltpu.VMEM((npair, 2 * LANES, 4 * LANES), BF16),
            pltpu.VMEM((npair, 6 * LANES, 2 * LANES), BF16),
            pltpu.VMEM((npair, nc + nx, 6 * LANES), BF16),
            pltpu.VMEM((sdim // LANES, nc + SSM_SEGS * _seg_pitch(seg_len), LANES), F32),
            pltpu.VMEM((t, LANES), F32),
            pltpu.VMEM((lc, LANES), F32),
            pltpu.VMEM((8, sdim), F32),
            pltpu.VMEM((seg_len, sdim), F32),
        ],
        compiler_params=_cparams("arbitrary"),
        name="ssm_scan",
    )(px, pc, bc, cc, arow)
    return (res[0], res[1]) if ctx_out else (res[0], None)


def _dft_table_kernel(e1_ref, e2_ref, o_ref, *, t):
    e1c, e1s = e1_ref[0, :, 0:t], e1_ref[0, :, t:2 * t]
    e2c, e2s = e2_ref[:, 0:t], e2_ref[:, t:2 * t]
    c, s = _cmul(e2c, e2s, e1c, e1s)
    o_ref[:, 0:t] = c.astype(BF16)
    o_ref[:, t:2 * t] = (-s).astype(BF16)


def _dft_tables(t, gw):
    def cis(rows_mult, nrows):
        k = jnp.arange(nrows, dtype=jnp.int32)[:, None] * rows_mult
        n = jnp.arange(t, dtype=jnp.int32)[None, :]
        ang = ((k * n) % t).astype(F32) * (2.0 * math.pi / t)
        return jnp.concatenate([jnp.cos(ang), jnp.sin(ang)], axis=1)
    r = DFT_ROWS
    e1 = cis(r, t // r).reshape(t // r, 1, 2 * t)
    e2 = cis(1, r)
    tab_t = pl.pallas_call(
        functools.partial(_dft_table_kernel, t=t),
        grid=(t // r,),
        in_specs=[
            pl.BlockSpec((1, 1, 2 * t), lambda i: (i, 0, 0)),
            pl.BlockSpec((r, 2 * t), lambda i: (0, 0)),
        ],
        out_specs=pl.BlockSpec((r, 2 * t), lambda i: (i, 0)),
        out_shape=jax.ShapeDtypeStruct((t, 2 * t), BF16),
        compiler_params=_cparams("arbitrary"),
        name="dft_table",
    )(e1, e2)
    kc = jnp.arange(gw, dtype=jnp.int32)
    ang = ((kc[:, None] * kc[None, :]) % gw).astype(F32) * (2.0 * math.pi / gw)
    tab_c = jnp.concatenate([jnp.cos(ang), jnp.sin(ang)], axis=1).astype(BF16)
    return tab_t, tab_c


def _fft_kernel(f_ref, cs_ref, tab_ref, o_ref, data, *, t, gw, groups, scale):
    i = pl.program_id(1)

    @pl.when(i == 0)
    def _():
        rb = min(512, t)
        for r in range(t // rb):
            for g in range(groups):
                fg = f_ref[r * rb:(r + 1) * rb, g * gw:(g + 1) * gw]
                z = jnp.dot(fg, cs_ref[...], preferred_element_type=F32)
                data[r * rb:(r + 1) * rb, g * gw:(g + 1) * gw] = z[:, :gw].astype(BF16)
                data[t + r * rb:t + (r + 1) * rb, g * gw:(g + 1) * gw] = z[:, gw:].astype(BF16)

    y = jnp.dot(tab_ref[...], data[...], preferred_element_type=F32)
    o_ref[...] = (y * scale).astype(BF16)


def _fourier(p, col_tile, t, batch, tabs):
    tab_t, tab_c = tabs
    wf = COL_TILE
    gw = wf // FFT_GROUPS
    tm = min(TM_DFT, t)
    return pl.pallas_call(
        functools.partial(_fft_kernel, t=t, gw=gw, groups=FFT_GROUPS, scale=1.0 / math.sqrt(t * gw)),
        grid=(batch, t // tm),
        in_specs=[
            pl.BlockSpec((t, wf), lambda b, i: (b, col_tile)),
            pl.BlockSpec((gw, 2 * gw), lambda b, i: (0, 0)),
            pl.BlockSpec((tm, 2 * t), lambda b, i: (i, 0)),
        ],
        out_specs=pl.BlockSpec((tm, wf), lambda b, i: (b * (t // tm) + i, 0)),
        out_shape=jax.ShapeDtypeStruct((batch * t, wf), BF16),
        scratch_shapes=[pltpu.VMEM((2 * t, wf), BF16)],
        compiler_params=_cparams("arbitrary", "arbitrary"),
        name="fourier",
    )(p, tab_c, tab_t)


def _cmul_const(xr, xi, c, s):
    def scaled(v, k):
        if abs(k) < 1e-12:
            return None
        return v if abs(k - 1.0) < 1e-12 else -v if abs(k + 1.0) < 1e-12 else v * k

    def add(a, b):
        return b if a is None else a if b is None else a + b

    return add(scaled(xr, c), scaled(xi, -s)), add(scaled(xi, c), scaled(xr, s))


def _dft4(y):
    (ar, ai), (br, bi), (cr, ci), (dr, di) = y
    sr, si, tr, ti = ar + cr, ai + ci, ar - cr, ai - ci
    ur, ui, vr, vi = br + dr, bi + di, br - dr, bi - di
    return [(sr + ur, si + ui), (tr + vi, ti - vr), (sr - ur, si - ui), (tr - vi, ti + vr)]


def _dft16(x):
    out = [None] * 16
    p = [_dft4([x[4 * a + b] for a in range(4)]) for b in range(4)]
    for ka in range(4):
        q = []
        for b in range(4):
            ang = -2.0 * math.pi * ka * b / 16.0
            q.append(_cmul_const(p[b][ka][0], p[b][ka][1], math.cos(ang), math.sin(ang)))
        r = _dft4(q)
        for kb in range(4):
            out[ka + 4 * kb] = r[kb]
    return out


def _fft_fast_kernel(f_ref, csc_ref, tw_ref, cst_ref, o_ref, z, o_scr, tabs, *, n1, n2, gw, scale):
    @pl.when((pl.program_id(0) == 0) & (pl.program_id(1) == 0))
    def _():
        c, s = cst_ref[:, 0:n1], cst_ref[:, n1:2 * n1]
        for k2 in range(n2):
            twr, twi = tw_ref[0, k2:k2 + 1, :], tw_ref[1, k2:k2 + 1, :]
            tabs[k2, :, 0:n1] = (c * twr + s * twi).astype(BF16)
            tabs[k2, :, n1:2 * n1] = (s * twr - c * twi).astype(BF16)

    slabs_per_dot = 4
    for q in range(n2 // slabs_per_dot):
        r0 = q * slabs_per_dot * n1
        zz = jnp.dot(f_ref[r0:r0 + slabs_per_dot * n1, :], csc_ref[...], preferred_element_type=F32)
        for s in range(slabs_per_dot):
            z[0, q * slabs_per_dot + s] = zz[s * n1:(s + 1) * n1, 0:gw]
            z[1, q * slabs_per_dot + s] = zz[s * n1:(s + 1) * n1, gw:2 * gw]

    def tile_body(r, carry):
        rows = pl.ds(pl.multiple_of(r * 8, 8), 8)
        for c in range(gw // LANES):
            cols = slice(c * LANES, (c + 1) * LANES)
            h = _dft16([(z[0, t, rows, cols], z[1, t, rows, cols]) for t in range(n2)])
            for k2 in range(n2):
                z[0, k2, rows, cols] = h[k2][0]
                z[1, k2, rows, cols] = h[k2][1]
        return carry

    lax.fori_loop(0, n1 // 8, tile_body, 0)

    for k2 in range(n2):
        rhs = jnp.concatenate([z[0, k2], z[1, k2]], axis=0).astype(BF16)
        y = jnp.dot(tabs[k2], rhs, preferred_element_type=F32) * scale
        for c in range(gw // LANES):
            o_scr[c, pl.ds(k2, n1, stride=n2), :] = y[:, c * LANES:(c + 1) * LANES]
    o_ref[...] = jnp.concatenate([o_scr[c] for c in range(gw // LANES)], axis=1).astype(BF16)


def _fourier_fast(p, col_tile, t, batch):
    n1, n2 = DFT_N1, DFT_N2
    assert t == n1 * n2
    wf = COL_TILE
    gw = wf // FFT_GROUPS
    ang = lambda a, b, n: ((a[:, None] * b[None, :]) % n).astype(F32) * (2.0 * math.pi / n)
    kc = jnp.arange(gw, dtype=jnp.int32)
    a_c = ang(kc, kc, gw)
    csc = jnp.concatenate([jnp.cos(a_c), -jnp.sin(a_c)], axis=1).astype(BF16)
    k1 = jnp.arange(n1, dtype=jnp.int32)
    a_t = ang(k1, k1, n1)
    cst = jnp.concatenate([jnp.cos(a_t), jnp.sin(a_t)], axis=1)
    a_w = ang(jnp.arange(n2, dtype=jnp.int32), k1, t)
    tw = jnp.stack([jnp.cos(a_w), -jnp.sin(a_w)], axis=0)
    ngrp = wf // gw
    return pl.pallas_call(
        functools.partial(_fft_fast_kernel, n1=n1, n2=n2, gw=gw, scale=1.0 / math.sqrt(t * gw)),
        grid=(batch, ngrp),
        in_specs=[
            pl.BlockSpec((t, gw), lambda b, g: (b, col_tile * ngrp + g)),
            pl.BlockSpec((gw, 2 * gw), lambda b, g: (0, 0)),
            pl.BlockSpec((2, n2, n1), lambda b, g: (0, 0, 0)),
            pl.BlockSpec((n1, 2 * n1), lambda b, g: (0, 0)),
        ],
        out_specs=pl.BlockSpec((t, gw), lambda b, g: (b, g)),
        out_shape=jax.ShapeDtypeStruct((batch * t, wf), BF16),
        scratch_shapes=[
            pltpu.VMEM((2, n2, n1, gw), F32),
            pltpu.VMEM((gw // LANES, t, LANES), F32),
            pltpu.VMEM((n2, n1, 2 * n1), BF16),
        ],
        compiler_params=_cparams("arbitrary", "arbitrary"),
        name="fourier_fast",
    )(p, csc, tw, cst)


def _branch_kernel(conv_ref, uz_ref, zc_ref, g01_ref, g2_ref, ys_ref, yf_ref, cw_ref, sd_ref, wa_ref, wb_ref,
                   fw_ref, pa_ref, pb_ref, pc_ref, m_ref, *, row_len):
    tn = zc_ref.shape[1]
    d = g2_ref.shape[1]
    tm = min(TM_BRANCH_SUB, conv_ref.shape[0])
    f = lambda r: r[...].astype(F32)
    for r in range(conv_ref.shape[0] // tm):
        rows = slice(r * tm, (r + 1) * tm)
        xa_ref, ba_ref, ca_ref, za_ref = (conv_ref.at[rows, k * tn:(k + 1) * tn] for k in range(4))
        u_ref, zb_ref = (uz_ref.at[rows, k * tn:(k + 1) * tn] for k in range(2))
        g0_ref, g1_ref = (g01_ref.at[rows, k * d:(k + 1) * d] for k in range(2))
        v = f(ca_ref) * f(xa_ref)
        pos = lax.broadcasted_iota(jnp.int32, v.shape, 0) % row_len
        v_prev = jnp.where(pos == 0, 0.0, pltpu.roll(v, 1, 0))
        v_next = jnp.where(pos == row_len - 1, 0.0, pltpu.roll(v, tm - 1, 0))
        cw = cw_ref[0]
        conv = v_prev * cw[0:1, :] + v * cw[1:2, :] + v_next * cw[2:3, :]
        a = f(ba_ref) * conv * _silu(f(za_ref))
        ya = jnp.dot(a.astype(BF16), pa_ref[0], preferred_element_type=F32)
        acc = _sigmoid(f(g0_ref)) * ya
        y = _gelu_tanh(ys_ref[rows, :] + sd_ref[0] * f(u_ref)).astype(BF16)
        glu = (jnp.dot(y, wa_ref[0], preferred_element_type=F32)
               * _sigmoid(jnp.dot(y, wb_ref[0], preferred_element_type=F32)) * _silu(f(zb_ref)))
        yb = jnp.dot(glu.astype(BF16), pb_ref[0], preferred_element_type=F32)
        acc = acc + _sigmoid(f(g1_ref)) * yb
        c = jnp.dot(yf_ref[rows, :], fw_ref[0], preferred_element_type=F32) * _silu(f(zc_ref.at[rows, :]))
        yc = jnp.dot(c.astype(BF16), pc_ref[0], preferred_element_type=F32)
        acc = acc + _sigmoid(f(g2_ref.at[rows, :])) * yc
        m_ref[rows, :] = acc.astype(BF16)


def _branches(p, ys, yf, conv_w, ssm_d, wa, wb, fw, pa, pb, pc, layer, row_len):
    n = p.shape[0]
    wc = conv_w.shape[-1]
    d = pa.shape[-1]
    tm = TM_BRANCH
    tn = COL_TILE
    col = lambda k: pl.BlockSpec((tm, tn), lambda i: (i, k))
    const = lambda shape: pl.BlockSpec(shape, lambda i: (layer,) + (0,) * (len(shape) - 1),
                                       pipeline_mode=pl.Buffered(1))
    g_col0 = N_BRANCH_TILES * tn
    assert g_col0 % (2 * d) == 0 and U_TILE % 2 == 0
    return pl.pallas_call(
        functools.partial(_branch_kernel, row_len=row_len),
        grid=(n // tm,),
        in_specs=[
            pl.BlockSpec((tm, U_TILE * tn), lambda i: (i, 0)),
            pl.BlockSpec((tm, 2 * tn), lambda i: (i, U_TILE // 2)),
            col(7),
            pl.BlockSpec((tm, 2 * d), lambda i: (i, g_col0 // (2 * d))),
            pl.BlockSpec((tm, d), lambda i: (i, g_col0 // d + 2)),
            pl.BlockSpec((tm, tn), lambda i: (i, 0)),
            pl.BlockSpec((tm, tn), lambda i: (i, 0)),
            const((1, 3, wc)), const((1, 1, tn)),
            const((1, tn, tn)), const((1, tn, tn)), const((1, tn, tn)),
            const((1, wc, d)), const((1, tn, d)), const((1, tn, d)),
        ],
        out_specs=pl.BlockSpec((tm, d), lambda i: (i, 0)),
        out_shape=jax.ShapeDtypeStruct((n, d), BF16),
        compiler_params=_cparams("arbitrary", vmem_limit=VMEM_LIMIT_BRANCH),
        name="branches",
    )(p, p, p, p, p, ys, yf, conv_w, ssm_d.reshape(ssm_d.shape[0], 1, -1),
      wa, wb, fw, pa, pb, pc)


def _out_kernel(*refs, with_next):
    if with_next:
        m_ref, x_ref, gt_ref, g_ref, w_ref, scn_ref, shn_ref, gn_ref, o_ref, h_ref, wb = refs
    else:
        m_ref, x_ref, gt_ref, g_ref, w_ref, o_ref, wb = refs

    @pl.when(pl.program_id(0) == 0)
    def _():
        wb[...] = w_ref[0].astype(BF16)

    o = jnp.dot(m_ref[...], wb[...], preferred_element_type=F32)
    ms = jnp.mean(o * o, axis=-1, keepdims=True)
    xn = x_ref[...] + o * lax.rsqrt(ms + RMS_EPS) * (g_ref[0] * gt_ref[0])
    o_ref[...] = xn
    if with_next:
        h_ref[...] = _modnorm(xn, gn_ref[0] * (1.0 + scn_ref[0]), shn_ref[0]).astype(BF16)


def _outproj(m, x2, mods_r, g_post, w_out, layer, rows_per_batch, ctx_row=None, g_pre_next=None):
    n, d = x2.shape
    tm = min(TM_OUT, rows_per_batch)
    tpb = rows_per_batch // tm
    in_specs = [
        pl.BlockSpec((tm, d), lambda i: (i, 0)),
        pl.BlockSpec((tm, d), lambda i: (i, 0)),
        _mod_spec(d, layer, MOD_GATE, tpb, ctx_row),
        pl.BlockSpec((1, 1, d), lambda i: (layer, 0, 0)),
        pl.BlockSpec((1, d, d), lambda i: (layer, 0, 0), pipeline_mode=pl.Buffered(1)),
    ]
    args = [m, x2, mods_r, g_post.reshape(g_post.shape[0], 1, d), w_out]
    out_specs = [pl.BlockSpec((tm, d), lambda i: (i, 0))]
    out_shape = [jax.ShapeDtypeStruct((n, d), F32)]
    if g_pre_next is not None:
        in_specs += [_mod_spec(d, layer + 1, MOD_SCALE, tpb, ctx_row), _mod_spec(d, layer + 1, MOD_SHIFT, tpb, ctx_row),
                     pl.BlockSpec((1, 1, d), lambda i: (layer + 1, 0, 0))]
        args += [mods_r, mods_r, g_pre_next.reshape(g_pre_next.shape[0], 1, d)]
        out_specs.append(pl.BlockSpec((tm, d), lambda i: (i, 0)))
        out_shape.append(jax.ShapeDtypeStruct((n, d), BF16))
    res = pl.pallas_call(
        functools.partial(_out_kernel, with_next=g_pre_next is not None),
        grid=(n // tm,),
        in_specs=in_specs,
        out_specs=out_specs,
        out_shape=out_shape,
        scratch_shapes=[pltpu.VMEM((d, d), BF16)],
        compiler_params=_cparams("arbitrary"),
        name="outproj",
    )(*args)
    return (res[0], res[1]) if g_pre_next is not None else (res[0], None)


def kernel(x, c, ctx, c_ctx, w_ada, b_ada, g_pre, g_post, w_in, conv_w, ssm_lam_re, ssm_lam_im, ssm_log_dt,
           ssm_b_re, ssm_b_im, ssm_c_re, ssm_c_im, ssm_d, glu_wa, glu_wb, fourier_w, proj_a, proj_b, proj_c,
           w_out):
    bn, t, d = x.shape
    lc = ctx.shape[1]
    depth = w_ada.shape[0]
    w_conv = conv_w.shape[-1]
    w_ssm = ssm_d.shape[-1]
    tn = COL_TILE
    assert w_conv == tn and w_ssm == tn and fourier_w.shape[-1] == tn and d % tn == 0
    assert bn + 1 <= MOD_ROWS and t % GRID_W == 0
    u_tile = 4 * w_conv // tn
    f_tile = u_tile + 2
    n_tiles = w_in.shape[-1] // tn
    assert u_tile == U_TILE and f_tile + 2 == N_BRANCH_TILES and (n_tiles - N_BRANCH_TILES) * tn == N_BRANCH * d

    cc = jnp.concatenate([c, c_ctx[None, :], jnp.zeros((MOD_ROWS - bn - 1, d), F32)], axis=0)
    mods = _ada_mods(cc, w_ada, b_ada)

    to_b = lambda w: w.astype(BF16)
    wa_b, wb_b, fw_b = to_b(glu_wa), to_b(glu_wb), to_b(fourier_w)
    pa_b, pb_b, pc_b = to_b(proj_a), to_b(proj_b), to_b(proj_c)
    fast_dft = t == DFT_N1 * DFT_N2
    tabs_x = None if fast_dft else _dft_tables(t, tn // FFT_GROUPS)
    tabs_c = _dft_tables(lc, tn // FFT_GROUPS)

    x2 = x.reshape(bn * t, d)
    c2 = ctx.reshape(bn * lc, d)
    mods_r = mods.reshape(depth * MOD_ROWS * 3, 1, d)
    ctx_row = bn
    ssm_p = _ssm_params(ssm_lam_re, ssm_lam_im, ssm_log_dt, ssm_b_re, ssm_b_im, ssm_c_re, ssm_c_im)
    hx = _prenorm(x2, mods_r, g_pre, 0, t)
    hc = _prenorm(c2, mods_r, g_pre, 0, lc, ctx_row)
    for l in range(depth):
        last = l == depth - 1
        g_pre_next = None if last else g_pre

        px, wq = _inproj(hx, w_in, l, 0, n_tiles, keep_weights=True)
        pc_ = _inproj(hc, wq, None, u_tile, 1) if last else _inproj(hc, wq, None, 0, n_tiles)
        u_lane_tile = u_tile * (tn // LANES)
        ys_x, ys_c = _ssm_scan(px, pc_, u_lane_tile, 0 if last else u_lane_tile, w_ssm, ssm_p, l, bn,
                               ctx_out=not last)

        yf_x = _fourier_fast(px, f_tile, t, bn) if fast_dft else _fourier(px, f_tile, t, bn, tabs_x)
        mx = _branches(px, ys_x, yf_x, conv_w, ssm_d, wa_b, wb_b, fw_b, pa_b, pb_b, pc_b, l, GRID_W)
        new_x2, hx = _outproj(mx, x2, mods_r, g_post, w_out, l, t, None, g_pre_next)
        if not last:
            yf_c = _fourier(pc_, f_tile, lc, bn, tabs_c)
            mc = _branches(pc_, ys_c, yf_c, conv_w, ssm_d, wa_b, wb_b, fw_b, pa_b, pb_b, pc_b, l, lc)
            c2, hc = _outproj(mc, c2, mods_r, g_post, w_out, l, lc, ctx_row, g_pre_next)
        x2 = new_x2
    return x2.reshape(bn, t, d)
```

```python
import functools
import math

import jax
import jax.numpy as jnp
from jax import lax
from jax.experimental import pallas as pl
from jax.experimental.pallas import tpu as pltpu

F32 = jnp.float32
BF16 = jnp.bfloat16

GRID_W = 64
FFT_GROUPS = 4
N_BRANCH = 3
RMS_EPS = 1e-6
LANES = 128
SSM_CHUNK = 8
SSM_SEGS = 8
DFT_ROWS = 64
DFT_N1, DFT_N2 = 256, 16
VMEM_LIMIT = 56 * 1024 * 1024
COL_TILE = 1024
N_BRANCH_TILES = 8
U_TILE = 4
TM_INPROJ, TM_PRENORM, TM_BRANCH, TM_OUT, TM_DFT = 2048, 2048, 256, 512, 256
TM_INPROJ_DOT = 1024


def _cparams(*sem):
    return pltpu.CompilerParams(dimension_semantics=sem, vmem_limit_bytes=VMEM_LIMIT)


def _sigmoid(v):
    return 0.5 * jnp.tanh(0.5 * v) + 0.5


def _silu(v):
    return v * _sigmoid(v)


def _gelu_tanh(v):
    return v * (0.5 * (1.0 + jnp.tanh(math.sqrt(2.0 / math.pi) * (v + 0.044715 * (v * v * v)))))


def _cmul(ar, ai, br, bi):
    return ar * br - ai * bi, ar * bi + ai * br


def _cmul_add(ar, ai, zr, zi, sr, si):
    return ar * zr - ai * zi + sr, ar * zi + ai * zr + si


def _ada_kernel(c_ref, w_ref, b_ref, o_ref):
    c = c_ref[...]
    s = _silu(c).astype(BF16)
    o_ref[0] = jnp.dot(s, w_ref[0].astype(BF16), preferred_element_type=F32) + b_ref[0]


def _ada_mods(cc, w_ada, b_ada):
    depth, d, n3 = w_ada.shape
    rows = cc.shape[0]
    tn = next(k * COL_TILE for k in (3, 2, 1) if n3 % (k * COL_TILE) == 0)
    assert n3 % tn == 0
    return pl.pallas_call(
        _ada_kernel,
        grid=(depth, n3 // tn),
        in_specs=[
            pl.BlockSpec((rows, d), lambda l, j: (0, 0)),
            pl.BlockSpec((1, d, tn), lambda l, j: (l, 0, j)),
            pl.BlockSpec((1, 1, tn), lambda l, j: (l, 0, j)),
        ],
        out_specs=pl.BlockSpec((1, rows, tn), lambda l, j: (l, 0, j)),
        out_shape=jax.ShapeDtypeStruct((depth, rows, n3), F32),
        compiler_params=_cparams("arbitrary", "arbitrary"),
        name="ada_mods",
    )(cc, w_ada, b_ada.reshape(depth, 1, n3))


def _modnorm(x, gs, sh):
    ms = jnp.mean(x * x, axis=-1, keepdims=True)
    return x * lax.rsqrt(ms + RMS_EPS) * gs + sh


def _prenorm_kernel(x_ref, sc_ref, sh_ref, g_ref, h_ref):
    rb = 16
    gs = g_ref[0] * (1.0 + sc_ref[0])
    sh = sh_ref[0]

    def body(r, carry):
        rows = pl.ds(pl.multiple_of(r * rb, rb), rb)
        h_ref[rows, :] = _modnorm(x_ref[rows, :], gs, sh).astype(BF16)
        return carry

    lax.fori_loop(0, x_ref.shape[0] // rb, body, 0, unroll=4)


MOD_SHIFT, MOD_SCALE, MOD_GATE = 0, 1, 2
MOD_ROWS = 8


def _mod_spec(d, layer, which, tiles_per_batch, ctx_row):
    def index(i):
        row = ctx_row if ctx_row is not None else i // tiles_per_batch
        return ((layer * MOD_ROWS + row) * 3 + which, 0, 0)
    return pl.BlockSpec((1, 1, d), index)


def _prenorm(x2, mods_r, g_pre, layer, rows_per_batch, ctx_row=None):
    n, d = x2.shape
    tm = min(TM_PRENORM, rows_per_batch)
    tpb = rows_per_batch // tm
    return pl.pallas_call(
        _prenorm_kernel,
        grid=(n // tm,),
        in_specs=[
            pl.BlockSpec((tm, d), lambda i: (i, 0)),
            _mod_spec(d, layer, MOD_SCALE, tpb, ctx_row),
            _mod_spec(d, layer, MOD_SHIFT, tpb, ctx_row),
            pl.BlockSpec((1, 1, d), lambda i: (layer, 0, 0)),
        ],
        out_specs=pl.BlockSpec((tm, d), lambda i: (i, 0)),
        out_shape=jax.ShapeDtypeStruct((n, d), BF16),
        compiler_params=_cparams("arbitrary"),
        name="prenorm",
    )(x2, mods_r, mods_r, g_pre.reshape(g_pre.shape[0], 1, d))


def _inproj_kernel(h_ref, w_ref, p_ref, *bf16_weights, cast):
    if cast:
        (wb,) = bf16_weights

        @pl.when(pl.program_id(1) == 0)
        def _():
            wb[...] = w_ref[0].astype(BF16)
    else:
        wb = w_ref

    sub = min(TM_INPROJ_DOT, h_ref.shape[0])
    for r in range(h_ref.shape[0] // sub):
        rows = slice(r * sub, (r + 1) * sub)
        p_ref[rows, :] = jnp.dot(h_ref[rows, :], wb[...], preferred_element_type=F32).astype(BF16)


def _inproj(h, w_in, layer, tile0, ntiles, keep_weights=False):
    n, d = h.shape
    tn = COL_TILE
    tm = min(TM_INPROJ, n)
    cast = layer is not None
    w_spec = (pl.BlockSpec((1, d, tn), lambda j, i: (layer, 0, tile0 + j)) if cast
              else pl.BlockSpec((d, tn), lambda j, i: (0, tile0 + j)))
    out_specs = [pl.BlockSpec((tm, tn), lambda j, i: (i, j))]
    out_shape = [jax.ShapeDtypeStruct((n, ntiles * tn), BF16)]
    scratch = []
    if keep_weights:
        out_specs.append(pl.BlockSpec((d, tn), lambda j, i: (0, j)))
        out_shape.append(jax.ShapeDtypeStruct((d, ntiles * tn), BF16))
    elif cast:
        scratch = [pltpu.VMEM((d, tn), BF16)]
    res = pl.pallas_call(
        functools.partial(_inproj_kernel, cast=cast),
        grid=(ntiles, n // tm),
        in_specs=[pl.BlockSpec((tm, d), lambda j, i: (i, 0)), w_spec],
        out_specs=out_specs,
        out_shape=out_shape,
        scratch_shapes=scratch,
        compiler_params=_cparams("arbitrary", "arbitrary"),
        name="inproj",
    )(h, w_in)
    return (res[0], res[1]) if keep_weights else res[0]


def _ssm_params(lam_re, lam_im, log_dt, b_re, b_im, c_re, c_im):
    nl, _, G, P = lam_re.shape
    H = b_re.shape[-1]
    gpt = LANES // H
    O = G // gpt
    lam_re = lam_re.astype(F32)
    lam_im = lam_im.astype(F32)
    dt = jnp.exp(log_dt.astype(F32))[..., None]
    lr = lam_re * dt
    li = lam_im * dt
    mag = jnp.exp(lr)
    a_re = mag * jnp.cos(li)
    a_im = mag * jnp.sin(li)
    n_re = a_re - 1.0
    n_im = a_im
    den = lam_re * lam_re + lam_im * lam_im
    q_re = (n_re * lam_re + n_im * lam_im) / den
    q_im = (n_im * lam_re - n_re * lam_im) / den
    bb_re = q_re[..., None] * b_re - q_im[..., None] * b_im
    bb_im = q_re[..., None] * b_im + q_im[..., None] * b_re

    bbs = jnp.stack([bb_re, bb_im], axis=2).reshape(nl, 2, 2, O, gpt, P, H)
    bc = jnp.transpose(bbs, (0, 3, 1, 2, 6, 4, 5)).reshape(nl, O, 4, H, gpt * P)
    cs = jnp.stack([c_re.astype(F32), c_im.astype(F32)], axis=2).reshape(nl, 2, 2, O, gpt, H, P)
    cc = jnp.transpose(cs, (0, 3, 1, 2, 5, 4, 6)).reshape(nl, O, 4, H, gpt * P)
    a4 = jnp.stack([a_re, a_im], axis=2).reshape(nl, 2, 2, O, gpt * P)
    a4 = jnp.transpose(a4, (0, 3, 1, 2, 4)).reshape(nl, O, 4, gpt * P)
    arow = jnp.concatenate([a4, jnp.zeros_like(a4)], axis=2)

    return bc, cc, arow


def _block_transpose8(v):
    bw = LANES // 8
    blk = lax.broadcasted_iota(jnp.int32, v[0].shape, 1) // bw
    for dist in (4, 2, 1):
        upper = (blk & dist) != 0
        nxt = list(v)
        for a in range(8):
            if a & dist:
                continue
            b = a + dist
            nxt[a] = jnp.where(upper, pltpu.roll(v[b], dist * bw, 1), v[a])
            nxt[b] = jnp.where(upper, v[b], pltpu.roll(v[a], LANES - dist * bw, 1))
        v = nxt
    return v


def _ssm_build_operators(bc_ref, cc_ref, arow_ref, ws, ws_lo, w3):
    L = SSM_CHUNK
    hh = bc_ref.shape[2]
    sw = bc_ref.shape[3]
    pp = sw // (LANES // hh)
    npair = sw // LANES
    assert L * hh == LANES and 2 * pp == LANES
    lane = lax.broadcasted_iota(jnp.int32, (LANES, LANES), 1)
    row = lax.broadcasted_iota(jnp.int32, (LANES, LANES), 0)
    lane_g0, row_g0 = lane < pp, row < pp
    lane_blk = lane // hh
    zeros = jnp.zeros((LANES, LANES), F32)
    zeros_b = jnp.zeros((LANES, LANES), BF16)

    def powers(re, im, n):
        out = [(jnp.ones_like(re), jnp.zeros_like(re))]
        for _ in range(n):
            out.append(_cmul(out[-1][0], out[-1][1], re, im))
        return out

    def split(v):
        hi = v.astype(BF16)
        return hi, (v - hi.astype(F32)).astype(BF16)

    def shift_rows(t, blocks):
        n = abs(blocks) * hh
        if n == 0:
            return t
        pad = jnp.zeros((n, LANES), F32)
        return jnp.concatenate([pad, t[:LANES - n]] if blocks > 0 else [t[n:], pad], axis=0)

    for q in range(npair):
        ql = slice(q * LANES, (q + 1) * LANES)
        taps = []
        for d in range(2):
            b_re = jnp.concatenate([bc_ref[0, 2 * d, :, ql]] * L, axis=0)
            b_im = jnp.concatenate([bc_ref[0, 2 * d + 1, :, ql]] * L, axis=0)
            prow = powers(arow_ref[0, 2 * d:2 * d + 1, ql], arow_ref[0, 2 * d + 1:2 * d + 2, ql], L)

            def power_rows(exps):
                return (jnp.concatenate([jnp.broadcast_to(prow[e][0], (hh, LANES)) for e in exps], axis=0),
                        jnp.concatenate([jnp.broadcast_to(prow[e][1], (hh, LANES)) for e in exps], axis=0))

            a_re, a_im = power_rows([L - 1 - j if d == 0 else j for j in range(L)])
            for part, v in zip((2 * d, 2 * d + 1), _cmul(b_re, b_im, a_re, a_im)):
                cols = slice(part * LANES, (part + 1) * LANES)
                for dst, piece in zip((ws, ws_lo), split(v)):
                    dst[q, 0:LANES, cols] = jnp.where(lane_g0, piece, zeros_b)
                    dst[q, LANES:2 * LANES, cols] = jnp.where(lane_g0, zeros_b, piece)
            ct_re = jnp.concatenate([cc_ref[0, 2 * d, :, ql]] * L, axis=0)
            ct_im = jnp.concatenate([cc_ref[0, 2 * d + 1, :, ql]] * L, axis=0)
            c_re, c_im = ct_re.T, ct_im.T
            (cr_hi, cr_lo), (ci_hi, ci_lo) = split(c_re), split(-c_im)
            c_hi = jnp.concatenate([cr_hi, ci_hi], axis=0)
            c_lo = jnp.concatenate([cr_lo, ci_lo], axis=0)
            dcols = slice(2 * d * LANES, (2 * d + 2) * LANES)
            taps.append(jnp.dot(jnp.concatenate([ws[q, :, dcols], ws[q, :, dcols], ws_lo[q, :, dcols]], axis=1),
                                jnp.concatenate([c_hi, c_lo, c_hi], axis=0), preferred_element_type=F32))
            cat_re, cat_im = _cmul(ct_re, ct_im, *power_rows([jo + 1 if d == 0 else L - jo for jo in range(L)]))
            for part, v in ((2 * d, cat_re.T), (2 * d + 1, -cat_im.T)):
                r0 = 2 * LANES + part * LANES
                w3[q, r0:r0 + LANES, 0:LANES] = jnp.where(row_g0, v, zeros).astype(BF16)
                w3[q, r0:r0 + LANES, LANES:2 * LANES] = jnp.where(row_g0, zeros, v).astype(BF16)
        for g2 in range(2):
            tf = taps[0][g2 * LANES:(g2 + 1) * LANES]
            tb = taps[1][g2 * LANES:(g2 + 1) * LANES]
            m = zeros
            for jo in range(L):
                m = jnp.where(lane_blk == jo, shift_rows(tf, jo - (L - 1)) + shift_rows(tb, jo), m)
            rows = slice(g2 * LANES, (g2 + 1) * LANES)
            w3[q, rows, g2 * LANES:(g2 + 1) * LANES] = m.astype(BF16)
            w3[q, rows, (1 - g2) * LANES:(2 - g2) * LANES] = zeros_b


def _ssm_build_powers(arow_ref, ap, pw, seg_len):
    L = SSM_CHUNK
    sw = arow_ref.shape[2]
    assert L & (L - 1) == 0 and seg_len & (seg_len - 1) == 0 and seg_len >= 8
    for d in range(2):
        cr, ci = slice(2 * d * sw, (2 * d + 1) * sw), slice((2 * d + 1) * sw, (2 * d + 2) * sw)
        re, im = arow_ref[0, 2 * d:2 * d + 1, :], arow_ref[0, 2 * d + 1:2 * d + 2, :]
        for _ in range(L.bit_length() - 1):
            re, im = _cmul(re, im, re, im)
        ap[0:1, cr] = re
        ap[0:1, ci] = im
        p = [(jnp.ones_like(re), jnp.zeros_like(re))]
        for _ in range(7):
            p.append(_cmul(p[-1][0], p[-1][1], re, im))
        order = list(range(8)) if d == 0 else list(range(7, -1, -1))
        first = slice(0, 8) if d == 0 else slice(seg_len - 8, seg_len)
        pw[first, cr] = jnp.concatenate([p[r][0] for r in order], axis=0)
        pw[first, ci] = jnp.concatenate([p[r][1] for r in order], axis=0)
        sq = _cmul(p[4][0], p[4][1], p[4][0], p[4][1])
        m = 8
        while m < seg_len:
            src = slice(0, m) if d == 0 else slice(seg_len - m, seg_len)
            dst = slice(m, 2 * m) if d == 0 else slice(seg_len - 2 * m, seg_len - m)
            xr, xi = _cmul(pw[src, cr], pw[src, ci], sq[0], sq[1])
            pw[dst, cr] = xr
            pw[dst, ci] = xi
            sq = _cmul(sq[0], sq[1], sq[0], sq[1])
            m *= 2
        ap[1:2, cr] = sq[0]
        ap[1:2, ci] = sq[1]


def _seg_pitch(seg_len):
    tiles = seg_len // 8 + 1
    return 8 * (tiles + 1 - tiles % 2)


def _ssm_kernel(*refs, nc, nx, seg_len, batch, ctx_out):
    if ctx_out:
        (ux_ref, uc_ref, bc_ref, cc_ref, arow_ref,
         yx_ref, yc_ref, ws, ws_lo, w3, lhs, st, uxf, ucf, ap, pw) = refs
    else:
        (ux_ref, uc_ref, bc_ref, cc_ref, arow_ref,
         yx_ref, ws, ws_lo, w3, lhs, st, uxf, ucf, ap, pw) = refs
        yc_ref = None
    L = SSM_CHUNK
    nlt = st.shape[0] // 4
    sw = nlt * LANES
    xw = 2 * LANES
    FR, FI, BR, BI = range(4)
    nseg = SSM_SEGS
    pitch = _seg_pitch(seg_len)
    t, lc = nx * L, nc * L

    _ssm_build_operators(bc_ref, cc_ref, arow_ref, ws, ws_lo, w3)
    _ssm_build_powers(arow_ref, ap, pw, seg_len)

    def ld(part, rows):
        return jnp.concatenate([st[part * nlt + q, rows, :] for q in range(nlt)], axis=1)

    def sto(part, rows, val):
        for q in range(nlt):
            st[part * nlt + q, rows, :] = val[:, q * LANES:(q + 1) * LANES]

    def part_cols(part):
        return slice(part * sw, (part + 1) * sw)

    def one_sequence(b, carry):
        ucf[...] = uc_ref[pl.ds(pl.multiple_of(b * lc, lc), lc), :].astype(F32)
        uxf[...] = ux_ref[pl.ds(pl.multiple_of(b * t, t), t), :].astype(F32)
        by_pos = [jnp.concatenate([ucf[pl.ds(j, nc, stride=L), :], uxf[pl.ds(j, nx, stride=L), :]], axis=0)
                  for j in range(L)]
        for g, tile_g in enumerate(_block_transpose8(by_pos)):
            lhs[g // 2, :, (g % 2) * LANES:(g % 2 + 1) * LANES] = tile_g.astype(BF16)

        for q in range(nlt):
            s = jnp.dot(lhs[q, :, 0:xw], ws[q], preferred_element_type=F32)
            for part in range(4):
                k = part * nlt + q
                st[k, 0:nc, :] = s[0:nc, part * LANES:(part + 1) * LANES]
                for g in range(nseg):
                    st[k, nc + g * pitch:nc + g * pitch + seg_len, :] = (
                        s[nc + g * seg_len:nc + (g + 1) * seg_len, part * LANES:(part + 1) * LANES])

        a_l = ap[0:1, :]
        a_seg = ap[1:2, :]
        al = [a_l[:, part_cols(p)] for p in range(4)]
        aseg = [a_seg[:, part_cols(p)] for p in range(4)]

        zero = jnp.zeros((1, sw), F32)

        def ctx_step(pr, pi):
            def step(i, carry):
                er, ei = carry
                row = pl.ds(i, 1)
                sr, si = ld(pr, row), ld(pi, row)
                sto(pr, row, er)
                sto(pi, row, ei)
                return _cmul_add(al[pr], al[pi], er, ei, sr, si)
            return step

        h0f = lax.fori_loop(0, nc, ctx_step(FR, FI), (zero, zero), unroll=4)
        bwd_step = ctx_step(BR, BI)
        h0b = lax.fori_loop(0, nc, lambda k, carry: bwd_step(nc - 1 - k, carry), (zero, zero), unroll=4)
        for part in range(4):
            for q in range(nlt):
                lhs[q, 0:nc, xw + part * LANES:xw + (part + 1) * LANES] = st[part * nlt + q, 0:nc, :].astype(BF16)

        zseg = jnp.zeros((nseg, sw), F32)

        def seg_pass(pr, pi, order):
            zr, zi = zseg, zseg
            for i in order:
                rows = pl.ds(nc + i, nseg, stride=pitch)
                sr, si = ld(pr, rows), ld(pi, rows)
                sto(pr, rows, zr)
                sto(pi, rows, zi)
                zr, zi = _cmul_add(al[pr], al[pi], zr, zi, sr, si)
            return zr, zi

        zfr, zfi = seg_pass(FR, FI, range(seg_len))
        zbr, zbi = seg_pass(BR, BI, range(seg_len - 1, -1, -1))

        ef = [h0f]
        for s in range(nseg - 1):
            ef.append(_cmul_add(aseg[FR], aseg[FI], ef[s][0], ef[s][1], zfr[s:s + 1], zfi[s:s + 1]))
        eb = [None] * nseg
        eb[nseg - 1] = h0b
        for s in range(nseg - 1, 0, -1):
            eb[s - 1] = _cmul_add(aseg[BR], aseg[BI], eb[s][0], eb[s][1], zbr[s:s + 1], zbi[s:s + 1])
        for s in range(nseg):
            srows = slice(nc + s * pitch, nc + s * pitch + seg_len)
            rows = slice(nc + s * seg_len, nc + (s + 1) * seg_len)
            for (pr, pi, e) in ((FR, FI, ef[s]), (BR, BI, eb[s])):
                xr, xi = _cmul_add(pw[:, part_cols(pr)], pw[:, part_cols(pi)], e[0], e[1],
                                   ld(pr, srows), ld(pi, srows))
                for q in range(nlt):
                    ql = slice(q * LANES, (q + 1) * LANES)
                    lhs[q, rows, xw + pr * LANES:xw + (pr + 1) * LANES] = xr[:, ql].astype(BF16)
                    lhs[q, rows, xw + pi * LANES:xw + (pi + 1) * LANES] = xi[:, ql].astype(BF16)

        by_group = []
        for q in range(nlt):
            y = jnp.dot(lhs[q], w3[q], preferred_element_type=F32)
            by_group += [y[:, 0:LANES], y[:, LANES:2 * LANES]]
        for j, tile_j in enumerate(_block_transpose8(by_group)):
            if ctx_out:
                yc_ref[pl.ds(b * lc + j, nc, stride=L), :] = tile_j[0:nc]
            yx_ref[pl.ds(b * t + j, nx, stride=L), :] = tile_j[nc:nc + nx]
        return carry

    lax.fori_loop(0, batch, one_sequence, 0)


def _ssm_scan(px, pc, col_x, col_c, w, params, layer, batch, ctx_out):
    bc, cc, arow = params
    L = SSM_CHUNK
    t = px.shape[0] // batch
    lc = pc.shape[0] // batch
    nx, nc = t // L, lc // L
    seg_len = nx // SSM_SEGS
    o = w // LANES
    sw = bc.shape[-1]
    sdim = 4 * sw
    npair = sw // LANES
    out_shape = [jax.ShapeDtypeStruct((batch * t, w), F32)]
    out_specs = [pl.BlockSpec((batch * t, LANES), lambda oi: (0, oi))]
    if ctx_out:
        out_shape.append(jax.ShapeDtypeStruct((batch * lc, w), F32))
        out_specs.append(pl.BlockSpec((batch * lc, LANES), lambda oi: (0, oi)))
    tile = lambda a: pl.BlockSpec((None, 1) + a.shape[2:], lambda oi: (layer, oi) + (0,) * (a.ndim - 2))
    res = pl.pallas_call(
        functools.partial(_ssm_kernel, nc=nc, nx=nx, seg_len=seg_len, batch=batch, ctx_out=ctx_out),
        grid=(o,),
        in_specs=[
            pl.BlockSpec((batch * t, LANES), lambda oi: (0, col_x + oi)),
            pl.BlockSpec((batch * lc, LANES), lambda oi: (0, col_c + oi)),
            tile(bc), tile(cc), tile(arow),
        ],
        out_specs=out_specs,
        out_shape=out_shape,
        scratch_shapes=[
            pltpu.VMEM((npair, 2 * LANES, 4 * LANES), BF16),
            pltpu.VMEM((npair, 2 * LANES, 4 * LANES), BF16),
            pltpu.VMEM((npair, 6 * LANES, 2 * LANES), BF16),
            pltpu.VMEM((npair, nc + nx, 6 * LANES), BF16),
            pltpu.VMEM((sdim // LANES, nc + SSM_SEGS * _seg_pitch(seg_len), LANES), F32),
            pltpu.VMEM((t, LANES), F32),
            pltpu.VMEM((lc, LANES), F32),
            pltpu.VMEM((8, sdim), F32),
            pltpu.VMEM((seg_len, sdim), F32),
        ],
        compiler_params=_cparams("arbitrary"),
        name="ssm_scan",
    )(px, pc, bc, cc, arow)
    return (res[0], res[1]) if ctx_out else (res[0], None)


def _dft_table_kernel(e1_ref, e2_ref, o_ref, *, t):
    e1c, e1s = e1_ref[0, :, 0:t], e1_ref[0, :, t:2 * t]
    e2c, e2s = e2_ref[:, 0:t], e2_ref[:, t:2 * t]
    c, s = _cmul(e2c, e2s, e1c, e1s)
    o_ref[:, 0:t] = c.astype(BF16)
    o_ref[:, t:2 * t] = (-s).astype(BF16)


def _dft_tables(t, gw):
    def cis(rows_mult, nrows):
        k = jnp.arange(nrows, dtype=jnp.int32)[:, None] * rows_mult
        n = jnp.arange(t, dtype=jnp.int32)[None, :]
        ang = ((k * n) % t).astype(F32) * (2.0 * math.pi / t)
        return jnp.concatenate([jnp.cos(ang), jnp.sin(ang)], axis=1)
    r = DFT_ROWS
    e1 = cis(r, t // r).reshape(t // r, 1, 2 * t)
    e2 = cis(1, r)
    tab_t = pl.pallas_call(
        functools.partial(_dft_table_kernel, t=t),
        grid=(t // r,),
        in_specs=[
            pl.BlockSpec((1, 1, 2 * t), lambda i: (i, 0, 0)),
            pl.BlockSpec((r, 2 * t), lambda i: (0, 0)),
        ],
        out_specs=pl.BlockSpec((r, 2 * t), lambda i: (i, 0)),
        out_shape=jax.ShapeDtypeStruct((t, 2 * t), BF16),
        compiler_params=_cparams("arbitrary"),
        name="dft_table",
    )(e1, e2)
    kc = jnp.arange(gw, dtype=jnp.int32)
    ang = ((kc[:, None] * kc[None, :]) % gw).astype(F32) * (2.0 * math.pi / gw)
    tab_c = jnp.concatenate([jnp.cos(ang), jnp.sin(ang)], axis=1).astype(BF16)
    return tab_t, tab_c


def _fft_kernel(f_ref, cs_ref, tab_ref, o_ref, data, *, t, gw, groups, scale):
    i = pl.program_id(1)

    @pl.when(i == 0)
    def _():
        rb = min(512, t)
        for r in range(t // rb):
            for g in range(groups):
                fg = f_ref[r * rb:(r + 1) * rb, g * gw:(g + 1) * gw]
                z = jnp.dot(fg, cs_ref[...], preferred_element_type=F32)
                data[r * rb:(r + 1) * rb, g * gw:(g + 1) * gw] = z[:, :gw].astype(BF16)
                data[t + r * rb:t + (r + 1) * rb, g * gw:(g + 1) * gw] = z[:, gw:].astype(BF16)

    y = jnp.dot(tab_ref[...], data[...], preferred_element_type=F32)
    o_ref[...] = (y * scale).astype(BF16)


def _fourier(p, col_tile, t, batch, tabs):
    tab_t, tab_c = tabs
    wf = COL_TILE
    gw = wf // FFT_GROUPS
    tm = min(TM_DFT, t)
    return pl.pallas_call(
        functools.partial(_fft_kernel, t=t, gw=gw, groups=FFT_GROUPS, scale=1.0 / math.sqrt(t * gw)),
        grid=(batch, t // tm),
        in_specs=[
            pl.BlockSpec((t, wf), lambda b, i: (b, col_tile)),
            pl.BlockSpec((gw, 2 * gw), lambda b, i: (0, 0)),
            pl.BlockSpec((tm, 2 * t), lambda b, i: (i, 0)),
        ],
        out_specs=pl.BlockSpec((tm, wf), lambda b, i: (b * (t // tm) + i, 0)),
        out_shape=jax.ShapeDtypeStruct((batch * t, wf), BF16),
        scratch_shapes=[pltpu.VMEM((2 * t, wf), BF16)],
        compiler_params=_cparams("arbitrary", "arbitrary"),
        name="fourier",
    )(p, tab_c, tab_t)


def _cmul_const(xr, xi, c, s):
    def scaled(v, k):
        if abs(k) < 1e-12:
            return None
        return v if abs(k - 1.0) < 1e-12 else -v if abs(k + 1.0) < 1e-12 else v * k

    def add(a, b):
        return b if a is None else a if b is None else a + b

    return add(scaled(xr, c), scaled(xi, -s)), add(scaled(xi, c), scaled(xr, s))


def _dft4(y):
    (ar, ai), (br, bi), (cr, ci), (dr, di) = y
    sr, si, tr, ti = ar + cr, ai + ci, ar - cr, ai - ci
    ur, ui, vr, vi = br + dr, bi + di, br - dr, bi - di
    return [(sr + ur, si + ui), (tr + vi, ti - vr), (sr - ur, si - ui), (tr - vi, ti + vr)]


def _dft16(x):
    out = [None] * 16
    p = [_dft4([x[4 * a + b] for a in range(4)]) for b in range(4)]
    for ka in range(4):
        q = []
        for b in range(4):
            ang = -2.0 * math.pi * ka * b / 16.0
            q.append(_cmul_const(p[b][ka][0], p[b][ka][1], math.cos(ang), math.sin(ang)))
        r = _dft4(q)
        for kb in range(4):
            out[ka + 4 * kb] = r[kb]
    return out


def _fft_fast_kernel(f_ref, csc_ref, tw_ref, cst_ref, o_ref, z, o_scr, tabs, *, n1, n2, gw, scale):
    @pl.when((pl.program_id(0) == 0) & (pl.program_id(1) == 0))
    def _():
        c, s = cst_ref[:, 0:n1], cst_ref[:, n1:2 * n1]
        for k2 in range(n2):
            twr, twi = tw_ref[0, k2:k2 + 1, :], tw_ref[1, k2:k2 + 1, :]
            tabs[k2, :, 0:n1] = (c * twr + s * twi).astype(BF16)
            tabs[k2, :, n1:2 * n1] = (s * twr - c * twi).astype(BF16)

    slabs_per_dot = 4
    for q in range(n2 // slabs_per_dot):
        r0 = q * slabs_per_dot * n1
        zz = jnp.dot(f_ref[r0:r0 + slabs_per_dot * n1, :], csc_ref[...], preferred_element_type=F32)
        for s in range(slabs_per_dot):
            z[0, q * slabs_per_dot + s] = zz[s * n1:(s + 1) * n1, 0:gw]
            z[1, q * slabs_per_dot + s] = zz[s * n1:(s + 1) * n1, gw:2 * gw]

    def tile_body(r, carry):
        rows = pl.ds(pl.multiple_of(r * 8, 8), 8)
        for c in range(gw // LANES):
            cols = slice(c * LANES, (c + 1) * LANES)
            h = _dft16([(z[0, t, rows, cols], z[1, t, rows, cols]) for t in range(n2)])
            for k2 in range(n2):
                z[0, k2, rows, cols] = h[k2][0]
                z[1, k2, rows, cols] = h[k2][1]
        return carry

    lax.fori_loop(0, n1 // 8, tile_body, 0)

    for k2 in range(n2):
        rhs = jnp.concatenate([z[0, k2], z[1, k2]], axis=0).astype(BF16)
        y = jnp.dot(tabs[k2], rhs, preferred_element_type=F32) * scale
        for c in range(gw // LANES):
            o_scr[c, pl.ds(k2, n1, stride=n2), :] = y[:, c * LANES:(c + 1) * LANES]
    o_ref[...] = jnp.concatenate([o_scr[c] for c in range(gw // LANES)], axis=1).astype(BF16)


def _fourier_fast(p, col_tile, t, batch):
    n1, n2 = DFT_N1, DFT_N2
    assert t == n1 * n2
    wf = COL_TILE
    gw = wf // FFT_GROUPS
    ang = lambda a, b, n: ((a[:, None] * b[None, :]) % n).astype(F32) * (2.0 * math.pi / n)
    kc = jnp.arange(gw, dtype=jnp.int32)
    a_c = ang(kc, kc, gw)
    csc = jnp.concatenate([jnp.cos(a_c), -jnp.sin(a_c)], axis=1).astype(BF16)
    k1 = jnp.arange(n1, dtype=jnp.int32)
    a_t = ang(k1, k1, n1)
    cst = jnp.concatenate([jnp.cos(a_t), jnp.sin(a_t)], axis=1)
    a_w = ang(jnp.arange(n2, dtype=jnp.int32), k1, t)
    tw = jnp.stack([jnp.cos(a_w), -jnp.sin(a_w)], axis=0)
    ngrp = wf // gw
    return pl.pallas_call(
        functools.partial(_fft_fast_kernel, n1=n1, n2=n2, gw=gw, scale=1.0 / math.sqrt(t * gw)),
        grid=(batch, ngrp),
        in_specs=[
            pl.BlockSpec((t, gw), lambda b, g: (b, col_tile * ngrp + g)),
            pl.BlockSpec((gw, 2 * gw), lambda b, g: (0, 0)),
            pl.BlockSpec((2, n2, n1), lambda b, g: (0, 0, 0)),
            pl.BlockSpec((n1, 2 * n1), lambda b, g: (0, 0)),
        ],
        out_specs=pl.BlockSpec((t, gw), lambda b, g: (b, g)),
        out_shape=jax.ShapeDtypeStruct((batch * t, wf), BF16),
        scratch_shapes=[
            pltpu.VMEM((2, n2, n1, gw), F32),
            pltpu.VMEM((gw // LANES, t, LANES), F32),
            pltpu.VMEM((n2, n1, 2 * n1), BF16),
        ],
        compiler_params=_cparams("arbitrary", "arbitrary"),
        name="fourier_fast",
    )(p, csc, tw, cst)


def _branch_kernel(conv_ref, uz_ref, zc_ref, g01_ref, g2_ref, ys_ref, yf_ref, cw_ref, sd_ref, wa_ref, wb_ref,
                   fw_ref, pa_ref, pb_ref, pc_ref, m_ref, *, row_len):
    tm = conv_ref.shape[0]
    tn = zc_ref.shape[1]
    d = g2_ref.shape[1]
    xa_ref, ba_ref, ca_ref, za_ref = (conv_ref.at[:, k * tn:(k + 1) * tn] for k in range(4))
    u_ref, zb_ref = (uz_ref.at[:, k * tn:(k + 1) * tn] for k in range(2))
    g0_ref, g1_ref = (g01_ref.at[:, k * d:(k + 1) * d] for k in range(2))
    f = lambda r: r[...].astype(F32)
    v = f(ca_ref) * f(xa_ref)
    pos = lax.broadcasted_iota(jnp.int32, v.shape, 0) % row_len
    v_prev = jnp.where(pos == 0, 0.0, pltpu.roll(v, 1, 0))
    v_next = jnp.where(pos == row_len - 1, 0.0, pltpu.roll(v, tm - 1, 0))
    cw = cw_ref[0]
    conv = v_prev * cw[0:1, :] + v * cw[1:2, :] + v_next * cw[2:3, :]
    a = f(ba_ref) * conv * _silu(f(za_ref))
    ya = jnp.dot(a.astype(BF16), pa_ref[0], preferred_element_type=F32)
    acc = _sigmoid(f(g0_ref)) * ya
    y = _gelu_tanh(ys_ref[...] + sd_ref[0] * f(u_ref)).astype(BF16)
    glu = (jnp.dot(y, wa_ref[0], preferred_element_type=F32)
           * _sigmoid(jnp.dot(y, wb_ref[0], preferred_element_type=F32)) * _silu(f(zb_ref)))
    yb = jnp.dot(glu.astype(BF16), pb_ref[0], preferred_element_type=F32)
    acc = acc + _sigmoid(f(g1_ref)) * yb
    c = jnp.dot(yf_ref[...], fw_ref[0], preferred_element_type=F32) * _silu(f(zc_ref))
    yc = jnp.dot(c.astype(BF16), pc_ref[0], preferred_element_type=F32)
    acc = acc + _sigmoid(f(g2_ref)) * yc
    m_ref[...] = acc.astype(BF16)


def _branches(p, ys, yf, conv_w, ssm_d, wa, wb, fw, pa, pb, pc, layer, row_len):
    n = p.shape[0]
    wc = conv_w.shape[-1]
    d = pa.shape[-1]
    tm = TM_BRANCH
    tn = COL_TILE
    col = lambda k: pl.BlockSpec((tm, tn), lambda i: (i, k))
    const = lambda shape: pl.BlockSpec(shape, lambda i: (layer,) + (0,) * (len(shape) - 1),
                                       pipeline_mode=pl.Buffered(1))
    g_col0 = N_BRANCH_TILES * tn
    assert g_col0 % (2 * d) == 0 and U_TILE % 2 == 0
    return pl.pallas_call(
        functools.partial(_branch_kernel, row_len=row_len),
        grid=(n // tm,),
        in_specs=[
            pl.BlockSpec((tm, U_TILE * tn), lambda i: (i, 0)),
            pl.BlockSpec((tm, 2 * tn), lambda i: (i, U_TILE // 2)),
            col(7),
            pl.BlockSpec((tm, 2 * d), lambda i: (i, g_col0 // (2 * d))),
            pl.BlockSpec((tm, d), lambda i: (i, g_col0 // d + 2)),
            pl.BlockSpec((tm, tn), lambda i: (i, 0)),
            pl.BlockSpec((tm, tn), lambda i: (i, 0)),
            const((1, 3, wc)), const((1, 1, tn)),
            const((1, tn, tn)), const((1, tn, tn)), const((1, tn, tn)),
            const((1, wc, d)), const((1, tn, d)), const((1, tn, d)),
        ],
        out_specs=pl.BlockSpec((tm, d), lambda i: (i, 0)),
        out_shape=jax.ShapeDtypeStruct((n, d), BF16),
        compiler_params=_cparams("arbitrary"),
        name="branches",
    )(p, p, p, p, p, ys, yf, conv_w, ssm_d.reshape(ssm_d.shape[0], 1, -1),
      wa, wb, fw, pa, pb, pc)


def _out_kernel(*refs, with_next):
    if with_next:
        m_ref, x_ref, gt_ref, g_ref, w_ref, scn_ref, shn_ref, gn_ref, o_ref, h_ref, wb = refs
    else:
        m_ref, x_ref, gt_ref, g_ref, w_ref, o_ref, wb = refs

    @pl.when(pl.program_id(0) == 0)
    def _():
        wb[...] = w_ref[0].astype(BF16)

    o = jnp.dot(m_ref[...], wb[...], preferred_element_type=F32)
    ms = jnp.mean(o * o, axis=-1, keepdims=True)
    xn = x_ref[...] + o * lax.rsqrt(ms + RMS_EPS) * (g_ref[0] * gt_ref[0])
    o_ref[...] = xn
    if with_next:
        h_ref[...] = _modnorm(xn, gn_ref[0] * (1.0 + scn_ref[0]), shn_ref[0]).astype(BF16)


def _outproj(m, x2, mods_r, g_post, w_out, layer, rows_per_batch, ctx_row=None, g_pre_next=None):
    n, d = x2.shape
    tm = min(TM_OUT, rows_per_batch)
    tpb = rows_per_batch // tm
    in_specs = [
        pl.BlockSpec((tm, d), lambda i: (i, 0)),
        pl.BlockSpec((tm, d), lambda i: (i, 0)),
        _mod_spec(d, layer, MOD_GATE, tpb, ctx_row),
        pl.BlockSpec((1, 1, d), lambda i: (layer, 0, 0)),
        pl.BlockSpec((1, d, d), lambda i: (layer, 0, 0), pipeline_mode=pl.Buffered(1)),
    ]
    args = [m, x2, mods_r, g_post.reshape(g_post.shape[0], 1, d), w_out]
    out_specs = [pl.BlockSpec((tm, d), lambda i: (i, 0))]
    out_shape = [jax.ShapeDtypeStruct((n, d), F32)]
    if g_pre_next is not None:
        in_specs += [_mod_spec(d, layer + 1, MOD_SCALE, tpb, ctx_row), _mod_spec(d, layer + 1, MOD_SHIFT, tpb, ctx_row),
                     pl.BlockSpec((1, 1, d), lambda i: (layer + 1, 0, 0))]
        args += [mods_r, mods_r, g_pre_next.reshape(g_pre_next.shape[0], 1, d)]
        out_specs.append(pl.BlockSpec((tm, d), lambda i: (i, 0)))
        out_shape.append(jax.ShapeDtypeStruct((n, d), BF16))
    res = pl.pallas_call(
        functools.partial(_out_kernel, with_next=g_pre_next is not None),
        grid=(n // tm,),
        in_specs=in_specs,
        out_specs=out_specs,
        out_shape=out_shape,
        scratch_shapes=[pltpu.VMEM((d, d), BF16)],
        compiler_params=_cparams("arbitrary"),
        name="outproj",
    )(*args)
    return (res[0], res[1]) if g_pre_next is not None else (res[0], None)


def kernel(x, c, ctx, c_ctx, w_ada, b_ada, g_pre, g_post, w_in, conv_w, ssm_lam_re, ssm_lam_im, ssm_log_dt,
           ssm_b_re, ssm_b_im, ssm_c_re, ssm_c_im, ssm_d, glu_wa, glu_wb, fourier_w, proj_a, proj_b, proj_c,
           w_out):
    bn, t, d = x.shape
    lc = ctx.shape[1]
    depth = w_ada.shape[0]
    w_conv = conv_w.shape[-1]
    w_ssm = ssm_d.shape[-1]
    tn = COL_TILE
    assert w_conv == tn and w_ssm == tn and fourier_w.shape[-1] == tn and d % tn == 0
    assert bn + 1 <= MOD_ROWS and t % GRID_W == 0
    u_tile = 4 * w_conv // tn
    f_tile = u_tile + 2
    n_tiles = w_in.shape[-1] // tn
    assert u_tile == U_TILE and f_tile + 2 == N_BRANCH_TILES and (n_tiles - N_BRANCH_TILES) * tn == N_BRANCH * d

    cc = jnp.concatenate([c, c_ctx[None, :], jnp.zeros((MOD_ROWS - bn - 1, d), F32)], axis=0)
    mods = _ada_mods(cc, w_ada, b_ada)

    to_b = lambda w: w.astype(BF16)
    wa_b, wb_b, fw_b = to_b(glu_wa), to_b(glu_wb), to_b(fourier_w)
    pa_b, pb_b, pc_b = to_b(proj_a), to_b(proj_b), to_b(proj_c)
    fast_dft = t == DFT_N1 * DFT_N2
    tabs_x = None if fast_dft else _dft_tables(t, tn // FFT_GROUPS)
    tabs_c = _dft_tables(lc, tn // FFT_GROUPS)

    x2 = x.reshape(bn * t, d)
    c2 = ctx.reshape(bn * lc, d)
    mods_r = mods.reshape(depth * MOD_ROWS * 3, 1, d)
    ctx_row = bn
    ssm_p = _ssm_params(ssm_lam_re, ssm_lam_im, ssm_log_dt, ssm_b_re, ssm_b_im, ssm_c_re, ssm_c_im)
    hx = _prenorm(x2, mods_r, g_pre, 0, t)
    hc = _prenorm(c2, mods_r, g_pre, 0, lc, ctx_row)
    for l in range(depth):
        last = l == depth - 1
        g_pre_next = None if last else g_pre

        px, wq = _inproj(hx, w_in, l, 0, n_tiles, keep_weights=True)
        pc_ = _inproj(hc, wq, None, u_tile, 1) if last else _inproj(hc, wq, None, 0, n_tiles)
        u_lane_tile = u_tile * (tn // LANES)
        ys_x, ys_c = _ssm_scan(px, pc_, u_lane_tile, 0 if last else u_lane_tile, w_ssm, ssm_p, l, bn,
                               ctx_out=not last)

        yf_x = _fourier_fast(px, f_tile, t, bn) if fast_dft else _fourier(px, f_tile, t, bn, tabs_x)
        mx = _branches(px, ys_x, yf_x, conv_w, ssm_d, wa_b, wb_b, fw_b, pa_b, pb_b, pc_b, l, GRID_W)
        new_x2, hx = _outproj(mx, x2, mods_r, g_post, w_out, l, t, None, g_pre_next)
        if not last:
            yf_c = _fourier(pc_, f_tile, lc, bn, tabs_c)
            mc = _branches(pc_, ys_c, yf_c, conv_w, ssm_d, wa_b, wb_b, fw_b, pa_b, pb_b, pc_b, l, lc)
            c2, hc = _outproj(mc, c2, mods_r, g_post, w_out, l, lc, ctx_row, g_pre_next)
        x2 = new_x2
    return x2.reshape(bn, t, d)
```

```python
import functools
import math

import jax
import jax.numpy as jnp
from jax import lax
from jax.experimental import pallas as pl
from jax.experimental.pallas import tpu as pltpu

F32 = jnp.float32
BF16 = jnp.bfloat16

GRID_W = 64
FFT_GROUPS = 4
N_BRANCH = 3
RMS_EPS = 1e-6
LANES = 128
SSM_CHUNK = 8
SSM_SEGS = 8
DFT_ROWS = 64
DFT_N1, DFT_N2 = 256, 16
VMEM_LIMIT = 56 * 1024 * 1024
COL_TILE = 1024
N_BRANCH_TILES = 8
U_TILE = 4
TM_INPROJ, TM_PRENORM, TM_BRANCH, TM_OUT, TM_DFT = 2048, 2048, 256, 512, 256
TM_INPROJ_DOT = 1024


def _cparams(*sem):
    return pltpu.CompilerParams(dimension_semantics=sem, vmem_limit_bytes=VMEM_LIMIT)


def _sigmoid(v):
    return 0.5 * jnp.tanh(0.5 * v) + 0.5


def _silu(v):
    return v * _sigmoid(v)


def _gelu_tanh(v):
    return v * (0.5 * (1.0 + jnp.tanh(math.sqrt(2.0 / math.pi) * (v + 0.044715 * (v * v * v)))))


def _cmul(ar, ai, br, bi):
    return ar * br - ai * bi, ar * bi + ai * br


def _cmul_add(ar, ai, zr, zi, sr, si):
    return ar * zr - ai * zi + sr, ar * zi + ai * zr + si


def _ada_kernel(c_ref, w_ref, b_ref, o_ref):
    c = c_ref[...]
    s = _silu(c).astype(BF16)
    o_ref[0] = jnp.dot(s, w_ref[0].astype(BF16), preferred_element_type=F32) + b_ref[0]


def _ada_mods(cc, w_ada, b_ada):
    depth, d, n3 = w_ada.shape
    rows = cc.shape[0]
    tn = next(k * COL_TILE for k in (3, 2, 1) if n3 % (k * COL_TILE) == 0)
    assert n3 % tn == 0
    return pl.pallas_call(
        _ada_kernel,
        grid=(depth, n3 // tn),
        in_specs=[
            pl.BlockSpec((rows, d), lambda l, j: (0, 0)),
            pl.BlockSpec((1, d, tn), lambda l, j: (l, 0, j)),
            pl.BlockSpec((1, 1, tn), lambda l, j: (l, 0, j)),
        ],
        out_specs=pl.BlockSpec((1, rows, tn), lambda l, j: (l, 0, j)),
        out_shape=jax.ShapeDtypeStruct((depth, rows, n3), F32),
        compiler_params=_cparams("arbitrary", "arbitrary"),
        name="ada_mods",
    )(cc, w_ada, b_ada.reshape(depth, 1, n3))


def _modnorm(x, gs, sh):
    ms = jnp.mean(x * x, axis=-1, keepdims=True)
    return x * lax.rsqrt(ms + RMS_EPS) * gs + sh


def _prenorm_kernel(x_ref, sc_ref, sh_ref, g_ref, h_ref):
    rb = 16
    gs = g_ref[0] * (1.0 + sc_ref[0])
    sh = sh_ref[0]

    def body(r, carry):
        rows = pl.ds(pl.multiple_of(r * rb, rb), rb)
        h_ref[rows, :] = _modnorm(x_ref[rows, :], gs, sh).astype(BF16)
        return carry

    lax.fori_loop(0, x_ref.shape[0] // rb, body, 0, unroll=4)


MOD_SHIFT, MOD_SCALE, MOD_GATE = 0, 1, 2
MOD_ROWS = 8


def _mod_spec(d, layer, which, tiles_per_batch, ctx_row):
    def index(i):
        row = ctx_row if ctx_row is not None else i // tiles_per_batch
        return ((layer * MOD_ROWS + row) * 3 + which, 0, 0)
    return pl.BlockSpec((1, 1, d), index)


def _prenorm(x2, mods_r, g_pre, layer, rows_per_batch, ctx_row=None):
    n, d = x2.shape
    tm = min(TM_PRENORM, rows_per_batch)
    tpb = rows_per_batch // tm
    return pl.pallas_call(
        _prenorm_kernel,
        grid=(n // tm,),
        in_specs=[
            pl.BlockSpec((tm, d), lambda i: (i, 0)),
            _mod_spec(d, layer, MOD_SCALE, tpb, ctx_row),
            _mod_spec(d, layer, MOD_SHIFT, tpb, ctx_row),
            pl.BlockSpec((1, 1, d), lambda i: (layer, 0, 0)),
        ],
        out_specs=pl.BlockSpec((tm, d), lambda i: (i, 0)),
        out_shape=jax.ShapeDtypeStruct((n, d), BF16),
        compiler_params=_cparams("arbitrary"),
        name="prenorm",
    )(x2, mods_r, mods_r, g_pre.reshape(g_pre.shape[0], 1, d))


def _inproj_kernel(h_ref, w_ref, p_ref, *bf16_weights, cast):
    if cast:
        (wb,) = bf16_weights

        @pl.when(pl.program_id(1) == 0)
        def _():
            wb[...] = w_ref[0].astype(BF16)
    else:
        wb = w_ref

    sub = min(TM_INPROJ_DOT, h_ref.shape[0])
    for r in range(h_ref.shape[0] // sub):
        rows = slice(r * sub, (r + 1) * sub)
        p_ref[rows, :] = jnp.dot(h_ref[rows, :], wb[...], preferred_element_type=F32).astype(BF16)


def _inproj(h, w_in, layer, tile0, ntiles, keep_weights=False):
    n, d = h.shape
    tn = COL_TILE
    tm = min(TM_INPROJ, n)
    cast = layer is not None
    w_spec = (pl.BlockSpec((1, d, tn), lambda j, i: (layer, 0, tile0 + j)) if cast
              else pl.BlockSpec((d, tn), lambda j, i: (0, tile0 + j)))
    out_specs = [pl.BlockSpec((tm, tn), lambda j, i: (i, j))]
    out_shape = [jax.ShapeDtypeStruct((n, ntiles * tn), BF16)]
    scratch = []
    if keep_weights:
        out_specs.append(pl.BlockSpec((d, tn), lambda j, i: (0, j)))
        out_shape.append(jax.ShapeDtypeStruct((d, ntiles * tn), BF16))
    elif cast:
        scratch = [pltpu.VMEM((d, tn), BF16)]
    res = pl.pallas_call(
        functools.partial(_inproj_kernel, cast=cast),
        grid=(ntiles, n // tm),
        in_specs=[pl.BlockSpec((tm, d), lambda j, i: (i, 0)), w_spec],
        out_specs=out_specs,
        out_shape=out_shape,
        scratch_shapes=scratch,
        compiler_params=_cparams("arbitrary", "arbitrary"),
        name="inproj",
    )(h, w_in)
    return (res[0], res[1]) if keep_weights else res[0]


def _ssm_params(lam_re, lam_im, log_dt, b_re, b_im, c_re, c_im):
    nl, _, G, P = lam_re.shape
    H = b_re.shape[-1]
    gpt = LANES // H
    O = G // gpt
    lam_re = lam_re.astype(F32)
    lam_im = lam_im.astype(F32)
    dt = jnp.exp(log_dt.astype(F32))[..., None]
    lr = lam_re * dt
    li = lam_im * dt
    mag = jnp.exp(lr)
    a_re = mag * jnp.cos(li)
    a_im = mag * jnp.sin(li)
    n_re = a_re - 1.0
    n_im = a_im
    den = lam_re * lam_re + lam_im * lam_im
    q_re = (n_re * lam_re + n_im * lam_im) / den
    q_im = (n_im * lam_re - n_re * lam_im) / den
    bb_re = q_re[..., None] * b_re - q_im[..., None] * b_im
    bb_im = q_re[..., None] * b_im + q_im[..., None] * b_re

    bbs = jnp.stack([bb_re, bb_im], axis=2).reshape(nl, 2, 2, O, gpt, P, H)
    bc = jnp.transpose(bbs, (0, 3, 1, 2, 6, 4, 5)).reshape(nl, O, 4, H, gpt * P)
    cs = jnp.stack([c_re.astype(F32), c_im.astype(F32)], axis=2).reshape(nl, 2, 2, O, gpt, H, P)
    cc = jnp.transpose(cs, (0, 3, 1, 2, 5, 4, 6)).reshape(nl, O, 4, H, gpt * P)
    a4 = jnp.stack([a_re, a_im], axis=2).reshape(nl, 2, 2, O, gpt * P)
    a4 = jnp.transpose(a4, (0, 3, 1, 2, 4)).reshape(nl, O, 4, gpt * P)
    arow = jnp.concatenate([a4, jnp.zeros_like(a4)], axis=2)

    return bc, cc, arow


def _block_transpose8(v):
    bw = LANES // 8
    blk = lax.broadcasted_iota(jnp.int32, v[0].shape, 1) // bw
    for dist in (4, 2, 1):
        upper = (blk & dist) != 0
        nxt = list(v)
        for a in range(8):
            if a & dist:
                continue
            b = a + dist
            nxt[a] = jnp.where(upper, pltpu.roll(v[b], dist * bw, 1), v[a])
            nxt[b] = jnp.where(upper, v[b], pltpu.roll(v[a], LANES - dist * bw, 1))
        v = nxt
    return v


def _ssm_build_operators(bc_ref, cc_ref, arow_ref, ws, ws_lo, w3):
    L = SSM_CHUNK
    hh = bc_ref.shape[2]
    sw = bc_ref.shape[3]
    pp = sw // (LANES // hh)
    npair = sw // LANES
    assert L * hh == LANES and 2 * pp == LANES
    lane = lax.broadcasted_iota(jnp.int32, (LANES, LANES), 1)
    row = lax.broadcasted_iota(jnp.int32, (LANES, LANES), 0)
    lane_g0, row_g0 = lane < pp, row < pp
    lane_blk = lane // hh
    zeros = jnp.zeros((LANES, LANES), F32)
    zeros_b = jnp.zeros((LANES, LANES), BF16)

    def powers(re, im, n):
        out = [(jnp.ones_like(re), jnp.zeros_like(re))]
        for _ in range(n):
            out.append(_cmul(out[-1][0], out[-1][1], re, im))
        return out

    def split(v):
        hi = v.astype(BF16)
        return hi, (v - hi.astype(F32)).astype(BF16)

    def shift_rows(t, blocks):
        n = abs(blocks) * hh
        if n == 0:
            return t
        pad = jnp.zeros((n, LANES), F32)
        return jnp.concatenate([pad, t[:LANES - n]] if blocks > 0 else [t[n:], pad], axis=0)

    for q in range(npair):
        ql = slice(q * LANES, (q + 1) * LANES)
        taps = []
        for d in range(2):
            b_re = jnp.concatenate([bc_ref[0, 2 * d, :, ql]] * L, axis=0)
            b_im = jnp.concatenate([bc_ref[0, 2 * d + 1, :, ql]] * L, axis=0)
            prow = powers(arow_ref[0, 2 * d:2 * d + 1, ql], arow_ref[0, 2 * d + 1:2 * d + 2, ql], L)

            def power_rows(exps):
                return (jnp.concatenate([jnp.broadcast_to(prow[e][0], (hh, LANES)) for e in exps], axis=0),
                        jnp.concatenate([jnp.broadcast_to(prow[e][1], (hh, LANES)) for e in exps], axis=0))

            a_re, a_im = power_rows([L - 1 - j if d == 0 else j for j in range(L)])
            for part, v in zip((2 * d, 2 * d + 1), _cmul(b_re, b_im, a_re, a_im)):
                cols = slice(part * LANES, (part + 1) * LANES)
                for dst, piece in zip((ws, ws_lo), split(v)):
                    dst[q, 0:LANES, cols] = jnp.where(lane_g0, piece, zeros_b)
                    dst[q, LANES:2 * LANES, cols] = jnp.where(lane_g0, zeros_b, piece)
            ct_re = jnp.concatenate([cc_ref[0, 2 * d, :, ql]] * L, axis=0)
            ct_im = jnp.concatenate([cc_ref[0, 2 * d + 1, :, ql]] * L, axis=0)
            c_re, c_im = ct_re.T, ct_im.T
            (cr_hi, cr_lo), (ci_hi, ci_lo) = split(c_re), split(-c_im)
            c_hi = jnp.concatenate([cr_hi, ci_hi], axis=0)
            c_lo = jnp.concatenate([cr_lo, ci_lo], axis=0)
            dcols = slice(2 * d * LANES, (2 * d + 2) * LANES)
            taps.append(jnp.dot(jnp.concatenate([ws[q, :, dcols], ws[q, :, dcols], ws_lo[q, :, dcols]], axis=1),
                                jnp.concatenate([c_hi, c_lo, c_hi], axis=0), preferred_element_type=F32))
            cat_re, cat_im = _cmul(ct_re, ct_im, *power_rows([jo + 1 if d == 0 else L - jo for jo in range(L)]))
            for part, v in ((2 * d, cat_re.T), (2 * d + 1, -cat_im.T)):
                r0 = 2 * LANES + part * LANES
                w3[q, r0:r0 + LANES, 0:LANES] = jnp.where(row_g0, v, zeros).astype(BF16)
                w3[q, r0:r0 + LANES, LANES:2 * LANES] = jnp.where(row_g0, zeros, v).astype(BF16)
        for g2 in range(2):
            tf = taps[0][g2 * LANES:(g2 + 1) * LANES]
            tb = taps[1][g2 * LANES:(g2 + 1) * LANES]
            m = zeros
            for jo in range(L):
                m = jnp.where(lane_blk == jo, shift_rows(tf, jo - (L - 1)) + shift_rows(tb, jo), m)
            rows = slice(g2 * LANES, (g2 + 1) * LANES)
            w3[q, rows, g2 * LANES:(g2 + 1) * LANES] = m.astype(BF16)
            w3[q, rows, (1 - g2) * LANES:(2 - g2) * LANES] = zeros_b


def _ssm_build_powers(arow_ref, ap, pw, seg_len):
    L = SSM_CHUNK
    sw = arow_ref.shape[2]
    assert L & (L - 1) == 0 and seg_len & (seg_len - 1) == 0 and seg_len >= 8
    for d in range(2):
        cr, ci = slice(2 * d * sw, (2 * d + 1) * sw), slice((2 * d + 1) * sw, (2 * d + 2) * sw)
        re, im = arow_ref[0, 2 * d:2 * d + 1, :], arow_ref[0, 2 * d + 1:2 * d + 2, :]
        for _ in range(L.bit_length() - 1):
            re, im = _cmul(re, im, re, im)
        ap[0:1, cr] = re
        ap[0:1, ci] = im
        p = [(jnp.ones_like(re), jnp.zeros_like(re))]
        for _ in range(7):
            p.append(_cmul(p[-1][0], p[-1][1], re, im))
        order = list(range(8)) if d == 0 else list(range(7, -1, -1))
        first = slice(0, 8) if d == 0 else slice(seg_len - 8, seg_len)
        pw[first, cr] = jnp.concatenate([p[r][0] for r in order], axis=0)
        pw[first, ci] = jnp.concatenate([p[r][1] for r in order], axis=0)
        sq = _cmul(p[4][0], p[4][1], p[4][0], p[4][1])
        m = 8
        while m < seg_len:
            src = slice(0, m) if d == 0 else slice(seg_len - m, seg_len)
            dst = slice(m, 2 * m) if d == 0 else slice(seg_len - 2 * m, seg_len - m)
            xr, xi = _cmul(pw[src, cr], pw[src, ci], sq[0], sq[1])
            pw[dst, cr] = xr
            pw[dst, ci] = xi
            sq = _cmul(sq[0], sq[1], sq[0], sq[1])
            m *= 2
        ap[1:2, cr] = sq[0]
        ap[1:2, ci] = sq[1]


def _seg_pitch(seg_len):
    tiles = seg_len // 8 + 1
    return 8 * (tiles + 1 - tiles % 2)


def _ssm_kernel(*refs, nc, nx, seg_len, batch, ctx_out):
    if ctx_out:
        (ux_ref, uc_ref, bc_ref, cc_ref, arow_ref,
         yx_ref, yc_ref, ws, ws_lo, w3, lhs, st, uxf, ucf, ap, pw) = refs
    else:
        (ux_ref, uc_ref, bc_ref, cc_ref, arow_ref,
         yx_ref, ws, ws_lo, w3, lhs, st, uxf, ucf, ap, pw) = refs
        yc_ref = None
    L = SSM_CHUNK
    nlt = st.shape[0] // 4
    sw = nlt * LANES
    xw = 2 * LANES
    FR, FI, BR, BI = range(4)
    nseg = SSM_SEGS
    pitch = _seg_pitch(seg_len)
    t, lc = nx * L, nc * L

    _ssm_build_operators(bc_ref, cc_ref, arow_ref, ws, ws_lo, w3)
    _ssm_build_powers(arow_ref, ap, pw, seg_len)

    def ld(part, rows):
        return jnp.concatenate([st[part * nlt + q, rows, :] for q in range(nlt)], axis=1)

    def sto(part, rows, val):
        for q in range(nlt):
            st[part * nlt + q, rows, :] = val[:, q * LANES:(q + 1) * LANES]

    def part_cols(part):
        return slice(part * sw, (part + 1) * sw)

    def one_sequence(b, carry):
        ucf[...] = uc_ref[pl.ds(pl.multiple_of(b * lc, lc), lc), :].astype(F32)
        uxf[...] = ux_ref[pl.ds(pl.multiple_of(b * t, t), t), :].astype(F32)
        by_pos = [jnp.concatenate([ucf[pl.ds(j, nc, stride=L), :], uxf[pl.ds(j, nx, stride=L), :]], axis=0)
                  for j in range(L)]
        for g, tile_g in enumerate(_block_transpose8(by_pos)):
            lhs[g // 2, :, (g % 2) * LANES:(g % 2 + 1) * LANES] = tile_g.astype(BF16)

        for q in range(nlt):
            s = jnp.dot(lhs[q, :, 0:xw], ws[q], preferred_element_type=F32)
            for part in range(4):
                k = part * nlt + q
                st[k, 0:nc, :] = s[0:nc, part * LANES:(part + 1) * LANES]
                for g in range(nseg):
                    st[k, nc + g * pitch:nc + g * pitch + seg_len, :] = (
                        s[nc + g * seg_len:nc + (g + 1) * seg_len, part * LANES:(part + 1) * LANES])

        a_l = ap[0:1, :]
        a_seg = ap[1:2, :]
        al = [a_l[:, part_cols(p)] for p in range(4)]
        aseg = [a_seg[:, part_cols(p)] for p in range(4)]

        zero = jnp.zeros((1, sw), F32)

        def ctx_step(pr, pi, i, carry):
            er, ei = carry
            row = slice(i, i + 1)
            sr, si = ld(pr, row), ld(pi, row)
            sto(pr, row, er)
            sto(pi, row, ei)
            return _cmul_add(al[pr], al[pi], er, ei, sr, si)

        h0f, h0b = (zero, zero), (zero, zero)
        for i in range(nc):
            h0f = ctx_step(FR, FI, i, h0f)
            h0b = ctx_step(BR, BI, nc - 1 - i, h0b)
        for part in range(4):
            for q in range(nlt):
                lhs[q, 0:nc, xw + part * LANES:xw + (part + 1) * LANES] = st[part * nlt + q, 0:nc, :].astype(BF16)

        zseg = jnp.zeros((nseg, sw), F32)

        def seg_pass(pr, pi, order):
            zr, zi = zseg, zseg
            for i in order:
                rows = pl.ds(nc + i, nseg, stride=pitch)
                sr, si = ld(pr, rows), ld(pi, rows)
                sto(pr, rows, zr)
                sto(pi, rows, zi)
                zr, zi = _cmul_add(al[pr], al[pi], zr, zi, sr, si)
            return zr, zi

        zfr, zfi = seg_pass(FR, FI, range(seg_len))
        zbr, zbi = seg_pass(BR, BI, range(seg_len - 1, -1, -1))

        ef = [h0f]
        for s in range(nseg - 1):
            ef.append(_cmul_add(aseg[FR], aseg[FI], ef[s][0], ef[s][1], zfr[s:s + 1], zfi[s:s + 1]))
        eb = [None] * nseg
        eb[nseg - 1] = h0b
        for s in range(nseg - 1, 0, -1):
            eb[s - 1] = _cmul_add(aseg[BR], aseg[BI], eb[s][0], eb[s][1], zbr[s:s + 1], zbi[s:s + 1])
        for s in range(nseg):
            srows = slice(nc + s * pitch, nc + s * pitch + seg_len)
            rows = slice(nc + s * seg_len, nc + (s + 1) * seg_len)
            for (pr, pi, e) in ((FR, FI, ef[s]), (BR, BI, eb[s])):
                xr, xi = _cmul_add(pw[:, part_cols(pr)], pw[:, part_cols(pi)], e[0], e[1],
                                   ld(pr, srows), ld(pi, srows))
                for q in range(nlt):
                    ql = slice(q * LANES, (q + 1) * LANES)
                    lhs[q, rows, xw + pr * LANES:xw + (pr + 1) * LANES] = xr[:, ql].astype(BF16)
                    lhs[q, rows, xw + pi * LANES:xw + (pi + 1) * LANES] = xi[:, ql].astype(BF16)

        by_group = []
        for q in range(nlt):
            y = jnp.dot(lhs[q], w3[q], preferred_element_type=F32)
            by_group += [y[:, 0:LANES], y[:, LANES:2 * LANES]]
        for j, tile_j in enumerate(_block_transpose8(by_group)):
            if ctx_out:
                yc_ref[pl.ds(b * lc + j, nc, stride=L), :] = tile_j[0:nc]
            yx_ref[pl.ds(b * t + j, nx, stride=L), :] = tile_j[nc:nc + nx]
        return carry

    lax.fori_loop(0, batch, one_sequence, 0)


def _ssm_scan(px, pc, col_x, col_c, w, params, layer, batch, ctx_out):
    bc, cc, arow = params
    L = SSM_CHUNK
    t = px.shape[0] // batch
    lc = pc.shape[0] // batch
    nx, nc = t // L, lc // L
    seg_len = nx // SSM_SEGS
    o = w // LANES
    sw = bc.shape[-1]
    sdim = 4 * sw
    npair = sw // LANES
    out_shape = [jax.ShapeDtypeStruct((batch * t, w), F32)]
    out_specs = [pl.BlockSpec((batch * t, LANES), lambda oi: (0, oi))]
    if ctx_out:
        out_shape.append(jax.ShapeDtypeStruct((batch * lc, w), F32))
        out_specs.append(pl.BlockSpec((batch * lc, LANES), lambda oi: (0, oi)))
    tile = lambda a: pl.BlockSpec((None, 1) + a.shape[2:], lambda oi: (layer, oi) + (0,) * (a.ndim - 2))
    res = pl.pallas_call(
        functools.partial(_ssm_kernel, nc=nc, nx=nx, seg_len=seg_len, batch=batch, ctx_out=ctx_out),
        grid=(o,),
        in_specs=[
            pl.BlockSpec((batch * t, LANES), lambda oi: (0, col_x + oi)),
            pl.BlockSpec((batch * lc, LANES), lambda oi: (0, col_c + oi)),
            tile(bc), tile(cc), tile(arow),
        ],
        out_specs=out_specs,
        out_shape=out_shape,
        scratch_shapes=[
            pltpu.VMEM((npair, 2 * LANES, 4 * LANES), BF16),
            pltpu.VMEM((npair, 2 * LANES, 4 * LANES), BF16),
            pltpu.VMEM((npair, 6 * LANES, 2 * LANES), BF16),
            pltpu.VMEM((npair, nc + nx, 6 * LANES), BF16),
            pltpu.VMEM((sdim // LANES, nc + SSM_SEGS * _seg_pitch(seg_len), LANES), F32),
            pltpu.VMEM((t, LANES), F32),
            pltpu.VMEM((lc, LANES), F32),
            pltpu.VMEM((8, sdim), F32),
            pltpu.VMEM((seg_len, sdim), F32),
        ],
        compiler_params=_cparams("arbitrary"),
        name="ssm_scan",
    )(px, pc, bc, cc, arow)
    return (res[0], res[1]) if ctx_out else (res[0], None)


def _dft_table_kernel(e1_ref, e2_ref, o_ref, *, t):
    e1c, e1s = e1_ref[0, :, 0:t], e1_ref[0, :, t:2 * t]
    e2c, e2s = e2_ref[:, 0:t], e2_ref[:, t:2 * t]
    c, s = _cmul(e2c, e2s, e1c, e1s)
    o_ref[:, 0:t] = c.astype(BF16)
    o_ref[:, t:2 * t] = (-s).astype(BF16)


def _dft_tables(t, gw):
    def cis(rows_mult, nrows):
        k = jnp.arange(nrows, dtype=jnp.int32)[:, None] * rows_mult
        n = jnp.arange(t, dtype=jnp.int32)[None, :]
        ang = ((k * n) % t).astype(F32) * (2.0 * math.pi / t)
        return jnp.concatenate([jnp.cos(ang), jnp.sin(ang)], axis=1)
    r = DFT_ROWS
    e1 = cis(r, t // r).reshape(t // r, 1, 2 * t)
    e2 = cis(1, r)
    tab_t = pl.pallas_call(
        functools.partial(_dft_table_kernel, t=t),
        grid=(t // r,),
        in_specs=[
            pl.BlockSpec((1, 1, 2 * t), lambda i: (i, 0, 0)),
            pl.BlockSpec((r, 2 * t), lambda i: (0, 0)),
        ],
        out_specs=pl.BlockSpec((r, 2 * t), lambda i: (i, 0)),
        out_shape=jax.ShapeDtypeStruct((t, 2 * t), BF16),
        compiler_params=_cparams("arbitrary"),
        name="dft_table",
    )(e1, e2)
    kc = jnp.arange(gw, dtype=jnp.int32)
    ang = ((kc[:, None] * kc[None, :]) % gw).astype(F32) * (2.0 * math.pi / gw)
    tab_c = jnp.concatenate([jnp.cos(ang), jnp.sin(ang)], axis=1).astype(BF16)
    return tab_t, tab_c


def _fft_kernel(f_ref, cs_ref, tab_ref, o_ref, data, *, t, gw, groups, scale):
    i = pl.program_id(1)

    @pl.when(i == 0)
    def _():
        rb = min(512, t)
        for r in range(t // rb):
            for g in range(groups):
                fg = f_ref[r * rb:(r + 1) * rb, g * gw:(g + 1) * gw]
                z = jnp.dot(fg, cs_ref[...], preferred_element_type=F32)
                data[r * rb:(r + 1) * rb, g * gw:(g + 1) * gw] = z[:, :gw].astype(BF16)
                data[t + r * rb:t + (r + 1) * rb, g * gw:(g + 1) * gw] = z[:, gw:].astype(BF16)

    y = jnp.dot(tab_ref[...], data[...], preferred_element_type=F32)
    o_ref[...] = (y * scale).astype(BF16)


def _fourier(p, col_tile, t, batch, tabs):
    tab_t, tab_c = tabs
    wf = COL_TILE
    gw = wf // FFT_GROUPS
    tm = min(TM_DFT, t)
    return pl.pallas_call(
        functools.partial(_fft_kernel, t=t, gw=gw, groups=FFT_GROUPS, scale=1.0 / math.sqrt(t * gw)),
        grid=(batch, t // tm),
        in_specs=[
            pl.BlockSpec((t, wf), lambda b, i: (b, col_tile)),
            pl.BlockSpec((gw, 2 * gw), lambda b, i: (0, 0)),
            pl.BlockSpec((tm, 2 * t), lambda b, i: (i, 0)),
        ],
        out_specs=pl.BlockSpec((tm, wf), lambda b, i: (b * (t // tm) + i, 0)),
        out_shape=jax.ShapeDtypeStruct((batch * t, wf), BF16),
        scratch_shapes=[pltpu.VMEM((2 * t, wf), BF16)],
        compiler_params=_cparams("arbitrary", "arbitrary"),
        name="fourier",
    )(p, tab_c, tab_t)


def _cmul_const(xr, xi, c, s):
    def scaled(v, k):
        if abs(k) < 1e-12:
            return None
        return v if abs(k - 1.0) < 1e-12 else -v if abs(k + 1.0) < 1e-12 else v * k

    def add(a, b):
        return b if a is None else a if b is None else a + b

    return add(scaled(xr, c), scaled(xi, -s)), add(scaled(xi, c), scaled(xr, s))


def _dft4(y):
    (ar, ai), (br, bi), (cr, ci), (dr, di) = y
    sr, si, tr, ti = ar + cr, ai + ci, ar - cr, ai - ci
    ur, ui, vr, vi = br + dr, bi + di, br - dr, bi - di
    return [(sr + ur, si + ui), (tr + vi, ti - vr), (sr - ur, si - ui), (tr - vi, ti + vr)]


def _dft16(x):
    out = [None] * 16
    p = [_dft4([x[4 * a + b] for a in range(4)]) for b in range(4)]
    for ka in range(4):
        q = []
        for b in range(4):
            ang = -2.0 * math.pi * ka * b / 16.0
            q.append(_cmul_const(p[b][ka][0], p[b][ka][1], math.cos(ang), math.sin(ang)))
        r = _dft4(q)
        for kb in range(4):
            out[ka + 4 * kb] = r[kb]
    return out


def _fft_fast_kernel(f_ref, csc_ref, tw_ref, cst_ref, o_ref, z, o_scr, tabs, *, n1, n2, gw, scale):
    @pl.when((pl.program_id(0) == 0) & (pl.program_id(1) == 0))
    def _():
        c, s = cst_ref[:, 0:n1], cst_ref[:, n1:2 * n1]
        for k2 in range(n2):
            twr, twi = tw_ref[0, k2:k2 + 1, :], tw_ref[1, k2:k2 + 1, :]
            tabs[k2, :, 0:n1] = (c * twr + s * twi).astype(BF16)
            tabs[k2, :, n1:2 * n1] = (s * twr - c * twi).astype(BF16)

    slabs_per_dot = 4
    for q in range(n2 // slabs_per_dot):
        r0 = q * slabs_per_dot * n1
        zz = jnp.dot(f_ref[r0:r0 + slabs_per_dot * n1, :], csc_ref[...], preferred_element_type=F32)
        for s in range(slabs_per_dot):
            z[0, q * slabs_per_dot + s] = zz[s * n1:(s + 1) * n1, 0:gw]
            z[1, q * slabs_per_dot + s] = zz[s * n1:(s + 1) * n1, gw:2 * gw]

    def tile_body(r, carry):
        rows = pl.ds(pl.multiple_of(r * 8, 8), 8)
        for c in range(gw // LANES):
            cols = slice(c * LANES, (c + 1) * LANES)
            h = _dft16([(z[0, t, rows, cols], z[1, t, rows, cols]) for t in range(n2)])
            for k2 in range(n2):
                z[0, k2, rows, cols] = h[k2][0]
                z[1, k2, rows, cols] = h[k2][1]
        return carry

    lax.fori_loop(0, n1 // 8, tile_body, 0)

    for k2 in range(n2):
        rhs = jnp.concatenate([z[0, k2], z[1, k2]], axis=0).astype(BF16)
        y = jnp.dot(tabs[k2], rhs, preferred_element_type=F32) * scale
        for c in range(gw // LANES):
            o_scr[c, pl.ds(k2, n1, stride=n2), :] = y[:, c * LANES:(c + 1) * LANES]
    o_ref[...] = jnp.concatenate([o_scr[c] for c in range(gw // LANES)], axis=1).astype(BF16)


def _fourier_fast(p, col_tile, t, batch):
    n1, n2 = DFT_N1, DFT_N2
    assert t == n1 * n2
    wf = COL_TILE
    gw = wf // FFT_GROUPS
    ang = lambda a, b, n: ((a[:, None] * b[None, :]) % n).astype(F32) * (2.0 * math.pi / n)
    kc = jnp.arange(gw, dtype=jnp.int32)
    a_c = ang(kc, kc, gw)
    csc = jnp.concatenate([jnp.cos(a_c), -jnp.sin(a_c)], axis=1).astype(BF16)
    k1 = jnp.arange(n1, dtype=jnp.int32)
    a_t = ang(k1, k1, n1)
    cst = jnp.concatenate([jnp.cos(a_t), jnp.sin(a_t)], axis=1)
    a_w = ang(jnp.arange(n2, dtype=jnp.int32), k1, t)
    tw = jnp.stack([jnp.cos(a_w), -jnp.sin(a_w)], axis=0)
    ngrp = wf // gw
    return pl.pallas_call(
        functools.partial(_fft_fast_kernel, n1=n1, n2=n2, gw=gw, scale=1.0 / math.sqrt(t * gw)),
        grid=(batch, ngrp),
        in_specs=[
            pl.BlockSpec((t, gw), lambda b, g: (b, col_tile * ngrp + g)),
            pl.BlockSpec((gw, 2 * gw), lambda b, g: (0, 0)),
            pl.BlockSpec((2, n2, n1), lambda b, g: (0, 0, 0)),
            pl.BlockSpec((n1, 2 * n1), lambda b, g: (0, 0)),
        ],
        out_specs=pl.BlockSpec((t, gw), lambda b, g: (b, g)),
        out_shape=jax.ShapeDtypeStruct((batch * t, wf), BF16),
        scratch_shapes=[
            pltpu.VMEM((2, n2, n1, gw), F32),
            pltpu.VMEM((gw // LANES, t, LANES), F32),
            pltpu.VMEM((n2, n1, 2 * n1), BF16),
        ],
        compiler_params=_cparams("arbitrary", "arbitrary"),
        name="fourier_fast",
    )(p, csc, tw, cst)


def _branch_kernel(conv_ref, uz_ref, zc_ref, g01_ref, g2_ref, ys_ref, yf_ref, cw_ref, sd_ref, wa_ref, wb_ref,
                   fw_ref, pa_ref, pb_ref, pc_ref, m_ref, *, row_len):
    tm = conv_ref.shape[0]
    tn = zc_ref.shape[1]
    d = g2_ref.shape[1]
    xa_ref, ba_ref, ca_ref, za_ref = (conv_ref.at[:, k * tn:(k + 1) * tn] for k in range(4))
    u_ref, zb_ref = (uz_ref.at[:, k * tn:(k + 1) * tn] for k in range(2))
    g0_ref, g1_ref = (g01_ref.at[:, k * d:(k + 1) * d] for k in range(2))
    f = lambda r: r[...].astype(F32)
    v = f(ca_ref) * f(xa_ref)
    pos = lax.broadcasted_iota(jnp.int32, v.shape, 0) % row_len
    v_prev = jnp.where(pos == 0, 0.0, pltpu.roll(v, 1, 0))
    v_next = jnp.where(pos == row_len - 1, 0.0, pltpu.roll(v, tm - 1, 0))
    cw = cw_ref[0]
    conv = v_prev * cw[0:1, :] + v * cw[1:2, :] + v_next * cw[2:3, :]
    a = f(ba_ref) * conv * _silu(f(za_ref))
    ya = jnp.dot(a.astype(BF16), pa_ref[0], preferred_element_type=F32)
    acc = _sigmoid(f(g0_ref)) * ya
    y = _gelu_tanh(ys_ref[...] + sd_ref[0] * f(u_ref)).astype(BF16)
    glu = (jnp.dot(y, wa_ref[0], preferred_element_type=F32)
           * _sigmoid(jnp.dot(y, wb_ref[0], preferred_element_type=F32)) * _silu(f(zb_ref)))
    yb = jnp.dot(glu.astype(BF16), pb_ref[0], preferred_element_type=F32)
    acc = acc + _sigmoid(f(g1_ref)) * yb
    c = jnp.dot(yf_ref[...], fw_ref[0], preferred_element_type=F32) * _silu(f(zc_ref))
    yc = jnp.dot(c.astype(BF16), pc_ref[0], preferred_element_type=F32)
    acc = acc + _sigmoid(f(g2_ref)) * yc
    m_ref[...] = acc.astype(BF16)


def _branches(p, ys, yf, conv_w, ssm_d, wa, wb, fw, pa, pb, pc, layer, row_len):
    n = p.shape[0]
    wc = conv_w.shape[-1]
    d = pa.shape[-1]
    tm = TM_BRANCH
    tn = COL_TILE
    col = lambda k: pl.BlockSpec((tm, tn), lambda i: (i, k))
    const = lambda shape: pl.BlockSpec(shape, lambda i: (layer,) + (0,) * (len(shape) - 1),
                                       pipeline_mode=pl.Buffered(1))
    g_col0 = N_BRANCH_TILES * tn
    assert g_col0 % (2 * d) == 0 and U_TILE % 2 == 0
    return pl.pallas_call(
        functools.partial(_branch_kernel, row_len=row_len),
        grid=(n // tm,),
        in_specs=[
            pl.BlockSpec((tm, U_TILE * tn), lambda i: (i, 0)),
            pl.BlockSpec((tm, 2 * tn), lambda i: (i, U_TILE // 2)),
            col(7),
            pl.BlockSpec((tm, 2 * d), lambda i: (i, g_col0 // (2 * d))),
            pl.BlockSpec((tm, d), lambda i: (i, g_col0 // d + 2)),
            pl.BlockSpec((tm, tn), lambda i: (i, 0)),
            pl.BlockSpec((tm, tn), lambda i: (i, 0)),
            const((1, 3, wc)), const((1, 1, tn)),
            const((1, tn, tn)), const((1, tn, tn)), const((1, tn, tn)),
            const((1, wc, d)), const((1, tn, d)), const((1, tn, d)),
        ],
        out_specs=pl.BlockSpec((tm, d), lambda i: (i, 0)),
        out_shape=jax.ShapeDtypeStruct((n, d), BF16),
        compiler_params=_cparams("arbitrary"),
        name="branches",
    )(p, p, p, p, p, ys, yf, conv_w, ssm_d.reshape(ssm_d.shape[0], 1, -1),
      wa, wb, fw, pa, pb, pc)


def _out_kernel(*refs, with_next):
    if with_next:
        m_ref, x_ref, gt_ref, g_ref, w_ref, scn_ref, shn_ref, gn_ref, o_ref, h_ref, wb = refs
    else:
        m_ref, x_ref, gt_ref, g_ref, w_ref, o_ref, wb = refs

    @pl.when(pl.program_id(0) == 0)
    def _():
        wb[...] = w_ref[0].astype(BF16)

    o = jnp.dot(m_ref[...], wb[...], preferred_element_type=F32)
    ms = jnp.mean(o * o, axis=-1, keepdims=True)
    xn = x_ref[...] + o * lax.rsqrt(ms + RMS_EPS) * (g_ref[0] * gt_ref[0])
    o_ref[...] = xn
    if with_next:
        h_ref[...] = _modnorm(xn, gn_ref[0] * (1.0 + scn_ref[0]), shn_ref[0]).astype(BF16)


def _outproj(m, x2, mods_r, g_post, w_out, layer, rows_per_batch, ctx_row=None, g_pre_next=None):
    n, d = x2.shape
    tm = min(TM_OUT, rows_per_batch)
    tpb = rows_per_batch // tm
    in_specs = [
        pl.BlockSpec((tm, d), lambda i: (i, 0)),
        pl.BlockSpec((tm, d), lambda i: (i, 0)),
        _mod_spec(d, layer, MOD_GATE, tpb, ctx_row),
        pl.BlockSpec((1, 1, d), lambda i: (layer, 0, 0)),
        pl.BlockSpec((1, d, d), lambda i: (layer, 0, 0), pipeline_mode=pl.Buffered(1)),
    ]
    args = [m, x2, mods_r, g_post.reshape(g_post.shape[0], 1, d), w_out]
    out_specs = [pl.BlockSpec((tm, d), lambda i: (i, 0))]
    out_shape = [jax.ShapeDtypeStruct((n, d), F32)]
    if g_pre_next is not None:
        in_specs += [_mod_spec(d, layer + 1, MOD_SCALE, tpb, ctx_row), _mod_spec(d, layer + 1, MOD_SHIFT, tpb, ctx_row),
                     pl.BlockSpec((1, 1, d), lambda i: (layer + 1, 0, 0))]
        args += [mods_r, mods_r, g_pre_next.reshape(g_pre_next.shape[0], 1, d)]
        out_specs.append(pl.BlockSpec((tm, d), lambda i: (i, 0)))
        out_shape.append(jax.ShapeDtypeStruct((n, d), BF16))
    res = pl.pallas_call(
        functools.partial(_out_kernel, with_next=g_pre_next is not None),
        grid=(n // tm,),
        in_specs=in_specs,
        out_specs=out_specs,
        out_shape=out_shape,
        scratch_shapes=[pltpu.VMEM((d, d), BF16)],
        compiler_params=_cparams("arbitrary"),
        name="outproj",
    )(*args)
    return (res[0], res[1]) if g_pre_next is not None else (res[0], None)


def kernel(x, c, ctx, c_ctx, w_ada, b_ada, g_pre, g_post, w_in, conv_w, ssm_lam_re, ssm_lam_im, ssm_log_dt,
           ssm_b_re, ssm_b_im, ssm_c_re, ssm_c_im, ssm_d, glu_wa, glu_wb, fourier_w, proj_a, proj_b, proj_c,
           w_out):
    bn, t, d = x.shape
    lc = ctx.shape[1]
    depth = w_ada.shape[0]
    w_conv = conv_w.shape[-1]
    w_ssm = ssm_d.shape[-1]
    tn = COL_TILE
    assert w_conv == tn and w_ssm == tn and fourier_w.shape[-1] == tn and d % tn == 0
    assert bn + 1 <= MOD_ROWS and t % GRID_W == 0
    u_tile = 4 * w_conv // tn
    f_tile = u_tile + 2
    n_tiles = w_in.shape[-1] // tn
    assert u_tile == U_TILE and f_tile + 2 == N_BRANCH_TILES and (n_tiles - N_BRANCH_TILES) * tn == N_BRANCH * d

    cc = jnp.concatenate([c, c_ctx[None, :], jnp.zeros((MOD_ROWS - bn - 1, d), F32)], axis=0)
    mods = _ada_mods(cc, w_ada, b_ada)

    to_b = lambda w: w.astype(BF16)
    wa_b, wb_b, fw_b = to_b(glu_wa), to_b(glu_wb), to_b(fourier_w)
    pa_b, pb_b, pc_b = to_b(proj_a), to_b(proj_b), to_b(proj_c)
    fast_dft = t == DFT_N1 * DFT_N2
    tabs_x = None if fast_dft else _dft_tables(t, tn // FFT_GROUPS)
    tabs_c = _dft_tables(lc, tn // FFT_GROUPS)

    x2 = x.reshape(bn * t, d)
    c2 = ctx.reshape(bn * lc, d)
    mods_r = mods.reshape(depth * MOD_ROWS * 3, 1, d)
    ctx_row = bn
    ssm_p = _ssm_params(ssm_lam_re, ssm_lam_im, ssm_log_dt, ssm_b_re, ssm_b_im, ssm_c_re, ssm_c_im)
    hx = _prenorm(x2, mods_r, g_pre, 0, t)
    hc = _prenorm(c2, mods_r, g_pre, 0, lc, ctx_row)
    for l in range(depth):
        last = l == depth - 1
        g_pre_next = None if last else g_pre

        px, wq = _inproj(hx, w_in, l, 0, n_tiles, keep_weights=True)
        pc_ = _inproj(hc, wq, None, u_tile, 1) if last else _inproj(hc, wq, None, 0, n_tiles)
        u_lane_tile = u_tile * (tn // LANES)
        ys_x, ys_c = _ssm_scan(px, pc_, u_lane_tile, 0 if last else u_lane_tile, w_ssm, ssm_p, l, bn,
                               ctx_out=not last)

        yf_x = _fourier_fast(px, f_tile, t, bn) if fast_dft else _fourier(px, f_tile, t, bn, tabs_x)
        mx = _branches(px, ys_x, yf_x, conv_w, ssm_d, wa_b, wb_b, fw_b, pa_b, pb_b, pc_b, l, GRID_W)
        new_x2, hx = _outproj(mx, x2, mods_r, g_post, w_out, l, t, None, g_pre_next)
        if not last:
            yf_c = _fourier(pc_, f_tile, lc, bn, tabs_c)
            mc = _branches(pc_, ys_c, yf_c, conv_w, ssm_d, wa_b, wb_b, fw_b, pa_b, pb_b, pc_b, l, lc)
            c2, hc = _outproj(mc, c2, mods_r, g_post, w_out, l, lc, ctx_row, g_pre_next)
        x2 = new_x2
    return x2.reshape(bn, t, d)
```

```python
import functools
import math

import jax
import jax.numpy as jnp
from jax import lax
from jax.experimental import pallas as pl
from jax.experimental.pallas import tpu as pltpu

F32 = jnp.float32
BF16 = jnp.bfloat16

GRID_W = 64
FFT_GROUPS = 4
N_BRANCH = 3
RMS_EPS = 1e-6
LANES = 128
SSM_CHUNK = 8
SSM_SEGS = 8
DFT_ROWS = 64
DFT_N1, DFT_N2 = 256, 16
VMEM_LIMIT = 56 * 1024 * 1024
COL_TILE = 1024
N_BRANCH_TILES = 8
U_TILE = 4
TM_INPROJ, TM_PRENORM, TM_BRANCH, TM_OUT, TM_DFT = 2048, 2048, 256, 512, 256
TM_INPROJ_DOT = 1024


def _cparams(*sem):
    return pltpu.CompilerParams(dimension_semantics=sem, vmem_limit_bytes=VMEM_LIMIT)


def _sigmoid(v):
    return 0.5 * jnp.tanh(0.5 * v) + 0.5


def _silu(v):
    return v * _sigmoid(v)


def _gelu_tanh(v):
    return v * (0.5 * (1.0 + jnp.tanh(math.sqrt(2.0 / math.pi) * (v + 0.044715 * (v * v * v)))))


def _cmul(ar, ai, br, bi):
    return ar * br - ai * bi, ar * bi + ai * br


def _cmul_add(ar, ai, zr, zi, sr, si):
    return ar * zr - ai * zi + sr, ar * zi + ai * zr + si


def _ada_kernel(c_ref, w_ref, b_ref, o_ref):
    c = c_ref[...]
    s = _silu(c).astype(BF16)
    o_ref[0] = jnp.dot(s, w_ref[0].astype(BF16), preferred_element_type=F32) + b_ref[0]


def _ada_mods(cc, w_ada, b_ada):
    depth, d, n3 = w_ada.shape
    rows = cc.shape[0]
    tn = next(k * COL_TILE for k in (3, 2, 1) if n3 % (k * COL_TILE) == 0)
    assert n3 % tn == 0
    return pl.pallas_call(
        _ada_kernel,
        grid=(depth, n3 // tn),
        in_specs=[
            pl.BlockSpec((rows, d), lambda l, j: (0, 0)),
            pl.BlockSpec((1, d, tn), lambda l, j: (l, 0, j)),
            pl.BlockSpec((1, 1, tn), lambda l, j: (l, 0, j)),
        ],
        out_specs=pl.BlockSpec((1, rows, tn), lambda l, j: (l, 0, j)),
        out_shape=jax.ShapeDtypeStruct((depth, rows, n3), F32),
        compiler_params=_cparams("arbitrary", "arbitrary"),
        name="ada_mods",
    )(cc, w_ada, b_ada.reshape(depth, 1, n3))


def _modnorm(x, gs, sh):
    ms = jnp.mean(x * x, axis=-1, keepdims=True)
    return x * lax.rsqrt(ms + RMS_EPS) * gs + sh


def _prenorm_kernel(x_ref, sc_ref, sh_ref, g_ref, h_ref):
    rb = 16
    gs = g_ref[0] * (1.0 + sc_ref[0])
    sh = sh_ref[0]

    def body(r, carry):
        rows = pl.ds(pl.multiple_of(r * rb, rb), rb)
        h_ref[rows, :] = _modnorm(x_ref[rows, :], gs, sh).astype(BF16)
        return carry

    lax.fori_loop(0, x_ref.shape[0] // rb, body, 0, unroll=4)


MOD_SHIFT, MOD_SCALE, MOD_GATE = 0, 1, 2
MOD_ROWS = 8


def _mod_spec(d, layer, which, tiles_per_batch, ctx_row):
    def index(i):
        row = ctx_row if ctx_row is not None else i // tiles_per_batch
        return ((layer * MOD_ROWS + row) * 3 + which, 0, 0)
    return pl.BlockSpec((1, 1, d), index)


def _prenorm(x2, mods_r, g_pre, layer, rows_per_batch, ctx_row=None):
    n, d = x2.shape
    tm = min(TM_PRENORM, rows_per_batch)
    tpb = rows_per_batch // tm
    return pl.pallas_call(
        _prenorm_kernel,
        grid=(n // tm,),
        in_specs=[
            pl.BlockSpec((tm, d), lambda i: (i, 0)),
            _mod_spec(d, layer, MOD_SCALE, tpb, ctx_row),
            _mod_spec(d, layer, MOD_SHIFT, tpb, ctx_row),
            pl.BlockSpec((1, 1, d), lambda i: (layer, 0, 0)),
        ],
        out_specs=pl.BlockSpec((tm, d), lambda i: (i, 0)),
        out_shape=jax.ShapeDtypeStruct((n, d), BF16),
        compiler_params=_cparams("arbitrary"),
        name="prenorm",
    )(x2, mods_r, mods_r, g_pre.reshape(g_pre.shape[0], 1, d))


def _inproj_kernel(h_ref, w_ref, p_ref, *bf16_weights, cast):
    if cast:
        (wb,) = bf16_weights

        @pl.when(pl.program_id(1) == 0)
        def _():
            wb[...] = w_ref[0].astype(BF16)
    else:
        wb = w_ref

    sub = min(TM_INPROJ_DOT, h_ref.shape[0])
    for r in range(h_ref.shape[0] // sub):
        rows = slice(r * sub, (r + 1) * sub)
        p_ref[rows, :] = jnp.dot(h_ref[rows, :], wb[...], preferred_element_type=F32).astype(BF16)


def _inproj(h, w_in, layer, tile0, ntiles, keep_weights=False):
    n, d = h.shape
    tn = COL_TILE
    tm = min(TM_INPROJ, n)
    cast = layer is not None
    w_spec = (pl.BlockSpec((1, d, tn), lambda j, i: (layer, 0, tile0 + j)) if cast
              else pl.BlockSpec((d, tn), lambda j, i: (0, tile0 + j)))
    out_specs = [pl.BlockSpec((tm, tn), lambda j, i: (i, j))]
    out_shape = [jax.ShapeDtypeStruct((n, ntiles * tn), BF16)]
    scratch = []
    if keep_weights:
        out_specs.append(pl.BlockSpec((d, tn), lambda j, i: (0, j)))
        out_shape.append(jax.ShapeDtypeStruct((d, ntiles * tn), BF16))
    elif cast:
        scratch = [pltpu.VMEM((d, tn), BF16)]
    res = pl.pallas_call(
        functools.partial(_inproj_kernel, cast=cast),
        grid=(ntiles, n // tm),
        in_specs=[pl.BlockSpec((tm, d), lambda j, i: (i, 0)), w_spec],
        out_specs=out_specs,
        out_shape=out_shape,
        scratch_shapes=scratch,
        compiler_params=_cparams("arbitrary", "arbitrary"),
        name="inproj",
    )(h, w_in)
    return (res[0], res[1]) if keep_weights else res[0]


def _ssm_params(lam_re, lam_im, log_dt, b_re, b_im, c_re, c_im):
    nl, _, G, P = lam_re.shape
    H = b_re.shape[-1]
    gpt = LANES // H
    O = G // gpt
    lam_re = lam_re.astype(F32)
    lam_im = lam_im.astype(F32)
    dt = jnp.exp(log_dt.astype(F32))[..., None]
    lr = lam_re * dt
    li = lam_im * dt
    mag = jnp.exp(lr)
    a_re = mag * jnp.cos(li)
    a_im = mag * jnp.sin(li)
    n_re = a_re - 1.0
    n_im = a_im
    den = lam_re * lam_re + lam_im * lam_im
    q_re = (n_re * lam_re + n_im * lam_im) / den
    q_im = (n_im * lam_re - n_re * lam_im) / den
    bb_re = q_re[..., None] * b_re - q_im[..., None] * b_im
    bb_im = q_re[..., None] * b_im + q_im[..., None] * b_re

    bbs = jnp.stack([bb_re, bb_im], axis=2).reshape(nl, 2, 2, O, gpt, P, H)
    bc = jnp.transpose(bbs, (0, 3, 1, 2, 6, 4, 5)).reshape(nl, O, 4, H, gpt * P)
    cs = jnp.stack([c_re.astype(F32), c_im.astype(F32)], axis=2).reshape(nl, 2, 2, O, gpt, H, P)
    cc = jnp.transpose(cs, (0, 3, 1, 2, 5, 4, 6)).reshape(nl, O, 4, H, gpt * P)
    a4 = jnp.stack([a_re, a_im], axis=2).reshape(nl, 2, 2, O, gpt * P)
    a4 = jnp.transpose(a4, (0, 3, 1, 2, 4)).reshape(nl, O, 4, gpt * P)
    arow = jnp.concatenate([a4, jnp.zeros_like(a4)], axis=2)

    return bc, cc, arow


def _block_transpose8(v):
    bw = LANES // 8
    blk = lax.broadcasted_iota(jnp.int32, v[0].shape, 1) // bw
    for dist in (4, 2, 1):
        upper = (blk & dist) != 0
        nxt = list(v)
        for a in range(8):
            if a & dist:
                continue
            b = a + dist
            nxt[a] = jnp.where(upper, pltpu.roll(v[b], dist * bw, 1), v[a])
            nxt[b] = jnp.where(upper, v[b], pltpu.roll(v[a], LANES - dist * bw, 1))
        v = nxt
    return v


def _ssm_build_operators(bc_ref, cc_ref, arow_ref, ws, ws_lo, w3):
    L = SSM_CHUNK
    hh = bc_ref.shape[2]
    sw = bc_ref.shape[3]
    pp = sw // (LANES // hh)
    npair = sw // LANES
    assert L * hh == LANES and 2 * pp == LANES
    lane = lax.broadcasted_iota(jnp.int32, (LANES, LANES), 1)
    row = lax.broadcasted_iota(jnp.int32, (LANES, LANES), 0)
    lane_g0, row_g0 = lane < pp, row < pp
    lane_blk = lane // hh
    zeros = jnp.zeros((LANES, LANES), F32)
    zeros_b = jnp.zeros((LANES, LANES), BF16)

    def powers(re, im, n):
        out = [(jnp.ones_like(re), jnp.zeros_like(re))]
        for _ in range(n):
            out.append(_cmul(out[-1][0], out[-1][1], re, im))
        return out

    def split(v):
        hi = v.astype(BF16)
        return hi, (v - hi.astype(F32)).astype(BF16)

    def shift_rows(t, blocks):
        n = abs(blocks) * hh
        if n == 0:
            return t
        pad = jnp.zeros((n, LANES), F32)
        return jnp.concatenate([pad, t[:LANES - n]] if blocks > 0 else [t[n:], pad], axis=0)

    for q in range(npair):
        ql = slice(q * LANES, (q + 1) * LANES)
        taps = []
        for d in range(2):
            b_re = jnp.concatenate([bc_ref[0, 2 * d, :, ql]] * L, axis=0)
            b_im = jnp.concatenate([bc_ref[0, 2 * d + 1, :, ql]] * L, axis=0)
            prow = powers(arow_ref[0, 2 * d:2 * d + 1, ql], arow_ref[0, 2 * d + 1:2 * d + 2, ql], L)

            def power_rows(exps):
                return (jnp.concatenate([jnp.broadcast_to(prow[e][0], (hh, LANES)) for e in exps], axis=0),
                        jnp.concatenate([jnp.broadcast_to(prow[e][1], (hh, LANES)) for e in exps], axis=0))

            a_re, a_im = power_rows([L - 1 - j if d == 0 else j for j in range(L)])
            for part, v in zip((2 * d, 2 * d + 1), _cmul(b_re, b_im, a_re, a_im)):
                cols = slice(part * LANES, (part + 1) * LANES)
                for dst, piece in zip((ws, ws_lo), split(v)):
                    dst[q, 0:LANES, cols] = jnp.where(lane_g0, piece, zeros_b)
                    dst[q, LANES:2 * LANES, cols] = jnp.where(lane_g0, zeros_b, piece)
            ct_re = jnp.concatenate([cc_ref[0, 2 * d, :, ql]] * L, axis=0)
            ct_im = jnp.concatenate([cc_ref[0, 2 * d + 1, :, ql]] * L, axis=0)
            c_re, c_im = ct_re.T, ct_im.T
            (cr_hi, cr_lo), (ci_hi, ci_lo) = split(c_re), split(-c_im)
            c_hi = jnp.concatenate([cr_hi, ci_hi], axis=0)
            c_lo = jnp.concatenate([cr_lo, ci_lo], axis=0)
            dcols = slice(2 * d * LANES, (2 * d + 2) * LANES)
            taps.append(jnp.dot(jnp.concatenate([ws[q, :, dcols], ws[q, :, dcols], ws_lo[q, :, dcols]], axis=1),
                                jnp.concatenate([c_hi, c_lo, c_hi], axis=0), preferred_element_type=F32))
            cat_re, cat_im = _cmul(ct_re, ct_im, *power_rows([jo + 1 if d == 0 else L - jo for jo in range(L)]))
            for part, v in ((2 * d, cat_re.T), (2 * d + 1, -cat_im.T)):
                r0 = 2 * LANES + part * LANES
                w3[q, r0:r0 + LANES, 0:LANES] = jnp.where(row_g0, v, zeros).astype(BF16)
                w3[q, r0:r0 + LANES, LANES:2 * LANES] = jnp.where(row_g0, zeros, v).astype(BF16)
        for g2 in range(2):
            tf = taps[0][g2 * LANES:(g2 + 1) * LANES]
            tb = taps[1][g2 * LANES:(g2 + 1) * LANES]
            m = zeros
            for jo in range(L):
                m = jnp.where(lane_blk == jo, shift_rows(tf, jo - (L - 1)) + shift_rows(tb, jo), m)
            rows = slice(g2 * LANES, (g2 + 1) * LANES)
            w3[q, rows, g2 * LANES:(g2 + 1) * LANES] = m.astype(BF16)
            w3[q, rows, (1 - g2) * LANES:(2 - g2) * LANES] = zeros_b


def _ssm_build_powers(arow_ref, ap, pw, seg_len):
    L = SSM_CHUNK
    sw = arow_ref.shape[2]
    assert L & (L - 1) == 0 and seg_len & (seg_len - 1) == 0 and seg_len >= 8
    for d in range(2):
        cr, ci = slice(2 * d * sw, (2 * d + 1) * sw), slice((2 * d + 1) * sw, (2 * d + 2) * sw)
        re, im = arow_ref[0, 2 * d:2 * d + 1, :], arow_ref[0, 2 * d + 1:2 * d + 2, :]
        for _ in range(L.bit_length() - 1):
            re, im = _cmul(re, im, re, im)
        ap[0:1, cr] = re
        ap[0:1, ci] = im
        p = [(jnp.ones_like(re), jnp.zeros_like(re))]
        for _ in range(7):
            p.append(_cmul(p[-1][0], p[-1][1], re, im))
        order = list(range(8)) if d == 0 else list(range(7, -1, -1))
        first = slice(0, 8) if d == 0 else slice(seg_len - 8, seg_len)
        pw[first, cr] = jnp.concatenate([p[r][0] for r in order], axis=0)
        pw[first, ci] = jnp.concatenate([p[r][1] for r in order], axis=0)
        sq = _cmul(p[4][0], p[4][1], p[4][0], p[4][1])
        m = 8
        while m < seg_len:
            src = slice(0, m) if d == 0 else slice(seg_len - m, seg_len)
            dst = slice(m, 2 * m) if d == 0 else slice(seg_len - 2 * m, seg_len - m)
            xr, xi = _cmul(pw[src, cr], pw[src, ci], sq[0], sq[1])
            pw[dst, cr] = xr
            pw[dst, ci] = xi
            sq = _cmul(sq[0], sq[1], sq[0], sq[1])
            m *= 2
        ap[1:2, cr] = sq[0]
        ap[1:2, ci] = sq[1]


def _seg_pitch(seg_len):
    tiles = seg_len // 8 + 1
    return 8 * (tiles + 1 - tiles % 2)


def _ssm_kernel(*refs, nc, nx, seg_len, batch, ctx_out):
    if ctx_out:
        (ux_ref, uc_ref, bc_ref, cc_ref, arow_ref,
         yx_ref, yc_ref, ws, ws_lo, w3, lhs, st, uxf, ucf, ap, pw) = refs
    else:
        (ux_ref, uc_ref, bc_ref, cc_ref, arow_ref,
         yx_ref, ws, ws_lo, w3, lhs, st, uxf, ucf, ap, pw) = refs
        yc_ref = None
    L = SSM_CHUNK
    nlt = st.shape[0] // 4
    sw = nlt * LANES
    xw = 2 * LANES
    FR, FI, BR, BI = range(4)
    nseg = SSM_SEGS
    pitch = _seg_pitch(seg_len)
    t, lc = nx * L, nc * L

    _ssm_build_operators(bc_ref, cc_ref, arow_ref, ws, ws_lo, w3)
    _ssm_build_powers(arow_ref, ap, pw, seg_len)

    def ld(part, rows):
        return jnp.concatenate([st[part * nlt + q, rows, :] for q in range(nlt)], axis=1)

    def sto(part, rows, val):
        for q in range(nlt):
            st[part * nlt + q, rows, :] = val[:, q * LANES:(q + 1) * LANES]

    def part_cols(part):
        return slice(part * sw, (part + 1) * sw)

    def one_sequence(b, carry):
        ucf[...] = uc_ref[pl.ds(pl.multiple_of(b * lc, lc), lc), :].astype(F32)
        uxf[...] = ux_ref[pl.ds(pl.multiple_of(b * t, t), t), :].astype(F32)
        by_pos = [jnp.concatenate([ucf[pl.ds(j, nc, stride=L), :], uxf[pl.ds(j, nx, stride=L), :]], axis=0)
                  for j in range(L)]
        for g, tile_g in enumerate(_block_transpose8(by_pos)):
            lhs[g // 2, :, (g % 2) * LANES:(g % 2 + 1) * LANES] = tile_g.astype(BF16)

        for q in range(nlt):
            s = jnp.dot(lhs[q, :, 0:xw], ws[q], preferred_element_type=F32)
            for part in range(4):
                k = part * nlt + q
                st[k, 0:nc, :] = s[0:nc, part * LANES:(part + 1) * LANES]
                for g in range(nseg):
                    st[k, nc + g * pitch:nc + g * pitch + seg_len, :] = (
                        s[nc + g * seg_len:nc + (g + 1) * seg_len, part * LANES:(part + 1) * LANES])

        a_l = ap[0:1, :]
        a_seg = ap[1:2, :]
        al = [a_l[:, part_cols(p)] for p in range(4)]
        aseg = [a_seg[:, part_cols(p)] for p in range(4)]

        zero = jnp.zeros((1, sw), F32)

        def ctx_step(pr, pi, i, carry):
            er, ei = carry
            row = slice(i, i + 1)
            sr, si = ld(pr, row), ld(pi, row)
            sto(pr, row, er)
            sto(pi, row, ei)
            return _cmul_add(al[pr], al[pi], er, ei, sr, si)

        h0f, h0b = (zero, zero), (zero, zero)
        for i in range(nc):
            h0f = ctx_step(FR, FI, i, h0f)
            h0b = ctx_step(BR, BI, nc - 1 - i, h0b)
        for part in range(4):
            for q in range(nlt):
                lhs[q, 0:nc, xw + part * LANES:xw + (part + 1) * LANES] = st[part * nlt + q, 0:nc, :].astype(BF16)

        zseg = jnp.zeros((nseg, sw), F32)

        def seg_pass(pr, pi, order):
            zr, zi = zseg, zseg
            for i in order:
                rows = pl.ds(nc + i, nseg, stride=pitch)
                sr, si = ld(pr, rows), ld(pi, rows)
                sto(pr, rows, zr)
                sto(pi, rows, zi)
                zr, zi = _cmul_add(al[pr], al[pi], zr, zi, sr, si)
            return zr, zi

        zfr, zfi = seg_pass(FR, FI, range(seg_len))
        zbr, zbi = seg_pass(BR, BI, range(seg_len - 1, -1, -1))

        ef = [h0f]
        for s in range(nseg - 1):
            ef.append(_cmul_add(aseg[FR], aseg[FI], ef[s][0], ef[s][1], zfr[s:s + 1], zfi[s:s + 1]))
        eb = [None] * nseg
        eb[nseg - 1] = h0b
        for s in range(nseg - 1, 0, -1):
            eb[s - 1] = _cmul_add(aseg[BR], aseg[BI], eb[s][0], eb[s][1], zbr[s:s + 1], zbi[s:s + 1])
        for s in range(nseg):
            srows = slice(nc + s * pitch, nc + s * pitch + seg_len)
            rows = slice(nc + s * seg_len, nc + (s + 1) * seg_len)
            for (pr, pi, e) in ((FR, FI, ef[s]), (BR, BI, eb[s])):
                xr, xi = _cmul_add(pw[:, part_cols(pr)], pw[:, part_cols(pi)], e[0], e[1],
                                   ld(pr, srows), ld(pi, srows))
                for q in range(nlt):
                    ql = slice(q * LANES, (q + 1) * LANES)
                    lhs[q, rows, xw + pr * LANES:xw + (pr + 1) * LANES] = xr[:, ql].astype(BF16)
                    lhs[q, rows, xw + pi * LANES:xw + (pi + 1) * LANES] = xi[:, ql].astype(BF16)

        by_group = []
        for q in range(nlt):
            y = jnp.dot(lhs[q], w3[q], preferred_element_type=F32)
            by_group += [y[:, 0:LANES], y[:, LANES:2 * LANES]]
        for j, tile_j in enumerate(_block_transpose8(by_group)):
            if ctx_out:
                yc_ref[pl.ds(b * lc + j, nc, stride=L), :] = tile_j[0:nc]
            yx_ref[pl.ds(b * t + j, nx, stride=L), :] = tile_j[nc:nc + nx]
        return carry

    lax.fori_loop(0, batch, one_sequence, 0)


def _ssm_scan(px, pc, col_x, col_c, w, params, layer, batch, ctx_out):
    bc, cc, arow = params
    L = SSM_CHUNK
    t = px.shape[0] // batch
    lc = pc.shape[0] // batch
    nx, nc = t // L, lc // L
    seg_len = nx // SSM_SEGS
    o = w // LANES
    sw = bc.shape[-1]
    sdim = 4 * sw
    npair = sw // LANES
    out_shape = [jax.ShapeDtypeStruct((batch * t, w), F32)]
    out_specs = [pl.BlockSpec((batch * t, LANES), lambda oi: (0, oi))]
    if ctx_out:
        out_shape.append(jax.ShapeDtypeStruct((batch * lc, w), F32))
        out_specs.append(pl.BlockSpec((batch * lc, LANES), lambda oi: (0, oi)))
    tile = lambda a: pl.BlockSpec((None, 1) + a.shape[2:], lambda oi: (layer, oi) + (0,) * (a.ndim - 2))
    res = pl.pallas_call(
        functools.partial(_ssm_kernel, nc=nc, nx=nx, seg_len=seg_len, batch=batch, ctx_out=ctx_out),
        grid=(o,),
        in_specs=[
            pl.BlockSpec((batch * t, LANES), lambda oi: (0, col_x + oi)),
            pl.BlockSpec((batch * lc, LANES), lambda oi: (0, col_c + oi)),
            tile(bc), tile(cc), tile(arow),
        ],
        out_specs=out_specs,
        out_shape=out_shape,
        scratch_shapes=[
            pltpu.VMEM((npair, 2 * LANES, 4 * LANES), BF16),
            pltpu.VMEM((npair, 2 * LANES, 4 * LANES), BF16),
            pltpu.VMEM((npair, 6 * LANES, 2 * LANES), BF16),
            pltpu.VMEM((npair, nc + nx, 6 * LANES), BF16),
            pltpu.VMEM((sdim // LANES, nc + SSM_SEGS * _seg_pitch(seg_len), LANES), F32),
            pltpu.VMEM((t, LANES), F32),
            pltpu.VMEM((lc, LANES), F32),
            pltpu.VMEM((8, sdim), F32),
            pltpu.VMEM((seg_len, sdim), F32),
        ],
        compiler_params=_cparams("arbitrary"),
        name="ssm_scan",
    )(px, pc, bc, cc, arow)
    return (res[0], res[1]) if ctx_out else (res[0], None)


def _dft_table_kernel(e1_ref, e2_ref, o_ref, *, t):
    e1c, e1s = e1_ref[0, :, 0:t], e1_ref[0, :, t:2 * t]
    e2c, e2s = e2_ref[:, 0:t], e2_ref[:, t:2 * t]
    c, s = _cmul(e2c, e2s, e1c, e1s)
    o_ref[:, 0:t] = c.astype(BF16)
    o_ref[:, t:2 * t] = (-s).astype(BF16)


def _dft_tables(t, gw):
    def cis(rows_mult, nrows):
        k = jnp.arange(nrows, dtype=jnp.int32)[:, None] * rows_mult
        n = jnp.arange(t, dtype=jnp.int32)[None, :]
        ang = ((k * n) % t).astype(F32) * (2.0 * math.pi / t)
        return jnp.concatenate([jnp.cos(ang), jnp.sin(ang)], axis=1)
    r = DFT_ROWS
    e1 = cis(r, t // r).reshape(t // r, 1, 2 * t)
    e2 = cis(1, r)
    tab_t = pl.pallas_call(
        functools.partial(_dft_table_kernel, t=t),
        grid=(t // r,),
        in_specs=[
            pl.BlockSpec((1, 1, 2 * t), lambda i: (i, 0, 0)),
            pl.BlockSpec((r, 2 * t), lambda i: (0, 0)),
        ],
        out_specs=pl.BlockSpec((r, 2 * t), lambda i: (i, 0)),
        out_shape=jax.ShapeDtypeStruct((t, 2 * t), BF16),
        compiler_params=_cparams("arbitrary"),
        name="dft_table",
    )(e1, e2)
    kc = jnp.arange(gw, dtype=jnp.int32)
    ang = ((kc[:, None] * kc[None, :]) % gw).astype(F32) * (2.0 * math.pi / gw)
    tab_c = jnp.concatenate([jnp.cos(ang), jnp.sin(ang)], axis=1).astype(BF16)
    return tab_t, tab_c


def _fft_kernel(f_ref, cs_ref, tab_ref, o_ref, data, *, t, gw, groups, scale):
    i = pl.program_id(1)

    @pl.when(i == 0)
    def _():
        rb = min(512, t)
        for r in range(t // rb):
            for g in range(groups):
                fg = f_ref[r * rb:(r + 1) * rb, g * gw:(g + 1) * gw]
                z = jnp.dot(fg, cs_ref[...], preferred_element_type=F32)
                data[r * rb:(r + 1) * rb, g * gw:(g + 1) * gw] = z[:, :gw].astype(BF16)
                data[t + r * rb:t + (r + 1) * rb, g * gw:(g + 1) * gw] = z[:, gw:].astype(BF16)

    y = jnp.dot(tab_ref[...], data[...], preferred_element_type=F32)
    o_ref[...] = (y * scale).astype(BF16)


def _fourier(p, col_tile, t, batch, tabs):
    tab_t, tab_c = tabs
    wf = COL_TILE
    gw = wf // FFT_GROUPS
    tm = min(TM_DFT, t)
    return pl.pallas_call(
        functools.partial(_fft_kernel, t=t, gw=gw, groups=FFT_GROUPS, scale=1.0 / math.sqrt(t * gw)),
        grid=(batch, t // tm),
        in_specs=[
            pl.BlockSpec((t, wf), lambda b, i: (b, col_tile)),
            pl.BlockSpec((gw, 2 * gw), lambda b, i: (0, 0)),
            pl.BlockSpec((tm, 2 * t), lambda b, i: (i, 0)),
        ],
        out_specs=pl.BlockSpec((tm, wf), lambda b, i: (b * (t // tm) + i, 0)),
        out_shape=jax.ShapeDtypeStruct((batch * t, wf), BF16),
        scratch_shapes=[pltpu.VMEM((2 * t, wf), BF16)],
        compiler_params=_cparams("arbitrary", "arbitrary"),
        name="fourier",
    )(p, tab_c, tab_t)


def _cmul_const(xr, xi, c, s):
    def scaled(v, k):
        if abs(k) < 1e-12:
            return None
        return v if abs(k - 1.0) < 1e-12 else -v if abs(k + 1.0) < 1e-12 else v * k

    def add(a, b):
        return b if a is None else a if b is None else a + b

    return add(scaled(xr, c), scaled(xi, -s)), add(scaled(xi, c), scaled(xr, s))


def _dft4(y):
    (ar, ai), (br, bi), (cr, ci), (dr, di) = y
    sr, si, tr, ti = ar + cr, ai + ci, ar - cr, ai - ci
    ur, ui, vr, vi = br + dr, bi + di, br - dr, bi - di
    return [(sr + ur, si + ui), (tr + vi, ti - vr), (sr - ur, si - ui), (tr - vi, ti + vr)]


def _dft16(x):
    out = [None] * 16
    p = [_dft4([x[4 * a + b] for a in range(4)]) for b in range(4)]
    for ka in range(4):
        q = []
        for b in range(4):
            ang = -2.0 * math.pi * ka * b / 16.0
            q.append(_cmul_const(p[b][ka][0], p[b][ka][1], math.cos(ang), math.sin(ang)))
        r = _dft4(q)
        for kb in range(4):
            out[ka + 4 * kb] = r[kb]
    return out


def _fft_fast_kernel(f_ref, csc_ref, tw_ref, cst_ref, o_ref, z, o_scr, tabs, *, n1, n2, gw, scale):
    @pl.when((pl.program_id(0) == 0) & (pl.program_id(1) == 0))
    def _():
        c, s = cst_ref[:, 0:n1], cst_ref[:, n1:2 * n1]
        for k2 in range(n2):
            twr, twi = tw_ref[0, k2:k2 + 1, :], tw_ref[1, k2:k2 + 1, :]
            tabs[k2, :, 0:n1] = (c * twr + s * twi).astype(BF16)
            tabs[k2, :, n1:2 * n1] = (s * twr - c * twi).astype(BF16)

    slabs_per_dot = 4
    for q in range(n2 // slabs_per_dot):
        r0 = q * slabs_per_dot * n1
        zz = jnp.dot(f_ref[r0:r0 + slabs_per_dot * n1, :], csc_ref[...], preferred_element_type=F32)
        for s in range(slabs_per_dot):
            z[0, q * slabs_per_dot + s] = zz[s * n1:(s + 1) * n1, 0:gw]
            z[1, q * slabs_per_dot + s] = zz[s * n1:(s + 1) * n1, gw:2 * gw]

    def tile_body(r, carry):
        rows = pl.ds(pl.multiple_of(r * 8, 8), 8)
        for c in range(gw // LANES):
            cols = slice(c * LANES, (c + 1) * LANES)
            h = _dft16([(z[0, t, rows, cols], z[1, t, rows, cols]) for t in range(n2)])
            for k2 in range(n2):
                z[0, k2, rows, cols] = h[k2][0]
                z[1, k2, rows, cols] = h[k2][1]
        return carry

    lax.fori_loop(0, n1 // 8, tile_body, 0)

    for k2 in range(n2):
        rhs = jnp.concatenate([z[0, k2], z[1, k2]], axis=0).astype(BF16)
        y = jnp.dot(tabs[k2], rhs, preferred_element_type=F32) * scale
        for c in range(gw // LANES):
            o_scr[c, pl.ds(k2, n1, stride=n2), :] = y[:, c * LANES:(c + 1) * LANES]
    o_ref[...] = jnp.concatenate([o_scr[c] for c in range(gw // LANES)], axis=1).astype(BF16)


def _fourier_fast(p, col_tile, t, batch):
    n1, n2 = DFT_N1, DFT_N2
    assert t == n1 * n2
    wf = COL_TILE
    gw = wf // FFT_GROUPS
    ang = lambda a, b, n: ((a[:, None] * b[None, :]) % n).astype(F32) * (2.0 * math.pi / n)
    kc = jnp.arange(gw, dtype=jnp.int32)
    a_c = ang(kc, kc, gw)
    csc = jnp.concatenate([jnp.cos(a_c), -jnp.sin(a_c)], axis=1).astype(BF16)
    k1 = jnp.arange(n1, dtype=jnp.int32)
    a_t = ang(k1, k1, n1)
    cst = jnp.concatenate([jnp.cos(a_t), jnp.sin(a_t)], axis=1)
    a_w = ang(jnp.arange(n2, dtype=jnp.int32), k1, t)
    tw = jnp.stack([jnp.cos(a_w), -jnp.sin(a_w)], axis=0)
    ngrp = wf // gw
    return pl.pallas_call(
        functools.partial(_fft_fast_kernel, n1=n1, n2=n2, gw=gw, scale=1.0 / math.sqrt(t * gw)),
        grid=(batch, ngrp),
        in_specs=[
            pl.BlockSpec((t, gw), lambda b, g: (b, col_tile * ngrp + g)),
            pl.BlockSpec((gw, 2 * gw), lambda b, g: (0, 0)),
            pl.BlockSpec((2, n2, n1), lambda b, g: (0, 0, 0)),
            pl.BlockSpec((n1, 2 * n1), lambda b, g: (0, 0)),
        ],
        out_specs=pl.BlockSpec((t, gw), lambda b, g: (b, g)),
        out_shape=jax.ShapeDtypeStruct((batch * t, wf), BF16),
        scratch_shapes=[
            pltpu.VMEM((2, n2, n1, gw), F32),
            pltpu.VMEM((gw // LANES, t, LANES), F32),
            pltpu.VMEM((n2, n1, 2 * n1), BF16),
        ],
        compiler_params=_cparams("arbitrary", "arbitrary"),
        name="fourier_fast",
    )(p, csc, tw, cst)


def _branch_kernel(conv_ref, uz_ref, zc_ref, g01_ref, g2_ref, ys_ref, yf_ref, cw_ref, sd_ref, wa_ref, wb_ref,
                   fw_ref, pa_ref, pb_ref, pc_ref, m_ref, *, row_len):
    tm = conv_ref.shape[0]
    tn = zc_ref.shape[1]
    d = g2_ref.shape[1]
    xa_ref, ba_ref, ca_ref, za_ref = (conv_ref.at[:, k * tn:(k + 1) * tn] for k in range(4))
    u_ref, zb_ref = (uz_ref.at[:, k * tn:(k + 1) * tn] for k in range(2))
    g0_ref, g1_ref = (g01_ref.at[:, k * d:(k + 1) * d] for k in range(2))
    kc = 512
    chunks =[slice(k * kc, (k + 1) * kc) for k in range(tn // kc)]
    f = lambda r, cs: r[:, cs].astype(F32)
    pos = lax.broadcasted_iota(jnp.int32, (tm, kc), 0) % row_len
    first, last = pos == 0, pos == row_len - 1
    ya = ga = gb = None
    add = lambda s, t: t if s is None else s + t
    for cs in chunks:
        v = f(ca_ref, cs) * f(xa_ref, cs)
        v_prev = jnp.where(first, 0.0, pltpu.roll(v, 1, 0))
        v_next = jnp.where(last, 0.0, pltpu.roll(v, tm - 1, 0))
        cw = cw_ref[0, :, cs]
        conv = v_prev * cw[0:1, :] + v * cw[1:2, :] + v_next * cw[2:3, :]
        a = f(ba_ref, cs) * conv * _silu(f(za_ref, cs))
        ya = add(ya, jnp.dot(a.astype(BF16), pa_ref[0, cs, :], preferred_element_type=F32))
        y = _gelu_tanh(ys_ref[:, cs] + sd_ref[0, :, cs] * f(u_ref, cs)).astype(BF16)
        ga = add(ga, jnp.dot(y, wa_ref[0, cs, :], preferred_element_type=F32))
        gb = add(gb, jnp.dot(y, wb_ref[0, cs, :], preferred_element_type=F32))
    cf = jnp.dot(yf_ref[...], fw_ref[0], preferred_element_type=F32)
    yb = yc = None
    for cs in chunks:
        glu = ga[:, cs] * _sigmoid(gb[:, cs]) * _silu(f(zb_ref, cs))
        yb = add(yb, jnp.dot(glu.astype(BF16), pb_ref[0, cs, :], preferred_element_type=F32))
        c = cf[:, cs] * _silu(f(zc_ref, cs))
        yc = add(yc, jnp.dot(c.astype(BF16), pc_ref[0, cs, :], preferred_element_type=F32))
    g = lambda r: r[...].astype(F32)
    acc = _sigmoid(g(g0_ref)) * ya + _sigmoid(g(g1_ref)) * yb + _sigmoid(g(g2_ref)) * yc
    m_ref[...] = acc.astype(BF16)


def _branches(p, ys, yf, conv_w, ssm_d, wa, wb, fw, pa, pb, pc, layer, row_len):
    n = p.shape[0]
    wc = conv_w.shape[-1]
    d = pa.shape[-1]
    tm = TM_BRANCH
    tn = COL_TILE
    col = lambda k: pl.BlockSpec((tm, tn), lambda i: (i, k))
    const = lambda shape: pl.BlockSpec(shape, lambda i: (layer,) + (0,) * (len(shape) - 1),
                                       pipeline_mode=pl.Buffered(1))
    g_col0 = N_BRANCH_TILES * tn
    assert g_col0 % (2 * d) == 0 and U_TILE % 2 == 0
    return pl.pallas_call(
        functools.partial(_branch_kernel, row_len=row_len),
        grid=(n // tm,),
        in_specs=[
            pl.BlockSpec((tm, U_TILE * tn), lambda i: (i, 0)),
            pl.BlockSpec((tm, 2 * tn), lambda i: (i, U_TILE // 2)),
            col(7),
            pl.BlockSpec((tm, 2 * d), lambda i: (i, g_col0 // (2 * d))),
            pl.BlockSpec((tm, d), lambda i: (i, g_col0 // d + 2)),
            pl.BlockSpec((tm, tn), lambda i: (i, 0)),
            pl.BlockSpec((tm, tn), lambda i: (i, 0)),
            const((1, 3, wc)), const((1, 1, tn)),
            const((1, tn, tn)), const((1, tn, tn)), const((1, tn, tn)),
            const((1, wc, d)), const((1, tn, d)), const((1, tn, d)),
        ],
        out_specs=pl.BlockSpec((tm, d), lambda i: (i, 0)),
        out_shape=jax.ShapeDtypeStruct((n, d), BF16),
        compiler_params=_cparams("arbitrary"),
        name="branches",
    )(p, p, p, p, p, ys, yf, conv_w, ssm_d.reshape(ssm_d.shape[0], 1, -1),
      wa, wb, fw, pa, pb, pc)


def _out_kernel(*refs, with_next):
    if with_next:
        m_ref, x_ref, gt_ref, g_ref, w_ref, scn_ref, shn_ref, gn_ref, o_ref, h_ref, wb = refs
    else:
        m_ref, x_ref, gt_ref, g_ref, w_ref, o_ref, wb = refs

    @pl.when(pl.program_id(0) == 0)
    def _():
        wb[...] = w_ref[0].astype(BF16)

    o = jnp.dot(m_ref[...], wb[...], preferred_element_type=F32)
    ms = jnp.mean(o * o, axis=-1, keepdims=True)
    xn = x_ref[...] + o * lax.rsqrt(ms + RMS_EPS) * (g_ref[0] * gt_ref[0])
    o_ref[...] = xn
    if with_next:
        h_ref[...] = _modnorm(xn, gn_ref[0] * (1.0 + scn_ref[0]), shn_ref[0]).astype(BF16)


def _outproj(m, x2, mods_r, g_post, w_out, layer, rows_per_batch, ctx_row=None, g_pre_next=None):
    n, d = x2.shape
    tm = min(TM_OUT, rows_per_batch)
    tpb = rows_per_batch // tm
    in_specs = [
        pl.BlockSpec((tm, d), lambda i: (i, 0)),
        pl.BlockSpec((tm, d), lambda i: (i, 0)),
        _mod_spec(d, layer, MOD_GATE, tpb, ctx_row),
        pl.BlockSpec((1, 1, d), lambda i: (layer, 0, 0)),
        pl.BlockSpec((1, d, d), lambda i: (layer, 0, 0), pipeline_mode=pl.Buffered(1)),
    ]
    args = [m, x2, mods_r, g_post.reshape(g_post.shape[0], 1, d), w_out]
    out_specs = [pl.BlockSpec((tm, d), lambda i: (i, 0))]
    out_shape = [jax.ShapeDtypeStruct((n, d), F32)]
    if g_pre_next is not None:
        in_specs += [_mod_spec(d, layer + 1, MOD_SCALE, tpb, ctx_row), _mod_spec(d, layer + 1, MOD_SHIFT, tpb, ctx_row),
                     pl.BlockSpec((1, 1, d), lambda i: (layer + 1, 0, 0))]
        args += [mods_r, mods_r, g_pre_next.reshape(g_pre_next.shape[0], 1, d)]
        out_specs.append(pl.BlockSpec((tm, d), lambda i: (i, 0)))
        out_shape.append(jax.ShapeDtypeStruct((n, d), BF16))
    res = pl.pallas_call(
        functools.partial(_out_kernel, with_next=g_pre_next is not None),
        grid=(n // tm,),
        in_specs=in_specs,
        out_specs=out_specs,
        out_shape=out_shape,
        scratch_shapes=[pltpu.VMEM((d, d), BF16)],
        compiler_params=_cparams("arbitrary"),
        name="outproj",
    )(*args)
    return (res[0], res[1]) if g_pre_next is not None else (res[0], None)


def kernel(x, c, ctx, c_ctx, w_ada, b_ada, g_pre, g_post, w_in, conv_w, ssm_lam_re, ssm_lam_im, ssm_log_dt,
           ssm_b_re, ssm_b_im, ssm_c_re, ssm_c_im, ssm_d, glu_wa, glu_wb, fourier_w, proj_a, proj_b, proj_c,
           w_out):
    bn, t, d = x.shape
    lc = ctx.shape[1]
    depth = w_ada.shape[0]
    w_conv = conv_w.shape[-1]
    w_ssm = ssm_d.shape[-1]
    tn = COL_TILE
    assert w_conv == tn and w_ssm == tn and fourier_w.shape[-1] == tn and d % tn == 0
    assert bn + 1 <= MOD_ROWS and t % GRID_W == 0
    u_tile = 4 * w_conv // tn
    f_tile = u_tile + 2
    n_tiles = w_in.shape[-1] // tn
    assert u_tile == U_TILE and f_tile + 2 == N_BRANCH_TILES and (n_tiles - N_BRANCH_TILES) * tn == N_BRANCH * d

    cc = jnp.concatenate([c, c_ctx[None, :], jnp.zeros((MOD_ROWS - bn - 1, d), F32)], axis=0)
    mods = _ada_mods(cc, w_ada, b_ada)

    to_b = lambda w: w.astype(BF16)
    wa_b, wb_b, fw_b = to_b(glu_wa), to_b(glu_wb), to_b(fourier_w)
    pa_b, pb_b, pc_b = to_b(proj_a), to_b(proj_b), to_b(proj_c)
    fast_dft = t == DFT_N1 * DFT_N2
    tabs_x = None if fast_dft else _dft_tables(t, tn // FFT_GROUPS)
    tabs_c = _dft_tables(lc, tn // FFT_GROUPS)

    x2 = x.reshape(bn * t, d)
    c2 = ctx.reshape(bn * lc, d)
    mods_r = mods.reshape(depth * MOD_ROWS * 3, 1, d)
    ctx_row = bn
    ssm_p = _ssm_params(ssm_lam_re, ssm_lam_im, ssm_log_dt, ssm_b_re, ssm_b_im, ssm_c_re, ssm_c_im)
    hx = _prenorm(x2, mods_r, g_pre, 0, t)
    hc = _prenorm(c2, mods_r, g_pre, 0, lc, ctx_row)
    for l in range(depth):
        last = l == depth - 1
        g_pre_next = None if last else g_pre

        px, wq = _inproj(hx, w_in, l, 0, n_tiles, keep_weights=True)
        pc_ = _inproj(hc, wq, None, u_tile, 1) if last else _inproj(hc, wq, None, 0, n_tiles)
        u_lane_tile = u_tile * (tn // LANES)
        ys_x, ys_c = _ssm_scan(px, pc_, u_lane_tile, 0 if last else u_lane_tile, w_ssm, ssm_p, l, bn,
                               ctx_out=not last)

        yf_x = _fourier_fast(px, f_tile, t, bn) if fast_dft else _fourier(px, f_tile, t, bn, tabs_x)
        mx = _branches(px, ys_x, yf_x, conv_w, ssm_d, wa_b, wb_b, fw_b, pa_b, pb_b, pc_b, l, GRID_W)
        new_x2, hx = _outproj(mx, x2, mods_r, g_post, w_out, l, t, None, g_pre_next)
        if not last:
            yf_c = _fourier(pc_, f_tile, lc, bn, tabs_c)
            mc = _branches(pc_, ys_c, yf_c, conv_w, ssm_d, wa_b, wb_b, fw_b, pa_b, pb_b, pc_b, l, lc)
            c2, hc = _outproj(mc, c2, mods_r, g_post, w_out, l, lc, ctx_row, g_pre_next)
        x2 = new_x2
    return x2.reshape(bn, t, d)
```
